```python
import math
import jax, jax.numpy as jnp
from jax import lax
import numpy as np

D_MODEL = 1024
BATCH = 16
SEQ = 4096
DEPTH = 2

HEAD_DIM = 64
H_SB = 4
H_FOX = 4
H_MLA = 4
H_DIL = 4
N_HEADS_OUT = H_SB + H_FOX + H_MLA + H_DIL
D_MIX = N_HEADS_OUT * HEAD_DIM
Q_BLOCK = 128
ROPE_THETA = 10000.0
MLA_Q_LORA = 256
MLA_KV_LORA = 128
MLA_NOPE = 64
MLA_ROPE = 32
MLA_V = HEAD_DIM
DIL_BRANCHES = ((128, 1), (512, 4), (2048, 16))
N_BR = len(DIL_BRANCHES)
N_SB = 3 * H_SB * HEAD_DIM
N_FOX = 3 * H_FOX * HEAD_DIM + H_FOX
N_MLA = MLA_Q_LORA + MLA_KV_LORA + MLA_ROPE
N_DIL = 3 * N_BR * H_DIL * HEAD_DIM
N_IN = N_SB + N_FOX + N_MLA + N_DIL
N_GROUPS = 4
EXPERTS_PER_GROUP = 4
N_EXPERTS = N_GROUPS * EXPERTS_PER_GROUP
TOP_K = 2
D_EXPERT = 512
MOE_BLOCK = 256
LN_EPS = 1e-5
RMS_EPS = 1e-6

kernel_name = 'hybrid_sb_fox_mla_dilated_hmoe'


def _layernorm(x, g, b):
    xf = x.astype(jnp.float32)
    mu = jnp.mean(xf, -1, keepdims=True)
    var = jnp.mean(jnp.square(xf - mu), -1, keepdims=True)
    return ((xf - mu) * lax.rsqrt(var + LN_EPS) * g + b).astype(x.dtype)


def _rmsnorm(x, g):
    xf = x.astype(jnp.float32)
    return (xf * lax.rsqrt(jnp.mean(jnp.square(xf), -1, keepdims=True) + RMS_EPS) * g).astype(x.dtype)


def _rope(x, pos):
    half = x.shape[-1] // 2
    inv_freq = ROPE_THETA ** (-jnp.arange(half, dtype=jnp.float32) / half)
    ang = pos.astype(jnp.float32)[:, None] * inv_freq[None, :]
    cos = jnp.cos(ang)[None, :, None, :]
    sin = jnp.sin(ang)[None, :, None, :]
    xf = x.astype(jnp.float32)
    x1, x2 = xf[..., :half], xf[..., half:]
    return jnp.concatenate([x1 * cos - x2 * sin, x2 * cos + x1 * sin], -1).astype(x.dtype)


def _sweep(block_fn, seq):
    out = lax.map(block_fn, jnp.arange(seq // Q_BLOCK, dtype=jnp.int32) * Q_BLOCK)
    nb, b, h, blk, dv = out.shape
    return out.transpose(1, 2, 0, 3, 4).reshape(b, h, nb * blk, dv)


def _causal_softmax_sweep(logits_fn, v):
    seq = v.shape[2]
    kpos = jnp.arange(seq)

    def block(t0):
        logits = logits_fn(t0)
        qpos = t0 + jnp.arange(Q_BLOCK)
        mask = kpos[None, :] <= qpos[:, None]
        p = jax.nn.softmax(jnp.where(mask, logits, -jnp.inf), axis=-1)
        return jnp.einsum('bhqk,bhkd->bhqd', p.astype(v.dtype), v)

    return _sweep(block, seq)


def _stick_breaking_attention(q, k, v):
    seq = q.shape[2]
    scale = q.shape[-1] ** -0.5
    kpos = jnp.arange(seq)

    def block(t0):
        qb = lax.dynamic_slice_in_dim(q, t0, Q_BLOCK, axis=2)
        z = jnp.einsum('bhqd,bhkd->bhqk', qb, k).astype(jnp.float32) * scale
        qpos = t0 + jnp.arange(Q_BLOCK)
        before = kpos[None, :] < qpos[:, None]
        log_keep = jnp.where(before, jax.nn.log_sigmoid(-z), 0.0)
        log_after = lax.cumsum(log_keep, axis=3, reverse=True) - log_keep
        w = jnp.where(before, jnp.exp(jax.nn.log_sigmoid(z) + log_after), 0.0)
        return jnp.einsum('bhqk,bhkd->bhqd', w.astype(v.dtype), v)

    return _sweep(block, seq)


def _forgetting_attention(q, k, v, log_f):
    scale = q.shape[-1] ** -0.5
    c = jnp.cumsum(log_f, axis=-1)

    def logits(t0):
        qb = lax.dynamic_slice_in_dim(q, t0, Q_BLOCK, axis=2)
        cb = lax.dynamic_slice_in_dim(c, t0, Q_BLOCK, axis=2)
        s = jnp.einsum('bhqd,bhkd->bhqk', qb, k).astype(jnp.float32) * scale
        return s + cb[..., :, None] - c[..., None, :]

    return _causal_softmax_sweep(logits, v)


def _mla_attention(q_nope, q_rope, k_nope, k_rope, v):
    scale = (q_nope.shape[-1] + q_rope.shape[-1]) ** -0.5

    def logits(t0):
        qn = lax.dynamic_slice_in_dim(q_nope, t0, Q_BLOCK, axis=2)
        qr = lax.dynamic_slice_in_dim(q_rope, t0, Q_BLOCK, axis=2)
        s = (jnp.einsum('bhqd,bhkd->bhqk', qn, k_nope).astype(jnp.float32)
             + jnp.einsum('bhqr,bkr->bhqk', qr, k_rope).astype(jnp.float32))
        return s * scale

    return _causal_softmax_sweep(logits, v)


def _dilated_attention(qs, ks, vs):
    seq = qs[0].shape[1]
    scale = qs[0].shape[-1] ** -0.5

    def block(t0):
        qpos = t0 + jnp.arange(Q_BLOCK)
        outs, lses = [], []
        for (window, dil), q, k, v in zip(DIL_BRANCHES, qs, ks, vs):
            n_keys = window // dil + 1
            idx = qpos[:, None] - dil * jnp.arange(n_keys)[None, :]
            valid = idx >= 0
            idx = jnp.maximum(idx, 0)
            qb = lax.dynamic_slice_in_dim(q, t0, Q_BLOCK, axis=1)
            kb = jnp.take(k, idx, axis=1)
            vb = jnp.take(v, idx, axis=1)
            z = jnp.einsum('bqhd,bqnhd->bhqn', qb, kb).astype(jnp.float32) * scale
            z = jnp.where(valid[None, None], z, -jnp.inf)
            m = jnp.max(z, -1, keepdims=True)
            p = jnp.exp(z - m)
            den = jnp.sum(p, -1, keepdims=True)
            outs.append(jnp.einsum('bhqn,bqnhd->bhqd', (p / den).astype(v.dtype), vb).astype(jnp.float32))
            lses.append(m + jnp.log(den))
        mix = jax.nn.softmax(jnp.stack(lses), axis=0)
        return jnp.sum(mix * jnp.stack(outs), axis=0).astype(vs[0].dtype)

    return _sweep(block, seq)


def _mixing(x, w_in, b_forget, g_cq, g_ckv, w_uq, w_ukv, g_head, w_out):
    bsz, seq, _ = x.shape
    pos = jnp.arange(seq)
    proj = x @ w_in
    p_sb, p_fox, p_mla, p_dil = jnp.split(proj, [N_SB, N_SB + N_FOX, N_SB + N_FOX + N_MLA], axis=-1)

    def heads(t, n):
        return t.reshape(bsz, seq, n, -1).transpose(0, 2, 1, 3)

    q, k, v = jnp.split(p_sb, 3, axis=-1)
    o_sb = _stick_breaking_attention(heads(q, H_SB), heads(k, H_SB), heads(v, H_SB))

    hd = H_FOX * HEAD_DIM
    q, k, v, f = jnp.split(p_fox, [hd, 2 * hd, 3 * hd], axis=-1)
    log_f = jax.nn.log_sigmoid((f + b_forget).astype(jnp.float32)).transpose(0, 2, 1)
    o_fox = _forgetting_attention(heads(q, H_FOX), heads(k, H_FOX), heads(v, H_FOX), log_f)

    c_q, c_kv, k_r = jnp.split(p_mla, [MLA_Q_LORA, MLA_Q_LORA + MLA_KV_LORA], axis=-1)
    q = (_rmsnorm(c_q, g_cq) @ w_uq).reshape(bsz, seq, H_MLA, MLA_NOPE + MLA_ROPE)
    kv = (_rmsnorm(c_kv, g_ckv) @ w_ukv).reshape(bsz, seq, H_MLA, MLA_NOPE + MLA_V)
    q_rope = _rope(q[..., MLA_NOPE:], pos)
    k_rope = _rope(k_r[:, :, None, :], pos)[:, :, 0]
    tr = lambda a: a.transpose(0, 2, 1, 3)
    o_mla = _mla_attention(tr(q[..., :MLA_NOPE]), tr(q_rope), tr(kv[..., :MLA_NOPE]), k_rope,
                           tr(kv[..., MLA_NOPE:]))

    q, k, v = jnp.split(p_dil, 3, axis=-1)
    br = lambda a: a.reshape(bsz, seq, N_BR, H_DIL, HEAD_DIM)
    flat = lambda a: a.reshape(bsz, seq, N_BR * H_DIL, HEAD_DIM)
    q = br(_rope(flat(q), pos))
    k = br(_rope(flat(k), pos))
    v = br(v)
    o_dil = _dilated_attention([q[:, :, i] for i in range(N_BR)],
                               [k[:, :, i] for i in range(N_BR)],
                               [v[:, :, i] for i in range(N_BR)])

    o = jnp.concatenate([o_sb, o_fox, o_mla, o_dil], axis=1)
    o = _rmsnorm(o, g_head[None, :, None, :])
    return o.transpose(0, 2, 1, 3).reshape(bsz, seq, D_MIX) @ w_out


def _hier_moe(x, w_group, b_group, w_expert, b_expert, w1, w3, w2):
    n_tok, d = x.shape
    xf = x.astype(jnp.float32)
    g_logits = xf @ w_group.astype(jnp.float32) + b_group.astype(jnp.float32)
    g_sel = jnp.argmax(g_logits, axis=-1)
    g_w = jnp.take_along_axis(jax.nn.softmax(g_logits, -1), g_sel[:, None], axis=-1)
    e_logits = (xf @ w_expert.astype(jnp.float32) + b_expert.astype(jnp.float32)).reshape(
        n_tok, N_GROUPS, EXPERTS_PER_GROUP)
    e_in_group = jnp.take_along_axis(e_logits, g_sel[:, None, None], axis=1)[:, 0]
    top_logit, top_idx = lax.top_k(e_in_group, TOP_K)
    gate = jax.nn.softmax(top_logit, -1) * g_w
    expert_id = (g_sel[:, None] * EXPERTS_PER_GROUP + top_idx).reshape(-1).astype(jnp.int32)

    n_assign = n_tok * TOP_K
    n_slots = n_assign + N_EXPERTS * MOE_BLOCK
    n_chunks = n_slots // MOE_BLOCK
    tok_id = jnp.repeat(jnp.arange(n_tok, dtype=jnp.int32), TOP_K)
    order = jnp.argsort(expert_id)
    e_sorted = expert_id[order]
    counts = jnp.bincount(expert_id, length=N_EXPERTS)
    start = jnp.cumsum(counts) - counts
    padded = (counts + MOE_BLOCK - 1) // MOE_BLOCK * MOE_BLOCK
    pad_end = jnp.cumsum(padded)
    pad_start = pad_end - padded
    dest = pad_start[e_sorted] + jnp.arange(n_assign) - start[e_sorted]
    slot_tok = jnp.full((n_slots,), n_tok, jnp.int32).at[dest].set(tok_id[order])
    slot_gate = jnp.zeros((n_slots,), jnp.float32).at[dest].set(gate.reshape(-1)[order])
    chunk_expert = jnp.minimum(
        jnp.searchsorted(pad_end, jnp.arange(n_chunks) * MOE_BLOCK, side='right'), N_EXPERTS - 1)
    x_pad = jnp.concatenate([x, jnp.zeros((1, d), x.dtype)], axis=0)
    xs = x_pad[slot_tok].reshape(n_chunks, MOE_BLOCK, d)

    def expert_block(args):
        xb, e = args
        h = jax.nn.silu(xb @ w1[e]) * (xb @ w3[e])
        return h @ w2[e]

    ys = lax.map(expert_block, (xs, chunk_expert)).reshape(n_slots, d)
    ys = ys * slot_gate[:, None].astype(ys.dtype)
    return jax.ops.segment_sum(ys, slot_tok, num_segments=n_tok + 1)[:n_tok]


def setup_inputs(seed: int = 0) -> dict:
    key = jax.random.key(seed)
    ks = jax.random.split(key, 20)
    L = DEPTH
    beta = (8.0 * DEPTH) ** -0.25
    f32 = jnp.float32

    def normal(k, shape, scale):
        return jax.random.normal(k, shape, f32) * scale

    return {
        'x': normal(ks[0], (BATCH, SEQ, D_MODEL), 1.0),
        'w_in': normal(ks[1], (L, D_MODEL, N_IN), D_MODEL ** -0.5),
        'b_forget': normal(ks[2], (L, H_FOX), 0.1),
        'g_cq': 1.0 + normal(ks[3], (L, MLA_Q_LORA), 0.02),
        'g_ckv': 1.0 + normal(ks[4], (L, MLA_KV_LORA), 0.02),
        'w_uq': normal(ks[5], (L, MLA_Q_LORA, H_MLA * (MLA_NOPE + MLA_ROPE)), MLA_Q_LORA ** -0.5),
        'w_ukv': normal(ks[6], (L, MLA_KV_LORA, H_MLA * (MLA_NOPE + MLA_V)), MLA_KV_LORA ** -0.5),
        'g_head': 1.0 + normal(ks[7], (L, N_HEADS_OUT, HEAD_DIM), 0.02),
        'w_out': normal(ks[8], (L, D_MIX, D_MODEL), beta * D_MIX ** -0.5),
        'ln1_g': 1.0 + normal(ks[9], (L, D_MODEL), 0.02),
        'ln1_b': normal(ks[10], (L, D_MODEL), 0.02),
        'w_group': normal(ks[11], (L, D_MODEL, N_GROUPS), D_MODEL ** -0.5),
        'b_group': normal(ks[12], (L, N_GROUPS), 0.01),
        'w_expert': normal(ks[13], (L, D_MODEL, N_EXPERTS), D_MODEL ** -0.5),
        'b_expert': normal(ks[14], (L, N_EXPERTS), 0.01),
        'w1': normal(ks[15], (L, N_EXPERTS, D_MODEL, D_EXPERT), D_MODEL ** -0.5),
        'w3': normal(ks[16], (L, N_EXPERTS, D_MODEL, D_EXPERT), D_MODEL ** -0.5),
        'w2': normal(ks[17], (L, N_EXPERTS, D_EXPERT, D_MODEL), beta * D_EXPERT ** -0.5),
        'ln2_g': 1.0 + normal(ks[18], (L, D_MODEL), 0.02),
        'ln2_b': normal(ks[19], (L, D_MODEL), 0.02),
    }


def reference(x, w_in, b_forget, g_cq, g_ckv, w_uq, w_ukv, g_head, w_out, ln1_g, ln1_b,
              w_group, b_group, w_expert, b_expert, w1, w3, w2, ln2_g, ln2_b):
    alpha = (2.0 * DEPTH) ** 0.25
    bsz, seq, d = x.shape
    for l in range(DEPTH):
        h = _mixing(x, w_in[l], b_forget[l], g_cq[l], g_ckv[l], w_uq[l], w_ukv[l], g_head[l], w_out[l])
        x = _layernorm(alpha * x + h, ln1_g[l], ln1_b[l])
        m = _hier_moe(x.reshape(bsz * seq, d), w_group[l], b_group[l], w_expert[l], b_expert[l],
                      w1[l], w3[l], w2[l]).reshape(bsz, seq, d)
        x = _layernorm(alpha * x + m, ln2_g[l], ln2_b[l])
    return x
```

```python
import functools

import jax
import jax.numpy as jnp
import numpy as np
from jax import lax
from jax.experimental import pallas as pl
from jax.experimental.pallas import tpu as pltpu

F32 = jnp.float32
BF16 = jnp.bfloat16

HEAD_DIM = 64
N_HEADS_PER_MIXER = 4
MLA_Q_LORA = 256
MLA_KV_LORA = 128
MLA_NOPE = 64
MLA_ROPE = 32
DIL_BRANCHES = ((128, 1), (512, 4), (2048, 16))
ROPE_THETA = 10000.0
N_GROUPS = 4
EXPERTS_PER_GROUP = 4
N_EXPERTS = N_GROUPS * EXPERTS_PER_GROUP
TOP_K = 2
MOE_BLOCK = 256
LN_EPS = 1e-5
RMS_EPS = 1e-6

LANES = 128
VMEM_LIMIT_BYTES = 56 * 1024 * 1024

N_SB = 3 * N_HEADS_PER_MIXER * HEAD_DIM
N_FOX_QKV = 3 * N_HEADS_PER_MIXER * HEAD_DIM
N_MLA = MLA_Q_LORA + MLA_KV_LORA + MLA_ROPE
N_BRANCH = 3 * N_HEADS_PER_MIXER * HEAD_DIM
MISC_W = 512
F_COL = N_MLA


def _cparams(sem):
    return pltpu.CompilerParams(dimension_semantics=sem, vmem_limit_bytes=VMEM_LIMIT_BYTES)


def _split3(a):
    hi = a.astype(BF16)
    r1 = a - hi.astype(F32)
    mid = r1.astype(BF16)
    lo = (r1 - mid.astype(F32)).astype(BF16)
    return hi, mid, lo


def _dot(a, b):
    return jnp.dot(a, b, preferred_element_type=F32)


def _dot_nt(a, b):
    return lax.dot_general(a, b, (((1,), (1,)), ((), ())), preferred_element_type=F32)


def _dot_exact_rhs(a, u):
    hi, mid, lo = _split3(a)
    return _dot(hi, u) + _dot(mid, u) + _dot(lo, u)


def _dot_f32(a, b):
    ah, am, al = _split3(a)
    bh, bm, bl = _split3(b)
    return (_dot(ah, bh) + (_dot(ah, bm) + _dot(am, bh))
            + (_dot(am, bm) + _dot(ah, bl) + _dot(al, bh)))


def _lane_iota(shape):
    return lax.broadcasted_iota(jnp.int32, shape, len(shape) - 1)


def _rotate_half(y, half):
    lane = _lane_iota(y.shape)
    fwd = pltpu.roll(y, half, 1)
    bwd = pltpu.roll(y, LANES - half, 1)
    return jnp.where((lane % (2 * half)) < half, -bwd, fwd)


def _log_sigmoid_pair(z):
    sp = jnp.log(1.0 + jnp.exp(-jnp.abs(z)))
    return jnp.minimum(z, 0.0) - sp, -jnp.maximum(z, 0.0) - sp


def _head_rms(o, g):
    lane = _lane_iota(o.shape)
    first = lane < HEAD_DIM
    sq = o * o
    ss_a = jnp.sum(jnp.where(first, sq, 0.0), axis=-1, keepdims=True)
    ss_b = jnp.sum(jnp.where(first, 0.0, sq), axis=-1, keepdims=True)
    ms = jnp.where(first, ss_a, ss_b) * (1.0 / HEAD_DIM)
    return o * lax.rsqrt(ms + RMS_EPS) * g


def _layernorm(y, g, b):
    mu = jnp.mean(y, axis=-1, keepdims=True)
    d = y - mu
    var = jnp.mean(d * d, axis=-1, keepdims=True)
    return d * lax.rsqrt(var + LN_EPS) * g + b


def _proj_kernel(x_ref, w_ref, cos_ref, sin_ref, *out_refs, outs):
    xb = x_ref[0].astype(BF16)
    col = 0
    for o_ref, (width, n_rope, _) in zip(out_refs, outs):
        for c in range(0, width, 2 * LANES):
            cw = min(2 * LANES, width - c)
            y = _dot(xb, w_ref[:, col + c:col + c + cw])
            for s in range(0, cw, LANES):
                ys = y[:, s:s + LANES]
                if c + s < n_rope:
                    ys = ys * cos_ref[...] + _rotate_half(ys, HEAD_DIM // 2) * sin_ref[...]
                o_ref[0, 0, :, c + s:c + s + LANES] = ys.astype(o_ref.dtype)
        col += width


def _proj_call(x, w, cos, sin, outs, r, tm):
    bsz, seq, d = x.shape
    n = seq // r
    xv = x.reshape(bsz, n, r * d)
    cosv = cos.reshape(n, r * LANES)
    sinv = sin.reshape(n, r * LANES)
    grid = (bsz, r, n // tm)
    in_specs = [
        pl.BlockSpec((1, tm, d), lambda b, p, i: (b, i, p)),
        pl.BlockSpec(w.shape, lambda b, p, i: (0, 0)),
        pl.BlockSpec((tm, LANES), lambda b, p, i: (i, p)),
        pl.BlockSpec((tm, LANES), lambda b, p, i: (i, p)),
    ]
    out_specs = [pl.BlockSpec((1, 1, tm, wd), lambda b, p, i: (b, p, i, 0)) for wd, _, _ in outs]
    out_shape = [jax.ShapeDtypeStruct((bsz, r, n, wd), dt) for wd, _, dt in outs]
    return pl.pallas_call(
        functools.partial(_proj_kernel, outs=outs),
        grid=grid, in_specs=in_specs, out_specs=out_specs, out_shape=out_shape,
        compiler_params=_cparams(("parallel", "parallel", "parallel")),
        name=f"proj_r{r}",
    )(xv, w, cosv, sinv)


def _fox_c_kernel(misc_ref, bias_ref, uinc_ref, ones_ref, out_ref, *, seq):
    lane0 = F_COL - 3 * LANES
    nblk = seq // LANES

    def body(j, carry):
        r0 = pl.multiple_of(j * LANES, LANES)
        f = misc_ref[pl.ds(r0, LANES), :] + bias_ref[...]
        lf, _ = _log_sigmoid_pair(f)
        lft = lf.T
        csum = _dot_exact_rhs(lft, uinc_ref[...]) + carry
        tot = _dot_exact_rhs(lft, ones_ref[...])
        out_ref[0, :, pl.ds(r0, LANES)] = -csum[lane0:lane0 + N_HEADS_PER_MIXER, :]
        return carry + tot

    lax.fori_loop(0, nblk, body, jnp.zeros((LANES, LANES), F32))


def _fox_c_call(misc, b_forget, bsz, seq):
    bias = jnp.zeros((1, LANES), F32).at[0, F_COL - 3 * LANES:F_COL - 3 * LANES + N_HEADS_PER_MIXER].set(b_forget)
    idx = jnp.arange(LANES)
    uinc = (idx[:, None] <= idx[None, :]).astype(BF16)
    ones = jnp.ones((LANES, LANES), BF16)
    return pl.pallas_call(
        functools.partial(_fox_c_kernel, seq=seq),
        grid=(bsz,),
        in_specs=[
            pl.BlockSpec((seq, LANES), lambda b: (b, 3)),
            pl.BlockSpec((1, LANES), lambda b: (0, 0)),
            pl.BlockSpec((LANES, LANES), lambda b: (0, 0)),
            pl.BlockSpec((LANES, LANES), lambda b: (0, 0)),
        ],
        out_specs=pl.BlockSpec((1, N_HEADS_PER_MIXER, seq), lambda b: (b, 0, 0)),
        out_shape=jax.ShapeDtypeStruct((bsz, N_HEADS_PER_MIXER, seq), F32),
        compiler_params=_cparams(("parallel",)),
        name="fox_c",
    )(misc, bias, uinc, ones)


def _mla_prep_kernel(misc_ref, gq_ref, gkv_ref, wq_ref, wk_ref, wv_ref, cos_ref, sin_ref,
                     q_ref, k_ref, v_ref):
    def rms(x, g):
        return x * lax.rsqrt(jnp.mean(x * x, axis=-1, keepdims=True) + RMS_EPS) * g

    cq = rms(misc_ref[:, 0:MLA_Q_LORA], gq_ref[...]).astype(BF16)
    ckv = rms(misc_ref[:, MLA_Q_LORA:MLA_Q_LORA + MLA_KV_LORA], gkv_ref[...]).astype(BF16)
    kr_blk = misc_ref[:, 3 * LANES:4 * LANES]
    lane = _lane_iota(kr_blk.shape)
    in_rope = (lane >= MLA_NOPE) & (lane < MLA_NOPE + MLA_ROPE)
    kr = jnp.where(in_rope, pltpu.roll(kr_blk, MLA_NOPE, 1), 0.0)
    cos = cos_ref[...]
    sin = sin_ref[...]

    def rope(y):
        return y * cos + _rotate_half(y, MLA_ROPE // 2) * sin

    q = _dot(cq, wq_ref[...])
    k = _dot(ckv, wk_ref[...])
    for h in range(N_HEADS_PER_MIXER):
        sl = slice(h * LANES, (h + 1) * LANES)
        q_ref[:, sl] = rope(q[:, sl]).astype(BF16)
        k_ref[:, sl] = rope(k[:, sl] + kr).astype(BF16)
    v_ref[...] = _dot(ckv, wv_ref[...]).astype(BF16)


def _mla_prep_call(misc, g_cq, g_ckv, wq, wk, wv, cos, sin, seq, tm):
    t = misc.shape[0]
    nper = seq // tm
    hw = N_HEADS_PER_MIXER * LANES
    vw = N_HEADS_PER_MIXER * HEAD_DIM
    full = lambda a: pl.BlockSpec(a.shape, lambda i: (0, 0))
    return pl.pallas_call(
        _mla_prep_kernel,
        grid=(t // tm,),
        in_specs=[
            pl.BlockSpec((tm, MISC_W), lambda i: (i, 0)),
            full(g_cq), full(g_ckv), full(wq), full(wk), full(wv),
            pl.BlockSpec((tm, LANES), lambda i: (i % nper, 0)),
            pl.BlockSpec((tm, LANES), lambda i: (i % nper, 0)),
        ],
        out_specs=[
            pl.BlockSpec((tm, hw), lambda i: (i, 0)),
            pl.BlockSpec((tm, hw), lambda i: (i, 0)),
            pl.BlockSpec((tm, vw), lambda i: (i, 0)),
        ],
        out_shape=[
            jax.ShapeDtypeStruct((t, hw), BF16),
            jax.ShapeDtypeStruct((t, hw), BF16),
            jax.ShapeDtypeStruct((t, vw), BF16),
        ],
        compiler_params=_cparams(("parallel",)),
        name="mla_prep",
    )(misc, g_cq, g_ckv, wq, wk, wv, cos, sin)


def _attn_kernel(*refs, mode, tq, scale):
    if mode == "sb":
        q_ref, k_ref, v_ref, g_ref, u_ref, o_ref = refs
    elif mode == "fox":
        q_ref, k_ref, v_ref, g_ref, nca_ref, ncb_ref, o_ref = refs
    else:
        q_ref, k_ref, v_ref, g_ref, o_ref = refs
    tk = tq
    i = pl.program_id(2)
    lane = _lane_iota((tq, LANES))
    first = lane < HEAD_DIM
    row = lax.broadcasted_iota(jnp.int32, (tq, tk), 0)
    colm = lax.broadcasted_iota(jnp.int32, (tq, tk), 1)

    if mode == "mla":
        q_heads = (q_ref[:, 0:LANES], q_ref[:, LANES:2 * LANES])
    else:
        q2 = q_ref[...]
        zero = jnp.zeros_like(q2)
        q_heads = (jnp.where(first, q2, zero), jnp.where(first, zero, q2))

    def k_head(j, hh):
        r0 = pl.multiple_of(j * tk, tk)
        if mode == "mla":
            return k_ref[pl.ds(r0, tk), hh * LANES:(hh + 1) * LANES]
        return k_ref[pl.ds(r0, tk), :]

    def v_blk(j):
        r0 = pl.multiple_of(j * tk, tk)
        return v_ref[pl.ds(r0, tk), :]

    if mode == "sb":
        def step(j, carry, diag):
            acc, ra, rb = carry
            v2 = v_blk(j)
            outs = []
            new_r = []
            for hh, rr in ((0, ra), (1, rb)):
                z = _dot_nt(q_heads[hh], k_head(j, hh))
                ls_pos, ls_neg = _log_sigmoid_pair(z)
                if diag:
                    before = colm < row
                    ls_neg = jnp.where(before, ls_neg, 0.0)
                hi = ls_neg.astype(BF16)
                lo = (ls_neg - hi.astype(F32)).astype(BF16)
                cext = _dot(hi, u_ref[...]) + _dot(lo, u_ref[...])
                logw = ls_pos + cext[:, 0:tk] + rr[:, 0:1]
                w = jnp.exp(logw)
                if diag:
                    w = jnp.where(before, w, 0.0)
                outs.append(_dot(w.astype(BF16), v2))
                new_r.append(rr + cext[:, tk:tk + LANES])
            acc = acc + jnp.where(first, outs[0], outs[1])
            return acc, new_r[0], new_r[1]

        zeros = jnp.zeros((tq, LANES), F32)
        carry = step(i, (zeros, zeros, zeros), True)
        carry = lax.fori_loop(0, i, lambda n, c: step(i - 1 - n, c, False), carry)
        o = carry[0]
    else:
        def step(j, carry, diag):
            acc, ma, la, mb, lb = carry
            v2 = v_blk(j)
            pv = []
            alphas = []
            stats = []
            for hh, (m_prev, l_prev) in ((0, (ma, la)), (1, (mb, lb))):
                s = _dot_nt(q_heads[hh], k_head(j, hh))
                if mode == "mla":
                    s = s * scale
                else:
                    nc_ref = nca_ref if hh == 0 else ncb_ref
                    r0 = pl.multiple_of(j * tk, tk)
                    s = s + nc_ref[0, :, pl.ds(r0, tk)]
                if diag:
                    s = jnp.where(colm <= row, s, -jnp.inf)
                m_new = jnp.maximum(m_prev, jnp.max(s, axis=-1, keepdims=True))
                alpha = jnp.exp(m_prev - m_new)
                p = jnp.exp(s - m_new)
                l_new = alpha * l_prev + jnp.sum(p, axis=-1, keepdims=True)
                pv.append(_dot(p.astype(BF16), v2))
                alphas.append(alpha)
                stats.append((m_new, l_new))
            acc = acc * jnp.where(first, alphas[0], alphas[1]) + jnp.where(first, pv[0], pv[1])
            return acc, stats[0][0], stats[0][1], stats[1][0], stats[1][1]

        neg = jnp.full((tq, 1), -jnp.inf, F32)
        zl = jnp.zeros((tq, 1), F32)
        carry = step(i, (jnp.zeros((tq, LANES), F32), neg, zl, neg, zl), True)
        carry = lax.fori_loop(0, i, lambda n, c: step(n, c, False), carry)
        acc, _, la, _, lb = carry
        o = acc / jnp.where(first, la, lb)
    o_ref[...] = _head_rms(o, g_ref[...]).astype(o_ref.dtype)


def _attn_call(mode, q, k, v, qcol, kcol, vcol, g_flat, gcol, bsz, seq, tq, extra=(), scale=1.0):
    t = bsz * seq
    nq = seq // tq
    qw = 2 * LANES if mode == "mla" else LANES
    qmul = 2 if mode == "mla" else 1
    in_specs = [
        pl.BlockSpec((tq, qw), lambda b, p, i: (b * nq + i, qcol // qmul + p)),
        pl.BlockSpec((seq, qw), lambda b, p, i: (b, kcol // qmul + p)),
        pl.BlockSpec((seq, LANES), lambda b, p, i: (b, vcol + p)),
        pl.BlockSpec((1, LANES), lambda b, p, i: (0, gcol + p)),
    ]
    args = [q, k, v, g_flat]
    if mode == "sb":
        (u,) = extra
        in_specs.append(pl.BlockSpec(u.shape, lambda b, p, i: (0, 0)))
        args.append(u)
    elif mode == "fox":
        (nc,) = extra
        in_specs.append(pl.BlockSpec((1, 1, seq), lambda b, p, i: (b * N_HEADS_PER_MIXER + 2 * p, 0, 0)))
        in_specs.append(pl.BlockSpec((1, 1, seq), lambda b, p, i: (b * N_HEADS_PER_MIXER + 2 * p + 1, 0, 0)))
        args += [nc, nc]
    return pl.pallas_call(
        functools.partial(_attn_kernel, mode=mode, tq=tq, scale=scale),
        grid=(bsz, N_HEADS_PER_MIXER // 2, nq),
        in_specs=in_specs,
        out_specs=pl.BlockSpec((tq, LANES), lambda b, p, i: (b * nq + i, p)),
        out_shape=jax.ShapeDtypeStruct((t, 2 * LANES), BF16),
        compiler_params=_cparams(("parallel", "parallel", "arbitrary")),
        name=f"attn_{mode}",
    )(*args)


def _dil_kernel(qkv_ref, o_ref, *, tqs, win):
    tb = win
    c = pl.program_id(2)
    lane = _lane_iota((tb, LANES))
    first = lane < HEAD_DIM
    row = lax.broadcasted_iota(jnp.int32, (tb, tb), 0)
    colm = lax.broadcasted_iota(jnp.int32, (tb, tb), 1)
    hq = N_HEADS_PER_MIXER * HEAD_DIM

    def tile(it, _):
        gi = c * (tqs // tb) + it
        r0 = pl.multiple_of(gi * tb, tb)
        rp = pl.multiple_of(jnp.maximum(gi - 1, 0) * tb, tb)
        no_prev = jnp.where(gi > 0, 0, tb)
        o0 = pl.multiple_of(it * tb, tb)
        for p in range(N_HEADS_PER_MIXER // 2):
            q2 = qkv_ref[0, 0, pl.ds(r0, tb), p * LANES:(p + 1) * LANES]
            kd = qkv_ref[0, 0, pl.ds(r0, tb), hq + p * LANES:hq + (p + 1) * LANES]
            kp = qkv_ref[0, 0, pl.ds(rp, tb), hq + p * LANES:hq + (p + 1) * LANES]
            vd = qkv_ref[0, 0, pl.ds(r0, tb), 2 * hq + p * LANES:2 * hq + (p + 1) * LANES]
            vp = qkv_ref[0, 0, pl.ds(rp, tb), 2 * hq + p * LANES:2 * hq + (p + 1) * LANES]
            zero = jnp.zeros_like(q2)
            outs = []
            lses = []
            for hh in range(2):
                qh = jnp.where(first, q2, zero) if hh == 0 else jnp.where(first, zero, q2)
                sd = jnp.where(colm <= row, _dot_nt(qh, kd), -jnp.inf)
                sp = jnp.where(colm >= row + no_prev, _dot_nt(qh, kp), -jnp.inf)
                m = jnp.maximum(jnp.max(sd, axis=-1, keepdims=True), jnp.max(sp, axis=-1, keepdims=True))
                pd = jnp.exp(sd - m)
                pp = jnp.exp(sp - m)
                l = jnp.sum(pd, axis=-1, keepdims=True) + jnp.sum(pp, axis=-1, keepdims=True)
                inv = 1.0 / l
                acc = _dot((pd * inv).astype(BF16), vd) + _dot((pp * inv).astype(BF16), vp)
                outs.append(acc)
                lses.append(m + jnp.log(l))
            o_ref[0, 0, pl.ds(o0, tb), p * LANES:(p + 1) * LANES] = jnp.where(first, outs[0], outs[1])
            o_ref[0, 0, pl.ds(o0, tb), hq + p * LANES:hq + (p + 1) * LANES] = jnp.where(first, lses[0], lses[1])
        return 0

    lax.fori_loop(0, tqs // tb, tile, 0)


def _dil_call(qkv, tqs, win):
    bsz, r, n, w = qkv.shape
    hq = N_HEADS_PER_MIXER * HEAD_DIM
    return pl.pallas_call(
        functools.partial(_dil_kernel, tqs=tqs, win=win),
        grid=(bsz, r, n // tqs),
        in_specs=[pl.BlockSpec((1, 1, n, w), lambda b, p, c: (b, p, 0, 0))],
        out_specs=pl.BlockSpec((1, 1, tqs, 2 * hq), lambda b, p, c: (b, p, c, 0)),
        out_shape=jax.ShapeDtypeStruct((bsz, r, n, 2 * hq), F32),
        compiler_params=_cparams(("parallel", "parallel", "arbitrary")),
        name=f"dil_r{r}",
    )(qkv)


def _post_kernel(x_ref, osb_ref, ofox_ref, omla_ref, d1_ref, d2_ref, d3_ref, gd_ref, wo_ref,
                 lng_ref, lnb_ref, wr_ref, br_ref, x1_ref, route_ref, *, alpha):
    hq = N_HEADS_PER_MIXER * HEAD_DIM
    h = _dot(osb_ref[...], wo_ref[0:hq, :])
    h += _dot(ofox_ref[...], wo_ref[hq:2 * hq, :])
    h += _dot(omla_ref[...], wo_ref[2 * hq:3 * hq, :])
    for p in range(N_HEADS_PER_MIXER // 2):
        sl = slice(p * LANES, (p + 1) * LANES)
        ll = slice(hq + p * LANES, hq + (p + 1) * LANES)
        l1, l2, l3 = d1_ref[:, ll], d2_ref[:, ll], d3_ref[:, ll]
        m = jnp.maximum(jnp.maximum(l1, l2), l3)
        e1, e2, e3 = jnp.exp(l1 - m), jnp.exp(l2 - m), jnp.exp(l3 - m)
        inv = 1.0 / (e1 + e2 + e3)
        od = (e1 * inv) * d1_ref[:, sl] + (e2 * inv) * d2_ref[:, sl] + (e3 * inv) * d3_ref[:, sl]
        od = _head_rms(od, gd_ref[:, sl]).astype(BF16)
        h += _dot(od, wo_ref[3 * hq + p * LANES:3 * hq + (p + 1) * LANES, :])
    x1 = _layernorm(alpha * x_ref[...] + h, lng_ref[...], lnb_ref[...])
    x1_ref[...] = x1

    logits = _dot_f32(x1, wr_ref[...]) + br_ref[...]
    lane = _lane_iota(logits.shape)
    lanef = lane.astype(F32)
    big = float(LANES)
    ninf = -jnp.inf
    gl = jnp.where(lane < N_GROUPS, logits, ninf)
    gmax = jnp.max(gl, axis=-1, keepdims=True)
    gsel = jnp.min(jnp.where(gl == gmax, lanef, big), axis=-1, keepdims=True)
    gw = 1.0 / jnp.sum(jnp.exp(gl - gmax), axis=-1, keepdims=True)
    e_lo = N_GROUPS + EXPERTS_PER_GROUP * gsel
    in_grp = (lanef >= e_lo) & (lanef < e_lo + EXPERTS_PER_GROUP)
    el = jnp.where(in_grp, logits, ninf)
    t1 = jnp.max(el, axis=-1, keepdims=True)
    i1 = jnp.min(jnp.where(el == t1, lanef, big), axis=-1, keepdims=True)
    el2 = jnp.where(lanef == i1, ninf, el)
    t2 = jnp.max(el2, axis=-1, keepdims=True)
    i2 = jnp.min(jnp.where(el2 == t2, lanef, big), axis=-1, keepdims=True)
    ex = jnp.exp(t2 - t1)
    den = 1.0 + ex
    g1 = gw / den
    g2 = gw * ex / den
    out = jnp.where(lane == 0, i1 - N_GROUPS,
                    jnp.where(lane == 1, i2 - N_GROUPS,
                              jnp.where(lane == 2, g1, jnp.where(lane == 3, g2, 0.0))))
    route_ref[...] = out


def _post_call(x, osb, ofox, omla, d1, d2, d3, g_dil, wo, lng, lnb, wr, br, alpha, tm):
    t, d = x.shape
    hq = N_HEADS_PER_MIXER * HEAD_DIM
    row = lambda w: pl.BlockSpec((tm, w), lambda i: (i, 0))
    full = lambda a: pl.BlockSpec(a.shape, lambda i: (0, 0))
    return pl.pallas_call(
        functools.partial(_post_kernel, alpha=alpha),
        grid=(t // tm,),
        in_specs=[row(d), row(hq), row(hq), row(hq), row(2 * hq), row(2 * hq), row(2 * hq),
                  full(g_dil), full(wo), full(lng), full(lnb), full(wr), full(br)],
        out_specs=[row(d), row(LANES)],
        out_shape=[jax.ShapeDtypeStruct((t, d), F32), jax.ShapeDtypeStruct((t, LANES), F32)],
        compiler_params=_cparams(("parallel",)),
        name="post_mixer",
    )(x, osb, ofox, omla, d1, d2, d3, g_dil, wo, lng, lnb, wr, br)


def _moe_kernel(ce_ref, nv_ref, src_ref, dst_ref, x_hbm, w1_ref, w3_ref, w2_ref, y_hbm,
                xs_ref, ys_ref, gsem, ssem):
    c = pl.program_id(0)
    nv = nv_ref[c]

    def gather_copy(i):
        return pltpu.make_async_copy(x_hbm.at[pl.ds(src_ref[0, 0, i], 1), :],
                                     xs_ref.at[pl.ds(i, 1), :], gsem)

    def scatter_copy(i):
        return pltpu.make_async_copy(ys_ref.at[pl.ds(i, 1), :],
                                     y_hbm.at[pl.ds(dst_ref[0, 0, i], 1), :], ssem)

    @pl.when(c == 0)
    def _():
        xs_ref[...] = jnp.zeros_like(xs_ref)

    @pl.when(nv > 0)
    def _():
        def start_g(i, _):
            gather_copy(i).start()
            return 0

        def wait_g(i, _):
            gather_copy(i).wait()
            return 0

        lax.fori_loop(0, nv, start_g, 0)
        lax.fori_loop(0, nv, wait_g, 0)
        xb = xs_ref[...].astype(BF16)
        a = _dot(xb, w1_ref[0])
        b = _dot(xb, w3_ref[0])
        hid = (a / (1.0 + jnp.exp(-a)) * b).astype(BF16)
        ys_ref[...] = _dot(hid, w2_ref[0])

        def start_s(i, _):
            scatter_copy(i).start()
            return 0

        def wait_s(i, _):
            scatter_copy(i).wait()
            return 0

        lax.fori_loop(0, nv, start_s, 0)
        lax.fori_loop(0, nv, wait_s, 0)


def _moe_call(chunk_expert, n_valid, src, dst, x1, w1, w3, w2, n_rows_out):
    n_chunks = chunk_expert.shape[0]
    t, d = x1.shape
    de = w1.shape[-1]
    grid_spec = pltpu.PrefetchScalarGridSpec(
        num_scalar_prefetch=2,
        grid=(n_chunks,),
        in_specs=[
            pl.BlockSpec((1, 1, MOE_BLOCK), lambda c, ce, na: (c, 0, 0), memory_space=pltpu.SMEM),
            pl.BlockSpec((1, 1, MOE_BLOCK), lambda c, ce, na: (c, 0, 0), memory_space=pltpu.SMEM),
            pl.BlockSpec(memory_space=pl.ANY),
            pl.BlockSpec((1, d, de), lambda c, ce, na: (ce[c], 0, 0)),
            pl.BlockSpec((1, d, de), lambda c, ce, na: (ce[c], 0, 0)),
            pl.BlockSpec((1, de, d), lambda c, ce, na: (ce[c], 0, 0)),
        ],
        out_specs=pl.BlockSpec(memory_space=pl.ANY),
        scratch_shapes=[
            pltpu.VMEM((MOE_BLOCK, d), F32),
            pltpu.VMEM((MOE_BLOCK, d), F32),
            pltpu.SemaphoreType.DMA(()),
            pltpu.SemaphoreType.DMA(()),
        ],
    )
    return pl.pallas_call(
        _moe_kernel,
        grid_spec=grid_spec,
        out_shape=jax.ShapeDtypeStruct((n_rows_out, d), F32),
        compiler_params=_cparams(("arbitrary",)),
        name="moe_experts",
    )(chunk_expert, n_valid, src, dst, x1, w1, w3, w2)


def _combine_kernel(x1_ref, y_ref, route_ref, lng_ref, lnb_ref, o_ref, *, alpha):
    d = x1_ref.shape[-1]
    g1 = route_ref[:, 2:3]
    g2 = route_ref[:, 3:4]
    m = g1 * y_ref[:, 0:d] + g2 * y_ref[:, d:2 * d]
    o_ref[...] = _layernorm(alpha * x1_ref[...] + m, lng_ref[...], lnb_ref[...])


def _combine_call(x1, y2, route, lng, lnb, alpha, tm):
    t, d = x1.shape
    full = lambda a: pl.BlockSpec(a.shape, lambda i: (0, 0))
    return pl.pallas_call(
        functools.partial(_combine_kernel, alpha=alpha),
        grid=(t // tm,),
        in_specs=[pl.BlockSpec((tm, d), lambda i: (i, 0)),
                  pl.BlockSpec((tm, 2 * d), lambda i: (i, 0)),
                  pl.BlockSpec((tm, LANES), lambda i: (i, 0)),
                  full(lng), full(lnb)],
        out_specs=pl.BlockSpec((tm, d), lambda i: (i, 0)),
        out_shape=jax.ShapeDtypeStruct((t, d), F32),
        compiler_params=_cparams(("parallel",)),
        name="moe_combine",
    )(x1, y2, route, lng, lnb)


def _rope_tables(seq, dim, lane_lo):
    half = dim // 2
    inv_freq = ROPE_THETA ** (-jnp.arange(half, dtype=F32) / half)
    ang = jnp.arange(seq, dtype=F32)[:, None] * inv_freq[None, :]
    cos = jnp.concatenate([jnp.cos(ang), jnp.cos(ang)], -1)
    sin = jnp.concatenate([jnp.sin(ang), jnp.sin(ang)], -1)
    if lane_lo == 0:
        reps = LANES // dim
        return jnp.tile(cos, (1, reps)), jnp.tile(sin, (1, reps))
    cos_t = jnp.ones((seq, LANES), F32).at[:, lane_lo:lane_lo + dim].set(cos)
    sin_t = jnp.zeros((seq, LANES), F32).at[:, lane_lo:lane_lo + dim].set(sin)
    return cos_t, sin_t


def _dispatch_tables(route, n_tok):
    expert_id = route[:, 0:TOP_K].astype(jnp.int32).reshape(-1)
    n_assign = n_tok * TOP_K
    n_slots = n_assign + N_EXPERTS * MOE_BLOCK
    n_chunks = n_slots // MOE_BLOCK
    onehot = (expert_id[:, None] == jnp.arange(N_EXPERTS, dtype=jnp.int32)[None, :]).astype(jnp.int32)
    ranks = jnp.cumsum(onehot, axis=0) - onehot
    rank = jnp.sum(ranks * onehot, axis=1)
    counts = jnp.sum(onehot, axis=0)
    padded = (counts + MOE_BLOCK - 1) // MOE_BLOCK * MOE_BLOCK
    pad_end = jnp.cumsum(padded)
    pad_start = pad_end - padded
    dest = pad_start[expert_id] + rank
    assign = jnp.arange(n_assign, dtype=jnp.int32)
    src = jnp.zeros((n_slots,), jnp.int32).at[dest].set(assign // TOP_K)
    dst = jnp.zeros((n_slots,), jnp.int32).at[dest].set(assign)
    chunk_start = jnp.arange(n_chunks, dtype=jnp.int32) * MOE_BLOCK
    chunk_expert = jnp.minimum(jnp.searchsorted(pad_end, chunk_start, side="right"),
                               N_EXPERTS - 1).astype(jnp.int32)
    n_valid = jnp.clip(pad_start[chunk_expert] + counts[chunk_expert] - chunk_start,
                       0, MOE_BLOCK).astype(jnp.int32)
    return (chunk_expert, n_valid, src.reshape(n_chunks, 1, MOE_BLOCK),
            dst.reshape(n_chunks, 1, MOE_BLOCK), n_assign)


def _pick_tile(n, pref):
    t = pref
    while n % t:
        t //= 2
    return t


def kernel(x, w_in, b_forget, g_cq, g_ckv, w_uq, w_ukv, g_head, w_out, ln1_g, ln1_b,
           w_group, b_group, w_expert, b_expert, w1, w3, w2, ln2_g, ln2_b):
    bsz, seq, d = x.shape
    depth = w_in.shape[0]
    t = bsz * seq
    alpha = (2.0 * depth) ** 0.25
    hq = N_HEADS_PER_MIXER * HEAD_DIM
    qk_scale = HEAD_DIM ** -0.5
    mla_scale = (MLA_NOPE + MLA_ROPE) ** -0.5
    win = DIL_BRANCHES[0][0]
    assert all(w // r == win for w, r in DIL_BRANCHES)
    assert seq % (DIL_BRANCHES[-1][1] * win) == 0 and d % LANES == 0

    cos64, sin64 = _rope_tables(seq, HEAD_DIM, 0)
    cos_m, sin_m = _rope_tables(seq, MLA_ROPE, MLA_NOPE)
    tq = _pick_tile(seq, 256)
    idx = jnp.arange(tq)
    u_sb = jnp.concatenate([(idx[:, None] > idx[None, :]).astype(BF16), jnp.ones((tq, LANES), BF16)], axis=1)

    for l in range(depth):
        wl = w_in[l]
        o_fox, o_mla, o_dil = N_SB, N_SB + N_FOX_QKV + N_HEADS_PER_MIXER, N_SB + N_FOX_QKV + N_HEADS_PER_MIXER + N_MLA
        qs = lambda w: w.at[:, 0:hq].multiply(qk_scale)
        w_sb = qs(wl[:, 0:N_SB])
        w_fx = qs(wl[:, o_fox:o_fox + N_FOX_QKV])
        w_f = wl[:, o_fox + N_FOX_QKV:o_mla]
        w_ml = wl[:, o_mla:o_dil]
        wd = wl[:, o_dil:].reshape(d, 3, len(DIL_BRANCHES), hq)
        w_br = [qs(jnp.concatenate([wd[:, 0, g], wd[:, 1, g], wd[:, 2, g]], axis=1)) for g in range(len(DIL_BRANCHES))]
        w_misc = jnp.concatenate([w_ml, w_f, jnp.zeros((d, MISC_W - N_MLA - N_HEADS_PER_MIXER), F32)], axis=1)
        w_tok = jnp.concatenate([w_sb, w_fx, w_br[0], w_misc], axis=1).astype(BF16)

        wq = jnp.pad(w_uq[l].reshape(MLA_Q_LORA, N_HEADS_PER_MIXER, MLA_NOPE + MLA_ROPE),
                     ((0, 0), (0, 0), (0, LANES - MLA_NOPE - MLA_ROPE))).reshape(MLA_Q_LORA, -1).astype(BF16)
        wkv = w_ukv[l].reshape(MLA_KV_LORA, N_HEADS_PER_MIXER, MLA_NOPE + HEAD_DIM)
        wk = jnp.pad(wkv[:, :, :MLA_NOPE], ((0, 0), (0, 0), (0, LANES - MLA_NOPE))).reshape(MLA_KV_LORA, -1).astype(BF16)
        wv = wkv[:, :, MLA_NOPE:].reshape(MLA_KV_LORA, -1).astype(BF16)
        g_flat = g_head[l].reshape(1, -1)
        wr = jnp.concatenate([w_group[l], w_expert[l],
                              jnp.zeros((d, LANES - N_GROUPS - N_EXPERTS), F32)], axis=1)
        br = jnp.concatenate([b_group[l], b_expert[l],
                              jnp.zeros((LANES - N_GROUPS - N_EXPERTS,), F32)]).reshape(1, LANES)

        tm = _pick_tile(seq, 512)
        sb, fx, d1, misc = _proj_call(
            x, w_tok, cos64, sin64,
            ((N_SB, 0, BF16), (N_FOX_QKV, 0, BF16), (N_BRANCH, 2 * hq, BF16), (MISC_W, 0, F32)), 1, tm)
        sb = sb.reshape(t, N_SB)
        fx = fx.reshape(t, N_FOX_QKV)
        misc = misc.reshape(t, MISC_W)

        o_sb = _attn_call("sb", sb, sb, sb, 0, 2, 4, g_flat, 0, bsz, seq, tq, extra=(u_sb,))
        neg_c = _fox_c_call(misc, b_forget[l], bsz, seq).reshape(bsz * N_HEADS_PER_MIXER, 1, seq)
        o_fx = _attn_call("fox", fx, fx, fx, 0, 2, 4, g_flat, 2, bsz, seq, tq, extra=(neg_c,))
        mq, mk, mv = _mla_prep_call(misc, g_cq[l].reshape(1, -1), g_ckv[l].reshape(1, -1), wq, wk, wv,
                                    cos_m, sin_m, seq, tm)
        o_ml = _attn_call("mla", mq, mk, mv, 0, 0, 0, g_flat, 4, bsz, seq, tq, scale=mla_scale)

        dil = []
        for g, (_, r) in enumerate(DIL_BRANCHES):
            n = seq // r
            if r == 1:
                qkv = d1
            else:
                (qkv,) = _proj_call(x, w_br[g].astype(BF16), cos64, sin64,
                                    ((N_BRANCH, 2 * hq, BF16),), r, _pick_tile(n, 512))
            og = _dil_call(qkv, _pick_tile(n, 1024), win)
            dil.append(og.transpose(0, 2, 1, 3).reshape(t, 2 * hq))

        x1, route = _post_call(
            x.reshape(t, d), o_sb, o_fx, o_ml, dil[0], dil[1], dil[2], g_flat[:, 3 * hq:], w_out[l].astype(BF16),
            ln1_g[l].reshape(1, d), ln1_b[l].reshape(1, d), wr, br, alpha, _pick_tile(t, 256))

        chunk_expert, n_valid, src, dst, n_rows = _dispatch_tables(route, t)
        y = _moe_call(chunk_expert, n_valid, src, dst, x1,
                      w1[l].astype(BF16), w3[l].astype(BF16), w2[l].astype(BF16), n_rows)
        x = _combine_call(x1, y.reshape(n_rows // TOP_K, TOP_K * d), route,
                          ln2_g[l].reshape(1, d), ln2_b[l].reshape(1, d), alpha, _pick_tile(t, 256)).reshape(bsz, seq, d)
    return x
```

```python
import functools

import jax
import jax.numpy as jnp
import numpy as np
from jax import lax
from jax.experimental import pallas as pl
from jax.experimental.pallas import tpu as pltpu

F32 = jnp.float32
BF16 = jnp.bfloat16

HEAD_DIM = 64
N_HEADS_PER_MIXER = 4
MLA_Q_LORA = 256
MLA_KV_LORA = 128
MLA_NOPE = 64
MLA_ROPE = 32
DIL_BRANCHES = ((128, 1), (512, 4), (2048, 16))
ROPE_THETA = 10000.0
N_GROUPS = 4
EXPERTS_PER_GROUP = 4
N_EXPERTS = N_GROUPS * EXPERTS_PER_GROUP
TOP_K = 2
MOE_BLOCK = 256
LN_EPS = 1e-5
RMS_EPS = 1e-6

LANES = 128
VMEM_LIMIT_BYTES = 56 * 1024 * 1024

N_SB = 3 * N_HEADS_PER_MIXER * HEAD_DIM
N_FOX_QKV = 3 * N_HEADS_PER_MIXER * HEAD_DIM
N_MLA = MLA_Q_LORA + MLA_KV_LORA + MLA_ROPE
N_BRANCH = 3 * N_HEADS_PER_MIXER * HEAD_DIM
MISC_W = 512
F_COL = N_MLA


def _cparams(sem):
    return pltpu.CompilerParams(dimension_semantics=sem, vmem_limit_bytes=VMEM_LIMIT_BYTES)


def _split3(a):
    hi = a.astype(BF16)
    r1 = a - hi.astype(F32)
    mid = r1.astype(BF16)
    lo = (r1 - mid.astype(F32)).astype(BF16)
    return hi, mid, lo


def _dot(a, b):
    return jnp.dot(a, b, preferred_element_type=F32)


def _dot_nt(a, b):
    return lax.dot_general(a, b, (((1,), (1,)), ((), ())), preferred_element_type=F32)


def _dot_exact_rhs(a, u):
    hi, mid, lo = _split3(a)
    return _dot(hi, u) + _dot(mid, u) + _dot(lo, u)


def _dot_f32(a, b):
    ah, am, al = _split3(a)
    bh, bm, bl = _split3(b)
    return (_dot(ah, bh) + (_dot(ah, bm) + _dot(am, bh))
            + (_dot(am, bm) + _dot(ah, bl) + _dot(al, bh)))


def _lane_iota(shape):
    return lax.broadcasted_iota(jnp.int32, shape, len(shape) - 1)


def _rotate_half(y, half):
    lane = _lane_iota(y.shape)
    fwd = pltpu.roll(y, half, 1)
    bwd = pltpu.roll(y, LANES - half, 1)
    return jnp.where((lane % (2 * half)) < half, -bwd, fwd)


def _log_sigmoid_pair(z):
    sp = jnp.log(1.0 + jnp.exp(-jnp.abs(z)))
    return jnp.minimum(z, 0.0) - sp, -jnp.maximum(z, 0.0) - sp


def _head_rms(o, g):
    lane = _lane_iota(o.shape)
    first = lane < HEAD_DIM
    sq = o * o
    ss_a = jnp.sum(jnp.where(first, sq, 0.0), axis=-1, keepdims=True)
    ss_b = jnp.sum(jnp.where(first, 0.0, sq), axis=-1, keepdims=True)
    ms = jnp.where(first, ss_a, ss_b) * (1.0 / HEAD_DIM)
    return o * lax.rsqrt(ms + RMS_EPS) * g


def _layernorm(y, g, b):
    mu = jnp.mean(y, axis=-1, keepdims=True)
    d = y - mu
    var = jnp.mean(d * d, axis=-1, keepdims=True)
    return d * lax.rsqrt(var + LN_EPS) * g + b


def _proj_kernel(x_ref, w_ref, cos_ref, sin_ref, *out_refs, outs):
    xb = x_ref[0].astype(BF16)
    col = 0
    for o_ref, (width, n_rope, _) in zip(out_refs, outs):
        for c in range(0, width, 2 * LANES):
            cw = min(2 * LANES, width - c)
            y = _dot(xb, w_ref[:, col + c:col + c + cw])
            for s in range(0, cw, LANES):
                ys = y[:, s:s + LANES]
                if c + s < n_rope:
                    ys = ys * cos_ref[...] + _rotate_half(ys, HEAD_DIM // 2) * sin_ref[...]
                o_ref[0, 0, :, c + s:c + s + LANES] = ys.astype(o_ref.dtype)
        col += width


def _proj_call(x, w, cos, sin, outs, r, tm):
    bsz, seq, d = x.shape
    n = seq // r
    xv = x.reshape(bsz, n, r * d)
    cosv = cos.reshape(n, r * LANES)
    sinv = sin.reshape(n, r * LANES)
    grid = (bsz, r, n // tm)
    in_specs = [
        pl.BlockSpec((1, tm, d), lambda b, p, i: (b, i, p)),
        pl.BlockSpec(w.shape, lambda b, p, i: (0, 0)),
        pl.BlockSpec((tm, LANES), lambda b, p, i: (i, p)),
        pl.BlockSpec((tm, LANES), lambda b, p, i: (i, p)),
    ]
    out_specs = [pl.BlockSpec((1, 1, tm, wd), lambda b, p, i: (b, p, i, 0)) for wd, _, _ in outs]
    out_shape = [jax.ShapeDtypeStruct((bsz, r, n, wd), dt) for wd, _, dt in outs]
    return pl.pallas_call(
        functools.partial(_proj_kernel, outs=outs),
        grid=grid, in_specs=in_specs, out_specs=out_specs, out_shape=out_shape,
        compiler_params=_cparams(("parallel", "parallel", "parallel")),
        name=f"proj_r{r}",
    )(xv, w, cosv, sinv)


def _fox_c_kernel(misc_ref, bias_ref, uinc_ref, ones_ref, out_ref, *, seq):
    lane0 = F_COL - 3 * LANES
    nblk = seq // LANES

    def body(j, carry):
        r0 = pl.multiple_of(j * LANES, LANES)
        f = misc_ref[pl.ds(r0, LANES), :] + bias_ref[...]
        lf, _ = _log_sigmoid_pair(f)
        lft = lf.T
        csum = _dot_exact_rhs(lft, uinc_ref[...]) + carry
        tot = _dot_exact_rhs(lft, ones_ref[...])
        out_ref[0, :, pl.ds(r0, LANES)] = -csum[lane0:lane0 + N_HEADS_PER_MIXER, :]
        return carry + tot

    lax.fori_loop(0, nblk, body, jnp.zeros((LANES, LANES), F32))


def _fox_c_call(misc, b_forget, bsz, seq):
    bias = jnp.zeros((1, LANES), F32).at[0, F_COL - 3 * LANES:F_COL - 3 * LANES + N_HEADS_PER_MIXER].set(b_forget)
    idx = jnp.arange(LANES)
    uinc = (idx[:, None] <= idx[None, :]).astype(BF16)
    ones = jnp.ones((LANES, LANES), BF16)
    return pl.pallas_call(
        functools.partial(_fox_c_kernel, seq=seq),
        grid=(bsz,),
        in_specs=[
            pl.BlockSpec((seq, LANES), lambda b: (b, 3)),
            pl.BlockSpec((1, LANES), lambda b: (0, 0)),
            pl.BlockSpec((LANES, LANES), lambda b: (0, 0)),
            pl.BlockSpec((LANES, LANES), lambda b: (0, 0)),
        ],
        out_specs=pl.BlockSpec((1, N_HEADS_PER_MIXER, seq), lambda b: (b, 0, 0)),
        out_shape=jax.ShapeDtypeStruct((bsz, N_HEADS_PER_MIXER, seq), F32),
        compiler_params=_cparams(("parallel",)),
        name="fox_c",
    )(misc, bias, uinc, ones)


def _mla_prep_kernel(misc_ref, gq_ref, gkv_ref, wq_ref, wk_ref, wv_ref, cos_ref, sin_ref,
                     q_ref, k_ref, v_ref):
    def rms(x, g):
        return x * lax.rsqrt(jnp.mean(x * x, axis=-1, keepdims=True) + RMS_EPS) * g

    cq = rms(misc_ref[:, 0:MLA_Q_LORA], gq_ref[...]).astype(BF16)
    ckv = rms(misc_ref[:, MLA_Q_LORA:MLA_Q_LORA + MLA_KV_LORA], gkv_ref[...]).astype(BF16)
    kr_blk = misc_ref[:, 3 * LANES:4 * LANES]
    lane = _lane_iota(kr_blk.shape)
    in_rope = (lane >= MLA_NOPE) & (lane < MLA_NOPE + MLA_ROPE)
    kr = jnp.where(in_rope, pltpu.roll(kr_blk, MLA_NOPE, 1), 0.0)
    cos = cos_ref[...]
    sin = sin_ref[...]

    def rope(y):
        return y * cos + _rotate_half(y, MLA_ROPE // 2) * sin

    q = _dot(cq, wq_ref[...])
    k = _dot(ckv, wk_ref[...])
    for h in range(N_HEADS_PER_MIXER):
        sl = slice(h * LANES, (h + 1) * LANES)
        q_ref[:, sl] = rope(q[:, sl]).astype(BF16)
        k_ref[:, sl] = rope(k[:, sl] + kr).astype(BF16)
    v_ref[...] = _dot(ckv, wv_ref[...]).astype(BF16)


def _mla_prep_call(misc, g_cq, g_ckv, wq, wk, wv, cos, sin, seq, tm):
    t = misc.shape[0]
    nper = seq // tm
    hw = N_HEADS_PER_MIXER * LANES
    vw = N_HEADS_PER_MIXER * HEAD_DIM
    full = lambda a: pl.BlockSpec(a.shape, lambda i: (0, 0))
    return pl.pallas_call(
        _mla_prep_kernel,
        grid=(t // tm,),
        in_specs=[
            pl.BlockSpec((tm, MISC_W), lambda i: (i, 0)),
            full(g_cq), full(g_ckv), full(wq), full(wk), full(wv),
            pl.BlockSpec((tm, LANES), lambda i: (i % nper, 0)),
            pl.BlockSpec((tm, LANES), lambda i: (i % nper, 0)),
        ],
        out_specs=[
            pl.BlockSpec((tm, hw), lambda i: (i, 0)),
            pl.BlockSpec((tm, hw), lambda i: (i, 0)),
            pl.BlockSpec((tm, vw), lambda i: (i, 0)),
        ],
        out_shape=[
            jax.ShapeDtypeStruct((t, hw), BF16),
            jax.ShapeDtypeStruct((t, hw), BF16),
            jax.ShapeDtypeStruct((t, vw), BF16),
        ],
        compiler_params=_cparams(("parallel",)),
        name="mla_prep",
    )(misc, g_cq, g_ckv, wq, wk, wv, cos, sin)


def _attn_kernel(*refs, mode, tq, scale):
    if mode == "sb":
        q_ref, k_ref, v_ref, g_ref, u_ref, o_ref = refs
    elif mode == "fox":
        q_ref, k_ref, v_ref, g_ref, nc_ref, o_ref = refs
    else:
        q_ref, k_ref, v_ref, g_ref, o_ref = refs
    tk = tq
    nh = N_HEADS_PER_MIXER
    i = pl.program_id(1)
    lane = _lane_iota((tq, LANES))
    first = lane < HEAD_DIM
    row = lax.broadcasted_iota(jnp.int32, (tq, tk), 0)
    colm = lax.broadcasted_iota(jnp.int32, (tq, tk), 1)

    q_heads = []
    for h in range(nh):
        if mode == "mla":
            q_heads.append(q_ref[:, h * LANES:(h + 1) * LANES])
        else:
            q2 = q_ref[:, (h // 2) * LANES:(h // 2 + 1) * LANES]
            zero = jnp.zeros_like(q2)
            q_heads.append(jnp.where(first, q2, zero) if h % 2 == 0 else jnp.where(first, zero, q2))

    def k_head(j, h):
        r0 = pl.multiple_of(j * tk, tk)
        kb = h if mode == "mla" else h // 2
        return k_ref[pl.ds(r0, tk), kb * LANES:(kb + 1) * LANES]

    def v_pair(j, p):
        r0 = pl.multiple_of(j * tk, tk)
        return v_ref[pl.ds(r0, tk), p * LANES:(p + 1) * LANES]

    if mode == "sb":
        def step(j, carry, diag):
            accs, rs = carry
            new_accs, new_rs = [], []
            for p in range(nh // 2):
                v2 = v_pair(j, p)
                outs = []
                for h in (2 * p, 2 * p + 1):
                    z = _dot_nt(q_heads[h], k_head(j, h))
                    ls_pos, ls_neg = _log_sigmoid_pair(z)
                    if diag:
                        before = colm < row
                        ls_neg = jnp.where(before, ls_neg, 0.0)
                    c = _dot(ls_neg.astype(BF16), u_ref[...])
                    w = jnp.exp(ls_pos + c + rs[h])
                    if diag:
                        w = jnp.where(before, w, 0.0)
                    outs.append(_dot(w.astype(BF16), v2))
                    new_rs.append(rs[h] + (c[:, 0:1] + ls_neg[:, 0:1]))
                new_accs.append(accs[p] + jnp.where(first, outs[0], outs[1]))
            return tuple(new_accs), tuple(new_rs)

        zacc = jnp.zeros((tq, LANES), F32)
        zr = jnp.zeros((tq, 1), F32)
        carry = step(i, ((zacc,) * (nh // 2), (zr,) * nh), True)
        carry = lax.fori_loop(0, i, lambda n, c: step(i - 1 - n, c, False), carry)
        outs = carry[0]
    else:
        def step(j, carry, diag):
            accs, ms, ls = carry
            new_accs, new_ms, new_ls = [], [], []
            r0 = pl.multiple_of(j * tk, tk)
            for p in range(nh // 2):
                v2 = v_pair(j, p)
                pv, alphas = [], []
                for h in (2 * p, 2 * p + 1):
                    s = _dot_nt(q_heads[h], k_head(j, h))
                    if mode == "mla":
                        s = s * scale
                    else:
                        s = s + nc_ref[0, h:h + 1, pl.ds(r0, tk)]
                    if diag:
                        s = jnp.where(colm <= row, s, -jnp.inf)
                    m_new = jnp.maximum(ms[h], jnp.max(s, axis=-1, keepdims=True))
                    alpha = jnp.exp(ms[h] - m_new)
                    pr = jnp.exp(s - m_new)
                    new_ls.append(alpha * ls[h] + jnp.sum(pr, axis=-1, keepdims=True))
                    new_ms.append(m_new)
                    pv.append(_dot(pr.astype(BF16), v2))
                    alphas.append(alpha)
                new_accs.append(accs[p] * jnp.where(first, alphas[0], alphas[1])
                                + jnp.where(first, pv[0], pv[1]))
            return tuple(new_accs), tuple(new_ms), tuple(new_ls)

        neg = jnp.full((tq, 1), -jnp.inf, F32)
        zl = jnp.zeros((tq, 1), F32)
        zacc = jnp.zeros((tq, LANES), F32)
        carry = step(i, ((zacc,) * (nh // 2), (neg,) * nh, (zl,) * nh), True)
        carry = lax.fori_loop(0, i, lambda n, c: step(n, c, False), carry)
        accs, _, ls = carry
        outs = [accs[p] / jnp.where(first, ls[2 * p], ls[2 * p + 1]) for p in range(nh // 2)]
    for p in range(nh // 2):
        sl = slice(p * LANES, (p + 1) * LANES)
        o_ref[:, sl] = _head_rms(outs[p], g_ref[:, sl]).astype(o_ref.dtype)


def _attn_call(mode, q, k, v, qcol, kcol, vcol, g_flat, gcol, bsz, seq, tq, extra=(), scale=1.0):
    t = bsz * seq
    nq = seq // tq
    hq = N_HEADS_PER_MIXER * HEAD_DIM
    qw = N_HEADS_PER_MIXER * LANES if mode == "mla" else hq
    in_specs = [
        pl.BlockSpec((tq, qw), lambda b, i: (b * nq + i, qcol)),
        pl.BlockSpec((seq, qw), lambda b, i: (b, kcol)),
        pl.BlockSpec((seq, hq), lambda b, i: (b, vcol)),
        pl.BlockSpec((1, hq), lambda b, i: (0, gcol)),
    ]
    args = [q, k, v, g_flat]
    if mode == "sb":
        (u,) = extra
        in_specs.append(pl.BlockSpec(u.shape, lambda b, i: (0, 0)))
        args.append(u)
    elif mode == "fox":
        (nc,) = extra
        in_specs.append(pl.BlockSpec((1, N_HEADS_PER_MIXER, seq), lambda b, i: (b, 0, 0)))
        args.append(nc)
    return pl.pallas_call(
        functools.partial(_attn_kernel, mode=mode, tq=tq, scale=scale),
        grid=(bsz, nq),
        in_specs=in_specs,
        out_specs=pl.BlockSpec((tq, hq), lambda b, i: (b * nq + i, 0)),
        out_shape=jax.ShapeDtypeStruct((t, hq), BF16),
        compiler_params=_cparams(("parallel", "arbitrary")),
        name=f"attn_{mode}",
    )(*args)


def _dil_kernel(qkv_ref, o_ref, *, tqs, win):
    tb = win
    c = pl.program_id(2)
    lane = _lane_iota((tb, LANES))
    first = lane < HEAD_DIM
    row = lax.broadcasted_iota(jnp.int32, (tb, tb), 0)
    colm = lax.broadcasted_iota(jnp.int32, (tb, tb), 1)
    hq = N_HEADS_PER_MIXER * HEAD_DIM

    def tile(it, _):
        gi = c * (tqs // tb) + it
        r0 = pl.multiple_of(gi * tb, tb)
        rp = pl.multiple_of(jnp.maximum(gi - 1, 0) * tb, tb)
        no_prev = jnp.where(gi > 0, 0, tb)
        o0 = pl.multiple_of(it * tb, tb)
        for p in range(N_HEADS_PER_MIXER // 2):
            q2 = qkv_ref[0, 0, pl.ds(r0, tb), p * LANES:(p + 1) * LANES]
            kd = qkv_ref[0, 0, pl.ds(r0, tb), hq + p * LANES:hq + (p + 1) * LANES]
            kp = qkv_ref[0, 0, pl.ds(rp, tb), hq + p * LANES:hq + (p + 1) * LANES]
            vd = qkv_ref[0, 0, pl.ds(r0, tb), 2 * hq + p * LANES:2 * hq + (p + 1) * LANES]
            vp = qkv_ref[0, 0, pl.ds(rp, tb), 2 * hq + p * LANES:2 * hq + (p + 1) * LANES]
            zero = jnp.zeros_like(q2)
            outs = []
            lses = []
            for hh in range(2):
                qh = jnp.where(first, q2, zero) if hh == 0 else jnp.where(first, zero, q2)
                sd = jnp.where(colm <= row, _dot_nt(qh, kd), -jnp.inf)
                sp = jnp.where(colm >= row + no_prev, _dot_nt(qh, kp), -jnp.inf)
                m = jnp.maximum(jnp.max(sd, axis=-1, keepdims=True), jnp.max(sp, axis=-1, keepdims=True))
                pd = jnp.exp(sd - m)
                pp = jnp.exp(sp - m)
                l = jnp.sum(pd, axis=-1, keepdims=True) + jnp.sum(pp, axis=-1, keepdims=True)
                inv = 1.0 / l
                acc = _dot((pd * inv).astype(BF16), vd) + _dot((pp * inv).astype(BF16), vp)
                outs.append(acc)
                lses.append(m + jnp.log(l))
            o_ref[0, 0, pl.ds(o0, tb), p * LANES:(p + 1) * LANES] = jnp.where(first, outs[0], outs[1])
            o_ref[0, 0, pl.ds(o0, tb), hq + p * LANES:hq + (p + 1) * LANES] = jnp.where(first, lses[0], lses[1])
        return 0

    lax.fori_loop(0, tqs // tb, tile, 0)


def _dil_call(qkv, tqs, win):
    bsz, r, n, w = qkv.shape
    hq = N_HEADS_PER_MIXER * HEAD_DIM
    return pl.pallas_call(
        functools.partial(_dil_kernel, tqs=tqs, win=win),
        grid=(bsz, r, n // tqs),
        in_specs=[pl.BlockSpec((1, 1, n, w), lambda b, p, c: (b, p, 0, 0))],
        out_specs=pl.BlockSpec((1, 1, tqs, 2 * hq), lambda b, p, c: (b, p, c, 0)),
        out_shape=jax.ShapeDtypeStruct((bsz, r, n, 2 * hq), F32),
        compiler_params=_cparams(("parallel", "parallel", "arbitrary")),
        name=f"dil_r{r}",
    )(qkv)


def _post_kernel(x_ref, osb_ref, ofox_ref, omla_ref, d1_ref, d2_ref, d3_ref, gd_ref, wo_ref,
                 lng_ref, lnb_ref, wr_ref, br_ref, x1_ref, route_ref, *, alpha):
    hq = N_HEADS_PER_MIXER * HEAD_DIM
    h = _dot(osb_ref[...], wo_ref[0:hq, :])
    h += _dot(ofox_ref[...], wo_ref[hq:2 * hq, :])
    h += _dot(omla_ref[...], wo_ref[2 * hq:3 * hq, :])
    for p in range(N_HEADS_PER_MIXER // 2):
        sl = slice(p * LANES, (p + 1) * LANES)
        ll = slice(hq + p * LANES, hq + (p + 1) * LANES)
        l1, l2, l3 = d1_ref[:, ll], d2_ref[:, ll], d3_ref[:, ll]
        m = jnp.maximum(jnp.maximum(l1, l2), l3)
        e1, e2, e3 = jnp.exp(l1 - m), jnp.exp(l2 - m), jnp.exp(l3 - m)
        inv = 1.0 / (e1 + e2 + e3)
        od = (e1 * inv) * d1_ref[:, sl] + (e2 * inv) * d2_ref[:, sl] + (e3 * inv) * d3_ref[:, sl]
        od = _head_rms(od, gd_ref[:, sl]).astype(BF16)
        h += _dot(od, wo_ref[3 * hq + p * LANES:3 * hq + (p + 1) * LANES, :])
    x1 = _layernorm(alpha * x_ref[...] + h, lng_ref[...], lnb_ref[...])
    x1_ref[...] = x1

    logits = _dot_f32(x1, wr_ref[...]) + br_ref[...]
    lane = _lane_iota(logits.shape)
    lanef = lane.astype(F32)
    big = float(LANES)
    ninf = -jnp.inf
    gl = jnp.where(lane < N_GROUPS, logits, ninf)
    gmax = jnp.max(gl, axis=-1, keepdims=True)
    gsel = jnp.min(jnp.where(gl == gmax, lanef, big), axis=-1, keepdims=True)
    gw = 1.0 / jnp.sum(jnp.exp(gl - gmax), axis=-1, keepdims=True)
    e_lo = N_GROUPS + EXPERTS_PER_GROUP * gsel
    in_grp = (lanef >= e_lo) & (lanef < e_lo + EXPERTS_PER_GROUP)
    el = jnp.where(in_grp, logits, ninf)
    t1 = jnp.max(el, axis=-1, keepdims=True)
    i1 = jnp.min(jnp.where(el == t1, lanef, big), axis=-1, keepdims=True)
    el2 = jnp.where(lanef == i1, ninf, el)
    t2 = jnp.max(el2, axis=-1, keepdims=True)
    i2 = jnp.min(jnp.where(el2 == t2, lanef, big), axis=-1, keepdims=True)
    ex = jnp.exp(t2 - t1)
    den = 1.0 + ex
    g1 = gw / den
    g2 = gw * ex / den
    out = jnp.where(lane == 0, i1 - N_GROUPS,
                    jnp.where(lane == 1, i2 - N_GROUPS,
                              jnp.where(lane == 2, g1, jnp.where(lane == 3, g2, 0.0))))
    route_ref[...] = out


def _post_call(x, osb, ofox, omla, d1, d2, d3, g_dil, wo, lng, lnb, wr, br, alpha, tm):
    t, d = x.shape
    hq = N_HEADS_PER_MIXER * HEAD_DIM
    row = lambda w: pl.BlockSpec((tm, w), lambda i: (i, 0))
    full = lambda a: pl.BlockSpec(a.shape, lambda i: (0, 0))
    return pl.pallas_call(
        functools.partial(_post_kernel, alpha=alpha),
        grid=(t // tm,),
        in_specs=[row(d), row(hq), row(hq), row(hq), row(2 * hq), row(2 * hq), row(2 * hq),
                  full(g_dil), full(wo), full(lng), full(lnb), full(wr), full(br)],
        out_specs=[row(d), row(LANES)],
        out_shape=[jax.ShapeDtypeStruct((t, d), F32), jax.ShapeDtypeStruct((t, LANES), F32)],
        compiler_params=_cparams(("parallel",)),
        name="post_mixer",
    )(x, osb, ofox, omla, d1, d2, d3, g_dil, wo, lng, lnb, wr, br)


def _moe_kernel(ce_ref, nv_ref, src_ref, dst_ref, x_hbm, w1_ref, w3_ref, w2_ref, y_hbm,
                xs_ref, ys_ref, gsem, ssem):
    c = pl.program_id(0)
    nv = nv_ref[c]

    def gather_copy(i):
        return pltpu.make_async_copy(x_hbm.at[pl.ds(src_ref[0, 0, i], 1), :],
                                     xs_ref.at[pl.ds(i, 1), :], gsem)

    def scatter_copy(i):
        return pltpu.make_async_copy(ys_ref.at[pl.ds(i, 1), :],
                                     y_hbm.at[pl.ds(dst_ref[0, 0, i], 1), :], ssem)

    @pl.when(c == 0)
    def _():
        xs_ref[...] = jnp.zeros_like(xs_ref)

    @pl.when(nv > 0)
    def _():
        def start_g(i, _):
            gather_copy(i).start()
            return 0

        def wait_g(i, _):
            gather_copy(i).wait()
            return 0

        lax.fori_loop(0, nv, start_g, 0)
        lax.fori_loop(0, nv, wait_g, 0)
        xb = xs_ref[...].astype(BF16)
        a = _dot(xb, w1_ref[0])
        b = _dot(xb, w3_ref[0])
        hid = (a / (1.0 + jnp.exp(-a)) * b).astype(BF16)
        ys_ref[...] = _dot(hid, w2_ref[0])

        def start_s(i, _):
            scatter_copy(i).start()
            return 0

        def wait_s(i, _):
            scatter_copy(i).wait()
            return 0

        lax.fori_loop(0, nv, start_s, 0)
        lax.fori_loop(0, nv, wait_s, 0)


def _moe_call(chunk_expert, n_valid, src, dst, x1, w1, w3, w2, n_rows_out):
    n_chunks = chunk_expert.shape[0]
    t, d = x1.shape
    de = w1.shape[-1]
    grid_spec = pltpu.PrefetchScalarGridSpec(
        num_scalar_prefetch=2,
        grid=(n_chunks,),
        in_specs=[
            pl.BlockSpec((1, 1, MOE_BLOCK), lambda c, ce, na: (c, 0, 0), memory_space=pltpu.SMEM),
            pl.BlockSpec((1, 1, MOE_BLOCK), lambda c, ce, na: (c, 0, 0), memory_space=pltpu.SMEM),
            pl.BlockSpec(memory_space=pl.ANY),
            pl.BlockSpec((1, d, de), lambda c, ce, na: (ce[c], 0, 0)),
            pl.BlockSpec((1, d, de), lambda c, ce, na: (ce[c], 0, 0)),
            pl.BlockSpec((1, de, d), lambda c, ce, na: (ce[c], 0, 0)),
        ],
        out_specs=pl.BlockSpec(memory_space=pl.ANY),
        scratch_shapes=[
            pltpu.VMEM((MOE_BLOCK, d), F32),
            pltpu.VMEM((MOE_BLOCK, d), F32),
            pltpu.SemaphoreType.DMA(()),
            pltpu.SemaphoreType.DMA(()),
        ],
    )
    return pl.pallas_call(
        _moe_kernel,
        grid_spec=grid_spec,
        out_shape=jax.ShapeDtypeStruct((n_rows_out, d), F32),
        compiler_params=_cparams(("arbitrary",)),
        name="moe_experts",
    )(chunk_expert, n_valid, src, dst, x1, w1, w3, w2)


def _combine_kernel(x1_ref, y_ref, route_ref, lng_ref, lnb_ref, o_ref, *, alpha):
    d = x1_ref.shape[-1]
    g1 = route_ref[:, 2:3]
    g2 = route_ref[:, 3:4]
    m = g1 * y_ref[:, 0:d] + g2 * y_ref[:, d:2 * d]
    o_ref[...] = _layernorm(alpha * x1_ref[...] + m, lng_ref[...], lnb_ref[...])


def _combine_call(x1, y2, route, lng, lnb, alpha, tm):
    t, d = x1.shape
    full = lambda a: pl.BlockSpec(a.shape, lambda i: (0, 0))
    return pl.pallas_call(
        functools.partial(_combine_kernel, alpha=alpha),
        grid=(t // tm,),
        in_specs=[pl.BlockSpec((tm, d), lambda i: (i, 0)),
                  pl.BlockSpec((tm, 2 * d), lambda i: (i, 0)),
                  pl.BlockSpec((tm, LANES), lambda i: (i, 0)),
                  full(lng), full(lnb)],
        out_specs=pl.BlockSpec((tm, d), lambda i: (i, 0)),
        out_shape=jax.ShapeDtypeStruct((t, d), F32),
        compiler_params=_cparams(("parallel",)),
        name="moe_combine",
    )(x1, y2, route, lng, lnb)


def _rope_tables(seq, dim, lane_lo):
    half = dim // 2
    inv_freq = ROPE_THETA ** (-jnp.arange(half, dtype=F32) / half)
    ang = jnp.arange(seq, dtype=F32)[:, None] * inv_freq[None, :]
    cos = jnp.concatenate([jnp.cos(ang), jnp.cos(ang)], -1)
    sin = jnp.concatenate([jnp.sin(ang), jnp.sin(ang)], -1)
    if lane_lo == 0:
        reps = LANES // dim
        return jnp.tile(cos, (1, reps)), jnp.tile(sin, (1, reps))
    cos_t = jnp.ones((seq, LANES), F32).at[:, lane_lo:lane_lo + dim].set(cos)
    sin_t = jnp.zeros((seq, LANES), F32).at[:, lane_lo:lane_lo + dim].set(sin)
    return cos_t, sin_t


def _dispatch_tables(route, n_tok):
    expert_id = route[:, 0:TOP_K].astype(jnp.int32).reshape(-1)
    n_assign = n_tok * TOP_K
    n_slots = n_assign + N_EXPERTS * MOE_BLOCK
    n_chunks = n_slots // MOE_BLOCK
    onehot = (expert_id[:, None] == jnp.arange(N_EXPERTS, dtype=jnp.int32)[None, :]).astype(jnp.int32)
    ranks = jnp.cumsum(onehot, axis=0) - onehot
    rank = jnp.sum(ranks * onehot, axis=1)
    counts = jnp.sum(onehot, axis=0)
    padded = (counts + MOE_BLOCK - 1) // MOE_BLOCK * MOE_BLOCK
    pad_end = jnp.cumsum(padded)
    pad_start = pad_end - padded
    dest = pad_start[expert_id] + rank
    assign = jnp.arange(n_assign, dtype=jnp.int32)
    src = jnp.zeros((n_slots,), jnp.int32).at[dest].set(assign // TOP_K)
    dst = jnp.zeros((n_slots,), jnp.int32).at[dest].set(assign)
    chunk_start = jnp.arange(n_chunks, dtype=jnp.int32) * MOE_BLOCK
    chunk_expert = jnp.minimum(jnp.searchsorted(pad_end, chunk_start, side="right"),
                               N_EXPERTS - 1).astype(jnp.int32)
    n_valid = jnp.clip(pad_start[chunk_expert] + counts[chunk_expert] - chunk_start,
                       0, MOE_BLOCK).astype(jnp.int32)
    return (chunk_expert, n_valid, src.reshape(n_chunks, 1, MOE_BLOCK),
            dst.reshape(n_chunks, 1, MOE_BLOCK), n_assign)


def _pick_tile(n, pref):
    t = pref
    while n % t:
        t //= 2
    return t


def kernel(x, w_in, b_forget, g_cq, g_ckv, w_uq, w_ukv, g_head, w_out, ln1_g, ln1_b,
           w_group, b_group, w_expert, b_expert, w1, w3, w2, ln2_g, ln2_b):
    bsz, seq, d = x.shape
    depth = w_in.shape[0]
    t = bsz * seq
    alpha = (2.0 * depth) ** 0.25
    hq = N_HEADS_PER_MIXER * HEAD_DIM
    qk_scale = HEAD_DIM ** -0.5
    mla_scale = (MLA_NOPE + MLA_ROPE) ** -0.5
    win = DIL_BRANCHES[0][0]
    assert all(w // r == win for w, r in DIL_BRANCHES)
    assert seq % (DIL_BRANCHES[-1][1] * win) == 0 and d % LANES == 0

    cos64, sin64 = _rope_tables(seq, HEAD_DIM, 0)
    cos_m, sin_m = _rope_tables(seq, MLA_ROPE, MLA_NOPE)
    tq = _pick_tile(seq, 256)
    idx = jnp.arange(tq)
    u_sb = (idx[:, None] > idx[None, :]).astype(BF16)

    for l in range(depth):
        wl = w_in[l]
        o_fox, o_mla, o_dil = N_SB, N_SB + N_FOX_QKV + N_HEADS_PER_MIXER, N_SB + N_FOX_QKV + N_HEADS_PER_MIXER + N_MLA
        qs = lambda w: w.at[:, 0:hq].multiply(qk_scale)
        w_sb = qs(wl[:, 0:N_SB])
        w_fx = qs(wl[:, o_fox:o_fox + N_FOX_QKV])
        w_f = wl[:, o_fox + N_FOX_QKV:o_mla]
        w_ml = wl[:, o_mla:o_dil]
        wd = wl[:, o_dil:].reshape(d, 3, len(DIL_BRANCHES), hq)
        w_br = [qs(jnp.concatenate([wd[:, 0, g], wd[:, 1, g], wd[:, 2, g]], axis=1)) for g in range(len(DIL_BRANCHES))]
        w_misc = jnp.concatenate([w_ml, w_f, jnp.zeros((d, MISC_W - N_MLA - N_HEADS_PER_MIXER), F32)], axis=1)
        w_tok = jnp.concatenate([w_sb, w_fx, w_br[0], w_misc], axis=1).astype(BF16)

        wq = jnp.pad(w_uq[l].reshape(MLA_Q_LORA, N_HEADS_PER_MIXER, MLA_NOPE + MLA_ROPE),
                     ((0, 0), (0, 0), (0, LANES - MLA_NOPE - MLA_ROPE))).reshape(MLA_Q_LORA, -1).astype(BF16)
        wkv = w_ukv[l].reshape(MLA_KV_LORA, N_HEADS_PER_MIXER, MLA_NOPE + HEAD_DIM)
        wk = jnp.pad(wkv[:, :, :MLA_NOPE], ((0, 0), (0, 0), (0, LANES - MLA_NOPE))).reshape(MLA_KV_LORA, -1).astype(BF16)
        wv = wkv[:, :, MLA_NOPE:].reshape(MLA_KV_LORA, -1).astype(BF16)
        g_flat = g_head[l].reshape(1, -1)
        wr = jnp.concatenate([w_group[l], w_expert[l],
                              jnp.zeros((d, LANES - N_GROUPS - N_EXPERTS), F32)], axis=1)
        br = jnp.concatenate([b_group[l], b_expert[l],
                              jnp.zeros((LANES - N_GROUPS - N_EXPERTS,), F32)]).reshape(1, LANES)

        tm = _pick_tile(seq, 512)
        sb, fx, d1, misc = _proj_call(
            x, w_tok, cos64, sin64,
            ((N_SB, 0, BF16), (N_FOX_QKV, 0, BF16), (N_BRANCH, 2 * hq, BF16), (MISC_W, 0, F32)), 1, tm)
        sb = sb.reshape(t, N_SB)
        fx = fx.reshape(t, N_FOX_QKV)
        misc = misc.reshape(t, MISC_W)

        o_sb = _attn_call("sb", sb, sb, sb, 0, 1, 2, g_flat, 0, bsz, seq, tq, extra=(u_sb,))
        neg_c = _fox_c_call(misc, b_forget[l], bsz, seq)
        o_fx = _attn_call("fox", fx, fx, fx, 0, 1, 2, g_flat, 1, bsz, seq, tq, extra=(neg_c,))
        mq, mk, mv = _mla_prep_call(misc, g_cq[l].reshape(1, -1), g_ckv[l].reshape(1, -1), wq, wk, wv,
                                    cos_m, sin_m, seq, tm)
        o_ml = _attn_call("mla", mq, mk, mv, 0, 0, 0, g_flat, 2, bsz, seq, tq, scale=mla_scale)

        dil = []
        for g, (_, r) in enumerate(DIL_BRANCHES):
            n = seq // r
            if r == 1:
                qkv = d1
            else:
                (qkv,) = _proj_call(x, w_br[g].astype(BF16), cos64, sin64,
                                    ((N_BRANCH, 2 * hq, BF16),), r, _pick_tile(n, 512))
            og = _dil_call(qkv, _pick_tile(n, 1024), win)
            dil.append(og.transpose(0, 2, 1, 3).reshape(t, 2 * hq))

        x1, route = _post_call(
            x.reshape(t, d), o_sb, o_fx, o_ml, dil[0], dil[1], dil[2], g_flat[:, 3 * hq:], w_out[l].astype(BF16),
            ln1_g[l].reshape(1, d), ln1_b[l].reshape(1, d), wr, br, alpha, _pick_tile(t, 256))

        chunk_expert, n_valid, src, dst, n_rows = _dispatch_tables(route, t)
        y = _moe_call(chunk_expert, n_valid, src, dst, x1,
                      w1[l].astype(BF16), w3[l].astype(BF16), w2[l].astype(BF16), n_rows)
        x = _combine_call(x1, y.reshape(n_rows // TOP_K, TOP_K * d), route,
                          ln2_g[l].reshape(1, d), ln2_b[l].reshape(1, d), alpha, _pick_tile(t, 256)).reshape(bsz, seq, d)
    return x
```

```python
import functools

import jax
import jax.numpy as jnp
import numpy as np
from jax import lax
from jax.experimental import pallas as pl
from jax.experimental.pallas import tpu as pltpu

F32 = jnp.float32
BF16 = jnp.bfloat16

HEAD_DIM = 64
N_HEADS_PER_MIXER = 4
MLA_Q_LORA = 256
MLA_KV_LORA = 128
MLA_NOPE = 64
MLA_ROPE = 32
DIL_BRANCHES = ((128, 1), (512, 4), (2048, 16))
ROPE_THETA = 10000.0
N_GROUPS = 4
EXPERTS_PER_GROUP = 4
N_EXPERTS = N_GROUPS * EXPERTS_PER_GROUP
TOP_K = 2
MOE_BLOCK = 256
LN_EPS = 1e-5
RMS_EPS = 1e-6

LANES = 128
VMEM_LIMIT_BYTES = 56 * 1024 * 1024

N_SB = 3 * N_HEADS_PER_MIXER * HEAD_DIM
N_FOX_QKV = 3 * N_HEADS_PER_MIXER * HEAD_DIM
N_MLA = MLA_Q_LORA + MLA_KV_LORA + MLA_ROPE
N_BRANCH = 3 * N_HEADS_PER_MIXER * HEAD_DIM
MISC_W = 512
F_COL = N_MLA


def _cparams(sem):
    return pltpu.CompilerParams(dimension_semantics=sem, vmem_limit_bytes=VMEM_LIMIT_BYTES)


def _split3(a):
    hi = a.astype(BF16)
    r1 = a - hi.astype(F32)
    mid = r1.astype(BF16)
    lo = (r1 - mid.astype(F32)).astype(BF16)
    return hi, mid, lo


def _dot(a, b):
    return jnp.dot(a, b, preferred_element_type=F32)


def _dot_nt(a, b):
    return lax.dot_general(a, b, (((1,), (1,)), ((), ())), preferred_element_type=F32)


def _dot_exact_rhs(a, u):
    hi, mid, lo = _split3(a)
    return _dot(hi, u) + _dot(mid, u) + _dot(lo, u)


def _dot_f32(a, b):
    ah, am, al = _split3(a)
    bh, bm, bl = _split3(b)
    return (_dot(ah, bh) + (_dot(ah, bm) + _dot(am, bh))
            + (_dot(am, bm) + _dot(ah, bl) + _dot(al, bh)))


def _lane_iota(shape):
    return lax.broadcasted_iota(jnp.int32, shape, len(shape) - 1)


def _rotate_half(y, half):
    lane = _lane_iota(y.shape)
    fwd = pltpu.roll(y, half, 1)
    bwd = pltpu.roll(y, LANES - half, 1)
    return jnp.where((lane % (2 * half)) < half, -bwd, fwd)


def _log_sigmoid_pair(z):
    sp = jnp.log(1.0 + jnp.exp(-jnp.abs(z)))
    return jnp.minimum(z, 0.0) - sp, -jnp.maximum(z, 0.0) - sp


def _head_rms(o, g):
    lane = _lane_iota(o.shape)
    first = lane < HEAD_DIM
    sq = o * o
    ss_a = jnp.sum(jnp.where(first, sq, 0.0), axis=-1, keepdims=True)
    ss_b = jnp.sum(jnp.where(first, 0.0, sq), axis=-1, keepdims=True)
    ms = jnp.where(first, ss_a, ss_b) * (1.0 / HEAD_DIM)
    return o * lax.rsqrt(ms + RMS_EPS) * g


def _layernorm(y, g, b):
    mu = jnp.mean(y, axis=-1, keepdims=True)
    d = y - mu
    var = jnp.mean(d * d, axis=-1, keepdims=True)
    return d * lax.rsqrt(var + LN_EPS) * g + b


def _proj_kernel(x_ref, w_ref, cos_ref, sin_ref, *out_refs, outs):
    xb = x_ref[0].astype(BF16)
    col = 0
    for o_ref, (width, n_rope, _) in zip(out_refs, outs):
        for c in range(0, width, 2 * LANES):
            cw = min(2 * LANES, width - c)
            y = _dot(xb, w_ref[:, col + c:col + c + cw])
            for s in range(0, cw, LANES):
                ys = y[:, s:s + LANES]
                if c + s < n_rope:
                    ys = ys * cos_ref[...] + _rotate_half(ys, HEAD_DIM // 2) * sin_ref[...]
                o_ref[0, 0, :, c + s:c + s + LANES] = ys.astype(o_ref.dtype)
        col += width


def _proj_call(x, w, cos, sin, outs, r, tm):
    bsz, seq, d = x.shape
    n = seq // r
    xv = x.reshape(bsz, n, r * d)
    cosv = cos.reshape(n, r * LANES)
    sinv = sin.reshape(n, r * LANES)
    grid = (bsz, r, n // tm)
    in_specs = [
        pl.BlockSpec((1, tm, d), lambda b, p, i: (b, i, p)),
        pl.BlockSpec(w.shape, lambda b, p, i: (0, 0)),
        pl.BlockSpec((tm, LANES), lambda b, p, i: (i, p)),
        pl.BlockSpec((tm, LANES), lambda b, p, i: (i, p)),
    ]
    out_specs = [pl.BlockSpec((1, 1, tm, wd), lambda b, p, i: (b, p, i, 0)) for wd, _, _ in outs]
    out_shape = [jax.ShapeDtypeStruct((bsz, r, n, wd), dt) for wd, _, dt in outs]
    return pl.pallas_call(
        functools.partial(_proj_kernel, outs=outs),
        grid=grid, in_specs=in_specs, out_specs=out_specs, out_shape=out_shape,
        compiler_params=_cparams(("parallel", "parallel", "parallel")),
        name=f"proj_r{r}",
    )(xv, w, cosv, sinv)


def _fox_c_kernel(misc_ref, bias_ref, uinc_ref, ones_ref, out_ref, *, seq):
    lane0 = F_COL - 3 * LANES
    nblk = seq // LANES

    def body(j, carry):
        r0 = pl.multiple_of(j * LANES, LANES)
        f = misc_ref[pl.ds(r0, LANES), :] + bias_ref[...]
        lf, _ = _log_sigmoid_pair(f)
        lft = lf.T
        csum = _dot_exact_rhs(lft, uinc_ref[...]) + carry
        tot = _dot_exact_rhs(lft, ones_ref[...])
        out_ref[0, :, pl.ds(r0, LANES)] = -csum[lane0:lane0 + N_HEADS_PER_MIXER, :]
        return carry + tot

    lax.fori_loop(0, nblk, body, jnp.zeros((LANES, LANES), F32))


def _fox_c_call(misc, b_forget, bsz, seq):
    bias = jnp.zeros((1, LANES), F32).at[0, F_COL - 3 * LANES:F_COL - 3 * LANES + N_HEADS_PER_MIXER].set(b_forget)
    idx = jnp.arange(LANES)
    uinc = (idx[:, None] <= idx[None, :]).astype(BF16)
    ones = jnp.ones((LANES, LANES), BF16)
    return pl.pallas_call(
        functools.partial(_fox_c_kernel, seq=seq),
        grid=(bsz,),
        in_specs=[
            pl.BlockSpec((seq, LANES), lambda b: (b, 3)),
            pl.BlockSpec((1, LANES), lambda b: (0, 0)),
            pl.BlockSpec((LANES, LANES), lambda b: (0, 0)),
            pl.BlockSpec((LANES, LANES), lambda b: (0, 0)),
        ],
        out_specs=pl.BlockSpec((1, N_HEADS_PER_MIXER, seq), lambda b: (b, 0, 0)),
        out_shape=jax.ShapeDtypeStruct((bsz, N_HEADS_PER_MIXER, seq), F32),
        compiler_params=_cparams(("parallel",)),
        name="fox_c",
    )(misc, bias, uinc, ones)


def _mla_prep_kernel(misc_ref, gq_ref, gkv_ref, wq_ref, wk_ref, wv_ref, cos_ref, sin_ref,
                     q_ref, k_ref, v_ref):
    def rms(x, g):
        return x * lax.rsqrt(jnp.mean(x * x, axis=-1, keepdims=True) + RMS_EPS) * g

    cq = rms(misc_ref[:, 0:MLA_Q_LORA], gq_ref[...]).astype(BF16)
    ckv = rms(misc_ref[:, MLA_Q_LORA:MLA_Q_LORA + MLA_KV_LORA], gkv_ref[...]).astype(BF16)
    kr_blk = misc_ref[:, 3 * LANES:4 * LANES]
    lane = _lane_iota(kr_blk.shape)
    in_rope = (lane >= MLA_NOPE) & (lane < MLA_NOPE + MLA_ROPE)
    kr = jnp.where(in_rope, pltpu.roll(kr_blk, MLA_NOPE, 1), 0.0)
    cos = cos_ref[...]
    sin = sin_ref[...]

    def rope(y):
        return y * cos + _rotate_half(y, MLA_ROPE // 2) * sin

    q = _dot(cq, wq_ref[...])
    k = _dot(ckv, wk_ref[...])
    for h in range(N_HEADS_PER_MIXER):
        sl = slice(h * LANES, (h + 1) * LANES)
        q_ref[:, sl] = rope(q[:, sl]).astype(BF16)
        k_ref[:, sl] = rope(k[:, sl] + kr).astype(BF16)
    v_ref[...] = _dot(ckv, wv_ref[...]).astype(BF16)


def _mla_prep_call(misc, g_cq, g_ckv, wq, wk, wv, cos, sin, seq, tm):
    t = misc.shape[0]
    nper = seq // tm
    hw = N_HEADS_PER_MIXER * LANES
    vw = N_HEADS_PER_MIXER * HEAD_DIM
    full = lambda a: pl.BlockSpec(a.shape, lambda i: (0, 0))
    return pl.pallas_call(
        _mla_prep_kernel,
        grid=(t // tm,),
        in_specs=[
            pl.BlockSpec((tm, MISC_W), lambda i: (i, 0)),
            full(g_cq), full(g_ckv), full(wq), full(wk), full(wv),
            pl.BlockSpec((tm, LANES), lambda i: (i % nper, 0)),
            pl.BlockSpec((tm, LANES), lambda i: (i % nper, 0)),
        ],
        out_specs=[
            pl.BlockSpec((tm, hw), lambda i: (i, 0)),
            pl.BlockSpec((tm, hw), lambda i: (i, 0)),
            pl.BlockSpec((tm, vw), lambda i: (i, 0)),
        ],
        out_shape=[
            jax.ShapeDtypeStruct((t, hw), BF16),
            jax.ShapeDtypeStruct((t, hw), BF16),
            jax.ShapeDtypeStruct((t, vw), BF16),
        ],
        compiler_params=_cparams(("parallel",)),
        name="mla_prep",
    )(misc, g_cq, g_ckv, wq, wk, wv, cos, sin)


def _attn_kernel(*refs, mode, tq, scale):
    if mode == "sb":
        q_ref, k_ref, v_ref, g_ref, u_ref, o_ref = refs
    elif mode == "fox":
        q_ref, k_ref, v_ref, g_ref, nc_ref, o_ref = refs
    else:
        q_ref, k_ref, v_ref, g_ref, o_ref = refs
    tk = tq
    nh = N_HEADS_PER_MIXER
    i = pl.program_id(1)
    lane = _lane_iota((tq, LANES))
    first = lane < HEAD_DIM
    row = lax.broadcasted_iota(jnp.int32, (tq, tk), 0)
    colm = lax.broadcasted_iota(jnp.int32, (tq, tk), 1)

    q_heads = []
    for h in range(nh):
        if mode == "mla":
            q_heads.append(q_ref[:, h * LANES:(h + 1) * LANES])
        else:
            q2 = q_ref[:, (h // 2) * LANES:(h // 2 + 1) * LANES]
            zero = jnp.zeros_like(q2)
            q_heads.append(jnp.where(first, q2, zero) if h % 2 == 0 else jnp.where(first, zero, q2))

    def k_head(j, h):
        r0 = pl.multiple_of(j * tk, tk)
        kb = h if mode == "mla" else h // 2
        return k_ref[pl.ds(r0, tk), kb * LANES:(kb + 1) * LANES]

    def v_pair(j, p):
        r0 = pl.multiple_of(j * tk, tk)
        return v_ref[pl.ds(r0, tk), p * LANES:(p + 1) * LANES]

    if mode == "sb":
        def step(j, carry, diag):
            accs, rs = carry
            new_accs, new_rs = [], []
            for p in range(nh // 2):
                v2 = v_pair(j, p)
                outs = []
                for h in (2 * p, 2 * p + 1):
                    z = _dot_nt(q_heads[h], k_head(j, h))
                    ls_pos, ls_neg = _log_sigmoid_pair(z)
                    if diag:
                        before = colm < row
                        ls_neg = jnp.where(before, ls_neg, 0.0)
                    c = _dot(ls_neg.astype(BF16), u_ref[...])
                    w = jnp.exp(ls_pos + c + rs[h])
                    if diag:
                        w = jnp.where(before, w, 0.0)
                    outs.append(_dot(w.astype(BF16), v2))
                    new_rs.append(rs[h] + (c[:, 0:1] + ls_neg[:, 0:1]))
                new_accs.append(accs[p] + jnp.where(first, outs[0], outs[1]))
            return tuple(new_accs), tuple(new_rs)

        zacc = jnp.zeros((tq, LANES), F32)
        zr = jnp.zeros((tq, 1), F32)
        carry = step(i, ((zacc,) * (nh // 2), (zr,) * nh), True)
        carry = lax.fori_loop(0, i, lambda n, c: step(i - 1 - n, c, False), carry)
        outs = carry[0]
    else:
        def step(j, carry, diag):
            accs, ms, ls = carry
            new_accs, new_ms, new_ls = [], [], []
            r0 = pl.multiple_of(j * tk, tk)
            for p in range(nh // 2):
                v2 = v_pair(j, p)
                pv, alphas = [], []
                for h in (2 * p, 2 * p + 1):
                    s = _dot_nt(q_heads[h], k_head(j, h))
                    if mode == "mla":
                        s = s * scale
                    else:
                        s = s + nc_ref[0, h:h + 1, pl.ds(r0, tk)]
                    if diag:
                        s = jnp.where(colm <= row, s, -jnp.inf)
                    m_new = jnp.maximum(ms[h], jnp.max(s, axis=-1, keepdims=True))
                    alpha = jnp.exp(ms[h] - m_new)
                    pr = jnp.exp(s - m_new)
                    new_ls.append(alpha * ls[h] + jnp.sum(pr, axis=-1, keepdims=True))
                    new_ms.append(m_new)
                    pv.append(_dot(pr.astype(BF16), v2))
                    alphas.append(alpha)
                new_accs.append(accs[p] * jnp.where(first, alphas[0], alphas[1])
                                + jnp.where(first, pv[0], pv[1]))
            return tuple(new_accs), tuple(new_ms), tuple(new_ls)

        neg = jnp.full((tq, 1), -jnp.inf, F32)
        zl = jnp.zeros((tq, 1), F32)
        zacc = jnp.zeros((tq, LANES), F32)
        carry = step(i, ((zacc,) * (nh // 2), (neg,) * nh, (zl,) * nh), True)
        carry = lax.fori_loop(0, i, lambda n, c: step(n, c, False), carry)
        accs, _, ls = carry
        outs = [accs[p] / jnp.where(first, ls[2 * p], ls[2 * p + 1]) for p in range(nh // 2)]
    for p in range(nh // 2):
        sl = slice(p * LANES, (p + 1) * LANES)
        o_ref[:, sl] = _head_rms(outs[p], g_ref[:, sl]).astype(o_ref.dtype)


def _attn_call(mode, q, k, v, qcol, kcol, vcol, g_flat, gcol, bsz, seq, tq, extra=(), scale=1.0):
    t = bsz * seq
    nq = seq // tq
    hq = N_HEADS_PER_MIXER * HEAD_DIM
    qw = N_HEADS_PER_MIXER * LANES if mode == "mla" else hq
    in_specs = [
        pl.BlockSpec((tq, qw), lambda b, i: (b * nq + i, qcol)),
        pl.BlockSpec((seq, qw), lambda b, i: (b, kcol)),
        pl.BlockSpec((seq, hq), lambda b, i: (b, vcol)),
        pl.BlockSpec((1, hq), lambda b, i: (0, gcol)),
    ]
    args = [q, k, v, g_flat]
    if mode == "sb":
        (u,) = extra
        in_specs.append(pl.BlockSpec(u.shape, lambda b, i: (0, 0)))
        args.append(u)
    elif mode == "fox":
        (nc,) = extra
        in_specs.append(pl.BlockSpec((1, N_HEADS_PER_MIXER, seq), lambda b, i: (b, 0, 0)))
        args.append(nc)
    return pl.pallas_call(
        functools.partial(_attn_kernel, mode=mode, tq=tq, scale=scale),
        grid=(bsz, nq),
        in_specs=in_specs,
        out_specs=pl.BlockSpec((tq, hq), lambda b, i: (b * nq + i, 0)),
        out_shape=jax.ShapeDtypeStruct((t, hq), BF16),
        compiler_params=_cparams(("parallel", "arbitrary")),
        name=f"attn_{mode}",
    )(*args)


def _dil_kernel(qkv_ref, o_ref, *, tqs, win):
    tb = win
    c = pl.program_id(2)
    lane = _lane_iota((tb, LANES))
    first = lane < HEAD_DIM
    row = lax.broadcasted_iota(jnp.int32, (tb, tb), 0)
    colm = lax.broadcasted_iota(jnp.int32, (tb, tb), 1)
    hq = N_HEADS_PER_MIXER * HEAD_DIM

    def tile(it, _):
        gi = c * (tqs // tb) + it
        r0 = pl.multiple_of(gi * tb, tb)
        rp = pl.multiple_of(jnp.maximum(gi - 1, 0) * tb, tb)
        no_prev = jnp.where(gi > 0, 0, tb)
        o0 = pl.multiple_of(it * tb, tb)
        for p in range(N_HEADS_PER_MIXER // 2):
            q2 = qkv_ref[0, 0, pl.ds(r0, tb), p * LANES:(p + 1) * LANES]
            kd = qkv_ref[0, 0, pl.ds(r0, tb), hq + p * LANES:hq + (p + 1) * LANES]
            kp = qkv_ref[0, 0, pl.ds(rp, tb), hq + p * LANES:hq + (p + 1) * LANES]
            vd = qkv_ref[0, 0, pl.ds(r0, tb), 2 * hq + p * LANES:2 * hq + (p + 1) * LANES]
            vp = qkv_ref[0, 0, pl.ds(rp, tb), 2 * hq + p * LANES:2 * hq + (p + 1) * LANES]
            zero = jnp.zeros_like(q2)
            outs = []
            lses = []
            for hh in range(2):
                qh = jnp.where(first, q2, zero) if hh == 0 else jnp.where(first, zero, q2)
                sd = jnp.where(colm <= row, _dot_nt(qh, kd), -jnp.inf)
                sp = jnp.where(colm >= row + no_prev, _dot_nt(qh, kp), -jnp.inf)
                m = jnp.maximum(jnp.max(sd, axis=-1, keepdims=True), jnp.max(sp, axis=-1, keepdims=True))
                pd = jnp.exp(sd - m)
                pp = jnp.exp(sp - m)
                l = jnp.sum(pd, axis=-1, keepdims=True) + jnp.sum(pp, axis=-1, keepdims=True)
                inv = 1.0 / l
                acc = _dot((pd * inv).astype(BF16), vd) + _dot((pp * inv).astype(BF16), vp)
                outs.append(acc)
                lses.append(m + jnp.log(l))
            o_ref[0, pl.ds(o0, tb), p * LANES:(p + 1) * LANES] = jnp.where(first, outs[0], outs[1])
            o_ref[0, pl.ds(o0, tb), hq + p * LANES:hq + (p + 1) * LANES] = jnp.where(first, lses[0], lses[1])
        return 0

    lax.fori_loop(0, tqs // tb, tile, 0)


def _dil_call(qkv, tqs, win):
    bsz, r, n, w = qkv.shape
    hq = N_HEADS_PER_MIXER * HEAD_DIM
    return pl.pallas_call(
        functools.partial(_dil_kernel, tqs=tqs, win=win),
        grid=(bsz, r, n // tqs),
        in_specs=[pl.BlockSpec((1, 1, n, w), lambda b, p, c: (b, p, 0, 0))],
        out_specs=pl.BlockSpec((1, tqs, 2 * hq), lambda b, p, c: (b, c, p)),
        out_shape=jax.ShapeDtypeStruct((bsz, n, r * 2 * hq), F32),
        compiler_params=_cparams(("parallel", "parallel", "arbitrary")),
        name=f"dil_r{r}",
    )(qkv)


def _post_kernel(x_ref, osb_ref, ofox_ref, omla_ref, d1_ref, d2_ref, d3_ref, gd_ref, wo_ref,
                 lng_ref, lnb_ref, wr_ref, br_ref, x1_ref, route_ref, *, alpha):
    hq = N_HEADS_PER_MIXER * HEAD_DIM
    h = _dot(osb_ref[...], wo_ref[0:hq, :])
    h += _dot(ofox_ref[...], wo_ref[hq:2 * hq, :])
    h += _dot(omla_ref[...], wo_ref[2 * hq:3 * hq, :])
    for p in range(N_HEADS_PER_MIXER // 2):
        sl = slice(p * LANES, (p + 1) * LANES)
        ll = slice(hq + p * LANES, hq + (p + 1) * LANES)
        l1, l2, l3 = d1_ref[:, ll], d2_ref[:, ll], d3_ref[:, ll]
        m = jnp.maximum(jnp.maximum(l1, l2), l3)
        e1, e2, e3 = jnp.exp(l1 - m), jnp.exp(l2 - m), jnp.exp(l3 - m)
        inv = 1.0 / (e1 + e2 + e3)
        od = (e1 * inv) * d1_ref[:, sl] + (e2 * inv) * d2_ref[:, sl] + (e3 * inv) * d3_ref[:, sl]
        od = _head_rms(od, gd_ref[:, sl]).astype(BF16)
        h += _dot(od, wo_ref[3 * hq + p * LANES:3 * hq + (p + 1) * LANES, :])
    x1 = _layernorm(alpha * x_ref[...] + h, lng_ref[...], lnb_ref[...])
    x1_ref[...] = x1

    logits = _dot_f32(x1, wr_ref[...]) + br_ref[...]
    lane = _lane_iota(logits.shape)
    lanef = lane.astype(F32)
    big = float(LANES)
    ninf = -jnp.inf
    gl = jnp.where(lane < N_GROUPS, logits, ninf)
    gmax = jnp.max(gl, axis=-1, keepdims=True)
    gsel = jnp.min(jnp.where(gl == gmax, lanef, big), axis=-1, keepdims=True)
    gw = 1.0 / jnp.sum(jnp.exp(gl - gmax), axis=-1, keepdims=True)
    e_lo = N_GROUPS + EXPERTS_PER_GROUP * gsel
    in_grp = (lanef >= e_lo) & (lanef < e_lo + EXPERTS_PER_GROUP)
    el = jnp.where(in_grp, logits, ninf)
    t1 = jnp.max(el, axis=-1, keepdims=True)
    i1 = jnp.min(jnp.where(el == t1, lanef, big), axis=-1, keepdims=True)
    el2 = jnp.where(lanef == i1, ninf, el)
    t2 = jnp.max(el2, axis=-1, keepdims=True)
    i2 = jnp.min(jnp.where(el2 == t2, lanef, big), axis=-1, keepdims=True)
    ex = jnp.exp(t2 - t1)
    den = 1.0 + ex
    g1 = gw / den
    g2 = gw * ex / den
    out = jnp.where(lane == 0, i1 - N_GROUPS,
                    jnp.where(lane == 1, i2 - N_GROUPS,
                              jnp.where(lane == 2, g1, jnp.where(lane == 3, g2, 0.0))))
    route_ref[...] = out


def _post_call(x, osb, ofox, omla, d1, d2, d3, g_dil, wo, lng, lnb, wr, br, alpha, tm):
    t, d = x.shape
    hq = N_HEADS_PER_MIXER * HEAD_DIM
    row = lambda w: pl.BlockSpec((tm, w), lambda i: (i, 0))
    full = lambda a: pl.BlockSpec(a.shape, lambda i: (0, 0))
    return pl.pallas_call(
        functools.partial(_post_kernel, alpha=alpha),
        grid=(t // tm,),
        in_specs=[row(d), row(hq), row(hq), row(hq), row(2 * hq), row(2 * hq), row(2 * hq),
                  full(g_dil), full(wo), full(lng), full(lnb), full(wr), full(br)],
        out_specs=[row(d), row(LANES)],
        out_shape=[jax.ShapeDtypeStruct((t, d), F32), jax.ShapeDtypeStruct((t, LANES), F32)],
        compiler_params=_cparams(("parallel",)),
        name="post_mixer",
    )(x, osb, ofox, omla, d1, d2, d3, g_dil, wo, lng, lnb, wr, br)


DMA_UNROLL = 8


def _moe_kernel(ce_ref, nv_ref, src_ref, srcn_ref, dst_ref, x_hbm, w1_ref, w3_ref, w2_ref, y_hbm,
                xs_ref, ys_ref, gsem, ssem):
    c = pl.program_id(0)
    nc = pl.num_programs(0)
    slot = c % 2
    other = 1 - slot
    nv = nv_ref[c]
    nv_next = jnp.where(c + 1 < nc, nv_ref[jnp.minimum(c + 1, nc - 1)], 0)
    nv_prev = jnp.where(c > 0, nv_ref[jnp.maximum(c - 1, 0)], 0)

    def gather_copy(idx_ref, i, s):
        return pltpu.make_async_copy(x_hbm.at[pl.ds(idx_ref[0, 0, i], 1), :],
                                     xs_ref.at[s, pl.ds(i, 1), :], gsem.at[s])

    def scatter_copy(i, s):
        return pltpu.make_async_copy(ys_ref.at[s, pl.ds(i, 1), :],
                                     y_hbm.at[pl.ds(dst_ref[0, 0, i], 1), :], ssem.at[s])

    def start_rows(n, make):
        ngrp = n // DMA_UNROLL

        def grp(g, _):
            for k in range(DMA_UNROLL):
                make(g * DMA_UNROLL + k).start()
            return 0

        def one(i, _):
            make(i).start()
            return 0

        lax.fori_loop(0, ngrp, grp, 0)
        lax.fori_loop(ngrp * DMA_UNROLL, n, one, 0)

    def wait_rows(n, make_row, make_block):
        @pl.when(n == MOE_BLOCK)
        def _():
            make_block().wait()

        @pl.when(n < MOE_BLOCK)
        def _():
            def one(i, _):
                make_row(i).wait()
                return 0
            lax.fori_loop(0, n, one, 0)

    def gather_block(s):
        return pltpu.make_async_copy(x_hbm.at[pl.ds(0, MOE_BLOCK), :], xs_ref.at[s], gsem.at[s])

    def scatter_block(s):
        return pltpu.make_async_copy(ys_ref.at[s], y_hbm.at[pl.ds(0, MOE_BLOCK), :], ssem.at[s])

    @pl.when(c == 0)
    def _():
        xs_ref[...] = jnp.zeros_like(xs_ref)
        start_rows(nv, lambda i: gather_copy(src_ref, i, 0))

    start_rows(nv_next, lambda i: gather_copy(srcn_ref, i, other))
    wait_rows(nv, lambda i: gather_copy(src_ref, i, slot), lambda: gather_block(slot))

    @pl.when(nv > 0)
    def _():
        xb = xs_ref[slot].astype(BF16)
        a = _dot(xb, w1_ref[0])
        b = _dot(xb, w3_ref[0])
        hid = (a / (1.0 + jnp.exp(-a)) * b).astype(BF16)
        ys_ref[slot] = _dot(hid, w2_ref[0])

    wait_rows(nv_prev, lambda i: scatter_copy(i, other), lambda: scatter_block(other))
    start_rows(nv, lambda i: scatter_copy(i, slot))

    @pl.when(c == nc - 1)
    def _():
        wait_rows(nv, lambda i: scatter_copy(i, slot), lambda: scatter_block(slot))


def _moe_call(chunk_expert, n_valid, src, dst, x1, w1, w3, w2, n_rows_out):
    n_chunks = chunk_expert.shape[0]
    t, d = x1.shape
    de = w1.shape[-1]
    grid_spec = pltpu.PrefetchScalarGridSpec(
        num_scalar_prefetch=2,
        grid=(n_chunks,),
        in_specs=[
            pl.BlockSpec((1, 1, MOE_BLOCK), lambda c, ce, nv: (c, 0, 0), memory_space=pltpu.SMEM),
            pl.BlockSpec((1, 1, MOE_BLOCK), lambda c, ce, nv: (jnp.minimum(c + 1, n_chunks - 1), 0, 0),
                         memory_space=pltpu.SMEM),
            pl.BlockSpec((1, 1, MOE_BLOCK), lambda c, ce, nv: (c, 0, 0), memory_space=pltpu.SMEM),
            pl.BlockSpec(memory_space=pl.ANY),
            pl.BlockSpec((1, d, de), lambda c, ce, nv: (ce[c], 0, 0)),
            pl.BlockSpec((1, d, de), lambda c, ce, nv: (ce[c], 0, 0)),
            pl.BlockSpec((1, de, d), lambda c, ce, nv: (ce[c], 0, 0)),
        ],
        out_specs=pl.BlockSpec(memory_space=pl.ANY),
        scratch_shapes=[
            pltpu.VMEM((2, MOE_BLOCK, d), F32),
            pltpu.VMEM((2, MOE_BLOCK, d), F32),
            pltpu.SemaphoreType.DMA((2,)),
            pltpu.SemaphoreType.DMA((2,)),
        ],
    )
    return pl.pallas_call(
        _moe_kernel,
        grid_spec=grid_spec,
        out_shape=jax.ShapeDtypeStruct((n_rows_out, d), F32),
        compiler_params=_cparams(("arbitrary",)),
        name="moe_experts",
    )(chunk_expert, n_valid, src, src, dst, x1, w1, w3, w2)


def _combine_kernel(x1_ref, y_ref, route_ref, lng_ref, lnb_ref, o_ref, *, alpha):
    d = x1_ref.shape[-1]
    g1 = route_ref[:, 2:3]
    g2 = route_ref[:, 3:4]
    m = g1 * y_ref[:, 0:d] + g2 * y_ref[:, d:2 * d]
    o_ref[...] = _layernorm(alpha * x1_ref[...] + m, lng_ref[...], lnb_ref[...])


def _combine_call(x1, y2, route, lng, lnb, alpha, tm):
    t, d = x1.shape
    full = lambda a: pl.BlockSpec(a.shape, lambda i: (0, 0))
    return pl.pallas_call(
        functools.partial(_combine_kernel, alpha=alpha),
        grid=(t // tm,),
        in_specs=[pl.BlockSpec((tm, d), lambda i: (i, 0)),
                  pl.BlockSpec((tm, 2 * d), lambda i: (i, 0)),
                  pl.BlockSpec((tm, LANES), lambda i: (i, 0)),
                  full(lng), full(lnb)],
        out_specs=pl.BlockSpec((tm, d), lambda i: (i, 0)),
        out_shape=jax.ShapeDtypeStruct((t, d), F32),
        compiler_params=_cparams(("parallel",)),
        name="moe_combine",
    )(x1, y2, route, lng, lnb)


def _rope_tables(seq, dim, lane_lo):
    half = dim // 2
    inv_freq = ROPE_THETA ** (-jnp.arange(half, dtype=F32) / half)
    ang = jnp.arange(seq, dtype=F32)[:, None] * inv_freq[None, :]
    cos = jnp.concatenate([jnp.cos(ang), jnp.cos(ang)], -1)
    sin = jnp.concatenate([jnp.sin(ang), jnp.sin(ang)], -1)
    if lane_lo == 0:
        reps = LANES // dim
        return jnp.tile(cos, (1, reps)), jnp.tile(sin, (1, reps))
    cos_t = jnp.ones((seq, LANES), F32).at[:, lane_lo:lane_lo + dim].set(cos)
    sin_t = jnp.zeros((seq, LANES), F32).at[:, lane_lo:lane_lo + dim].set(sin)
    return cos_t, sin_t


def _dispatch_tables(route, n_tok):
    expert_id = route[:, 0:TOP_K].astype(jnp.int32).reshape(-1)
    n_assign = n_tok * TOP_K
    n_slots = n_assign + N_EXPERTS * MOE_BLOCK
    n_chunks = n_slots // MOE_BLOCK
    onehot = (expert_id[:, None] == jnp.arange(N_EXPERTS, dtype=jnp.int32)[None, :]).astype(jnp.int32)
    ranks = jnp.cumsum(onehot, axis=0) - onehot
    rank = jnp.sum(ranks * onehot, axis=1)
    counts = jnp.sum(onehot, axis=0)
    padded = (counts + MOE_BLOCK - 1) // MOE_BLOCK * MOE_BLOCK
    pad_end = jnp.cumsum(padded)
    pad_start = pad_end - padded
    dest = pad_start[expert_id] + rank
    assign = jnp.arange(n_assign, dtype=jnp.int32)
    dst = jnp.zeros((n_slots,), jnp.int32).at[dest].set(assign)
    src = dst // TOP_K
    chunk_start = jnp.arange(n_chunks, dtype=jnp.int32) * MOE_BLOCK
    chunk_expert = jnp.minimum(jnp.searchsorted(pad_end, chunk_start, side="right"),
                               N_EXPERTS - 1).astype(jnp.int32)
    n_valid = jnp.clip(pad_start[chunk_expert] + counts[chunk_expert] - chunk_start,
                       0, MOE_BLOCK).astype(jnp.int32)
    return (chunk_expert, n_valid, src.reshape(n_chunks, 1, MOE_BLOCK),
            dst.reshape(n_chunks, 1, MOE_BLOCK), n_assign)


def _pick_tile(n, pref):
    t = pref
    while n % t:
        t //= 2
    return t


def kernel(x, w_in, b_forget, g_cq, g_ckv, w_uq, w_ukv, g_head, w_out, ln1_g, ln1_b,
           w_group, b_group, w_expert, b_expert, w1, w3, w2, ln2_g, ln2_b):
    bsz, seq, d = x.shape
    depth = w_in.shape[0]
    t = bsz * seq
    alpha = (2.0 * depth) ** 0.25
    hq = N_HEADS_PER_MIXER * HEAD_DIM
    qk_scale = HEAD_DIM ** -0.5
    mla_scale = (MLA_NOPE + MLA_ROPE) ** -0.5
    win = DIL_BRANCHES[0][0]
    assert all(w // r == win for w, r in DIL_BRANCHES)
    assert seq % (DIL_BRANCHES[-1][1] * win) == 0 and d % LANES == 0

    cos64, sin64 = _rope_tables(seq, HEAD_DIM, 0)
    cos_m, sin_m = _rope_tables(seq, MLA_ROPE, MLA_NOPE)
    tq = _pick_tile(seq, 256)
    idx = jnp.arange(tq)
    u_sb = (idx[:, None] > idx[None, :]).astype(BF16)

    for l in range(depth):
        wl = w_in[l]
        o_fox, o_mla, o_dil = N_SB, N_SB + N_FOX_QKV + N_HEADS_PER_MIXER, N_SB + N_FOX_QKV + N_HEADS_PER_MIXER + N_MLA
        qs = lambda w: w.at[:, 0:hq].multiply(qk_scale)
        w_sb = qs(wl[:, 0:N_SB])
        w_fx = qs(wl[:, o_fox:o_fox + N_FOX_QKV])
        w_f = wl[:, o_fox + N_FOX_QKV:o_mla]
        w_ml = wl[:, o_mla:o_dil]
        wd = wl[:, o_dil:].reshape(d, 3, len(DIL_BRANCHES), hq)
        w_br = [qs(jnp.concatenate([wd[:, 0, g], wd[:, 1, g], wd[:, 2, g]], axis=1)) for g in range(len(DIL_BRANCHES))]
        w_misc = jnp.concatenate([w_ml, w_f, jnp.zeros((d, MISC_W - N_MLA - N_HEADS_PER_MIXER), F32)], axis=1)
        w_tok = jnp.concatenate([w_sb, w_fx, w_br[0], w_misc], axis=1).astype(BF16)

        wq = jnp.pad(w_uq[l].reshape(MLA_Q_LORA, N_HEADS_PER_MIXER, MLA_NOPE + MLA_ROPE),
                     ((0, 0), (0, 0), (0, LANES - MLA_NOPE - MLA_ROPE))).reshape(MLA_Q_LORA, -1).astype(BF16)
        wkv = w_ukv[l].reshape(MLA_KV_LORA, N_HEADS_PER_MIXER, MLA_NOPE + HEAD_DIM)
        wk = jnp.pad(wkv[:, :, :MLA_NOPE], ((0, 0), (0, 0), (0, LANES - MLA_NOPE))).reshape(MLA_KV_LORA, -1).astype(BF16)
        wv = wkv[:, :, MLA_NOPE:].reshape(MLA_KV_LORA, -1).astype(BF16)
        g_flat = g_head[l].reshape(1, -1)
        wr = jnp.concatenate([w_group[l], w_expert[l],
                              jnp.zeros((d, LANES - N_GROUPS - N_EXPERTS), F32)], axis=1)
        br = jnp.concatenate([b_group[l], b_expert[l],
                              jnp.zeros((LANES - N_GROUPS - N_EXPERTS,), F32)]).reshape(1, LANES)

        tm = _pick_tile(seq, 512)
        sb, fx, d1, misc = _proj_call(
            x, w_tok, cos64, sin64,
            ((N_SB, 0, BF16), (N_FOX_QKV, 0, BF16), (N_BRANCH, 2 * hq, BF16), (MISC_W, 0, F32)), 1, tm)
        sb = sb.reshape(t, N_SB)
        fx = fx.reshape(t, N_FOX_QKV)
        misc = misc.reshape(t, MISC_W)

        o_sb = _attn_call("sb", sb, sb, sb, 0, 1, 2, g_flat, 0, bsz, seq, tq, extra=(u_sb,))
        neg_c = _fox_c_call(misc, b_forget[l], bsz, seq)
        o_fx = _attn_call("fox", fx, fx, fx, 0, 1, 2, g_flat, 1, bsz, seq, tq, extra=(neg_c,))
        mq, mk, mv = _mla_prep_call(misc, g_cq[l].reshape(1, -1), g_ckv[l].reshape(1, -1), wq, wk, wv,
                                    cos_m, sin_m, seq, tm)
        o_ml = _attn_call("mla", mq, mk, mv, 0, 0, 0, g_flat, 2, bsz, seq, tq, scale=mla_scale)

        dil = []
        for g, (_, r) in enumerate(DIL_BRANCHES):
            n = seq // r
            if r == 1:
                qkv = d1
            else:
                (qkv,) = _proj_call(x, w_br[g].astype(BF16), cos64, sin64,
                                    ((N_BRANCH, 2 * hq, BF16),), r, _pick_tile(n, 512))
            og = _dil_call(qkv, _pick_tile(n, 1024), win)
            dil.append(og.reshape(t, 2 * hq))

        x1, route = _post_call(
            x.reshape(t, d), o_sb, o_fx, o_ml, dil[0], dil[1], dil[2], g_flat[:, 3 * hq:], w_out[l].astype(BF16),
            ln1_g[l].reshape(1, d), ln1_b[l].reshape(1, d), wr, br, alpha, _pick_tile(t, 256))

        chunk_expert, n_valid, src, dst, n_rows = _dispatch_tables(route, t)
        y = _moe_call(chunk_expert, n_valid, src, dst, x1,
                      w1[l].astype(BF16), w3[l].astype(BF16), w2[l].astype(BF16), n_rows)
        x = _combine_call(x1, y.reshape(n_rows // TOP_K, TOP_K * d), route,
                          ln2_g[l].reshape(1, d), ln2_b[l].reshape(1, d), alpha, _pick_tile(t, 256)).reshape(bsz, seq, d)
    return x
```

```python
import functools

import jax
import jax.numpy as jnp
import numpy as np
from jax import lax
from jax.experimental import pallas as pl
from jax.experimental.pallas import tpu as pltpu

F32 = jnp.float32
BF16 = jnp.bfloat16

HEAD_DIM = 64
N_HEADS_PER_MIXER = 4
MLA_Q_LORA = 256
MLA_KV_LORA = 128
MLA_NOPE = 64
MLA_ROPE = 32
DIL_BRANCHES = ((128, 1), (512, 4), (2048, 16))
ROPE_THETA = 10000.0
N_GROUPS = 4
EXPERTS_PER_GROUP = 4
N_EXPERTS = N_GROUPS * EXPERTS_PER_GROUP
TOP_K = 2
MOE_BLOCK = 256
LN_EPS = 1e-5
RMS_EPS = 1e-6

LANES = 128
VMEM_LIMIT_BYTES = 56 * 1024 * 1024

N_SB = 3 * N_HEADS_PER_MIXER * HEAD_DIM
N_FOX_QKV = 3 * N_HEADS_PER_MIXER * HEAD_DIM
N_MLA = MLA_Q_LORA + MLA_KV_LORA + MLA_ROPE
N_BRANCH = 3 * N_HEADS_PER_MIXER * HEAD_DIM
MISC_W = 512
F_COL = N_MLA


def _cparams(sem):
    return pltpu.CompilerParams(dimension_semantics=sem, vmem_limit_bytes=VMEM_LIMIT_BYTES)


def _split3(a):
    hi = a.astype(BF16)
    r1 = a - hi.astype(F32)
    mid = r1.astype(BF16)
    lo = (r1 - mid.astype(F32)).astype(BF16)
    return hi, mid, lo


def _dot(a, b):
    return jnp.dot(a, b, preferred_element_type=F32)


def _dot_nt(a, b):
    return lax.dot_general(a, b, (((1,), (1,)), ((), ())), preferred_element_type=F32)


def _dot_exact_rhs(a, u):
    hi, mid, lo = _split3(a)
    return _dot(hi, u) + _dot(mid, u) + _dot(lo, u)


def _dot_f32(a, b):
    ah, am, al = _split3(a)
    bh, bm, bl = _split3(b)
    return (_dot(ah, bh) + (_dot(ah, bm) + _dot(am, bh))
            + (_dot(am, bm) + _dot(ah, bl) + _dot(al, bh)))


def _lane_iota(shape):
    return lax.broadcasted_iota(jnp.int32, shape, len(shape) - 1)


def _rotate_half(y, half):
    lane = _lane_iota(y.shape)
    fwd = pltpu.roll(y, half, 1)
    bwd = pltpu.roll(y, LANES - half, 1)
    return jnp.where((lane % (2 * half)) < half, -bwd, fwd)


def _log_sigmoid_pair(z):
    sp = jnp.log(1.0 + jnp.exp(-jnp.abs(z)))
    return jnp.minimum(z, 0.0) - sp, -jnp.maximum(z, 0.0) - sp


def _head_rms(o, g):
    lane = _lane_iota(o.shape)
    first = lane < HEAD_DIM
    sq = o * o
    ss_a = jnp.sum(jnp.where(first, sq, 0.0), axis=-1, keepdims=True)
    ss_b = jnp.sum(jnp.where(first, 0.0, sq), axis=-1, keepdims=True)
    ms = jnp.where(first, ss_a, ss_b) * (1.0 / HEAD_DIM)
    return o * lax.rsqrt(ms + RMS_EPS) * g


def _layernorm(y, g, b):
    mu = jnp.mean(y, axis=-1, keepdims=True)
    d = y - mu
    var = jnp.mean(d * d, axis=-1, keepdims=True)
    return d * lax.rsqrt(var + LN_EPS) * g + b


def _proj_kernel(x_ref, w_ref, cos_ref, sin_ref, *refs, outs):
    out_refs, stage_ref = refs[:len(outs)], refs[len(outs)]
    tm = x_ref.shape[1]
    xb = x_ref[0].astype(BF16)
    col = 0
    slab = 0
    for o_ref, (width, n_rope, _, r) in zip(out_refs, outs):
        for c in range(0, width, 2 * LANES):
            cw = min(2 * LANES, width - c)
            y = _dot(xb, w_ref[:, col + c:col + c + cw])
            for s in range(0, cw, LANES):
                ys = y[:, s:s + LANES]
                if c + s < n_rope:
                    ys = ys * cos_ref[...] + _rotate_half(ys, HEAD_DIM // 2) * sin_ref[...]
                if r == 1:
                    o_ref[0, 0, :, c + s:c + s + LANES] = ys.astype(o_ref.dtype)
                else:
                    st = stage_ref.at[slab % stage_ref.shape[0]]
                    slab += 1
                    st[...] = ys
                    for p in range(r):
                        o_ref[0, p, :, c + s:c + s + LANES] = (
                            st[pl.ds(p, tm // r, stride=r), :].astype(o_ref.dtype))
        col += width


PROJ_STAGE_SLABS = 4


def _proj_call(x, w, cos, sin, outs, tm):
    bsz, seq, d = x.shape
    nt = seq // tm
    in_specs = [
        pl.BlockSpec((1, tm, d), lambda b, i: (b, i, 0)),
        pl.BlockSpec(w.shape, lambda b, i: (0, 0)),
        pl.BlockSpec((tm, LANES), lambda b, i: (i, 0)),
        pl.BlockSpec((tm, LANES), lambda b, i: (i, 0)),
    ]
    out_specs = [pl.BlockSpec((1, r, tm // r, wd), lambda b, i: (b, 0, i, 0)) for wd, _, _, r in outs]
    out_shape = [jax.ShapeDtypeStruct((bsz, r, seq // r, wd), dt) for wd, _, dt, r in outs]
    return pl.pallas_call(
        functools.partial(_proj_kernel, outs=outs),
        grid=(bsz, nt), in_specs=in_specs, out_specs=out_specs, out_shape=out_shape,
        scratch_shapes=[pltpu.VMEM((PROJ_STAGE_SLABS, tm, LANES), F32)],
        compiler_params=_cparams(("parallel", "parallel")),
        name="proj",
    )(x, w, cos, sin)


def _fox_c_kernel(misc_ref, bias_ref, uinc_ref, ones_ref, out_ref, *, seq):
    lane0 = F_COL - 3 * LANES
    nblk = seq // LANES

    def body(j, carry):
        r0 = pl.multiple_of(j * LANES, LANES)
        f = misc_ref[pl.ds(r0, LANES), :] + bias_ref[...]
        lf, _ = _log_sigmoid_pair(f)
        lft = lf.T
        csum = _dot_exact_rhs(lft, uinc_ref[...]) + carry
        tot = _dot_exact_rhs(lft, ones_ref[...])
        out_ref[0, :, pl.ds(r0, LANES)] = -csum[lane0:lane0 + N_HEADS_PER_MIXER, :]
        return carry + tot

    lax.fori_loop(0, nblk, body, jnp.zeros((LANES, LANES), F32))


def _fox_c_call(misc, b_forget, bsz, seq):
    bias = jnp.zeros((1, LANES), F32).at[0, F_COL - 3 * LANES:F_COL - 3 * LANES + N_HEADS_PER_MIXER].set(b_forget)
    idx = jnp.arange(LANES)
    uinc = (idx[:, None] <= idx[None, :]).astype(BF16)
    ones = jnp.ones((LANES, LANES), BF16)
    return pl.pallas_call(
        functools.partial(_fox_c_kernel, seq=seq),
        grid=(bsz,),
        in_specs=[
            pl.BlockSpec((seq, LANES), lambda b: (b, 3)),
            pl.BlockSpec((1, LANES), lambda b: (0, 0)),
            pl.BlockSpec((LANES, LANES), lambda b: (0, 0)),
            pl.BlockSpec((LANES, LANES), lambda b: (0, 0)),
        ],
        out_specs=pl.BlockSpec((1, N_HEADS_PER_MIXER, seq), lambda b: (b, 0, 0)),
        out_shape=jax.ShapeDtypeStruct((bsz, N_HEADS_PER_MIXER, seq), F32),
        compiler_params=_cparams(("parallel",)),
        name="fox_c",
    )(misc, bias, uinc, ones)


def _mla_prep_kernel(misc_ref, gq_ref, gkv_ref, wq_ref, wk_ref, wv_ref, cos_ref, sin_ref,
                     q_ref, k_ref, v_ref):
    def rms(x, g):
        return x * lax.rsqrt(jnp.mean(x * x, axis=-1, keepdims=True) + RMS_EPS) * g

    cq = rms(misc_ref[:, 0:MLA_Q_LORA], gq_ref[...]).astype(BF16)
    ckv = rms(misc_ref[:, MLA_Q_LORA:MLA_Q_LORA + MLA_KV_LORA], gkv_ref[...]).astype(BF16)
    kr_blk = misc_ref[:, 3 * LANES:4 * LANES]
    lane = _lane_iota(kr_blk.shape)
    in_rope = (lane >= MLA_NOPE) & (lane < MLA_NOPE + MLA_ROPE)
    kr = jnp.where(in_rope, pltpu.roll(kr_blk, MLA_NOPE, 1), 0.0)
    cos = cos_ref[...]
    sin = sin_ref[...]

    def rope(y):
        return y * cos + _rotate_half(y, MLA_ROPE // 2) * sin

    q = _dot(cq, wq_ref[...])
    k = _dot(ckv, wk_ref[...])
    for h in range(N_HEADS_PER_MIXER):
        sl = slice(h * LANES, (h + 1) * LANES)
        q_ref[:, sl] = rope(q[:, sl]).astype(BF16)
        k_ref[:, sl] = rope(k[:, sl] + kr).astype(BF16)
    v_ref[...] = _dot(ckv, wv_ref[...]).astype(BF16)


def _mla_prep_call(misc, g_cq, g_ckv, wq, wk, wv, cos, sin, seq, tm):
    t = misc.shape[0]
    nper = seq // tm
    hw = N_HEADS_PER_MIXER * LANES
    vw = N_HEADS_PER_MIXER * HEAD_DIM
    full = lambda a: pl.BlockSpec(a.shape, lambda i: (0, 0))
    return pl.pallas_call(
        _mla_prep_kernel,
        grid=(t // tm,),
        in_specs=[
            pl.BlockSpec((tm, MISC_W), lambda i: (i, 0)),
            full(g_cq), full(g_ckv), full(wq), full(wk), full(wv),
            pl.BlockSpec((tm, LANES), lambda i: (i % nper, 0)),
            pl.BlockSpec((tm, LANES), lambda i: (i % nper, 0)),
        ],
        out_specs=[
            pl.BlockSpec((tm, hw), lambda i: (i, 0)),
            pl.BlockSpec((tm, hw), lambda i: (i, 0)),
            pl.BlockSpec((tm, vw), lambda i: (i, 0)),
        ],
        out_shape=[
            jax.ShapeDtypeStruct((t, hw), BF16),
            jax.ShapeDtypeStruct((t, hw), BF16),
            jax.ShapeDtypeStruct((t, vw), BF16),
        ],
        compiler_params=_cparams(("parallel",)),
        name="mla_prep",
    )(misc, g_cq, g_ckv, wq, wk, wv, cos, sin)


def _attn_kernel(*refs, mode, tq, scale):
    if mode == "sb":
        q_ref, k_ref, v_ref, g_ref, u_ref, o_ref = refs
    elif mode == "fox":
        q_ref, k_ref, v_ref, g_ref, nc_ref, o_ref = refs
    else:
        q_ref, k_ref, v_ref, g_ref, o_ref = refs
    tk = tq
    nh = N_HEADS_PER_MIXER
    i = pl.program_id(1)
    lane = _lane_iota((tq, LANES))
    first = lane < HEAD_DIM
    row = lax.broadcasted_iota(jnp.int32, (tq, tk), 0)
    colm = lax.broadcasted_iota(jnp.int32, (tq, tk), 1)

    q_heads = []
    for h in range(nh):
        if mode == "mla":
            q_heads.append(q_ref[:, h * LANES:(h + 1) * LANES])
        else:
            q2 = q_ref[:, (h // 2) * LANES:(h // 2 + 1) * LANES]
            zero = jnp.zeros_like(q2)
            q_heads.append(jnp.where(first, q2, zero) if h % 2 == 0 else jnp.where(first, zero, q2))

    def k_head(j, h):
        r0 = pl.multiple_of(j * tk, tk)
        kb = h if mode == "mla" else h // 2
        return k_ref[pl.ds(r0, tk), kb * LANES:(kb + 1) * LANES]

    def v_pair(j, p):
        r0 = pl.multiple_of(j * tk, tk)
        return v_ref[pl.ds(r0, tk), p * LANES:(p + 1) * LANES]

    if mode == "sb":
        def step(j, carry, diag):
            accs, rs = carry
            new_accs, new_rs = [], []
            for p in range(nh // 2):
                v2 = v_pair(j, p)
                outs = []
                for h in (2 * p, 2 * p + 1):
                    z = _dot_nt(q_heads[h], k_head(j, h))
                    ls_pos, ls_neg = _log_sigmoid_pair(z)
                    if diag:
                        before = colm < row
                        ls_neg = jnp.where(before, ls_neg, 0.0)
                    c = _dot(ls_neg.astype(BF16), u_ref[...])
                    w = jnp.exp(ls_pos + c + rs[h])
                    if diag:
                        w = jnp.where(before, w, 0.0)
                    outs.append(_dot(w.astype(BF16), v2))
                    new_rs.append(rs[h] + (c[:, 0:1] + ls_neg[:, 0:1]))
                new_accs.append(accs[p] + jnp.where(first, outs[0], outs[1]))
            return tuple(new_accs), tuple(new_rs)

        zacc = jnp.zeros((tq, LANES), F32)
        zr = jnp.zeros((tq, 1), F32)
        carry = step(i, ((zacc,) * (nh // 2), (zr,) * nh), True)
        carry = lax.fori_loop(0, i, lambda n, c: step(i - 1 - n, c, False), carry)
        outs = carry[0]
    else:
        def step(j, carry, diag):
            accs, ms, ls = carry
            new_accs, new_ms, new_ls = [], [], []
            r0 = pl.multiple_of(j * tk, tk)
            for p in range(nh // 2):
                v2 = v_pair(j, p)
                pv, alphas = [], []
                for h in (2 * p, 2 * p + 1):
                    s = _dot_nt(q_heads[h], k_head(j, h))
                    if mode == "mla":
                        s = s * scale
                    else:
                        s = s + nc_ref[0, h:h + 1, pl.ds(r0, tk)]
                    if diag:
                        s = jnp.where(colm <= row, s, -jnp.inf)
                    m_new = jnp.maximum(ms[h], jnp.max(s, axis=-1, keepdims=True))
                    alpha = jnp.exp(ms[h] - m_new)
                    pr = jnp.exp(s - m_new)
                    new_ls.append(alpha * ls[h] + jnp.sum(pr, axis=-1, keepdims=True))
                    new_ms.append(m_new)
                    pv.append(_dot(pr.astype(BF16), v2))
                    alphas.append(alpha)
                new_accs.append(accs[p] * jnp.where(first, alphas[0], alphas[1])
                                + jnp.where(first, pv[0], pv[1]))
            return tuple(new_accs), tuple(new_ms), tuple(new_ls)

        neg = jnp.full((tq, 1), -jnp.inf, F32)
        zl = jnp.zeros((tq, 1), F32)
        zacc = jnp.zeros((tq, LANES), F32)
        carry = step(i, ((zacc,) * (nh // 2), (neg,) * nh, (zl,) * nh), True)
        carry = lax.fori_loop(0, i, lambda n, c: step(n, c, False), carry)
        accs, _, ls = carry
        outs = [accs[p] / jnp.where(first, ls[2 * p], ls[2 * p + 1]) for p in range(nh // 2)]
    for p in range(nh // 2):
        sl = slice(p * LANES, (p + 1) * LANES)
        o_ref[:, sl] = _head_rms(outs[p], g_ref[:, sl]).astype(o_ref.dtype)


def _attn_call(mode, q, k, v, qcol, kcol, vcol, g_flat, gcol, bsz, seq, tq, extra=(), scale=1.0):
    t = bsz * seq
    nq = seq // tq
    hq = N_HEADS_PER_MIXER * HEAD_DIM
    qw = N_HEADS_PER_MIXER * LANES if mode == "mla" else hq
    in_specs = [
        pl.BlockSpec((tq, qw), lambda b, i: (b * nq + i, qcol)),
        pl.BlockSpec((seq, qw), lambda b, i: (b, kcol)),
        pl.BlockSpec((seq, hq), lambda b, i: (b, vcol)),
        pl.BlockSpec((1, hq), lambda b, i: (0, gcol)),
    ]
    args = [q, k, v, g_flat]
    if mode == "sb":
        (u,) = extra
        in_specs.append(pl.BlockSpec(u.shape, lambda b, i: (0, 0)))
        args.append(u)
    elif mode == "fox":
        (nc,) = extra
        in_specs.append(pl.BlockSpec((1, N_HEADS_PER_MIXER, seq), lambda b, i: (b, 0, 0)))
        args.append(nc)
    return pl.pallas_call(
        functools.partial(_attn_kernel, mode=mode, tq=tq, scale=scale),
        grid=(bsz, nq),
        in_specs=in_specs,
        out_specs=pl.BlockSpec((tq, hq), lambda b, i: (b * nq + i, 0)),
        out_shape=jax.ShapeDtypeStruct((t, hq), BF16),
        compiler_params=_cparams(("parallel", "arbitrary")),
        name=f"attn_{mode}",
    )(*args)


def _dil_kernel(qkv_ref, o_ref, *, tqs, win):
    tb = win
    c = pl.program_id(2)
    lane = _lane_iota((tb, LANES))
    first = lane < HEAD_DIM
    row = lax.broadcasted_iota(jnp.int32, (tb, tb), 0)
    colm = lax.broadcasted_iota(jnp.int32, (tb, tb), 1)
    hq = N_HEADS_PER_MIXER * HEAD_DIM

    def tile(it, _):
        gi = c * (tqs // tb) + it
        r0 = pl.multiple_of(gi * tb, tb)
        rp = pl.multiple_of(jnp.maximum(gi - 1, 0) * tb, tb)
        no_prev = jnp.where(gi > 0, 0, tb)
        o0 = pl.multiple_of(it * tb, tb)
        for p in range(N_HEADS_PER_MIXER // 2):
            q2 = qkv_ref[0, 0, pl.ds(r0, tb), p * LANES:(p + 1) * LANES]
            kd = qkv_ref[0, 0, pl.ds(r0, tb), hq + p * LANES:hq + (p + 1) * LANES]
            kp = qkv_ref[0, 0, pl.ds(rp, tb), hq + p * LANES:hq + (p + 1) * LANES]
            vd = qkv_ref[0, 0, pl.ds(r0, tb), 2 * hq + p * LANES:2 * hq + (p + 1) * LANES]
            vp = qkv_ref[0, 0, pl.ds(rp, tb), 2 * hq + p * LANES:2 * hq + (p + 1) * LANES]
            zero = jnp.zeros_like(q2)
            outs = []
            lses = []
            for hh in range(2):
                qh = jnp.where(first, q2, zero) if hh == 0 else jnp.where(first, zero, q2)
                sd = jnp.where(colm <= row, _dot_nt(qh, kd), -jnp.inf)
                sp = jnp.where(colm >= row + no_prev, _dot_nt(qh, kp), -jnp.inf)
                m = jnp.maximum(jnp.max(sd, axis=-1, keepdims=True), jnp.max(sp, axis=-1, keepdims=True))
                pd = jnp.exp(sd - m)
                pp = jnp.exp(sp - m)
                l = jnp.sum(pd, axis=-1, keepdims=True) + jnp.sum(pp, axis=-1, keepdims=True)
                inv = 1.0 / l
                acc = _dot((pd * inv).astype(BF16), vd) + _dot((pp * inv).astype(BF16), vp)
                outs.append(acc)
                lses.append(m + jnp.log(l))
            o_ref[0, pl.ds(o0, tb), p * LANES:(p + 1) * LANES] = jnp.where(first, outs[0], outs[1])
            o_ref[0, pl.ds(o0, tb), hq + p * LANES:hq + (p + 1) * LANES] = jnp.where(first, lses[0], lses[1])
        return 0

    lax.fori_loop(0, tqs // tb, tile, 0, unroll=min(4, tqs // tb))


def _dil_call(qkv, tqs, win):
    bsz, r, n, w = qkv.shape
    hq = N_HEADS_PER_MIXER * HEAD_DIM
    return pl.pallas_call(
        functools.partial(_dil_kernel, tqs=tqs, win=win),
        grid=(bsz, r, n // tqs),
        in_specs=[pl.BlockSpec((1, 1, n, w), lambda b, p, c: (b, p, 0, 0))],
        out_specs=pl.BlockSpec((1, tqs, 2 * hq), lambda b, p, c: (b, c, p)),
        out_shape=jax.ShapeDtypeStruct((bsz, n, r * 2 * hq), F32),
        compiler_params=_cparams(("parallel", "parallel", "arbitrary")),
        name=f"dil_r{r}",
    )(qkv)


def _post_kernel(x_ref, osb_ref, ofox_ref, omla_ref, d1_ref, d2_ref, d3_ref, gd_ref, wo_ref,
                 lng_ref, lnb_ref, wr_ref, br_ref, x1_ref, route_ref, *, alpha):
    hq = N_HEADS_PER_MIXER * HEAD_DIM
    h = _dot(osb_ref[...], wo_ref[0:hq, :])
    h += _dot(ofox_ref[...], wo_ref[hq:2 * hq, :])
    h += _dot(omla_ref[...], wo_ref[2 * hq:3 * hq, :])
    for p in range(N_HEADS_PER_MIXER // 2):
        sl = slice(p * LANES, (p + 1) * LANES)
        ll = slice(hq + p * LANES, hq + (p + 1) * LANES)
        l1, l2, l3 = d1_ref[:, ll], d2_ref[:, ll], d3_ref[:, ll]
        m = jnp.maximum(jnp.maximum(l1, l2), l3)
        e1, e2, e3 = jnp.exp(l1 - m), jnp.exp(l2 - m), jnp.exp(l3 - m)
        inv = 1.0 / (e1 + e2 + e3)
        od = (e1 * inv) * d1_ref[:, sl] + (e2 * inv) * d2_ref[:, sl] + (e3 * inv) * d3_ref[:, sl]
        od = _head_rms(od, gd_ref[:, sl]).astype(BF16)
        h += _dot(od, wo_ref[3 * hq + p * LANES:3 * hq + (p + 1) * LANES, :])
    x1 = _layernorm(alpha * x_ref[...] + h, lng_ref[...], lnb_ref[...])
    x1_ref[...] = x1

    logits = _dot_f32(x1, wr_ref[...]) + br_ref[...]
    lane = _lane_iota(logits.shape)
    lanef = lane.astype(F32)
    big = float(LANES)
    ninf = -jnp.inf
    gl = jnp.where(lane < N_GROUPS, logits, ninf)
    gmax = jnp.max(gl, axis=-1, keepdims=True)
    gsel = jnp.min(jnp.where(gl == gmax, lanef, big), axis=-1, keepdims=True)
    gw = 1.0 / jnp.sum(jnp.exp(gl - gmax), axis=-1, keepdims=True)
    e_lo = N_GROUPS + EXPERTS_PER_GROUP * gsel
    in_grp = (lanef >= e_lo) & (lanef < e_lo + EXPERTS_PER_GROUP)
    el = jnp.where(in_grp, logits, ninf)
    t1 = jnp.max(el, axis=-1, keepdims=True)
    i1 = jnp.min(jnp.where(el == t1, lanef, big), axis=-1, keepdims=True)
    el2 = jnp.where(lanef == i1, ninf, el)
    t2 = jnp.max(el2, axis=-1, keepdims=True)
    i2 = jnp.min(jnp.where(el2 == t2, lanef, big), axis=-1, keepdims=True)
    ex = jnp.exp(t2 - t1)
    den = 1.0 + ex
    g1 = gw / den
    g2 = gw * ex / den
    out = jnp.where(lane == 0, i1 - N_GROUPS,
                    jnp.where(lane == 1, i2 - N_GROUPS,
                              jnp.where(lane == 2, g1, jnp.where(lane == 3, g2, 0.0))))
    route_ref[...] = out


def _post_call(x, osb, ofox, omla, d1, d2, d3, g_dil, wo, lng, lnb, wr, br, alpha, tm):
    t, d = x.shape
    hq = N_HEADS_PER_MIXER * HEAD_DIM
    row = lambda w: pl.BlockSpec((tm, w), lambda i: (i, 0))
    full = lambda a: pl.BlockSpec(a.shape, lambda i: (0, 0))
    return pl.pallas_call(
        functools.partial(_post_kernel, alpha=alpha),
        grid=(t // tm,),
        in_specs=[row(d), row(hq), row(hq), row(hq), row(2 * hq), row(2 * hq), row(2 * hq),
                  full(g_dil), full(wo), full(lng), full(lnb), full(wr), full(br)],
        out_specs=[row(d), row(LANES)],
        out_shape=[jax.ShapeDtypeStruct((t, d), F32), jax.ShapeDtypeStruct((t, LANES), F32)],
        compiler_params=_cparams(("parallel",)),
        name="post_mixer",
    )(x, osb, ofox, omla, d1, d2, d3, g_dil, wo, lng, lnb, wr, br)


DMA_UNROLL = 8


def _moe_kernel(ce_ref, nv_ref, src_ref, srcn_ref, dst_ref, x_hbm, w1_ref, w3_ref, w2_ref, y_hbm,
                xs_ref, ys_ref, gsem, ssem):
    c = pl.program_id(0)
    nc = pl.num_programs(0)
    slot = c % 2
    other = 1 - slot
    nv = nv_ref[c]
    nv_next = jnp.where(c + 1 < nc, nv_ref[jnp.minimum(c + 1, nc - 1)], 0)
    nv_prev = jnp.where(c > 0, nv_ref[jnp.maximum(c - 1, 0)], 0)

    def gather_copy(idx_ref, i, s):
        return pltpu.make_async_copy(x_hbm.at[pl.ds(idx_ref[0, 0, i], 1), :],
                                     xs_ref.at[s, pl.ds(i, 1), :], gsem.at[s])

    def scatter_copy(i, s):
        return pltpu.make_async_copy(ys_ref.at[s, pl.ds(i, 1), :],
                                     y_hbm.at[pl.ds(dst_ref[0, 0, i], 1), :], ssem.at[s])

    def start_rows(n, make):
        ngrp = n // DMA_UNROLL

        def grp(g, _):
            for k in range(DMA_UNROLL):
                make(g * DMA_UNROLL + k).start()
            return 0

        def one(i, _):
            make(i).start()
            return 0

        lax.fori_loop(0, ngrp, grp, 0)
        lax.fori_loop(ngrp * DMA_UNROLL, n, one, 0)

    def wait_rows(n, make_row, make_block):
        @pl.when(n == MOE_BLOCK)
        def _():
            make_block().wait()

        @pl.when(n < MOE_BLOCK)
        def _():
            def one(i, _):
                make_row(i).wait()
                return 0
            lax.fori_loop(0, n, one, 0)

    def gather_block(s):
        return pltpu.make_async_copy(x_hbm.at[pl.ds(0, MOE_BLOCK), :], xs_ref.at[s], gsem.at[s])

    def scatter_block(s):
        return pltpu.make_async_copy(ys_ref.at[s], y_hbm.at[pl.ds(0, MOE_BLOCK), :], ssem.at[s])

    @pl.when(c == 0)
    def _():
        xs_ref[...] = jnp.zeros_like(xs_ref)
        start_rows(nv, lambda i: gather_copy(src_ref, i, 0))

    start_rows(nv_next, lambda i: gather_copy(srcn_ref, i, other))
    wait_rows(nv, lambda i: gather_copy(src_ref, i, slot), lambda: gather_block(slot))

    @pl.when(nv > 0)
    def _():
        xb = xs_ref[slot].astype(BF16)
        a = _dot(xb, w1_ref[0])
        b = _dot(xb, w3_ref[0])
        hid = (a / (1.0 + jnp.exp(-a)) * b).astype(BF16)
        ys_ref[slot] = _dot(hid, w2_ref[0])

    wait_rows(nv_prev, lambda i: scatter_copy(i, other), lambda: scatter_block(other))
    start_rows(nv, lambda i: scatter_copy(i, slot))

    @pl.when(c == nc - 1)
    def _():
        wait_rows(nv, lambda i: scatter_copy(i, slot), lambda: scatter_block(slot))


def _moe_call(chunk_expert, n_valid, src, dst, x1, w1, w3, w2, n_rows_out):
    n_chunks = chunk_expert.shape[0]
    t, d = x1.shape
    de = w1.shape[-1]
    grid_spec = pltpu.PrefetchScalarGridSpec(
        num_scalar_prefetch=2,
        grid=(n_chunks,),
        in_specs=[
            pl.BlockSpec((1, 1, MOE_BLOCK), lambda c, ce, nv: (c, 0, 0), memory_space=pltpu.SMEM),
            pl.BlockSpec((1, 1, MOE_BLOCK), lambda c, ce, nv: (jnp.minimum(c + 1, n_chunks - 1), 0, 0),
                         memory_space=pltpu.SMEM),
            pl.BlockSpec((1, 1, MOE_BLOCK), lambda c, ce, nv: (c, 0, 0), memory_space=pltpu.SMEM),
            pl.BlockSpec(memory_space=pl.ANY),
            pl.BlockSpec((1, d, de), lambda c, ce, nv: (ce[c], 0, 0)),
            pl.BlockSpec((1, d, de), lambda c, ce, nv: (ce[c], 0, 0)),
            pl.BlockSpec((1, de, d), lambda c, ce, nv: (ce[c], 0, 0)),
        ],
        out_specs=pl.BlockSpec(memory_space=pl.ANY),
        scratch_shapes=[
            pltpu.VMEM((2, MOE_BLOCK, d), F32),
            pltpu.VMEM((2, MOE_BLOCK, d), F32),
            pltpu.SemaphoreType.DMA((2,)),
            pltpu.SemaphoreType.DMA((2,)),
        ],
    )
    return pl.pallas_call(
        _moe_kernel,
        grid_spec=grid_spec,
        out_shape=jax.ShapeDtypeStruct((n_rows_out, d), F32),
        compiler_params=_cparams(("arbitrary",)),
        name="moe_experts",
    )(chunk_expert, n_valid, src, src, dst, x1, w1, w3, w2)


def _combine_kernel(x1_ref, ya_ref, yb_ref, route_ref, lng_ref, lnb_ref, o_ref, *, alpha):
    g1 = route_ref[:, 2:3]
    g2 = route_ref[:, 3:4]
    m = g1 * ya_ref[...] + g2 * yb_ref[...]
    o_ref[...] = _layernorm(alpha * x1_ref[...] + m, lng_ref[...], lnb_ref[...])


def _combine_call(x1, y, route, lng, lnb, alpha, tm):
    t, d = x1.shape
    nt = t // tm
    full = lambda a: pl.BlockSpec(a.shape, lambda i: (0, 0))
    return pl.pallas_call(
        functools.partial(_combine_kernel, alpha=alpha),
        grid=(nt,),
        in_specs=[pl.BlockSpec((tm, d), lambda i: (i, 0)),
                  pl.BlockSpec((tm, d), lambda i: (i, 0)),
                  pl.BlockSpec((tm, d), lambda i: (i + nt, 0)),
                  pl.BlockSpec((tm, LANES), lambda i: (i, 0)),
                  full(lng), full(lnb)],
        out_specs=pl.BlockSpec((tm, d), lambda i: (i, 0)),
        out_shape=jax.ShapeDtypeStruct((t, d), F32),
        compiler_params=_cparams(("parallel",)),
        name="moe_combine",
    )(x1, y, y, route, lng, lnb)


def _rope_tables(seq, dim, lane_lo):
    half = dim // 2
    inv_freq = ROPE_THETA ** (-jnp.arange(half, dtype=F32) / half)
    ang = jnp.arange(seq, dtype=F32)[:, None] * inv_freq[None, :]
    cos = jnp.concatenate([jnp.cos(ang), jnp.cos(ang)], -1)
    sin = jnp.concatenate([jnp.sin(ang), jnp.sin(ang)], -1)
    if lane_lo == 0:
        reps = LANES // dim
        return jnp.tile(cos, (1, reps)), jnp.tile(sin, (1, reps))
    cos_t = jnp.ones((seq, LANES), F32).at[:, lane_lo:lane_lo + dim].set(cos)
    sin_t = jnp.zeros((seq, LANES), F32).at[:, lane_lo:lane_lo + dim].set(sin)
    return cos_t, sin_t


def _dispatch_tables(route, n_tok):
    expert_id = route[:, 0:TOP_K].astype(jnp.int32).reshape(-1)
    n_assign = n_tok * TOP_K
    n_slots = n_assign + N_EXPERTS * MOE_BLOCK
    n_chunks = n_slots // MOE_BLOCK
    onehot = (expert_id[:, None] == jnp.arange(N_EXPERTS, dtype=jnp.int32)[None, :]).astype(jnp.int32)
    ranks = jnp.cumsum(onehot, axis=0) - onehot
    rank = jnp.sum(ranks * onehot, axis=1)
    counts = jnp.sum(onehot, axis=0)
    padded = (counts + MOE_BLOCK - 1) // MOE_BLOCK * MOE_BLOCK
    pad_end = jnp.cumsum(padded)
    pad_start = pad_end - padded
    dest = pad_start[expert_id] + rank
    assign = jnp.arange(n_assign, dtype=jnp.int32)
    slot_assign = jnp.zeros((n_slots,), jnp.int32).at[dest].set(assign)
    src = slot_assign // TOP_K
    dst = (slot_assign % TOP_K) * n_tok + src
    chunk_start = jnp.arange(n_chunks, dtype=jnp.int32) * MOE_BLOCK
    chunk_expert = jnp.minimum(jnp.searchsorted(pad_end, chunk_start, side="right"),
                               N_EXPERTS - 1).astype(jnp.int32)
    n_valid = jnp.clip(pad_start[chunk_expert] + counts[chunk_expert] - chunk_start,
                       0, MOE_BLOCK).astype(jnp.int32)
    return (chunk_expert, n_valid, src.reshape(n_chunks, 1, MOE_BLOCK),
            dst.reshape(n_chunks, 1, MOE_BLOCK), n_assign)


def _pick_tile(n, pref):
    t = pref
    while n % t:
        t //= 2
    return t


def kernel(x, w_in, b_forget, g_cq, g_ckv, w_uq, w_ukv, g_head, w_out, ln1_g, ln1_b,
           w_group, b_group, w_expert, b_expert, w1, w3, w2, ln2_g, ln2_b):
    bsz, seq, d = x.shape
    depth = w_in.shape[0]
    t = bsz * seq
    alpha = (2.0 * depth) ** 0.25
    hq = N_HEADS_PER_MIXER * HEAD_DIM
    qk_scale = HEAD_DIM ** -0.5
    mla_scale = (MLA_NOPE + MLA_ROPE) ** -0.5
    win = DIL_BRANCHES[0][0]
    assert all(w // r == win for w, r in DIL_BRANCHES)
    assert seq % (DIL_BRANCHES[-1][1] * win) == 0 and d % LANES == 0

    cos64, sin64 = _rope_tables(seq, HEAD_DIM, 0)
    cos_m, sin_m = _rope_tables(seq, MLA_ROPE, MLA_NOPE)
    tq = _pick_tile(seq, 256)
    idx = jnp.arange(tq)
    u_sb = (idx[:, None] > idx[None, :]).astype(BF16)

    for l in range(depth):
        wl = w_in[l]
        o_fox, o_mla, o_dil = N_SB, N_SB + N_FOX_QKV + N_HEADS_PER_MIXER, N_SB + N_FOX_QKV + N_HEADS_PER_MIXER + N_MLA
        qs = lambda w: w.at[:, 0:hq].multiply(qk_scale)
        w_sb = qs(wl[:, 0:N_SB])
        w_fx = qs(wl[:, o_fox:o_fox + N_FOX_QKV])
        w_f = wl[:, o_fox + N_FOX_QKV:o_mla]
        w_ml = wl[:, o_mla:o_dil]
        wd = wl[:, o_dil:].reshape(d, 3, len(DIL_BRANCHES), hq)
        w_br = [qs(jnp.concatenate([wd[:, 0, g], wd[:, 1, g], wd[:, 2, g]], axis=1)) for g in range(len(DIL_BRANCHES))]
        w_misc = jnp.concatenate([w_ml, w_f, jnp.zeros((d, MISC_W - N_MLA - N_HEADS_PER_MIXER), F32)], axis=1)
        w_tok = jnp.concatenate([w_sb, w_fx, w_misc] + w_br, axis=1).astype(BF16)

        wq = jnp.pad(w_uq[l].reshape(MLA_Q_LORA, N_HEADS_PER_MIXER, MLA_NOPE + MLA_ROPE),
                     ((0, 0), (0, 0), (0, LANES - MLA_NOPE - MLA_ROPE))).reshape(MLA_Q_LORA, -1).astype(BF16)
        wkv = w_ukv[l].reshape(MLA_KV_LORA, N_HEADS_PER_MIXER, MLA_NOPE + HEAD_DIM)
        wk = jnp.pad(wkv[:, :, :MLA_NOPE], ((0, 0), (0, 0), (0, LANES - MLA_NOPE))).reshape(MLA_KV_LORA, -1).astype(BF16)
        wv = wkv[:, :, MLA_NOPE:].reshape(MLA_KV_LORA, -1).astype(BF16)
        g_flat = g_head[l].reshape(1, -1)
        wr = jnp.concatenate([w_group[l], w_expert[l],
                              jnp.zeros((d, LANES - N_GROUPS - N_EXPERTS), F32)], axis=1)
        br = jnp.concatenate([b_group[l], b_expert[l],
                              jnp.zeros((LANES - N_GROUPS - N_EXPERTS,), F32)]).reshape(1, LANES)

        tm = _pick_tile(seq, 512)
        sb, fx, misc, *qkv_br = _proj_call(
            x, w_tok, cos64, sin64,
            ((N_SB, 0, BF16, 1), (N_FOX_QKV, 0, BF16, 1), (MISC_W, 0, F32, 1))
            + tuple((N_BRANCH, 2 * hq, BF16, r) for _, r in DIL_BRANCHES), tm)
        sb = sb.reshape(t, N_SB)
        fx = fx.reshape(t, N_FOX_QKV)
        misc = misc.reshape(t, MISC_W)

        o_sb = _attn_call("sb", sb, sb, sb, 0, 1, 2, g_flat, 0, bsz, seq, tq, extra=(u_sb,))
        neg_c = _fox_c_call(misc, b_forget[l], bsz, seq)
        o_fx = _attn_call("fox", fx, fx, fx, 0, 1, 2, g_flat, 1, bsz, seq, tq, extra=(neg_c,))
        mq, mk, mv = _mla_prep_call(misc, g_cq[l].reshape(1, -1), g_ckv[l].reshape(1, -1), wq, wk, wv,
                                    cos_m, sin_m, seq, tm)
        o_ml = _attn_call("mla", mq, mk, mv, 0, 0, 0, g_flat, 2, bsz, seq, tq, scale=mla_scale)

        dil = []
        for g, (_, r) in enumerate(DIL_BRANCHES):
            n = seq // r
            og = _dil_call(qkv_br[g], _pick_tile(n, 1024), win)
            dil.append(og.reshape(t, 2 * hq))

        x1, route = _post_call(
            x.reshape(t, d), o_sb, o_fx, o_ml, dil[0], dil[1], dil[2], g_flat[:, 3 * hq:], w_out[l].astype(BF16),
            ln1_g[l].reshape(1, d), ln1_b[l].reshape(1, d), wr, br, alpha, _pick_tile(t, 256))

        chunk_expert, n_valid, src, dst, n_rows = _dispatch_tables(route, t)
        y = _moe_call(chunk_expert, n_valid, src, dst, x1,
                      w1[l].astype(BF16), w3[l].astype(BF16), w2[l].astype(BF16), n_rows)
        x = _combine_call(x1, y, route,
                          ln2_g[l].reshape(1, d), ln2_b[l].reshape(1, d), alpha, _pick_tile(t, 256)).reshape(bsz, seq, d)
    return x
```

```python
import functools

import jax
import jax.numpy as jnp
import numpy as np
from jax import lax
from jax.experimental import pallas as pl
from jax.experimental.pallas import tpu as pltpu

F32 = jnp.float32
BF16 = jnp.bfloat16

HEAD_DIM = 64
N_HEADS_PER_MIXER = 4
MLA_Q_LORA = 256
MLA_KV_LORA = 128
MLA_NOPE = 64
MLA_ROPE = 32
DIL_BRANCHES = ((128, 1), (512, 4), (2048, 16))
ROPE_THETA = 10000.0
N_GROUPS = 4
EXPERTS_PER_GROUP = 4
N_EXPERTS = N_GROUPS * EXPERTS_PER_GROUP
TOP_K = 2
MOE_BLOCK = 256
LN_EPS = 1e-5
RMS_EPS = 1e-6

LANES = 128
VMEM_LIMIT_BYTES = 56 * 1024 * 1024

N_SB = 3 * N_HEADS_PER_MIXER * HEAD_DIM
N_FOX_QKV = 3 * N_HEADS_PER_MIXER * HEAD_DIM
N_MLA = MLA_Q_LORA + MLA_KV_LORA + MLA_ROPE
N_BRANCH = 3 * N_HEADS_PER_MIXER * HEAD_DIM
MISC_W = 512
F_COL = N_MLA


def _cparams(sem):
    return pltpu.CompilerParams(dimension_semantics=sem, vmem_limit_bytes=VMEM_LIMIT_BYTES)


def _split3(a):
    hi = a.astype(BF16)
    r1 = a - hi.astype(F32)
    mid = r1.astype(BF16)
    lo = (r1 - mid.astype(F32)).astype(BF16)
    return hi, mid, lo


def _dot(a, b):
    return jnp.dot(a, b, preferred_element_type=F32)


def _dot_nt(a, b):
    return lax.dot_general(a, b, (((1,), (1,)), ((), ())), preferred_element_type=F32)


def _dot_exact_rhs(a, u):
    hi, mid, lo = _split3(a)
    return _dot(hi, u) + _dot(mid, u) + _dot(lo, u)


def _dot_f32(a, b):
    ah, am, al = _split3(a)
    bh, bm, bl = _split3(b)
    return (_dot(ah, bh) + (_dot(ah, bm) + _dot(am, bh))
            + (_dot(am, bm) + _dot(ah, bl) + _dot(al, bh)))


def _lane_iota(shape):
    return lax.broadcasted_iota(jnp.int32, shape, len(shape) - 1)


def _rotate_half(y, half):
    lane = _lane_iota(y.shape)
    fwd = pltpu.roll(y, half, 1)
    bwd = pltpu.roll(y, LANES - half, 1)
    return jnp.where((lane % (2 * half)) < half, -bwd, fwd)


def _log_sigmoid_pair(z):
    sp = jnp.log(1.0 + jnp.exp(-jnp.abs(z)))
    return jnp.minimum(z, 0.0) - sp, -jnp.maximum(z, 0.0) - sp


def _head_rms(o, g):
    lane = _lane_iota(o.shape)
    first = lane < HEAD_DIM
    sq = o * o
    ss_a = jnp.sum(jnp.where(first, sq, 0.0), axis=-1, keepdims=True)
    ss_b = jnp.sum(jnp.where(first, 0.0, sq), axis=-1, keepdims=True)
    ms = jnp.where(first, ss_a, ss_b) * (1.0 / HEAD_DIM)
    return o * lax.rsqrt(ms + RMS_EPS) * g


def _layernorm(y, g, b):
    mu = jnp.mean(y, axis=-1, keepdims=True)
    d = y - mu
    var = jnp.mean(d * d, axis=-1, keepdims=True)
    return d * lax.rsqrt(var + LN_EPS) * g + b


def _proj_kernel(x_ref, w_ref, cos_ref, sin_ref, *refs, outs):
    out_refs, stage_ref = refs[:len(outs)], refs[len(outs)]
    tm = x_ref.shape[1]
    xb = x_ref[0].astype(BF16)
    col = 0
    slab = 0
    for o_ref, (width, n_rope, _, r) in zip(out_refs, outs):
        for c in range(0, width, 2 * LANES):
            cw = min(2 * LANES, width - c)
            y = _dot(xb, w_ref[:, col + c:col + c + cw])
            for s in range(0, cw, LANES):
                ys = y[:, s:s + LANES]
                if c + s < n_rope:
                    ys = ys * cos_ref[...] + _rotate_half(ys, HEAD_DIM // 2) * sin_ref[...]
                if r == 1:
                    o_ref[0, 0, :, c + s:c + s + LANES] = ys.astype(o_ref.dtype)
                elif r == 0:
                    o_ref[0, c + s:c + s + LANES, :] = ys.T.astype(o_ref.dtype)
                else:
                    st = stage_ref.at[slab % stage_ref.shape[0]]
                    slab += 1
                    st[...] = ys
                    for p in range(r):
                        o_ref[0, p, :, c + s:c + s + LANES] = (
                            st[pl.ds(p, tm // r, stride=r), :].astype(o_ref.dtype))
        col += width


PROJ_STAGE_SLABS = 4


def _proj_call(x, w, cos, sin, outs, tm):
    bsz, seq, d = x.shape
    nt = seq // tm
    in_specs = [
        pl.BlockSpec((1, tm, d), lambda b, i: (b, i, 0)),
        pl.BlockSpec(w.shape, lambda b, i: (0, 0)),
        pl.BlockSpec((tm, LANES), lambda b, i: (i, 0)),
        pl.BlockSpec((tm, LANES), lambda b, i: (i, 0)),
    ]
    out_specs = [pl.BlockSpec((1, wd, tm), lambda b, i: (b, 0, i)) if r == 0
                 else pl.BlockSpec((1, r, tm // r, wd), lambda b, i: (b, 0, i, 0)) for wd, _, _, r in outs]
    out_shape = [jax.ShapeDtypeStruct((bsz, wd, seq) if r == 0 else (bsz, r, seq // r, wd), dt)
                 for wd, _, dt, r in outs]
    return pl.pallas_call(
        functools.partial(_proj_kernel, outs=outs),
        grid=(bsz, nt), in_specs=in_specs, out_specs=out_specs, out_shape=out_shape,
        scratch_shapes=[pltpu.VMEM((PROJ_STAGE_SLABS, tm, LANES), F32)],
        compiler_params=_cparams(("parallel", "parallel")),
        name="proj",
    )(x, w, cos, sin)


def _fox_c_kernel(misc_ref, bias_ref, linc_ref, sel_ref, out_ref, *, seq):
    nblk = seq // LANES

    def body(j, carry):
        r0 = pl.multiple_of(j * LANES, LANES)
        f = misc_ref[pl.ds(r0, LANES), :] + bias_ref[...]
        lf, _ = _log_sigmoid_pair(f)
        hi, mid, lo = _split3(lf)
        lt = linc_ref[...]
        csum = (_dot(lt, hi) + _dot(lt, mid) + _dot(lt, lo)) + carry
        for h in range(N_HEADS_PER_MIXER):
            out_ref[0, h, pl.ds(r0, LANES), :] = -_dot_exact_rhs(csum, sel_ref[h])
        return csum[LANES - 1:LANES, :]

    lax.fori_loop(0, nblk, body, jnp.zeros((1, LANES), F32))


def _fox_c_call(misc, b_forget, bsz, seq):
    lane0 = F_COL - 3 * LANES
    bias = jnp.zeros((1, LANES), F32).at[0, lane0:lane0 + N_HEADS_PER_MIXER].set(b_forget)
    idx = jnp.arange(LANES)
    linc = (idx[None, :] <= idx[:, None]).astype(BF16)
    sel = (idx[None, :, None] == (lane0 + jnp.arange(N_HEADS_PER_MIXER))[:, None, None]).astype(BF16)
    sel = jnp.broadcast_to(sel, (N_HEADS_PER_MIXER, LANES, LANES))
    return pl.pallas_call(
        functools.partial(_fox_c_kernel, seq=seq),
        grid=(bsz,),
        in_specs=[
            pl.BlockSpec((seq, LANES), lambda b: (b, 3)),
            pl.BlockSpec((1, LANES), lambda b: (0, 0)),
            pl.BlockSpec((LANES, LANES), lambda b: (0, 0)),
            pl.BlockSpec((N_HEADS_PER_MIXER, LANES, LANES), lambda b: (0, 0, 0)),
        ],
        out_specs=pl.BlockSpec((1, N_HEADS_PER_MIXER, seq, LANES), lambda b: (b, 0, 0, 0)),
        out_shape=jax.ShapeDtypeStruct((bsz, N_HEADS_PER_MIXER, seq, LANES), F32),
        compiler_params=_cparams(("parallel",)),
        name="fox_c",
    )(misc, bias, linc, sel)


def _mla_prep_kernel(misc_ref, gq_ref, gkv_ref, wq_ref, wk_ref, wv_ref, cos_ref, sin_ref,
                     q_ref, k_ref, v_ref):
    def rms(x, g):
        return x * lax.rsqrt(jnp.mean(x * x, axis=-1, keepdims=True) + RMS_EPS) * g

    cq = rms(misc_ref[:, 0:MLA_Q_LORA], gq_ref[...]).astype(BF16)
    ckv = rms(misc_ref[:, MLA_Q_LORA:MLA_Q_LORA + MLA_KV_LORA], gkv_ref[...]).astype(BF16)
    kr_blk = misc_ref[:, 3 * LANES:4 * LANES]
    lane = _lane_iota(kr_blk.shape)
    in_rope = (lane >= MLA_NOPE) & (lane < MLA_NOPE + MLA_ROPE)
    kr = jnp.where(in_rope, pltpu.roll(kr_blk, MLA_NOPE, 1), 0.0)
    cos = cos_ref[...]
    sin = sin_ref[...]

    def rope(y):
        return y * cos + _rotate_half(y, MLA_ROPE // 2) * sin

    q = _dot(cq, wq_ref[...])
    k = _dot(ckv, wk_ref[...])
    for h in range(N_HEADS_PER_MIXER):
        sl = slice(h * LANES, (h + 1) * LANES)
        q_ref[:, sl] = rope(q[:, sl]).astype(BF16)
        k_ref[:, sl] = rope(k[:, sl] + kr).astype(BF16)
    v_ref[0] = _dot(ckv, wv_ref[...]).T.astype(BF16)


def _mla_prep_call(misc, g_cq, g_ckv, wq, wk, wv, cos, sin, seq, tm):
    t = misc.shape[0]
    nper = seq // tm
    hw = N_HEADS_PER_MIXER * LANES
    vw = N_HEADS_PER_MIXER * HEAD_DIM
    full = lambda a: pl.BlockSpec(a.shape, lambda i: (0, 0))
    return pl.pallas_call(
        _mla_prep_kernel,
        grid=(t // tm,),
        in_specs=[
            pl.BlockSpec((tm, MISC_W), lambda i: (i, 0)),
            full(g_cq), full(g_ckv), full(wq), full(wk), full(wv),
            pl.BlockSpec((tm, LANES), lambda i: (i % nper, 0)),
            pl.BlockSpec((tm, LANES), lambda i: (i % nper, 0)),
        ],
        out_specs=[
            pl.BlockSpec((tm, hw), lambda i: (i, 0)),
            pl.BlockSpec((tm, hw), lambda i: (i, 0)),
            pl.BlockSpec((1, vw, tm), lambda i: (i // nper, 0, i % nper)),
        ],
        out_shape=[
            jax.ShapeDtypeStruct((t, hw), BF16),
            jax.ShapeDtypeStruct((t, hw), BF16),
            jax.ShapeDtypeStruct((t // seq, vw, seq), BF16),
        ],
        compiler_params=_cparams(("parallel",)),
        name="mla_prep",
    )(misc, g_cq, g_ckv, wq, wk, wv, cos, sin)


def _attn_kernel(*refs, mode, tq, scale):
    if mode == "sb":
        q_ref, k_ref, v_ref, g_ref, u_ref, o_ref = refs
    elif mode == "fox":
        q_ref, k_ref, v_ref, g_ref, nc_ref, o_ref = refs
    else:
        q_ref, k_ref, v_ref, g_ref, o_ref = refs
    tk = tq
    nh = N_HEADS_PER_MIXER
    i = pl.program_id(1)
    lane = _lane_iota((tq, LANES))
    first = lane < HEAD_DIM
    key = lax.broadcasted_iota(jnp.int32, (tk, tq), 0)
    qry = lax.broadcasted_iota(jnp.int32, (tk, tq), 1)
    top = lax.broadcasted_iota(jnp.int32, (LANES, tq), 0) < HEAD_DIM

    q_heads = []
    for h in range(nh):
        if mode == "mla":
            q_heads.append(q_ref[:, h * LANES:(h + 1) * LANES])
        else:
            q2 = q_ref[:, (h // 2) * LANES:(h // 2 + 1) * LANES]
            zero = jnp.zeros_like(q2)
            q_heads.append(jnp.where(first, q2, zero) if h % 2 == 0 else jnp.where(first, zero, q2))

    def scores_t(j, h):
        r0 = pl.multiple_of(j * tk, tk)
        kb = h if mode == "mla" else h // 2
        return _dot_nt(k_ref[pl.ds(r0, tk), kb * LANES:(kb + 1) * LANES], q_heads[h])

    def vt_pair(j, p):
        r0 = pl.multiple_of(j * tk, tk)
        return v_ref[0, p * LANES:(p + 1) * LANES, pl.ds(r0, tk)]

    if mode == "sb":
        def step(j, carry, diag):
            accs, rs = carry
            new_accs, new_rs = [], []
            for p in range(nh // 2):
                vt = vt_pair(j, p)
                outs = []
                for h in (2 * p, 2 * p + 1):
                    ls_pos, ls_neg = _log_sigmoid_pair(scores_t(j, h))
                    if diag:
                        before = key < qry
                        ls_neg = jnp.where(before, ls_neg, 0.0)
                    c = _dot(u_ref[...], ls_neg.astype(BF16))
                    w = jnp.exp(ls_pos + c + rs[h])
                    if diag:
                        w = jnp.where(before, w, 0.0)
                    outs.append(_dot(vt, w.astype(BF16)))
                    new_rs.append(rs[h] + jnp.sum(ls_neg, axis=0, keepdims=True))
                new_accs.append(accs[p] + jnp.where(top, outs[0], outs[1]))
            return tuple(new_accs), tuple(new_rs)

        zacc = jnp.zeros((LANES, tq), F32)
        zr = jnp.zeros((1, tq), F32)
        carry = step(i, ((zacc,) * (nh // 2), (zr,) * nh), True)
        carry = lax.fori_loop(0, i, lambda n, c: step(i - 1 - n, c, False), carry)
        outs_t = carry[0]
    else:
        def step(j, carry, diag):
            accs, ms, ls = carry
            new_accs, new_ms, new_ls = [], [], []
            r0 = pl.multiple_of(j * tk, tk)
            for p in range(nh // 2):
                vt = vt_pair(j, p)
                pv, alphas = [], []
                for h in (2 * p, 2 * p + 1):
                    s = scores_t(j, h)
                    if mode == "mla":
                        s = s * scale
                    else:
                        nb = nc_ref[0, h, pl.ds(r0, tk), :]
                        s = s + jnp.concatenate([nb] * (tq // LANES), axis=1)
                    if diag:
                        s = jnp.where(key <= qry, s, -jnp.inf)
                    m_new = jnp.maximum(ms[h], jnp.max(s, axis=0, keepdims=True))
                    alpha = jnp.exp(ms[h] - m_new)
                    pr = jnp.exp(s - m_new)
                    new_ls.append(alpha * ls[h] + jnp.sum(pr, axis=0, keepdims=True))
                    new_ms.append(m_new)
                    pv.append(_dot(vt, pr.astype(BF16)))
                    alphas.append(alpha)
                new_accs.append(accs[p] * jnp.where(top, alphas[0], alphas[1])
                                + jnp.where(top, pv[0], pv[1]))
            return tuple(new_accs), tuple(new_ms), tuple(new_ls)

        neg = jnp.full((1, tq), -jnp.inf, F32)
        zl = jnp.zeros((1, tq), F32)
        zacc = jnp.zeros((LANES, tq), F32)
        carry = step(i, ((zacc,) * (nh // 2), (neg,) * nh, (zl,) * nh), True)
        carry = lax.fori_loop(0, i, lambda n, c: step(n, c, False), carry)
        accs, _, ls = carry
        outs_t = [accs[p] / jnp.where(top, ls[2 * p], ls[2 * p + 1]) for p in range(nh // 2)]
    for p in range(nh // 2):
        sl = slice(p * LANES, (p + 1) * LANES)
        o_ref[:, sl] = _head_rms(outs_t[p].T, g_ref[:, sl]).astype(o_ref.dtype)


def _attn_call(mode, q, k, vt, qcol, kcol, g_flat, gcol, bsz, seq, tq, extra=(), scale=1.0):
    t = bsz * seq
    nq = seq // tq
    hq = N_HEADS_PER_MIXER * HEAD_DIM
    qw = N_HEADS_PER_MIXER * LANES if mode == "mla" else hq
    in_specs = [
        pl.BlockSpec((tq, qw), lambda b, i: (b * nq + i, qcol)),
        pl.BlockSpec((seq, qw), lambda b, i: (b, kcol)),
        pl.BlockSpec((1, hq, seq), lambda b, i: (b, 0, 0)),
        pl.BlockSpec((1, hq), lambda b, i: (0, gcol)),
    ]
    args = [q, k, vt, g_flat]
    if mode == "sb":
        (u,) = extra
        in_specs.append(pl.BlockSpec(u.shape, lambda b, i: (0, 0)))
        args.append(u)
    elif mode == "fox":
        (nc,) = extra
        in_specs.append(pl.BlockSpec((1, N_HEADS_PER_MIXER, seq, LANES), lambda b, i: (b, 0, 0, 0)))
        args.append(nc)
    return pl.pallas_call(
        functools.partial(_attn_kernel, mode=mode, tq=tq, scale=scale),
        grid=(bsz, nq),
        in_specs=in_specs,
        out_specs=pl.BlockSpec((tq, hq), lambda b, i: (b * nq + i, 0)),
        out_shape=jax.ShapeDtypeStruct((t, hq), BF16),
        compiler_params=_cparams(("parallel", "arbitrary")),
        name=f"attn_{mode}",
    )(*args)


def _dil_kernel(qkv_ref, o_ref, *, tqs, win):
    tb = win
    c = pl.program_id(2)
    lane = _lane_iota((tb, LANES))
    first = lane < HEAD_DIM
    row = lax.broadcasted_iota(jnp.int32, (tb, tb), 0)
    colm = lax.broadcasted_iota(jnp.int32, (tb, tb), 1)
    hq = N_HEADS_PER_MIXER * HEAD_DIM

    def tile(it, _):
        gi = c * (tqs // tb) + it
        r0 = pl.multiple_of(gi * tb, tb)
        rp = pl.multiple_of(jnp.maximum(gi - 1, 0) * tb, tb)
        no_prev = jnp.where(gi > 0, 0, tb)
        o0 = pl.multiple_of(it * tb, tb)
        for p in range(N_HEADS_PER_MIXER // 2):
            q2 = qkv_ref[0, 0, pl.ds(r0, tb), p * LANES:(p + 1) * LANES]
            kd = qkv_ref[0, 0, pl.ds(r0, tb), hq + p * LANES:hq + (p + 1) * LANES]
            kp = qkv_ref[0, 0, pl.ds(rp, tb), hq + p * LANES:hq + (p + 1) * LANES]
            vd = qkv_ref[0, 0, pl.ds(r0, tb), 2 * hq + p * LANES:2 * hq + (p + 1) * LANES]
            vp = qkv_ref[0, 0, pl.ds(rp, tb), 2 * hq + p * LANES:2 * hq + (p + 1) * LANES]
            zero = jnp.zeros_like(q2)
            outs = []
            lses = []
            for hh in range(2):
                qh = jnp.where(first, q2, zero) if hh == 0 else jnp.where(first, zero, q2)
                sd = jnp.where(colm <= row, _dot_nt(qh, kd), -jnp.inf)
                sp = jnp.where(colm >= row + no_prev, _dot_nt(qh, kp), -jnp.inf)
                m = jnp.maximum(jnp.max(sd, axis=-1, keepdims=True), jnp.max(sp, axis=-1, keepdims=True))
                pd = jnp.exp(sd - m)
                pp = jnp.exp(sp - m)
                l = jnp.sum(pd, axis=-1, keepdims=True) + jnp.sum(pp, axis=-1, keepdims=True)
                inv = 1.0 / l
                acc = _dot((pd * inv).astype(BF16), vd) + _dot((pp * inv).astype(BF16), vp)
                outs.append(acc)
                lses.append(m + jnp.log(l))
            o_ref[0, pl.ds(o0, tb), p * LANES:(p + 1) * LANES] = jnp.where(first, outs[0], outs[1])
            o_ref[0, pl.ds(o0, tb), hq + p * LANES:hq + (p + 1) * LANES] = jnp.where(first, lses[0], lses[1])
        return 0

    lax.fori_loop(0, tqs // tb, tile, 0, unroll=min(4, tqs // tb))


def _dil_call(qkv, tqs, win):
    bsz, r, n, w = qkv.shape
    hq = N_HEADS_PER_MIXER * HEAD_DIM
    return pl.pallas_call(
        functools.partial(_dil_kernel, tqs=tqs, win=win),
        grid=(bsz, r, n // tqs),
        in_specs=[pl.BlockSpec((1, 1, n, w), lambda b, p, c: (b, p, 0, 0))],
        out_specs=pl.BlockSpec((1, tqs, 2 * hq), lambda b, p, c: (b, c, p)),
        out_shape=jax.ShapeDtypeStruct((bsz, n, r * 2 * hq), F32),
        compiler_params=_cparams(("parallel", "parallel", "arbitrary")),
        name=f"dil_r{r}",
    )(qkv)


def _post_kernel(x_ref, osb_ref, ofox_ref, omla_ref, d1_ref, d2_ref, d3_ref, gd_ref, wo_ref,
                 lng_ref, lnb_ref, wr_ref, br_ref, x1_ref, route_ref, *, alpha):
    hq = N_HEADS_PER_MIXER * HEAD_DIM
    h = _dot(osb_ref[...], wo_ref[0:hq, :])
    h += _dot(ofox_ref[...], wo_ref[hq:2 * hq, :])
    h += _dot(omla_ref[...], wo_ref[2 * hq:3 * hq, :])
    for p in range(N_HEADS_PER_MIXER // 2):
        sl = slice(p * LANES, (p + 1) * LANES)
        ll = slice(hq + p * LANES, hq + (p + 1) * LANES)
        l1, l2, l3 = d1_ref[:, ll], d2_ref[:, ll], d3_ref[:, ll]
        m = jnp.maximum(jnp.maximum(l1, l2), l3)
        e1, e2, e3 = jnp.exp(l1 - m), jnp.exp(l2 - m), jnp.exp(l3 - m)
        inv = 1.0 / (e1 + e2 + e3)
        od = (e1 * inv) * d1_ref[:, sl] + (e2 * inv) * d2_ref[:, sl] + (e3 * inv) * d3_ref[:, sl]
        od = _head_rms(od, gd_ref[:, sl]).astype(BF16)
        h += _dot(od, wo_ref[3 * hq + p * LANES:3 * hq + (p + 1) * LANES, :])
    x1 = _layernorm(alpha * x_ref[...] + h, lng_ref[...], lnb_ref[...])
    x1_ref[...] = x1

    logits = _dot_f32(x1, wr_ref[...]) + br_ref[...]
    lane = _lane_iota(logits.shape)
    lanef = lane.astype(F32)
    big = float(LANES)
    ninf = -jnp.inf
    gl = jnp.where(lane < N_GROUPS, logits, ninf)
    gmax = jnp.max(gl, axis=-1, keepdims=True)
    gsel = jnp.min(jnp.where(gl == gmax, lanef, big), axis=-1, keepdims=True)
    gw = 1.0 / jnp.sum(jnp.exp(gl - gmax), axis=-1, keepdims=True)
    e_lo = N_GROUPS + EXPERTS_PER_GROUP * gsel
    in_grp = (lanef >= e_lo) & (lanef < e_lo + EXPERTS_PER_GROUP)
    el = jnp.where(in_grp, logits, ninf)
    t1 = jnp.max(el, axis=-1, keepdims=True)
    i1 = jnp.min(jnp.where(el == t1, lanef, big), axis=-1, keepdims=True)
    el2 = jnp.where(lanef == i1, ninf, el)
    t2 = jnp.max(el2, axis=-1, keepdims=True)
    i2 = jnp.min(jnp.where(el2 == t2, lanef, big), axis=-1, keepdims=True)
    ex = jnp.exp(t2 - t1)
    den = 1.0 + ex
    g1 = gw / den
    g2 = gw * ex / den
    out = jnp.where(lane == 0, i1 - N_GROUPS,
                    jnp.where(lane == 1, i2 - N_GROUPS,
                              jnp.where(lane == 2, g1, jnp.where(lane == 3, g2, 0.0))))
    route_ref[...] = out


def _post_call(x, osb, ofox, omla, d1, d2, d3, g_dil, wo, lng, lnb, wr, br, alpha, tm):
    t, d = x.shape
    hq = N_HEADS_PER_MIXER * HEAD_DIM
    row = lambda w: pl.BlockSpec((tm, w), lambda i: (i, 0))
    full = lambda a: pl.BlockSpec(a.shape, lambda i: (0, 0))
    return pl.pallas_call(
        functools.partial(_post_kernel, alpha=alpha),
        grid=(t // tm,),
        in_specs=[row(d), row(hq), row(hq), row(hq), row(2 * hq), row(2 * hq), row(2 * hq),
                  full(g_dil), full(wo), full(lng), full(lnb), full(wr), full(br)],
        out_specs=[row(d), row(LANES)],
        out_shape=[jax.ShapeDtypeStruct((t, d), F32), jax.ShapeDtypeStruct((t, LANES), F32)],
        compiler_params=_cparams(("parallel",)),
        name="post_mixer",
    )(x, osb, ofox, omla, d1, d2, d3, g_dil, wo, lng, lnb, wr, br)


DMA_UNROLL = 8


def _moe_kernel(ce_ref, nv_ref, src_ref, srcn_ref, dst_ref, x_hbm, w1_ref, w3_ref, w2_ref, y_hbm,
                xs_ref, ys_ref, gsem, ssem):
    c = pl.program_id(0)
    nc = pl.num_programs(0)
    slot = c % 2
    other = 1 - slot
    nv = nv_ref[c]
    nv_next = jnp.where(c + 1 < nc, nv_ref[jnp.minimum(c + 1, nc - 1)], 0)
    nv_prev = jnp.where(c > 0, nv_ref[jnp.maximum(c - 1, 0)], 0)

    def gather_copy(idx_ref, i, s):
        return pltpu.make_async_copy(x_hbm.at[pl.ds(idx_ref[0, 0, i], 1), :],
                                     xs_ref.at[s, pl.ds(i, 1), :], gsem.at[s])

    def scatter_copy(i, s):
        return pltpu.make_async_copy(ys_ref.at[s, pl.ds(i, 1), :],
                                     y_hbm.at[pl.ds(dst_ref[0, 0, i], 1), :], ssem.at[s])

    def start_rows(n, make):
        ngrp = n // DMA_UNROLL

        def grp(g, _):
            for k in range(DMA_UNROLL):
                make(g * DMA_UNROLL + k).start()
            return 0

        def one(i, _):
            make(i).start()
            return 0

        lax.fori_loop(0, ngrp, grp, 0)
        lax.fori_loop(ngrp * DMA_UNROLL, n, one, 0)

    def wait_rows(n, make_row, make_block):
        @pl.when(n == MOE_BLOCK)
        def _():
            make_block().wait()

        @pl.when(n < MOE_BLOCK)
        def _():
            def one(i, _):
                make_row(i).wait()
                return 0
            lax.fori_loop(0, n, one, 0)

    def gather_block(s):
        return pltpu.make_async_copy(x_hbm.at[pl.ds(0, MOE_BLOCK), :], xs_ref.at[s], gsem.at[s])

    def scatter_block(s):
        return pltpu.make_async_copy(ys_ref.at[s], y_hbm.at[pl.ds(0, MOE_BLOCK), :], ssem.at[s])

    @pl.when(c == 0)
    def _():
        xs_ref[...] = jnp.zeros_like(xs_ref)
        start_rows(nv, lambda i: gather_copy(src_ref, i, 0))

    start_rows(nv_next, lambda i: gather_copy(srcn_ref, i, other))
    wait_rows(nv, lambda i: gather_copy(src_ref, i, slot), lambda: gather_block(slot))

    @pl.when(nv > 0)
    def _():
        xb = xs_ref[slot].astype(BF16)
        a = _dot(xb, w1_ref[0])
        b = _dot(xb, w3_ref[0])
        hid = (a / (1.0 + jnp.exp(-a)) * b).astype(BF16)
        ys_ref[slot] = _dot(hid, w2_ref[0])

    wait_rows(nv_prev, lambda i: scatter_copy(i, other), lambda: scatter_block(other))
    start_rows(nv, lambda i: scatter_copy(i, slot))

    @pl.when(c == nc - 1)
    def _():
        wait_rows(nv, lambda i: scatter_copy(i, slot), lambda: scatter_block(slot))


def _moe_call(chunk_expert, n_valid, src, dst, x1, w1, w3, w2, n_rows_out):
    n_chunks = chunk_expert.shape[0]
    t, d = x1.shape
    de = w1.shape[-1]
    grid_spec = pltpu.PrefetchScalarGridSpec(
        num_scalar_prefetch=2,
        grid=(n_chunks,),
        in_specs=[
            pl.BlockSpec((1, 1, MOE_BLOCK), lambda c, ce, nv: (c, 0, 0), memory_space=pltpu.SMEM),
            pl.BlockSpec((1, 1, MOE_BLOCK), lambda c, ce, nv: (jnp.minimum(c + 1, n_chunks - 1), 0, 0),
                         memory_space=pltpu.SMEM),
            pl.BlockSpec((1, 1, MOE_BLOCK), lambda c, ce, nv: (c, 0, 0), memory_space=pltpu.SMEM),
            pl.BlockSpec(memory_space=pl.ANY),
            pl.BlockSpec((1, d, de), lambda c, ce, nv: (ce[c], 0, 0)),
            pl.BlockSpec((1, d, de), lambda c, ce, nv: (ce[c], 0, 0)),
            pl.BlockSpec((1, de, d), lambda c, ce, nv: (ce[c], 0, 0)),
        ],
        out_specs=pl.BlockSpec(memory_space=pl.ANY),
        scratch_shapes=[
            pltpu.VMEM((2, MOE_BLOCK, d), F32),
            pltpu.VMEM((2, MOE_BLOCK, d), F32),
            pltpu.SemaphoreType.DMA((2,)),
            pltpu.SemaphoreType.DMA((2,)),
        ],
    )
    return pl.pallas_call(
        _moe_kernel,
        grid_spec=grid_spec,
        out_shape=jax.ShapeDtypeStruct((n_rows_out, d), F32),
        compiler_params=_cparams(("arbitrary",)),
        name="moe_experts",
    )(chunk_expert, n_valid, src, src, dst, x1, w1, w3, w2)


def _combine_kernel(x1_ref, ya_ref, yb_ref, route_ref, lng_ref, lnb_ref, o_ref, *, alpha):
    g1 = route_ref[:, 2:3]
    g2 = route_ref[:, 3:4]
    m = g1 * ya_ref[...] + g2 * yb_ref[...]
    o_ref[...] = _layernorm(alpha * x1_ref[...] + m, lng_ref[...], lnb_ref[...])


def _combine_call(x1, y, route, lng, lnb, alpha, tm):
    t, d = x1.shape
    nt = t // tm
    full = lambda a: pl.BlockSpec(a.shape, lambda i: (0, 0))
    return pl.pallas_call(
        functools.partial(_combine_kernel, alpha=alpha),
        grid=(nt,),
        in_specs=[pl.BlockSpec((tm, d), lambda i: (i, 0)),
                  pl.BlockSpec((tm, d), lambda i: (i, 0)),
                  pl.BlockSpec((tm, d), lambda i: (i + nt, 0)),
                  pl.BlockSpec((tm, LANES), lambda i: (i, 0)),
                  full(lng), full(lnb)],
        out_specs=pl.BlockSpec((tm, d), lambda i: (i, 0)),
        out_shape=jax.ShapeDtypeStruct((t, d), F32),
        compiler_params=_cparams(("parallel",)),
        name="moe_combine",
    )(x1, y, y, route, lng, lnb)


def _rope_tables(seq, dim, lane_lo):
    half = dim // 2
    inv_freq = ROPE_THETA ** (-jnp.arange(half, dtype=F32) / half)
    ang = jnp.arange(seq, dtype=F32)[:, None] * inv_freq[None, :]
    cos = jnp.concatenate([jnp.cos(ang), jnp.cos(ang)], -1)
    sin = jnp.concatenate([jnp.sin(ang), jnp.sin(ang)], -1)
    if lane_lo == 0:
        reps = LANES // dim
        return jnp.tile(cos, (1, reps)), jnp.tile(sin, (1, reps))
    cos_t = jnp.ones((seq, LANES), F32).at[:, lane_lo:lane_lo + dim].set(cos)
    sin_t = jnp.zeros((seq, LANES), F32).at[:, lane_lo:lane_lo + dim].set(sin)
    return cos_t, sin_t


def _dispatch_tables(route, n_tok):
    expert_id = route[:, 0:TOP_K].astype(jnp.int32).reshape(-1)
    n_assign = n_tok * TOP_K
    n_slots = n_assign + N_EXPERTS * MOE_BLOCK
    n_chunks = n_slots // MOE_BLOCK
    onehot = (expert_id[:, None] == jnp.arange(N_EXPERTS, dtype=jnp.int32)[None, :]).astype(jnp.int32)
    ranks = jnp.cumsum(onehot, axis=0) - onehot
    rank = jnp.sum(ranks * onehot, axis=1)
    counts = jnp.sum(onehot, axis=0)
    padded = (counts + MOE_BLOCK - 1) // MOE_BLOCK * MOE_BLOCK
    pad_end = jnp.cumsum(padded)
    pad_start = pad_end - padded
    dest = pad_start[expert_id] + rank
    assign = jnp.arange(n_assign, dtype=jnp.int32)
    slot_assign = jnp.zeros((n_slots,), jnp.int32).at[dest].set(assign)
    src = slot_assign // TOP_K
    dst = (slot_assign % TOP_K) * n_tok + src
    chunk_start = jnp.arange(n_chunks, dtype=jnp.int32) * MOE_BLOCK
    chunk_expert = jnp.minimum(jnp.searchsorted(pad_end, chunk_start, side="right"),
                               N_EXPERTS - 1).astype(jnp.int32)
    n_valid = jnp.clip(pad_start[chunk_expert] + counts[chunk_expert] - chunk_start,
                       0, MOE_BLOCK).astype(jnp.int32)
    return (chunk_expert, n_valid, src.reshape(n_chunks, 1, MOE_BLOCK),
            dst.reshape(n_chunks, 1, MOE_BLOCK), n_assign)


def _pick_tile(n, pref):
    t = pref
    while n % t:
        t //= 2
    return t


def kernel(x, w_in, b_forget, g_cq, g_ckv, w_uq, w_ukv, g_head, w_out, ln1_g, ln1_b,
           w_group, b_group, w_expert, b_expert, w1, w3, w2, ln2_g, ln2_b):
    bsz, seq, d = x.shape
    depth = w_in.shape[0]
    t = bsz * seq
    alpha = (2.0 * depth) ** 0.25
    hq = N_HEADS_PER_MIXER * HEAD_DIM
    qk_scale = HEAD_DIM ** -0.5
    mla_scale = (MLA_NOPE + MLA_ROPE) ** -0.5
    win = DIL_BRANCHES[0][0]
    assert all(w // r == win for w, r in DIL_BRANCHES)
    assert seq % (DIL_BRANCHES[-1][1] * win) == 0 and d % LANES == 0

    cos64, sin64 = _rope_tables(seq, HEAD_DIM, 0)
    cos_m, sin_m = _rope_tables(seq, MLA_ROPE, MLA_NOPE)
    tq = _pick_tile(seq, 256)
    idx = jnp.arange(tq)
    u_sb = (idx[None, :] > idx[:, None]).astype(BF16)

    for l in range(depth):
        wl = w_in[l]
        o_fox, o_mla, o_dil = N_SB, N_SB + N_FOX_QKV + N_HEADS_PER_MIXER, N_SB + N_FOX_QKV + N_HEADS_PER_MIXER + N_MLA
        qs = lambda w: w.at[:, 0:hq].multiply(qk_scale)
        w_sb = qs(wl[:, 0:N_SB])
        w_fx = qs(wl[:, o_fox:o_fox + N_FOX_QKV])
        w_f = wl[:, o_fox + N_FOX_QKV:o_mla]
        w_ml = wl[:, o_mla:o_dil]
        wd = wl[:, o_dil:].reshape(d, 3, len(DIL_BRANCHES), hq)
        w_br = [qs(jnp.concatenate([wd[:, 0, g], wd[:, 1, g], wd[:, 2, g]], axis=1)) for g in range(len(DIL_BRANCHES))]
        w_misc = jnp.concatenate([w_ml, w_f, jnp.zeros((d, MISC_W - N_MLA - N_HEADS_PER_MIXER), F32)], axis=1)
        w_tok = jnp.concatenate([w_sb, w_fx, w_misc] + w_br, axis=1).astype(BF16)

        wq = jnp.pad(w_uq[l].reshape(MLA_Q_LORA, N_HEADS_PER_MIXER, MLA_NOPE + MLA_ROPE),
                     ((0, 0), (0, 0), (0, LANES - MLA_NOPE - MLA_ROPE))).reshape(MLA_Q_LORA, -1).astype(BF16)
        wkv = w_ukv[l].reshape(MLA_KV_LORA, N_HEADS_PER_MIXER, MLA_NOPE + HEAD_DIM)
        wk = jnp.pad(wkv[:, :, :MLA_NOPE], ((0, 0), (0, 0), (0, LANES - MLA_NOPE))).reshape(MLA_KV_LORA, -1).astype(BF16)
        wv = wkv[:, :, MLA_NOPE:].reshape(MLA_KV_LORA, -1).astype(BF16)
        g_flat = g_head[l].reshape(1, -1)
        wr = jnp.concatenate([w_group[l], w_expert[l],
                              jnp.zeros((d, LANES - N_GROUPS - N_EXPERTS), F32)], axis=1)
        br = jnp.concatenate([b_group[l], b_expert[l],
                              jnp.zeros((LANES - N_GROUPS - N_EXPERTS,), F32)]).reshape(1, LANES)

        tm = _pick_tile(seq, 512)
        sb, sb_vt, fx, fx_vt, misc, *qkv_br = _proj_call(
            x, w_tok, cos64, sin64,
            ((2 * hq, 0, BF16, 1), (hq, 0, BF16, 0), (2 * hq, 0, BF16, 1), (hq, 0, BF16, 0), (MISC_W, 0, F32, 1))
            + tuple((N_BRANCH, 2 * hq, BF16, r) for _, r in DIL_BRANCHES), tm)
        sb = sb.reshape(t, 2 * hq)
        fx = fx.reshape(t, 2 * hq)
        misc = misc.reshape(t, MISC_W)

        o_sb = _attn_call("sb", sb, sb, sb_vt, 0, 1, g_flat, 0, bsz, seq, tq, extra=(u_sb,))
        neg_c = _fox_c_call(misc, b_forget[l], bsz, seq)
        o_fx = _attn_call("fox", fx, fx, fx_vt, 0, 1, g_flat, 1, bsz, seq, tq, extra=(neg_c,))
        mq, mk, mvt = _mla_prep_call(misc, g_cq[l].reshape(1, -1), g_ckv[l].reshape(1, -1), wq, wk, wv,
                                     cos_m, sin_m, seq, tm)
        o_ml = _attn_call("mla", mq, mk, mvt, 0, 0, g_flat, 2, bsz, seq, tq, scale=mla_scale)

        dil = []
        for g, (_, r) in enumerate(DIL_BRANCHES):
            n = seq // r
            og = _dil_call(qkv_br[g], _pick_tile(n, 1024), win)
            dil.append(og.reshape(t, 2 * hq))

        x1, route = _post_call(
            x.reshape(t, d), o_sb, o_fx, o_ml, dil[0], dil[1], dil[2], g_flat[:, 3 * hq:], w_out[l].astype(BF16),
            ln1_g[l].reshape(1, d), ln1_b[l].reshape(1, d), wr, br, alpha, _pick_tile(t, 256))

        chunk_expert, n_valid, src, dst, n_rows = _dispatch_tables(route, t)
        y = _moe_call(chunk_expert, n_valid, src, dst, x1,
                      w1[l].astype(BF16), w3[l].astype(BF16), w2[l].astype(BF16), n_rows)
        x = _combine_call(x1, y, route,
                          ln2_g[l].reshape(1, d), ln2_b[l].reshape(1, d), alpha, _pick_tile(t, 256)).reshape(bsz, seq, d)
    return x
```

```python
import functools

import jax
import jax.numpy as jnp
import numpy as np
from jax import lax
from jax.experimental import pallas as pl
from jax.experimental.pallas import tpu as pltpu

F32 = jnp.float32
BF16 = jnp.bfloat16

HEAD_DIM = 64
N_HEADS_PER_MIXER = 4
MLA_Q_LORA = 256
MLA_KV_LORA = 128
MLA_NOPE = 64
MLA_ROPE = 32
DIL_BRANCHES = ((128, 1), (512, 4), (2048, 16))
ROPE_THETA = 10000.0
N_GROUPS = 4
EXPERTS_PER_GROUP = 4
N_EXPERTS = N_GROUPS * EXPERTS_PER_GROUP
TOP_K = 2
MOE_BLOCK = 256
LN_EPS = 1e-5
RMS_EPS = 1e-6

LANES = 128
VMEM_LIMIT_BYTES = 56 * 1024 * 1024

N_SB = 3 * N_HEADS_PER_MIXER * HEAD_DIM
N_FOX_QKV = 3 * N_HEADS_PER_MIXER * HEAD_DIM
N_MLA = MLA_Q_LORA + MLA_KV_LORA + MLA_ROPE
N_BRANCH = 3 * N_HEADS_PER_MIXER * HEAD_DIM
MISC_W = 512
F_COL = N_MLA


def _cparams(sem):
    return pltpu.CompilerParams(dimension_semantics=sem, vmem_limit_bytes=VMEM_LIMIT_BYTES)


def _split3(a):
    hi = a.astype(BF16)
    r1 = a - hi.astype(F32)
    mid = r1.astype(BF16)
    lo = (r1 - mid.astype(F32)).astype(BF16)
    return hi, mid, lo


def _dot(a, b):
    return jnp.dot(a, b, preferred_element_type=F32)


def _dot_nt(a, b):
    return lax.dot_general(a, b, (((1,), (1,)), ((), ())), preferred_element_type=F32)


def _dot_exact_rhs(a, u):
    hi, mid, lo = _split3(a)
    return _dot(hi, u) + _dot(mid, u) + _dot(lo, u)


def _dot_f32(a, b):
    ah, am, al = _split3(a)
    bh, bm, bl = _split3(b)
    return (_dot(ah, bh) + (_dot(ah, bm) + _dot(am, bh))
            + (_dot(am, bm) + _dot(ah, bl) + _dot(al, bh)))


def _lane_iota(shape):
    return lax.broadcasted_iota(jnp.int32, shape, len(shape) - 1)


def _rotate_half(y, half):
    lane = _lane_iota(y.shape)
    fwd = pltpu.roll(y, half, 1)
    bwd = pltpu.roll(y, LANES - half, 1)
    return jnp.where((lane % (2 * half)) < half, -bwd, fwd)


def _log_sigmoid_pair(z):
    sp = jnp.log(1.0 + jnp.exp(-jnp.abs(z)))
    return jnp.minimum(z, 0.0) - sp, -jnp.maximum(z, 0.0) - sp


def _head_rms(o, g):
    lane = _lane_iota(o.shape)
    first = lane < HEAD_DIM
    sq = o * o
    ss_a = jnp.sum(jnp.where(first, sq, 0.0), axis=-1, keepdims=True)
    ss_b = jnp.sum(jnp.where(first, 0.0, sq), axis=-1, keepdims=True)
    ms = jnp.where(first, ss_a, ss_b) * (1.0 / HEAD_DIM)
    return o * lax.rsqrt(ms + RMS_EPS) * g


def _layernorm(y, g, b):
    mu = jnp.mean(y, axis=-1, keepdims=True)
    d = y - mu
    var = jnp.mean(d * d, axis=-1, keepdims=True)
    return d * lax.rsqrt(var + LN_EPS) * g + b


def _proj_kernel(x_ref, w_ref, cos_ref, sin_ref, *refs, outs):
    out_refs, stage_ref = refs[:len(outs)], refs[len(outs)]
    tm = x_ref.shape[1]
    xb = x_ref[0].astype(BF16)
    col = 0
    slab = 0
    for o_ref, (width, n_rope, _, r) in zip(out_refs, outs):
        for c in range(0, width, 2 * LANES):
            cw = min(2 * LANES, width - c)
            y = _dot(xb, w_ref[:, col + c:col + c + cw])
            for s in range(0, cw, LANES):
                ys = y[:, s:s + LANES]
                if c + s < n_rope:
                    ys = ys * cos_ref[...] + _rotate_half(ys, HEAD_DIM // 2) * sin_ref[...]
                if r == 1:
                    o_ref[0, 0, :, c + s:c + s + LANES] = ys.astype(o_ref.dtype)
                else:
                    st = stage_ref.at[slab % stage_ref.shape[0]]
                    slab += 1
                    st[...] = ys
                    for p in range(r):
                        o_ref[0, p, :, c + s:c + s + LANES] = (
                            st[pl.ds(p, tm // r, stride=r), :].astype(o_ref.dtype))
        col += width


PROJ_STAGE_SLABS = 4


def _proj_call(x, w, cos, sin, outs, tm):
    bsz, seq, d = x.shape
    nt = seq // tm
    in_specs = [
        pl.BlockSpec((1, tm, d), lambda b, i: (b, i, 0)),
        pl.BlockSpec(w.shape, lambda b, i: (0, 0)),
        pl.BlockSpec((tm, LANES), lambda b, i: (i, 0)),
        pl.BlockSpec((tm, LANES), lambda b, i: (i, 0)),
    ]
    out_specs = [pl.BlockSpec((1, r, tm // r, wd), lambda b, i: (b, 0, i, 0)) for wd, _, _, r in outs]
    out_shape = [jax.ShapeDtypeStruct((bsz, r, seq // r, wd), dt) for wd, _, dt, r in outs]
    return pl.pallas_call(
        functools.partial(_proj_kernel, outs=outs),
        grid=(bsz, nt), in_specs=in_specs, out_specs=out_specs, out_shape=out_shape,
        scratch_shapes=[pltpu.VMEM((PROJ_STAGE_SLABS, tm, LANES), F32)],
        compiler_params=_cparams(("parallel", "parallel")),
        name="proj",
    )(x, w, cos, sin)


def _fox_c_kernel(misc_ref, bias_ref, uinc_ref, ones_ref, out_ref, *, seq):
    lane0 = F_COL - 3 * LANES
    nblk = seq // LANES

    def body(j, carry):
        r0 = pl.multiple_of(j * LANES, LANES)
        f = misc_ref[pl.ds(r0, LANES), :] + bias_ref[...]
        lf, _ = _log_sigmoid_pair(f)
        lft = lf.T
        csum = _dot_exact_rhs(lft, uinc_ref[...]) + carry
        tot = _dot_exact_rhs(lft, ones_ref[...])
        out_ref[0, :, pl.ds(r0, LANES)] = -csum[lane0:lane0 + N_HEADS_PER_MIXER, :]
        return carry + tot

    lax.fori_loop(0, nblk, body, jnp.zeros((LANES, LANES), F32))


def _fox_c_call(misc, b_forget, bsz, seq):
    bias = jnp.zeros((1, LANES), F32).at[0, F_COL - 3 * LANES:F_COL - 3 * LANES + N_HEADS_PER_MIXER].set(b_forget)
    idx = jnp.arange(LANES)
    uinc = (idx[:, None] <= idx[None, :]).astype(BF16)
    ones = jnp.ones((LANES, LANES), BF16)
    return pl.pallas_call(
        functools.partial(_fox_c_kernel, seq=seq),
        grid=(bsz,),
        in_specs=[
            pl.BlockSpec((seq, LANES), lambda b: (b, 3)),
            pl.BlockSpec((1, LANES), lambda b: (0, 0)),
            pl.BlockSpec((LANES, LANES), lambda b: (0, 0)),
            pl.BlockSpec((LANES, LANES), lambda b: (0, 0)),
        ],
        out_specs=pl.BlockSpec((1, N_HEADS_PER_MIXER, seq), lambda b: (b, 0, 0)),
        out_shape=jax.ShapeDtypeStruct((bsz, N_HEADS_PER_MIXER, seq), F32),
        compiler_params=_cparams(("parallel",)),
        name="fox_c",
    )(misc, bias, uinc, ones)


def _mla_prep_kernel(misc_ref, gq_ref, gkv_ref, wq_ref, wk_ref, wv_ref, cos_ref, sin_ref,
                     q_ref, k_ref, v_ref):
    def rms(x, g):
        return x * lax.rsqrt(jnp.mean(x * x, axis=-1, keepdims=True) + RMS_EPS) * g

    cq = rms(misc_ref[:, 0:MLA_Q_LORA], gq_ref[...]).astype(BF16)
    ckv = rms(misc_ref[:, MLA_Q_LORA:MLA_Q_LORA + MLA_KV_LORA], gkv_ref[...]).astype(BF16)
    kr_blk = misc_ref[:, 3 * LANES:4 * LANES]
    lane = _lane_iota(kr_blk.shape)
    in_rope = (lane >= MLA_NOPE) & (lane < MLA_NOPE + MLA_ROPE)
    kr = jnp.where(in_rope, pltpu.roll(kr_blk, MLA_NOPE, 1), 0.0)
    cos = cos_ref[...]
    sin = sin_ref[...]

    def rope(y):
        return y * cos + _rotate_half(y, MLA_ROPE // 2) * sin

    q = _dot(cq, wq_ref[...])
    k = _dot(ckv, wk_ref[...])
    for h in range(N_HEADS_PER_MIXER):
        sl = slice(h * LANES, (h + 1) * LANES)
        q_ref[:, sl] = rope(q[:, sl]).astype(BF16)
        k_ref[:, sl] = rope(k[:, sl] + kr).astype(BF16)
    v_ref[...] = _dot(ckv, wv_ref[...]).astype(BF16)


def _mla_prep_call(misc, g_cq, g_ckv, wq, wk, wv, cos, sin, seq, tm):
    t = misc.shape[0]
    nper = seq // tm
    hw = N_HEADS_PER_MIXER * LANES
    vw = N_HEADS_PER_MIXER * HEAD_DIM
    full = lambda a: pl.BlockSpec(a.shape, lambda i: (0, 0))
    return pl.pallas_call(
        _mla_prep_kernel,
        grid=(t // tm,),
        in_specs=[
            pl.BlockSpec((tm, MISC_W), lambda i: (i, 0)),
            full(g_cq), full(g_ckv), full(wq), full(wk), full(wv),
            pl.BlockSpec((tm, LANES), lambda i: (i % nper, 0)),
            pl.BlockSpec((tm, LANES), lambda i: (i % nper, 0)),
        ],
        out_specs=[
            pl.BlockSpec((tm, hw), lambda i: (i, 0)),
            pl.BlockSpec((tm, hw), lambda i: (i, 0)),
            pl.BlockSpec((tm, vw), lambda i: (i, 0)),
        ],
        out_shape=[
            jax.ShapeDtypeStruct((t, hw), BF16),
            jax.ShapeDtypeStruct((t, hw), BF16),
            jax.ShapeDtypeStruct((t, vw), BF16),
        ],
        compiler_params=_cparams(("parallel",)),
        name="mla_prep",
    )(misc, g_cq, g_ckv, wq, wk, wv, cos, sin)


def _attn_kernel(*refs, mode, tq, tk, scale):
    if mode == "sb":
        q_ref, k_ref, v_ref, g_ref, u_ref, o_ref = refs
    elif mode == "fox":
        q_ref, k_ref, v_ref, g_ref, nc_ref, o_ref = refs
    else:
        q_ref, k_ref, v_ref, g_ref, o_ref = refs
    nh = N_HEADS_PER_MIXER
    i = pl.program_id(1)
    lane = _lane_iota((tq, LANES))
    first = lane < HEAD_DIM
    row = lax.broadcasted_iota(jnp.int32, (tq, tk), 0)
    colm = lax.broadcasted_iota(jnp.int32, (tq, tk), 1)

    q_heads = []
    for h in range(nh):
        if mode == "mla":
            q_heads.append(q_ref[:, h * LANES:(h + 1) * LANES])
        else:
            q2 = q_ref[:, (h // 2) * LANES:(h // 2 + 1) * LANES]
            zero = jnp.zeros_like(q2)
            q_heads.append(jnp.where(first, q2, zero) if h % 2 == 0 else jnp.where(first, zero, q2))

    def k_head(j, h):
        r0 = pl.multiple_of(j * tk, tk)
        kb = h if mode == "mla" else h // 2
        return k_ref[pl.ds(r0, tk), kb * LANES:(kb + 1) * LANES]

    def v_pair(j, p):
        r0 = pl.multiple_of(j * tk, tk)
        return v_ref[pl.ds(r0, tk), p * LANES:(p + 1) * LANES]

    if mode == "sb":
        def step(j, carry, diag):
            accs, rs = carry
            new_accs, new_rs = [], []
            for p in range(nh // 2):
                v2 = v_pair(j, p)
                outs = []
                for h in (2 * p, 2 * p + 1):
                    z = _dot_nt(q_heads[h], k_head(j, h))
                    ls_pos, ls_neg = _log_sigmoid_pair(z)
                    if diag:
                        before = colm < row
                        ls_neg = jnp.where(before, ls_neg, 0.0)
                    c = _dot(ls_neg.astype(BF16), u_ref[...])
                    w = jnp.exp(ls_pos + c + rs[h])
                    if diag:
                        w = jnp.where(before, w, 0.0)
                    outs.append(_dot(w.astype(BF16), v2))
                    new_rs.append(rs[h] + (c[:, 0:1] + ls_neg[:, 0:1]))
                new_accs.append(accs[p] + jnp.where(first, outs[0], outs[1]))
            return tuple(new_accs), tuple(new_rs)

        zacc = jnp.zeros((tq, LANES), F32)
        zr = jnp.zeros((tq, 1), F32)
        carry = step(i, ((zacc,) * (nh // 2), (zr,) * nh), True)
        carry = lax.fori_loop(0, i, lambda n, c: step(i - 1 - n, c, False), carry)
        outs = carry[0]
    else:
        ones = jnp.ones((tk, LANES), BF16)
        jd = (i * tq) // tk
        off = i * tq - jd * tk

        def step(j, carry, diag):
            accs, ms, ls = carry
            new_accs, new_ms, new_ls = [], [], []
            r0 = pl.multiple_of(j * tk, tk)
            for p in range(nh // 2):
                v_aug = jnp.concatenate([v_pair(j, p), ones], axis=1)
                pv, alphas = [], []
                for h in (2 * p, 2 * p + 1):
                    s = _dot_nt(q_heads[h], k_head(j, h))
                    if mode == "mla":
                        s = s * scale
                    else:
                        s = s + nc_ref[0, h:h + 1, pl.ds(r0, tk)]
                    if diag:
                        s = jnp.where(colm <= row + off, s, -jnp.inf)
                    m_new = jnp.maximum(ms[h], jnp.max(s, axis=-1, keepdims=True))
                    alpha = jnp.exp(ms[h] - m_new)
                    pvx = _dot(jnp.exp(s - m_new).astype(BF16), v_aug)
                    new_ls.append(alpha * ls[h] + pvx[:, LANES:2 * LANES])
                    new_ms.append(m_new)
                    pv.append(pvx[:, 0:LANES])
                    alphas.append(alpha)
                new_accs.append(accs[p] * jnp.where(first, alphas[0], alphas[1])
                                + jnp.where(first, pv[0], pv[1]))
            return tuple(new_accs), tuple(new_ms), tuple(new_ls)

        neg = jnp.full((tq, 1), -jnp.inf, F32)
        zacc = jnp.zeros((tq, LANES), F32)
        carry = step(jd, ((zacc,) * (nh // 2), (neg,) * nh, (zacc,) * nh), True)
        carry = lax.fori_loop(0, jd, lambda n, c: step(n, c, False), carry)
        accs, _, ls = carry
        outs = [accs[p] / jnp.where(first, ls[2 * p], ls[2 * p + 1]) for p in range(nh // 2)]
    for p in range(nh // 2):
        sl = slice(p * LANES, (p + 1) * LANES)
        o_ref[:, sl] = _head_rms(outs[p], g_ref[:, sl]).astype(o_ref.dtype)


def _attn_call(mode, q, k, v, qcol, kcol, vcol, g_flat, gcol, bsz, seq, tq, tk, extra=(), scale=1.0):
    t = bsz * seq
    nq = seq // tq
    hq = N_HEADS_PER_MIXER * HEAD_DIM
    qw = N_HEADS_PER_MIXER * LANES if mode == "mla" else hq
    assert tk % tq == 0 and seq % tk == 0 and (mode != "sb" or tk == tq)
    in_specs = [
        pl.BlockSpec((tq, qw), lambda b, i: (b * nq + i, qcol)),
        pl.BlockSpec((seq, qw), lambda b, i: (b, kcol)),
        pl.BlockSpec((seq, hq), lambda b, i: (b, vcol)),
        pl.BlockSpec((1, hq), lambda b, i: (0, gcol)),
    ]
    args = [q, k, v, g_flat]
    if mode == "sb":
        (u,) = extra
        in_specs.append(pl.BlockSpec(u.shape, lambda b, i: (0, 0)))
        args.append(u)
    elif mode == "fox":
        (nc,) = extra
        in_specs.append(pl.BlockSpec((1, N_HEADS_PER_MIXER, seq), lambda b, i: (b, 0, 0)))
        args.append(nc)
    return pl.pallas_call(
        functools.partial(_attn_kernel, mode=mode, tq=tq, tk=tk, scale=scale),
        grid=(bsz, nq),
        in_specs=in_specs,
        out_specs=pl.BlockSpec((tq, hq), lambda b, i: (b * nq + i, 0)),
        out_shape=jax.ShapeDtypeStruct((t, hq), BF16),
        compiler_params=_cparams(("parallel", "arbitrary")),
        name=f"attn_{mode}",
    )(*args)


def _dil_kernel(qkv_ref, o_ref, *, tqs, win):
    tb = win
    c = pl.program_id(2)
    lane = _lane_iota((tb, LANES))
    first = lane < HEAD_DIM
    row = lax.broadcasted_iota(jnp.int32, (tb, tb), 0)
    colm = lax.broadcasted_iota(jnp.int32, (tb, tb), 1)
    hq = N_HEADS_PER_MIXER * HEAD_DIM

    def tile(it, _):
        gi = c * (tqs // tb) + it
        r0 = pl.multiple_of(gi * tb, tb)
        rp = pl.multiple_of(jnp.maximum(gi - 1, 0) * tb, tb)
        no_prev = jnp.where(gi > 0, 0, tb)
        o0 = pl.multiple_of(it * tb, tb)
        for p in range(N_HEADS_PER_MIXER // 2):
            q2 = qkv_ref[0, 0, pl.ds(r0, tb), p * LANES:(p + 1) * LANES]
            kd = qkv_ref[0, 0, pl.ds(r0, tb), hq + p * LANES:hq + (p + 1) * LANES]
            kp = qkv_ref[0, 0, pl.ds(rp, tb), hq + p * LANES:hq + (p + 1) * LANES]
            vd = qkv_ref[0, 0, pl.ds(r0, tb), 2 * hq + p * LANES:2 * hq + (p + 1) * LANES]
            vp = qkv_ref[0, 0, pl.ds(rp, tb), 2 * hq + p * LANES:2 * hq + (p + 1) * LANES]
            zero = jnp.zeros_like(q2)
            outs = []
            lses = []
            for hh in range(2):
                qh = jnp.where(first, q2, zero) if hh == 0 else jnp.where(first, zero, q2)
                sd = jnp.where(colm <= row, _dot_nt(qh, kd), -jnp.inf)
                sp = jnp.where(colm >= row + no_prev, _dot_nt(qh, kp), -jnp.inf)
                m = jnp.maximum(jnp.max(sd, axis=-1, keepdims=True), jnp.max(sp, axis=-1, keepdims=True))
                pd = jnp.exp(sd - m)
                pp = jnp.exp(sp - m)
                l = jnp.sum(pd, axis=-1, keepdims=True) + jnp.sum(pp, axis=-1, keepdims=True)
                inv = 1.0 / l
                acc = _dot((pd * inv).astype(BF16), vd) + _dot((pp * inv).astype(BF16), vp)
                outs.append(acc)
                lses.append(m + jnp.log(l))
            o_ref[0, pl.ds(o0, tb), p * LANES:(p + 1) * LANES] = jnp.where(first, outs[0], outs[1])
            o_ref[0, pl.ds(o0, tb), hq + p * LANES:hq + (p + 1) * LANES] = jnp.where(first, lses[0], lses[1])
        return 0

    lax.fori_loop(0, tqs // tb, tile, 0, unroll=min(4, tqs // tb))


def _dil_call(qkv, tqs, win):
    bsz, r, n, w = qkv.shape
    hq = N_HEADS_PER_MIXER * HEAD_DIM
    return pl.pallas_call(
        functools.partial(_dil_kernel, tqs=tqs, win=win),
        grid=(bsz, r, n // tqs),
        in_specs=[pl.BlockSpec((1, 1, n, w), lambda b, p, c: (b, p, 0, 0))],
        out_specs=pl.BlockSpec((1, tqs, 2 * hq), lambda b, p, c: (b, c, p)),
        out_shape=jax.ShapeDtypeStruct((bsz, n, r * 2 * hq), F32),
        compiler_params=_cparams(("parallel", "parallel", "arbitrary")),
        name=f"dil_r{r}",
    )(qkv)


def _post_kernel(x_ref, osb_ref, ofox_ref, omla_ref, d1_ref, d2_ref, d3_ref, gd_ref, wo_ref,
                 lng_ref, lnb_ref, wr_ref, br_ref, x1_ref, route_ref, *, alpha):
    hq = N_HEADS_PER_MIXER * HEAD_DIM
    h = _dot(osb_ref[...], wo_ref[0:hq, :])
    h += _dot(ofox_ref[...], wo_ref[hq:2 * hq, :])
    h += _dot(omla_ref[...], wo_ref[2 * hq:3 * hq, :])
    for p in range(N_HEADS_PER_MIXER // 2):
        sl = slice(p * LANES, (p + 1) * LANES)
        ll = slice(hq + p * LANES, hq + (p + 1) * LANES)
        l1, l2, l3 = d1_ref[:, ll], d2_ref[:, ll], d3_ref[:, ll]
        m = jnp.maximum(jnp.maximum(l1, l2), l3)
        e1, e2, e3 = jnp.exp(l1 - m), jnp.exp(l2 - m), jnp.exp(l3 - m)
        inv = 1.0 / (e1 + e2 + e3)
        od = (e1 * inv) * d1_ref[:, sl] + (e2 * inv) * d2_ref[:, sl] + (e3 * inv) * d3_ref[:, sl]
        od = _head_rms(od, gd_ref[:, sl]).astype(BF16)
        h += _dot(od, wo_ref[3 * hq + p * LANES:3 * hq + (p + 1) * LANES, :])
    x1 = _layernorm(alpha * x_ref[...] + h, lng_ref[...], lnb_ref[...])
    x1_ref[...] = x1

    logits = _dot_f32(x1, wr_ref[...]) + br_ref[...]
    lane = _lane_iota(logits.shape)
    lanef = lane.astype(F32)
    big = float(LANES)
    ninf = -jnp.inf
    gl = jnp.where(lane < N_GROUPS, logits, ninf)
    gmax = jnp.max(gl, axis=-1, keepdims=True)
    gsel = jnp.min(jnp.where(gl == gmax, lanef, big), axis=-1, keepdims=True)
    gw = 1.0 / jnp.sum(jnp.exp(gl - gmax), axis=-1, keepdims=True)
    e_lo = N_GROUPS + EXPERTS_PER_GROUP * gsel
    in_grp = (lanef >= e_lo) & (lanef < e_lo + EXPERTS_PER_GROUP)
    el = jnp.where(in_grp, logits, ninf)
    t1 = jnp.max(el, axis=-1, keepdims=True)
    i1 = jnp.min(jnp.where(el == t1, lanef, big), axis=-1, keepdims=True)
    el2 = jnp.where(lanef == i1, ninf, el)
    t2 = jnp.max(el2, axis=-1, keepdims=True)
    i2 = jnp.min(jnp.where(el2 == t2, lanef, big), axis=-1, keepdims=True)
    ex = jnp.exp(t2 - t1)
    den = 1.0 + ex
    g1 = gw / den
    g2 = gw * ex / den
    out = jnp.where(lane == 0, i1 - N_GROUPS,
                    jnp.where(lane == 1, i2 - N_GROUPS,
                              jnp.where(lane == 2, g1, jnp.where(lane == 3, g2, 0.0))))
    route_ref[...] = out


def _post_call(x, osb, ofox, omla, d1, d2, d3, g_dil, wo, lng, lnb, wr, br, alpha, tm):
    t, d = x.shape
    hq = N_HEADS_PER_MIXER * HEAD_DIM
    row = lambda w: pl.BlockSpec((tm, w), lambda i: (i, 0))
    full = lambda a: pl.BlockSpec(a.shape, lambda i: (0, 0))
    return pl.pallas_call(
        functools.partial(_post_kernel, alpha=alpha),
        grid=(t // tm,),
        in_specs=[row(d), row(hq), row(hq), row(hq), row(2 * hq), row(2 * hq), row(2 * hq),
                  full(g_dil), full(wo), full(lng), full(lnb), full(wr), full(br)],
        out_specs=[row(d), row(LANES)],
        out_shape=[jax.ShapeDtypeStruct((t, d), F32), jax.ShapeDtypeStruct((t, LANES), F32)],
        compiler_params=_cparams(("parallel",)),
        name="post_mixer",
    )(x, osb, ofox, omla, d1, d2, d3, g_dil, wo, lng, lnb, wr, br)


DMA_UNROLL = 8


def _moe_kernel(ce_ref, nv_ref, src_ref, srcn_ref, dst_ref, x_hbm, w1_ref, w3_ref, w2_ref, y_hbm,
                xs_ref, ys_ref, gsem, ssem):
    c = pl.program_id(0)
    nc = pl.num_programs(0)
    slot = c % 2
    other = 1 - slot
    nv = nv_ref[c]
    nv_next = jnp.where(c + 1 < nc, nv_ref[jnp.minimum(c + 1, nc - 1)], 0)
    nv_prev = jnp.where(c > 0, nv_ref[jnp.maximum(c - 1, 0)], 0)

    def gather_copy(idx_ref, i, s):
        return pltpu.make_async_copy(x_hbm.at[pl.ds(idx_ref[0, 0, i], 1), :],
                                     xs_ref.at[s, pl.ds(i, 1), :], gsem.at[s])

    def scatter_copy(i, s):
        return pltpu.make_async_copy(ys_ref.at[s, pl.ds(i, 1), :],
                                     y_hbm.at[pl.ds(dst_ref[0, 0, i], 1), :], ssem.at[s])

    def start_rows(n, make):
        ngrp = n // DMA_UNROLL

        def grp(g, _):
            for k in range(DMA_UNROLL):
                make(g * DMA_UNROLL + k).start()
            return 0

        def one(i, _):
            make(i).start()
            return 0

        lax.fori_loop(0, ngrp, grp, 0)
        lax.fori_loop(ngrp * DMA_UNROLL, n, one, 0)

    def wait_rows(n, make_row, make_block):
        @pl.when(n == MOE_BLOCK)
        def _():
            make_block().wait()

        @pl.when(n < MOE_BLOCK)
        def _():
            def one(i, _):
                make_row(i).wait()
                return 0
            lax.fori_loop(0, n, one, 0)

    def gather_block(s):
        return pltpu.make_async_copy(x_hbm.at[pl.ds(0, MOE_BLOCK), :], xs_ref.at[s], gsem.at[s])

    def scatter_block(s):
        return pltpu.make_async_copy(ys_ref.at[s], y_hbm.at[pl.ds(0, MOE_BLOCK), :], ssem.at[s])

    @pl.when(c == 0)
    def _():
        xs_ref[...] = jnp.zeros_like(xs_ref)
        start_rows(nv, lambda i: gather_copy(src_ref, i, 0))

    start_rows(nv_next, lambda i: gather_copy(srcn_ref, i, other))
    wait_rows(nv, lambda i: gather_copy(src_ref, i, slot), lambda: gather_block(slot))

    @pl.when(nv > 0)
    def _():
        xb = xs_ref[slot].astype(BF16)
        a = _dot(xb, w1_ref[0])
        b = _dot(xb, w3_ref[0])
        hid = (a / (1.0 + jnp.exp(-a)) * b).astype(BF16)
        ys_ref[slot] = _dot(hid, w2_ref[0])

    wait_rows(nv_prev, lambda i: scatter_copy(i, other), lambda: scatter_block(other))
    start_rows(nv, lambda i: scatter_copy(i, slot))

    @pl.when(c == nc - 1)
    def _():
        wait_rows(nv, lambda i: scatter_copy(i, slot), lambda: scatter_block(slot))


def _moe_call(chunk_expert, n_valid, src, dst, x1, w1, w3, w2, n_rows_out):
    n_chunks = chunk_expert.shape[0]
    t, d = x1.shape
    de = w1.shape[-1]
    grid_spec = pltpu.PrefetchScalarGridSpec(
        num_scalar_prefetch=2,
        grid=(n_chunks,),
        in_specs=[
            pl.BlockSpec((1, 1, MOE_BLOCK), lambda c, ce, nv: (c, 0, 0), memory_space=pltpu.SMEM),
            pl.BlockSpec((1, 1, MOE_BLOCK), lambda c, ce, nv: (jnp.minimum(c + 1, n_chunks - 1), 0, 0),
                         memory_space=pltpu.SMEM),
            pl.BlockSpec((1, 1, MOE_BLOCK), lambda c, ce, nv: (c, 0, 0), memory_space=pltpu.SMEM),
            pl.BlockSpec(memory_space=pl.ANY),
            pl.BlockSpec((1, d, de), lambda c, ce, nv: (ce[c], 0, 0)),
            pl.BlockSpec((1, d, de), lambda c, ce, nv: (ce[c], 0, 0)),
            pl.BlockSpec((1, de, d), lambda c, ce, nv: (ce[c], 0, 0)),
        ],
        out_specs=pl.BlockSpec(memory_space=pl.ANY),
        scratch_shapes=[
            pltpu.VMEM((2, MOE_BLOCK, d), F32),
            pltpu.VMEM((2, MOE_BLOCK, d), F32),
            pltpu.SemaphoreType.DMA((2,)),
            pltpu.SemaphoreType.DMA((2,)),
        ],
    )
    return pl.pallas_call(
        _moe_kernel,
        grid_spec=grid_spec,
        out_shape=jax.ShapeDtypeStruct((n_rows_out, d), F32),
        compiler_params=_cparams(("arbitrary",)),
        name="moe_experts",
    )(chunk_expert, n_valid, src, src, dst, x1, w1, w3, w2)


def _combine_kernel(x1_ref, ya_ref, yb_ref, route_ref, lng_ref, lnb_ref, o_ref, *, alpha):
    g1 = route_ref[:, 2:3]
    g2 = route_ref[:, 3:4]
    m = g1 * ya_ref[...] + g2 * yb_ref[...]
    o_ref[...] = _layernorm(alpha * x1_ref[...] + m, lng_ref[...], lnb_ref[...])


def _combine_call(x1, y, route, lng, lnb, alpha, tm):
    t, d = x1.shape
    nt = t // tm
    full = lambda a: pl.BlockSpec(a.shape, lambda i: (0, 0))
    return pl.pallas_call(
        functools.partial(_combine_kernel, alpha=alpha),
        grid=(nt,),
        in_specs=[pl.BlockSpec((tm, d), lambda i: (i, 0)),
                  pl.BlockSpec((tm, d), lambda i: (i, 0)),
                  pl.BlockSpec((tm, d), lambda i: (i + nt, 0)),
                  pl.BlockSpec((tm, LANES), lambda i: (i, 0)),
                  full(lng), full(lnb)],
        out_specs=pl.BlockSpec((tm, d), lambda i: (i, 0)),
        out_shape=jax.ShapeDtypeStruct((t, d), F32),
        compiler_params=_cparams(("parallel",)),
        name="moe_combine",
    )(x1, y, y, route, lng, lnb)


def _rope_tables(seq, dim, lane_lo):
    half = dim // 2
    inv_freq = ROPE_THETA ** (-jnp.arange(half, dtype=F32) / half)
    ang = jnp.arange(seq, dtype=F32)[:, None] * inv_freq[None, :]
    cos = jnp.concatenate([jnp.cos(ang), jnp.cos(ang)], -1)
    sin = jnp.concatenate([jnp.sin(ang), jnp.sin(ang)], -1)
    if lane_lo == 0:
        reps = LANES // dim
        return jnp.tile(cos, (1, reps)), jnp.tile(sin, (1, reps))
    cos_t = jnp.ones((seq, LANES), F32).at[:, lane_lo:lane_lo + dim].set(cos)
    sin_t = jnp.zeros((seq, LANES), F32).at[:, lane_lo:lane_lo + dim].set(sin)
    return cos_t, sin_t


def _dispatch_tables(route, n_tok):
    expert_id = route[:, 0:TOP_K].astype(jnp.int32).reshape(-1)
    n_assign = n_tok * TOP_K
    n_slots = n_assign + N_EXPERTS * MOE_BLOCK
    n_chunks = n_slots // MOE_BLOCK
    onehot = (expert_id[:, None] == jnp.arange(N_EXPERTS, dtype=jnp.int32)[None, :]).astype(jnp.int32)
    ranks = jnp.cumsum(onehot, axis=0) - onehot
    rank = jnp.sum(ranks * onehot, axis=1)
    counts = jnp.sum(onehot, axis=0)
    padded = (counts + MOE_BLOCK - 1) // MOE_BLOCK * MOE_BLOCK
    pad_end = jnp.cumsum(padded)
    pad_start = pad_end - padded
    dest = pad_start[expert_id] + rank
    assign = jnp.arange(n_assign, dtype=jnp.int32)
    slot_assign = jnp.zeros((n_slots,), jnp.int32).at[dest].set(assign)
    src = slot_assign // TOP_K
    dst = (slot_assign % TOP_K) * n_tok + src
    chunk_start = jnp.arange(n_chunks, dtype=jnp.int32) * MOE_BLOCK
    chunk_expert = jnp.minimum(jnp.searchsorted(pad_end, chunk_start, side="right"),
                               N_EXPERTS - 1).astype(jnp.int32)
    n_valid = jnp.clip(pad_start[chunk_expert] + counts[chunk_expert] - chunk_start,
                       0, MOE_BLOCK).astype(jnp.int32)
    return (chunk_expert, n_valid, src.reshape(n_chunks, 1, MOE_BLOCK),
            dst.reshape(n_chunks, 1, MOE_BLOCK), n_assign)


def _pick_tile(n, pref):
    t = pref
    while n % t:
        t //= 2
    return t


def kernel(x, w_in, b_forget, g_cq, g_ckv, w_uq, w_ukv, g_head, w_out, ln1_g, ln1_b,
           w_group, b_group, w_expert, b_expert, w1, w3, w2, ln2_g, ln2_b):
    bsz, seq, d = x.shape
    depth = w_in.shape[0]
    t = bsz * seq
    alpha = (2.0 * depth) ** 0.25
    hq = N_HEADS_PER_MIXER * HEAD_DIM
    qk_scale = HEAD_DIM ** -0.5
    mla_scale = (MLA_NOPE + MLA_ROPE) ** -0.5
    win = DIL_BRANCHES[0][0]
    assert all(w // r == win for w, r in DIL_BRANCHES)
    assert seq % (DIL_BRANCHES[-1][1] * win) == 0 and d % LANES == 0

    cos64, sin64 = _rope_tables(seq, HEAD_DIM, 0)
    cos_m, sin_m = _rope_tables(seq, MLA_ROPE, MLA_NOPE)
    tq = _pick_tile(seq, 256)
    idx = jnp.arange(tq)
    u_sb = (idx[:, None] > idx[None, :]).astype(BF16)
    tk_sm = _pick_tile(seq, 2 * tq)

    for l in range(depth):
        wl = w_in[l]
        o_fox, o_mla, o_dil = N_SB, N_SB + N_FOX_QKV + N_HEADS_PER_MIXER, N_SB + N_FOX_QKV + N_HEADS_PER_MIXER + N_MLA
        qs = lambda w: w.at[:, 0:hq].multiply(qk_scale)
        w_sb = qs(wl[:, 0:N_SB])
        w_fx = qs(wl[:, o_fox:o_fox + N_FOX_QKV])
        w_f = wl[:, o_fox + N_FOX_QKV:o_mla]
        w_ml = wl[:, o_mla:o_dil]
        wd = wl[:, o_dil:].reshape(d, 3, len(DIL_BRANCHES), hq)
        w_br = [qs(jnp.concatenate([wd[:, 0, g], wd[:, 1, g], wd[:, 2, g]], axis=1)) for g in range(len(DIL_BRANCHES))]
        w_misc = jnp.concatenate([w_ml, w_f, jnp.zeros((d, MISC_W - N_MLA - N_HEADS_PER_MIXER), F32)], axis=1)
        w_tok = jnp.concatenate([w_sb, w_fx, w_misc] + w_br, axis=1).astype(BF16)

        wq = jnp.pad(w_uq[l].reshape(MLA_Q_LORA, N_HEADS_PER_MIXER, MLA_NOPE + MLA_ROPE),
                     ((0, 0), (0, 0), (0, LANES - MLA_NOPE - MLA_ROPE))).reshape(MLA_Q_LORA, -1).astype(BF16)
        wkv = w_ukv[l].reshape(MLA_KV_LORA, N_HEADS_PER_MIXER, MLA_NOPE + HEAD_DIM)
        wk = jnp.pad(wkv[:, :, :MLA_NOPE], ((0, 0), (0, 0), (0, LANES - MLA_NOPE))).reshape(MLA_KV_LORA, -1).astype(BF16)
        wv = wkv[:, :, MLA_NOPE:].reshape(MLA_KV_LORA, -1).astype(BF16)
        g_flat = g_head[l].reshape(1, -1)
        wr = jnp.concatenate([w_group[l], w_expert[l],
                              jnp.zeros((d, LANES - N_GROUPS - N_EXPERTS), F32)], axis=1)
        br = jnp.concatenate([b_group[l], b_expert[l],
                              jnp.zeros((LANES - N_GROUPS - N_EXPERTS,), F32)]).reshape(1, LANES)

        tm = _pick_tile(seq, 512)
        sb, fx, misc, *qkv_br = _proj_call(
            x, w_tok, cos64, sin64,
            ((N_SB, 0, BF16, 1), (N_FOX_QKV, 0, BF16, 1), (MISC_W, 0, F32, 1))
            + tuple((N_BRANCH, 2 * hq, BF16, r) for _, r in DIL_BRANCHES), tm)
        sb = sb.reshape(t, N_SB)
        fx = fx.reshape(t, N_FOX_QKV)
        misc = misc.reshape(t, MISC_W)

        o_sb = _attn_call("sb", sb, sb, sb, 0, 1, 2, g_flat, 0, bsz, seq, tq, tq, extra=(u_sb,))
        neg_c = _fox_c_call(misc, b_forget[l], bsz, seq)
        o_fx = _attn_call("fox", fx, fx, fx, 0, 1, 2, g_flat, 1, bsz, seq, tq, tk_sm, extra=(neg_c,))
        mq, mk, mv = _mla_prep_call(misc, g_cq[l].reshape(1, -1), g_ckv[l].reshape(1, -1), wq, wk, wv,
                                    cos_m, sin_m, seq, tm)
        o_ml = _attn_call("mla", mq, mk, mv, 0, 0, 0, g_flat, 2, bsz, seq, tq, tk_sm, scale=mla_scale)

        dil = []
        for g, (_, r) in enumerate(DIL_BRANCHES):
            n = seq // r
            og = _dil_call(qkv_br[g], _pick_tile(n, 1024), win)
            dil.append(og.reshape(t, 2 * hq))

        x1, route = _post_call(
            x.reshape(t, d), o_sb, o_fx, o_ml, dil[0], dil[1], dil[2], g_flat[:, 3 * hq:], w_out[l].astype(BF16),
            ln1_g[l].reshape(1, d), ln1_b[l].reshape(1, d), wr, br, alpha, _pick_tile(t, 256))

        chunk_expert, n_valid, src, dst, n_rows = _dispatch_tables(route, t)
        y = _moe_call(chunk_expert, n_valid, src, dst, x1,
                      w1[l].astype(BF16), w3[l].astype(BF16), w2[l].astype(BF16), n_rows)
        x = _combine_call(x1, y, route,
                          ln2_g[l].reshape(1, d), ln2_b[l].reshape(1, d), alpha, _pick_tile(t, 256)).reshape(bsz, seq, d)
    return x
```

```python
import functools

import jax
import jax.numpy as jnp
import numpy as np
from jax import lax
from jax.experimental import pallas as pl
from jax.experimental.pallas import tpu as pltpu

F32 = jnp.float32
BF16 = jnp.bfloat16

HEAD_DIM = 64
N_HEADS_PER_MIXER = 4
MLA_Q_LORA = 256
MLA_KV_LORA = 128
MLA_NOPE = 64
MLA_ROPE = 32
DIL_BRANCHES = ((128, 1), (512, 4), (2048, 16))
ROPE_THETA = 10000.0
N_GROUPS = 4
EXPERTS_PER_GROUP = 4
N_EXPERTS = N_GROUPS * EXPERTS_PER_GROUP
TOP_K = 2
MOE_BLOCK = 256
LN_EPS = 1e-5
RMS_EPS = 1e-6

LANES = 128
VMEM_LIMIT_BYTES = 56 * 1024 * 1024

N_SB = 3 * N_HEADS_PER_MIXER * HEAD_DIM
N_FOX_QKV = 3 * N_HEADS_PER_MIXER * HEAD_DIM
N_MLA = MLA_Q_LORA + MLA_KV_LORA + MLA_ROPE
N_BRANCH = 3 * N_HEADS_PER_MIXER * HEAD_DIM
MISC_W = 512
F_COL = N_MLA


def _cparams(sem):
    return pltpu.CompilerParams(dimension_semantics=sem, vmem_limit_bytes=VMEM_LIMIT_BYTES)


def _split3(a):
    hi = a.astype(BF16)
    r1 = a - hi.astype(F32)
    mid = r1.astype(BF16)
    lo = (r1 - mid.astype(F32)).astype(BF16)
    return hi, mid, lo


def _dot(a, b):
    return jnp.dot(a, b, preferred_element_type=F32)


def _dot_nt(a, b):
    return lax.dot_general(a, b, (((1,), (1,)), ((), ())), preferred_element_type=F32)


def _dot_exact_rhs(a, u):
    hi, mid, lo = _split3(a)
    return _dot(hi, u) + _dot(mid, u) + _dot(lo, u)


def _dot_f32(a, b):
    ah = a.astype(BF16)
    al = (a - ah.astype(F32)).astype(BF16)
    bh = b.astype(BF16)
    bl = (b - bh.astype(F32)).astype(BF16)
    return _dot(ah, bh) + (_dot(ah, bl) + _dot(al, bh))


def _lane_iota(shape):
    return lax.broadcasted_iota(jnp.int32, shape, len(shape) - 1)


def _rotate_half(y, half):
    lane = _lane_iota(y.shape)
    fwd = pltpu.roll(y, half, 1)
    bwd = pltpu.roll(y, LANES - half, 1)
    return jnp.where((lane % (2 * half)) < half, -bwd, fwd)


def _log_sigmoid_pair(z):
    sp = jnp.log(1.0 + jnp.exp(-jnp.abs(z)))
    return jnp.minimum(z, 0.0) - sp, -jnp.maximum(z, 0.0) - sp


def _head_rms(o, g):
    lane = _lane_iota(o.shape)
    first = lane < HEAD_DIM
    sq = o * o
    ss_a = jnp.sum(jnp.where(first, sq, 0.0), axis=-1, keepdims=True)
    ss_b = jnp.sum(jnp.where(first, 0.0, sq), axis=-1, keepdims=True)
    ms = jnp.where(first, ss_a, ss_b) * (1.0 / HEAD_DIM)
    return o * lax.rsqrt(ms + RMS_EPS) * g


def _layernorm(y, g, b):
    mu = jnp.mean(y, axis=-1, keepdims=True)
    d = y - mu
    var = jnp.mean(d * d, axis=-1, keepdims=True)
    return d * lax.rsqrt(var + LN_EPS) * g + b


def _proj_kernel(x_ref, w_ref, cos_ref, sin_ref, *refs, outs):
    out_refs, stage_ref = refs[:len(outs)], refs[len(outs)]
    tm = x_ref.shape[1]
    xb = x_ref[0].astype(BF16)
    col = 0
    slab = 0
    for o_ref, (width, n_rope, _, r) in zip(out_refs, outs):
        for c in range(0, width, 2 * LANES):
            cw = min(2 * LANES, width - c)
            y = _dot(xb, w_ref[:, col + c:col + c + cw])
            for s in range(0, cw, LANES):
                ys = y[:, s:s + LANES]
                if c + s < n_rope:
                    ys = ys * cos_ref[...] + _rotate_half(ys, HEAD_DIM // 2) * sin_ref[...]
                if r == 1:
                    o_ref[0, 0, :, c + s:c + s + LANES] = ys.astype(o_ref.dtype)
                else:
                    st = stage_ref.at[slab % stage_ref.shape[0]]
                    slab += 1
                    st[...] = ys
                    for p in range(r):
                        o_ref[0, p, :, c + s:c + s + LANES] = (
                            st[pl.ds(p, tm // r, stride=r), :].astype(o_ref.dtype))
        col += width


PROJ_STAGE_SLABS = 4


def _proj_call(x, w, cos, sin, outs, tm):
    bsz, seq, d = x.shape
    nt = seq // tm
    in_specs = [
        pl.BlockSpec((1, tm, d), lambda b, i: (b, i, 0)),
        pl.BlockSpec(w.shape, lambda b, i: (0, 0)),
        pl.BlockSpec((tm, LANES), lambda b, i: (i, 0)),
        pl.BlockSpec((tm, LANES), lambda b, i: (i, 0)),
    ]
    out_specs = [pl.BlockSpec((1, r, tm // r, wd), lambda b, i: (b, 0, i, 0)) for wd, _, _, r in outs]
    out_shape = [jax.ShapeDtypeStruct((bsz, r, seq // r, wd), dt) for wd, _, dt, r in outs]
    return pl.pallas_call(
        functools.partial(_proj_kernel, outs=outs),
        grid=(bsz, nt), in_specs=in_specs, out_specs=out_specs, out_shape=out_shape,
        scratch_shapes=[pltpu.VMEM((PROJ_STAGE_SLABS, tm, LANES), F32)],
        compiler_params=_cparams(("parallel", "parallel")),
        name="proj",
    )(x, w, cos, sin)


def _fox_c_kernel(misc_ref, bias_ref, uinc_ref, ones_ref, out_ref, *, seq):
    lane0 = F_COL - 3 * LANES
    nblk = seq // LANES

    def body(j, carry):
        r0 = pl.multiple_of(j * LANES, LANES)
        f = misc_ref[pl.ds(r0, LANES), :] + bias_ref[...]
        lf, _ = _log_sigmoid_pair(f)
        lft = lf.T
        csum = _dot_exact_rhs(lft, uinc_ref[...]) + carry
        tot = _dot_exact_rhs(lft, ones_ref[...])
        out_ref[0, :, pl.ds(r0, LANES)] = -csum[lane0:lane0 + N_HEADS_PER_MIXER, :]
        return carry + tot

    lax.fori_loop(0, nblk, body, jnp.zeros((LANES, LANES), F32))


def _fox_c_call(misc, b_forget, bsz, seq):
    bias = jnp.zeros((1, LANES), F32).at[0, F_COL - 3 * LANES:F_COL - 3 * LANES + N_HEADS_PER_MIXER].set(b_forget)
    idx = jnp.arange(LANES)
    uinc = (idx[:, None] <= idx[None, :]).astype(BF16)
    ones = jnp.ones((LANES, LANES), BF16)
    return pl.pallas_call(
        functools.partial(_fox_c_kernel, seq=seq),
        grid=(bsz,),
        in_specs=[
            pl.BlockSpec((seq, LANES), lambda b: (b, 3)),
            pl.BlockSpec((1, LANES), lambda b: (0, 0)),
            pl.BlockSpec((LANES, LANES), lambda b: (0, 0)),
            pl.BlockSpec((LANES, LANES), lambda b: (0, 0)),
        ],
        out_specs=pl.BlockSpec((1, N_HEADS_PER_MIXER, seq), lambda b: (b, 0, 0)),
        out_shape=jax.ShapeDtypeStruct((bsz, N_HEADS_PER_MIXER, seq), F32),
        compiler_params=_cparams(("parallel",)),
        name="fox_c",
    )(misc, bias, uinc, ones)


def _mla_prep_kernel(misc_ref, gq_ref, gkv_ref, wq_ref, wk_ref, wv_ref, cos_ref, sin_ref,
                     q_ref, k_ref, v_ref):
    def rms(x, g):
        return x * lax.rsqrt(jnp.mean(x * x, axis=-1, keepdims=True) + RMS_EPS) * g

    cq = rms(misc_ref[:, 0:MLA_Q_LORA], gq_ref[...]).astype(BF16)
    ckv = rms(misc_ref[:, MLA_Q_LORA:MLA_Q_LORA + MLA_KV_LORA], gkv_ref[...]).astype(BF16)
    kr_blk = misc_ref[:, 3 * LANES:4 * LANES]
    lane = _lane_iota(kr_blk.shape)
    in_rope = (lane >= MLA_NOPE) & (lane < MLA_NOPE + MLA_ROPE)
    kr = jnp.where(in_rope, pltpu.roll(kr_blk, MLA_NOPE, 1), 0.0)
    cos = cos_ref[...]
    sin = sin_ref[...]

    def rope(y):
        return y * cos + _rotate_half(y, MLA_ROPE // 2) * sin

    q = _dot(cq, wq_ref[...])
    k = _dot(ckv, wk_ref[...])
    for h in range(N_HEADS_PER_MIXER):
        sl = slice(h * LANES, (h + 1) * LANES)
        q_ref[:, sl] = rope(q[:, sl]).astype(BF16)
        k_ref[:, sl] = rope(k[:, sl] + kr).astype(BF16)
    v_ref[...] = _dot(ckv, wv_ref[...]).astype(BF16)


def _mla_prep_call(misc, g_cq, g_ckv, wq, wk, wv, cos, sin, seq, tm):
    t = misc.shape[0]
    nper = seq // tm
    hw = N_HEADS_PER_MIXER * LANES
    vw = N_HEADS_PER_MIXER * HEAD_DIM
    full = lambda a: pl.BlockSpec(a.shape, lambda i: (0, 0))
    return pl.pallas_call(
        _mla_prep_kernel,
        grid=(t // tm,),
        in_specs=[
            pl.BlockSpec((tm, MISC_W), lambda i: (i, 0)),
            full(g_cq), full(g_ckv), full(wq), full(wk), full(wv),
            pl.BlockSpec((tm, LANES), lambda i: (i % nper, 0)),
            pl.BlockSpec((tm, LANES), lambda i: (i % nper, 0)),
        ],
        out_specs=[
            pl.BlockSpec((tm, hw), lambda i: (i, 0)),
            pl.BlockSpec((tm, hw), lambda i: (i, 0)),
            pl.BlockSpec((tm, vw), lambda i: (i, 0)),
        ],
        out_shape=[
            jax.ShapeDtypeStruct((t, hw), BF16),
            jax.ShapeDtypeStruct((t, hw), BF16),
            jax.ShapeDtypeStruct((t, vw), BF16),
        ],
        compiler_params=_cparams(("parallel",)),
        name="mla_prep",
    )(misc, g_cq, g_ckv, wq, wk, wv, cos, sin)


def _attn_kernel(*refs, mode, tq, tk, scale):
    if mode == "sb":
        q_ref, k_ref, v_ref, g_ref, u_ref, o_ref = refs
    elif mode == "fox":
        q_ref, k_ref, v_ref, g_ref, nc_ref, o_ref = refs
    else:
        q_ref, k_ref, v_ref, g_ref, o_ref = refs
    nh = N_HEADS_PER_MIXER
    i = pl.program_id(1)
    lane = _lane_iota((tq, LANES))
    first = lane < HEAD_DIM
    row = lax.broadcasted_iota(jnp.int32, (tq, tk), 0)
    colm = lax.broadcasted_iota(jnp.int32, (tq, tk), 1)

    q_heads = []
    for h in range(nh):
        if mode == "mla":
            q_heads.append(q_ref[:, h * LANES:(h + 1) * LANES])
        else:
            q2 = q_ref[:, (h // 2) * LANES:(h // 2 + 1) * LANES]
            zero = jnp.zeros_like(q2)
            q_heads.append(jnp.where(first, q2, zero) if h % 2 == 0 else jnp.where(first, zero, q2))

    def k_head(j, h):
        r0 = pl.multiple_of(j * tk, tk)
        kb = h if mode == "mla" else h // 2
        return k_ref[pl.ds(r0, tk), kb * LANES:(kb + 1) * LANES]

    def v_pair(j, p):
        r0 = pl.multiple_of(j * tk, tk)
        return v_ref[pl.ds(r0, tk), p * LANES:(p + 1) * LANES]

    if mode == "sb":
        def step(j, carry, diag):
            accs, rs = carry
            new_accs, new_rs = [], []
            for p in range(nh // 2):
                v2 = v_pair(j, p)
                outs = []
                for h in (2 * p, 2 * p + 1):
                    z = _dot_nt(q_heads[h], k_head(j, h))
                    ls_pos, ls_neg = _log_sigmoid_pair(z)
                    if diag:
                        before = colm < row
                        ls_neg = jnp.where(before, ls_neg, 0.0)
                    c = _dot(ls_neg.astype(BF16), u_ref[...])
                    w = jnp.exp(ls_pos + c + rs[h])
                    if diag:
                        w = jnp.where(before, w, 0.0)
                    outs.append(_dot(w.astype(BF16), v2))
                    new_rs.append(rs[h] + (c[:, 0:1] + ls_neg[:, 0:1]))
                new_accs.append(accs[p] + jnp.where(first, outs[0], outs[1]))
            return tuple(new_accs), tuple(new_rs)

        zacc = jnp.zeros((tq, LANES), F32)
        zr = jnp.zeros((tq, 1), F32)
        carry = step(i, ((zacc,) * (nh // 2), (zr,) * nh), True)
        carry = lax.fori_loop(0, i, lambda n, c: step(i - 1 - n, c, False), carry)
        outs = carry[0]
    else:
        ones = jnp.ones((tk, LANES), BF16)
        jd = (i * tq) // tk
        off = i * tq - jd * tk

        def step(j, carry, diag):
            accs, ms, ls = carry
            new_accs, new_ms, new_ls = [], [], []
            r0 = pl.multiple_of(j * tk, tk)
            for p in range(nh // 2):
                v_aug = jnp.concatenate([v_pair(j, p), ones], axis=1)
                pv, alphas = [], []
                for h in (2 * p, 2 * p + 1):
                    s = _dot_nt(q_heads[h], k_head(j, h))
                    if mode == "mla":
                        s = s * scale
                    else:
                        s = s + nc_ref[0, h:h + 1, pl.ds(r0, tk)]
                    if diag:
                        s = jnp.where(colm <= row + off, s, -jnp.inf)
                    m_new = jnp.maximum(ms[h], jnp.max(s, axis=-1, keepdims=True))
                    alpha = jnp.exp(ms[h] - m_new)
                    pvx = _dot(jnp.exp(s - m_new).astype(BF16), v_aug)
                    new_ls.append(alpha * ls[h] + pvx[:, LANES:2 * LANES])
                    new_ms.append(m_new)
                    pv.append(pvx[:, 0:LANES])
                    alphas.append(alpha)
                new_accs.append(accs[p] * jnp.where(first, alphas[0], alphas[1])
                                + jnp.where(first, pv[0], pv[1]))
            return tuple(new_accs), tuple(new_ms), tuple(new_ls)

        neg = jnp.full((tq, 1), -jnp.inf, F32)
        zacc = jnp.zeros((tq, LANES), F32)
        carry = step(jd, ((zacc,) * (nh // 2), (neg,) * nh, (zacc,) * nh), True)
        carry = lax.fori_loop(0, jd, lambda n, c: step(n, c, False), carry)
        accs, _, ls = carry
        outs = [accs[p] / jnp.where(first, ls[2 * p], ls[2 * p + 1]) for p in range(nh // 2)]
    for p in range(nh // 2):
        sl = slice(p * LANES, (p + 1) * LANES)
        o_ref[:, sl] = _head_rms(outs[p], g_ref[:, sl]).astype(o_ref.dtype)


def _attn_call(mode, q, k, v, qcol, kcol, vcol, g_flat, gcol, bsz, seq, tq, tk, extra=(), scale=1.0):
    t = bsz * seq
    nq = seq // tq
    hq = N_HEADS_PER_MIXER * HEAD_DIM
    qw = N_HEADS_PER_MIXER * LANES if mode == "mla" else hq
    assert tk % tq == 0 and seq % tk == 0 and (mode != "sb" or tk == tq)
    in_specs = [
        pl.BlockSpec((tq, qw), lambda b, i: (b * nq + i, qcol)),
        pl.BlockSpec((seq, qw), lambda b, i: (b, kcol)),
        pl.BlockSpec((seq, hq), lambda b, i: (b, vcol)),
        pl.BlockSpec((1, hq), lambda b, i: (0, gcol)),
    ]
    args = [q, k, v, g_flat]
    if mode == "sb":
        (u,) = extra
        in_specs.append(pl.BlockSpec(u.shape, lambda b, i: (0, 0)))
        args.append(u)
    elif mode == "fox":
        (nc,) = extra
        in_specs.append(pl.BlockSpec((1, N_HEADS_PER_MIXER, seq), lambda b, i: (b, 0, 0)))
        args.append(nc)
    return pl.pallas_call(
        functools.partial(_attn_kernel, mode=mode, tq=tq, tk=tk, scale=scale),
        grid=(bsz, nq),
        in_specs=in_specs,
        out_specs=pl.BlockSpec((tq, hq), lambda b, i: (b * nq + i, 0)),
        out_shape=jax.ShapeDtypeStruct((t, hq), BF16),
        compiler_params=_cparams(("parallel", "arbitrary")),
        name=f"attn_{mode}",
    )(*args)


def _dil_kernel(qkv_ref, o_ref, *, tq, tk, win):
    i = pl.program_id(2)
    lane = _lane_iota((tq, LANES))
    first = lane < HEAD_DIM
    hq = N_HEADS_PER_MIXER * HEAD_DIM
    r0 = pl.multiple_of(i * tq, tq)
    k0 = pl.multiple_of(jnp.maximum(i - 1, 0) * tq, tq)
    row = lax.broadcasted_iota(jnp.int32, (tq, tk), 0)
    colm = lax.broadcasted_iota(jnp.int32, (tq, tk), 1)
    delta = row + (r0 - k0) - colm
    band = jnp.abs(2 * delta - win) <= win
    ones = jnp.ones((tk, LANES), BF16)
    for p in range(N_HEADS_PER_MIXER // 2):
        q2 = qkv_ref[0, 0, pl.ds(r0, tq), p * LANES:(p + 1) * LANES]
        k2 = qkv_ref[0, 0, pl.ds(k0, tk), hq + p * LANES:hq + (p + 1) * LANES]
        v_aug = jnp.concatenate(
            [qkv_ref[0, 0, pl.ds(k0, tk), 2 * hq + p * LANES:2 * hq + (p + 1) * LANES], ones], axis=1)
        zero = jnp.zeros_like(q2)
        outs = []
        lses = []
        for hh in range(2):
            qh = jnp.where(first, q2, zero) if hh == 0 else jnp.where(first, zero, q2)
            s = jnp.where(band, _dot_nt(qh, k2), -jnp.inf)
            m = jnp.max(s, axis=-1, keepdims=True)
            pvx = _dot(jnp.exp(s - m).astype(BF16), v_aug)
            l = pvx[:, LANES:2 * LANES]
            outs.append(pvx[:, 0:LANES] / l)
            lses.append(m + jnp.log(l))
        o_ref[0, :, p * LANES:(p + 1) * LANES] = jnp.where(first, outs[0], outs[1])
        o_ref[0, :, hq + p * LANES:hq + (p + 1) * LANES] = jnp.where(first, lses[0], lses[1])


def _dil_call(qkv, tqs, win):
    bsz, r, n, w = qkv.shape
    hq = N_HEADS_PER_MIXER * HEAD_DIM
    tk = 2 * tqs if n >= 2 * tqs else tqs
    assert tqs >= win and n % tqs == 0 and (tk == 2 * tqs or n == tqs)
    return pl.pallas_call(
        functools.partial(_dil_kernel, tq=tqs, tk=tk, win=win),
        grid=(bsz, r, n // tqs),
        in_specs=[pl.BlockSpec((1, 1, n, w), lambda b, p, c: (b, p, 0, 0))],
        out_specs=pl.BlockSpec((1, tqs, 2 * hq), lambda b, p, c: (b, c, p)),
        out_shape=jax.ShapeDtypeStruct((bsz, n, r * 2 * hq), F32),
        compiler_params=_cparams(("parallel", "parallel", "arbitrary")),
        name=f"dil_r{r}",
    )(qkv)


def _post_kernel(x_ref, osb_ref, ofox_ref, omla_ref, d1_ref, d2_ref, d3_ref, gd_ref, wo_ref,
                 lng_ref, lnb_ref, wr_ref, br_ref, x1_ref, route_ref, *, alpha):
    hq = N_HEADS_PER_MIXER * HEAD_DIM
    h = _dot(osb_ref[...], wo_ref[0:hq, :])
    h += _dot(ofox_ref[...], wo_ref[hq:2 * hq, :])
    h += _dot(omla_ref[...], wo_ref[2 * hq:3 * hq, :])
    for p in range(N_HEADS_PER_MIXER // 2):
        sl = slice(p * LANES, (p + 1) * LANES)
        ll = slice(hq + p * LANES, hq + (p + 1) * LANES)
        l1, l2, l3 = d1_ref[:, ll], d2_ref[:, ll], d3_ref[:, ll]
        m = jnp.maximum(jnp.maximum(l1, l2), l3)
        e1, e2, e3 = jnp.exp(l1 - m), jnp.exp(l2 - m), jnp.exp(l3 - m)
        inv = 1.0 / (e1 + e2 + e3)
        od = (e1 * inv) * d1_ref[:, sl] + (e2 * inv) * d2_ref[:, sl] + (e3 * inv) * d3_ref[:, sl]
        od = _head_rms(od, gd_ref[:, sl]).astype(BF16)
        h += _dot(od, wo_ref[3 * hq + p * LANES:3 * hq + (p + 1) * LANES, :])
    x1 = _layernorm(alpha * x_ref[...] + h, lng_ref[...], lnb_ref[...])
    x1_ref[...] = x1

    logits = _dot_f32(x1, wr_ref[...]) + br_ref[...]
    lane = _lane_iota(logits.shape)
    lanef = lane.astype(F32)
    big = float(LANES)
    ninf = -jnp.inf
    gl = jnp.where(lane < N_GROUPS, logits, ninf)
    gmax = jnp.max(gl, axis=-1, keepdims=True)
    gsel = jnp.min(jnp.where(gl == gmax, lanef, big), axis=-1, keepdims=True)
    gw = 1.0 / jnp.sum(jnp.exp(gl - gmax), axis=-1, keepdims=True)
    e_lo = N_GROUPS + EXPERTS_PER_GROUP * gsel
    in_grp = (lanef >= e_lo) & (lanef < e_lo + EXPERTS_PER_GROUP)
    el = jnp.where(in_grp, logits, ninf)
    t1 = jnp.max(el, axis=-1, keepdims=True)
    i1 = jnp.min(jnp.where(el == t1, lanef, big), axis=-1, keepdims=True)
    el2 = jnp.where(lanef == i1, ninf, el)
    t2 = jnp.max(el2, axis=-1, keepdims=True)
    i2 = jnp.min(jnp.where(el2 == t2, lanef, big), axis=-1, keepdims=True)
    ex = jnp.exp(t2 - t1)
    den = 1.0 + ex
    g1 = gw / den
    g2 = gw * ex / den
    out = jnp.where(lane == 0, i1 - N_GROUPS,
                    jnp.where(lane == 1, i2 - N_GROUPS,
                              jnp.where(lane == 2, g1, jnp.where(lane == 3, g2, 0.0))))
    route_ref[...] = out


def _post_call(x, osb, ofox, omla, d1, d2, d3, g_dil, wo, lng, lnb, wr, br, alpha, tm):
    t, d = x.shape
    hq = N_HEADS_PER_MIXER * HEAD_DIM
    row = lambda w: pl.BlockSpec((tm, w), lambda i: (i, 0))
    full = lambda a: pl.BlockSpec(a.shape, lambda i: (0, 0))
    return pl.pallas_call(
        functools.partial(_post_kernel, alpha=alpha),
        grid=(t // tm,),
        in_specs=[row(d), row(hq), row(hq), row(hq), row(2 * hq), row(2 * hq), row(2 * hq),
                  full(g_dil), full(wo), full(lng), full(lnb), full(wr), full(br)],
        out_specs=[row(d), row(LANES)],
        out_shape=[jax.ShapeDtypeStruct((t, d), F32), jax.ShapeDtypeStruct((t, LANES), F32)],
        compiler_params=_cparams(("parallel",)),
        name="post_mixer",
    )(x, osb, ofox, omla, d1, d2, d3, g_dil, wo, lng, lnb, wr, br)


DMA_UNROLL = 8


def _moe_kernel(ce_ref, nv_ref, src_ref, srcn_ref, dst_ref, x_hbm, w1_ref, w3_ref, w2_ref, y_hbm,
                xs_ref, ys_ref, gsem, ssem):
    c = pl.program_id(0)
    nc = pl.num_programs(0)
    slot = c % 2
    other = 1 - slot
    nv = nv_ref[c]
    nv_next = jnp.where(c + 1 < nc, nv_ref[jnp.minimum(c + 1, nc - 1)], 0)
    nv_prev = jnp.where(c > 0, nv_ref[jnp.maximum(c - 1, 0)], 0)

    def gather_copy(idx_ref, i, s):
        return pltpu.make_async_copy(x_hbm.at[pl.ds(idx_ref[0, 0, i], 1), :],
                                     xs_ref.at[s, pl.ds(i, 1), :], gsem.at[s])

    def scatter_copy(i, s):
        return pltpu.make_async_copy(ys_ref.at[s, pl.ds(i, 1), :],
                                     y_hbm.at[pl.ds(dst_ref[0, 0, i], 1), :], ssem.at[s])

    def start_rows(n, make):
        ngrp = n // DMA_UNROLL

        def grp(g, _):
            for k in range(DMA_UNROLL):
                make(g * DMA_UNROLL + k).start()
            return 0

        def one(i, _):
            make(i).start()
            return 0

        lax.fori_loop(0, ngrp, grp, 0)
        lax.fori_loop(ngrp * DMA_UNROLL, n, one, 0)

    def wait_rows(n, make_row, make_block):
        @pl.when(n == MOE_BLOCK)
        def _():
            make_block().wait()

        @pl.when(n < MOE_BLOCK)
        def _():
            def one(i, _):
                make_row(i).wait()
                return 0
            lax.fori_loop(0, n, one, 0)

    def gather_block(s):
        return pltpu.make_async_copy(x_hbm.at[pl.ds(0, MOE_BLOCK), :], xs_ref.at[s], gsem.at[s])

    def scatter_block(s):
        return pltpu.make_async_copy(ys_ref.at[s], y_hbm.at[pl.ds(0, MOE_BLOCK), :], ssem.at[s])

    @pl.when(c == 0)
    def _():
        xs_ref[...] = jnp.zeros_like(xs_ref)
        start_rows(nv, lambda i: gather_copy(src_ref, i, 0))

    start_rows(nv_next, lambda i: gather_copy(srcn_ref, i, other))
    wait_rows(nv, lambda i: gather_copy(src_ref, i, slot), lambda: gather_block(slot))

    @pl.when(nv > 0)
    def _():
        xb = xs_ref[slot].astype(BF16)
        a = _dot(xb, w1_ref[0])
        b = _dot(xb, w3_ref[0])
        hid = (a / (1.0 + jnp.exp(-a)) * b).astype(BF16)
        ys_ref[slot] = _dot(hid, w2_ref[0])

    wait_rows(nv_prev, lambda i: scatter_copy(i, other), lambda: scatter_block(other))
    start_rows(nv, lambda i: scatter_copy(i, slot))

    @pl.when(c == nc - 1)
    def _():
        wait_rows(nv, lambda i: scatter_copy(i, slot), lambda: scatter_block(slot))


def _moe_call(chunk_expert, n_valid, src, dst, x1, w1, w3, w2, n_rows_out):
    n_chunks = chunk_expert.shape[0]
    t, d = x1.shape
    de = w1.shape[-1]
    grid_spec = pltpu.PrefetchScalarGridSpec(
        num_scalar_prefetch=2,
        grid=(n_chunks,),
        in_specs=[
            pl.BlockSpec((1, 1, MOE_BLOCK), lambda c, ce, nv: (c, 0, 0), memory_space=pltpu.SMEM),
            pl.BlockSpec((1, 1, MOE_BLOCK), lambda c, ce, nv: (jnp.minimum(c + 1, n_chunks - 1), 0, 0),
                         memory_space=pltpu.SMEM),
            pl.BlockSpec((1, 1, MOE_BLOCK), lambda c, ce, nv: (c, 0, 0), memory_space=pltpu.SMEM),
            pl.BlockSpec(memory_space=pl.ANY),
            pl.BlockSpec((1, d, de), lambda c, ce, nv: (ce[c], 0, 0)),
            pl.BlockSpec((1, d, de), lambda c, ce, nv: (ce[c], 0, 0)),
            pl.BlockSpec((1, de, d), lambda c, ce, nv: (ce[c], 0, 0)),
        ],
        out_specs=pl.BlockSpec(memory_space=pl.ANY),
        scratch_shapes=[
            pltpu.VMEM((2, MOE_BLOCK, d), F32),
            pltpu.VMEM((2, MOE_BLOCK, d), F32),
            pltpu.SemaphoreType.DMA((2,)),
            pltpu.SemaphoreType.DMA((2,)),
        ],
    )
    return pl.pallas_call(
        _moe_kernel,
        grid_spec=grid_spec,
        out_shape=jax.ShapeDtypeStruct((n_rows_out, d), F32),
        compiler_params=_cparams(("arbitrary",)),
        name="moe_experts",
    )(chunk_expert, n_valid, src, src, dst, x1, w1, w3, w2)


def _combine_kernel(x1_ref, ya_ref, yb_ref, route_ref, lng_ref, lnb_ref, o_ref, *, alpha):
    g1 = route_ref[:, 2:3]
    g2 = route_ref[:, 3:4]
    m = g1 * ya_ref[...] + g2 * yb_ref[...]
    o_ref[...] = _layernorm(alpha * x1_ref[...] + m, lng_ref[...], lnb_ref[...])


def _combine_call(x1, y, route, lng, lnb, alpha, tm):
    t, d = x1.shape
    nt = t // tm
    full = lambda a: pl.BlockSpec(a.shape, lambda i: (0, 0))
    return pl.pallas_call(
        functools.partial(_combine_kernel, alpha=alpha),
        grid=(nt,),
        in_specs=[pl.BlockSpec((tm, d), lambda i: (i, 0)),
                  pl.BlockSpec((tm, d), lambda i: (i, 0)),
                  pl.BlockSpec((tm, d), lambda i: (i + nt, 0)),
                  pl.BlockSpec((tm, LANES), lambda i: (i, 0)),
                  full(lng), full(lnb)],
        out_specs=pl.BlockSpec((tm, d), lambda i: (i, 0)),
        out_shape=jax.ShapeDtypeStruct((t, d), F32),
        compiler_params=_cparams(("parallel",)),
        name="moe_combine",
    )(x1, y, y, route, lng, lnb)


def _rope_tables(seq, dim, lane_lo):
    half = dim // 2
    inv_freq = ROPE_THETA ** (-jnp.arange(half, dtype=F32) / half)
    ang = jnp.arange(seq, dtype=F32)[:, None] * inv_freq[None, :]
    cos = jnp.concatenate([jnp.cos(ang), jnp.cos(ang)], -1)
    sin = jnp.concatenate([jnp.sin(ang), jnp.sin(ang)], -1)
    if lane_lo == 0:
        reps = LANES // dim
        return jnp.tile(cos, (1, reps)), jnp.tile(sin, (1, reps))
    cos_t = jnp.ones((seq, LANES), F32).at[:, lane_lo:lane_lo + dim].set(cos)
    sin_t = jnp.zeros((seq, LANES), F32).at[:, lane_lo:lane_lo + dim].set(sin)
    return cos_t, sin_t


def _dispatch_tables(route, n_tok):
    expert_id = route[:, 0:TOP_K].astype(jnp.int32).reshape(-1)
    n_assign = n_tok * TOP_K
    n_slots = n_assign + N_EXPERTS * MOE_BLOCK
    n_chunks = n_slots // MOE_BLOCK
    onehot = (expert_id[:, None] == jnp.arange(N_EXPERTS, dtype=jnp.int32)[None, :]).astype(jnp.int32)
    ranks = jnp.cumsum(onehot, axis=0) - onehot
    rank = jnp.sum(ranks * onehot, axis=1)
    counts = jnp.sum(onehot, axis=0)
    padded = (counts + MOE_BLOCK - 1) // MOE_BLOCK * MOE_BLOCK
    pad_end = jnp.cumsum(padded)
    pad_start = pad_end - padded
    dest = pad_start[expert_id] + rank
    assign = jnp.arange(n_assign, dtype=jnp.int32)
    slot_assign = jnp.zeros((n_slots,), jnp.int32).at[dest].set(assign)
    src = slot_assign // TOP_K
    dst = (slot_assign % TOP_K) * n_tok + src
    chunk_start = jnp.arange(n_chunks, dtype=jnp.int32) * MOE_BLOCK
    chunk_expert = jnp.minimum(jnp.searchsorted(pad_end, chunk_start, side="right"),
                               N_EXPERTS - 1).astype(jnp.int32)
    n_valid = jnp.clip(pad_start[chunk_expert] + counts[chunk_expert] - chunk_start,
                       0, MOE_BLOCK).astype(jnp.int32)
    return (chunk_expert, n_valid, src.reshape(n_chunks, 1, MOE_BLOCK),
            dst.reshape(n_chunks, 1, MOE_BLOCK), n_assign)


def _pick_tile(n, pref):
    t = pref
    while n % t:
        t //= 2
    return t


def kernel(x, w_in, b_forget, g_cq, g_ckv, w_uq, w_ukv, g_head, w_out, ln1_g, ln1_b,
           w_group, b_group, w_expert, b_expert, w1, w3, w2, ln2_g, ln2_b):
    bsz, seq, d = x.shape
    depth = w_in.shape[0]
    t = bsz * seq
    alpha = (2.0 * depth) ** 0.25
    hq = N_HEADS_PER_MIXER * HEAD_DIM
    qk_scale = HEAD_DIM ** -0.5
    mla_scale = (MLA_NOPE + MLA_ROPE) ** -0.5
    win = DIL_BRANCHES[0][0]
    assert all(w // r == win for w, r in DIL_BRANCHES)
    assert seq % (DIL_BRANCHES[-1][1] * win) == 0 and d % LANES == 0

    cos64, sin64 = _rope_tables(seq, HEAD_DIM, 0)
    cos_m, sin_m = _rope_tables(seq, MLA_ROPE, MLA_NOPE)
    tq = _pick_tile(seq, 256)
    idx = jnp.arange(tq)
    u_sb = (idx[:, None] > idx[None, :]).astype(BF16)
    tk_sm = _pick_tile(seq, 2 * tq)

    for l in range(depth):
        wl = w_in[l]
        o_fox, o_mla, o_dil = N_SB, N_SB + N_FOX_QKV + N_HEADS_PER_MIXER, N_SB + N_FOX_QKV + N_HEADS_PER_MIXER + N_MLA
        qs = lambda w: w.at[:, 0:hq].multiply(qk_scale)
        w_sb = qs(wl[:, 0:N_SB])
        w_fx = qs(wl[:, o_fox:o_fox + N_FOX_QKV])
        w_f = wl[:, o_fox + N_FOX_QKV:o_mla]
        w_ml = wl[:, o_mla:o_dil]
        wd = wl[:, o_dil:].reshape(d, 3, len(DIL_BRANCHES), hq)
        w_br = [qs(jnp.concatenate([wd[:, 0, g], wd[:, 1, g], wd[:, 2, g]], axis=1)) for g in range(len(DIL_BRANCHES))]
        w_misc = jnp.concatenate([w_ml, w_f, jnp.zeros((d, MISC_W - N_MLA - N_HEADS_PER_MIXER), F32)], axis=1)
        w_tok = jnp.concatenate([w_sb, w_fx, w_misc] + w_br, axis=1).astype(BF16)

        wq = jnp.pad(w_uq[l].reshape(MLA_Q_LORA, N_HEADS_PER_MIXER, MLA_NOPE + MLA_ROPE),
                     ((0, 0), (0, 0), (0, LANES - MLA_NOPE - MLA_ROPE))).reshape(MLA_Q_LORA, -1).astype(BF16)
        wkv = w_ukv[l].reshape(MLA_KV_LORA, N_HEADS_PER_MIXER, MLA_NOPE + HEAD_DIM)
        wk = jnp.pad(wkv[:, :, :MLA_NOPE], ((0, 0), (0, 0), (0, LANES - MLA_NOPE))).reshape(MLA_KV_LORA, -1).astype(BF16)
        wv = wkv[:, :, MLA_NOPE:].reshape(MLA_KV_LORA, -1).astype(BF16)
        g_flat = g_head[l].reshape(1, -1)
        wr = jnp.concatenate([w_group[l], w_expert[l],
                              jnp.zeros((d, LANES - N_GROUPS - N_EXPERTS), F32)], axis=1)
        br = jnp.concatenate([b_group[l], b_expert[l],
                              jnp.zeros((LANES - N_GROUPS - N_EXPERTS,), F32)]).reshape(1, LANES)

        tm = _pick_tile(seq, 512)
        sb, fx, misc, *qkv_br = _proj_call(
            x, w_tok, cos64, sin64,
            ((N_SB, 0, BF16, 1), (N_FOX_QKV, 0, BF16, 1), (MISC_W, 0, F32, 1))
            + tuple((N_BRANCH, 2 * hq, BF16, r) for _, r in DIL_BRANCHES), tm)
        sb = sb.reshape(t, N_SB)
        fx = fx.reshape(t, N_FOX_QKV)
        misc = misc.reshape(t, MISC_W)

        o_sb = _attn_call("sb", sb, sb, sb, 0, 1, 2, g_flat, 0, bsz, seq, tq, tq, extra=(u_sb,))
        neg_c = _fox_c_call(misc, b_forget[l], bsz, seq)
        o_fx = _attn_call("fox", fx, fx, fx, 0, 1, 2, g_flat, 1, bsz, seq, tq, tk_sm, extra=(neg_c,))
        mq, mk, mv = _mla_prep_call(misc, g_cq[l].reshape(1, -1), g_ckv[l].reshape(1, -1), wq, wk, wv,
                                    cos_m, sin_m, seq, tm)
        o_ml = _attn_call("mla", mq, mk, mv, 0, 0, 0, g_flat, 2, bsz, seq, tq, tk_sm, scale=mla_scale)

        dil = []
        for g, (_, r) in enumerate(DIL_BRANCHES):
            n = seq // r
            og = _dil_call(qkv_br[g], max(win, min(tq, n // 2)), win)
            dil.append(og.reshape(t, 2 * hq))

        x1, route = _post_call(
            x.reshape(t, d), o_sb, o_fx, o_ml, dil[0], dil[1], dil[2], g_flat[:, 3 * hq:], w_out[l].astype(BF16),
            ln1_g[l].reshape(1, d), ln1_b[l].reshape(1, d), wr, br, alpha, _pick_tile(t, 256))

        chunk_expert, n_valid, src, dst, n_rows = _dispatch_tables(route, t)
        y = _moe_call(chunk_expert, n_valid, src, dst, x1,
                      w1[l].astype(BF16), w3[l].astype(BF16), w2[l].astype(BF16), n_rows)
        x = _combine_call(x1, y, route,
                          ln2_g[l].reshape(1, d), ln2_b[l].reshape(1, d), alpha, _pick_tile(t, 256)).reshape(bsz, seq, d)
    return x
```

```python
import functools

import jax
import jax.numpy as jnp
import numpy as np
from jax import lax
from jax.experimental import pallas as pl
from jax.experimental.pallas import tpu as pltpu

F32 = jnp.float32
BF16 = jnp.bfloat16

HEAD_DIM = 64
N_HEADS_PER_MIXER = 4
MLA_Q_LORA = 256
MLA_KV_LORA = 128
MLA_NOPE = 64
MLA_ROPE = 32
DIL_BRANCHES = ((128, 1), (512, 4), (2048, 16))
ROPE_THETA = 10000.0
N_GROUPS = 4
EXPERTS_PER_GROUP = 4
N_EXPERTS = N_GROUPS * EXPERTS_PER_GROUP
TOP_K = 2
MOE_BLOCK = 256
LN_EPS = 1e-5
RMS_EPS = 1e-6
LOG2E = 1.4426950408889634

LANES = 128
VMEM_LIMIT_BYTES = 56 * 1024 * 1024

N_SB = 3 * N_HEADS_PER_MIXER * HEAD_DIM
N_FOX_QKV = 3 * N_HEADS_PER_MIXER * HEAD_DIM
N_MLA = MLA_Q_LORA + MLA_KV_LORA + MLA_ROPE
N_BRANCH = 3 * N_HEADS_PER_MIXER * HEAD_DIM
MISC_W = 512
F_COL = N_MLA


def _cparams(sem):
    return pltpu.CompilerParams(dimension_semantics=sem, vmem_limit_bytes=VMEM_LIMIT_BYTES)


def _split3(a):
    hi = a.astype(BF16)
    r1 = a - hi.astype(F32)
    mid = r1.astype(BF16)
    lo = (r1 - mid.astype(F32)).astype(BF16)
    return hi, mid, lo


def _dot(a, b):
    return jnp.dot(a, b, preferred_element_type=F32)


def _dot_nt(a, b):
    return lax.dot_general(a, b, (((1,), (1,)), ((), ())), preferred_element_type=F32)


def _dot_exact_rhs(a, u):
    hi, mid, lo = _split3(a)
    return _dot(hi, u) + _dot(mid, u) + _dot(lo, u)


def _dot_f32(a, b):
    ah = a.astype(BF16)
    al = (a - ah.astype(F32)).astype(BF16)
    bh = b.astype(BF16)
    bl = (b - bh.astype(F32)).astype(BF16)
    return _dot(ah, bh) + (_dot(ah, bl) + _dot(al, bh))


def _lane_iota(shape):
    return lax.broadcasted_iota(jnp.int32, shape, len(shape) - 1)


def _rotate_half(y, half):
    lane = _lane_iota(y.shape)
    fwd = pltpu.roll(y, half, 1)
    bwd = pltpu.roll(y, LANES - half, 1)
    return jnp.where((lane % (2 * half)) < half, -bwd, fwd)


def _log_sigmoid_pair(z):
    sp = jnp.log(1.0 + jnp.exp(-jnp.abs(z)))
    return jnp.minimum(z, 0.0) - sp, -jnp.maximum(z, 0.0) - sp


def _head_rms(o, g):
    lane = _lane_iota(o.shape)
    first = lane < HEAD_DIM
    sq = o * o
    ss_a = jnp.sum(jnp.where(first, sq, 0.0), axis=-1, keepdims=True)
    ss_b = jnp.sum(jnp.where(first, 0.0, sq), axis=-1, keepdims=True)
    ms = jnp.where(first, ss_a, ss_b) * (1.0 / HEAD_DIM)
    return o * lax.rsqrt(ms + RMS_EPS) * g


def _layernorm(y, g, b):
    mu = jnp.mean(y, axis=-1, keepdims=True)
    d = y - mu
    var = jnp.mean(d * d, axis=-1, keepdims=True)
    return d * lax.rsqrt(var + LN_EPS) * g + b


def _proj_kernel(x_ref, w_ref, cos_ref, sin_ref, *refs, outs):
    out_refs, stage_ref = refs[:len(outs)], refs[len(outs)]
    tm = x_ref.shape[1]
    xb = x_ref[0].astype(BF16)
    col = 0
    slab = 0
    for o_ref, (width, n_rope, _, r) in zip(out_refs, outs):
        for c in range(0, width, 2 * LANES):
            cw = min(2 * LANES, width - c)
            y = _dot(xb, w_ref[:, col + c:col + c + cw])
            for s in range(0, cw, LANES):
                ys = y[:, s:s + LANES]
                if c + s < n_rope:
                    ys = ys * cos_ref[...] + _rotate_half(ys, HEAD_DIM // 2) * sin_ref[...]
                if r == 1:
                    o_ref[0, 0, :, c + s:c + s + LANES] = ys.astype(o_ref.dtype)
                else:
                    st = stage_ref.at[slab % stage_ref.shape[0]]
                    slab += 1
                    st[...] = ys
                    for p in range(r):
                        o_ref[0, p, :, c + s:c + s + LANES] = (
                            st[pl.ds(p, tm // r, stride=r), :].astype(o_ref.dtype))
        col += width


PROJ_STAGE_SLABS = 4


def _proj_call(x, w, cos, sin, outs, tm):
    bsz, seq, d = x.shape
    nt = seq // tm
    in_specs = [
        pl.BlockSpec((1, tm, d), lambda b, i: (b, i, 0)),
        pl.BlockSpec(w.shape, lambda b, i: (0, 0)),
        pl.BlockSpec((tm, LANES), lambda b, i: (i, 0)),
        pl.BlockSpec((tm, LANES), lambda b, i: (i, 0)),
    ]
    out_specs = [pl.BlockSpec((1, r, tm // r, wd), lambda b, i: (b, 0, i, 0)) for wd, _, _, r in outs]
    out_shape = [jax.ShapeDtypeStruct((bsz, r, seq // r, wd), dt) for wd, _, dt, r in outs]
    return pl.pallas_call(
        functools.partial(_proj_kernel, outs=outs),
        grid=(bsz, nt), in_specs=in_specs, out_specs=out_specs, out_shape=out_shape,
        scratch_shapes=[pltpu.VMEM((PROJ_STAGE_SLABS, tm, LANES), F32)],
        compiler_params=_cparams(("parallel", "parallel")),
        name="proj",
    )(x, w, cos, sin)


def _fox_c_kernel(misc_ref, bias_ref, uinc_ref, ones_ref, out_ref, *, seq):
    lane0 = F_COL - 3 * LANES
    nblk = seq // LANES

    def body(j, carry):
        r0 = pl.multiple_of(j * LANES, LANES)
        f = misc_ref[pl.ds(r0, LANES), :] + bias_ref[...]
        lf, _ = _log_sigmoid_pair(f)
        lft = lf.T
        csum = _dot_exact_rhs(lft, uinc_ref[...]) + carry
        tot = _dot_exact_rhs(lft, ones_ref[...])
        out_ref[0, :, pl.ds(r0, LANES)] = -csum[lane0:lane0 + N_HEADS_PER_MIXER, :]
        return carry + tot

    lax.fori_loop(0, nblk, body, jnp.zeros((LANES, LANES), F32))


def _fox_c_call(misc, b_forget, bsz, seq):
    bias = jnp.zeros((1, LANES), F32).at[0, F_COL - 3 * LANES:F_COL - 3 * LANES + N_HEADS_PER_MIXER].set(b_forget)
    idx = jnp.arange(LANES)
    uinc = (idx[:, None] <= idx[None, :]).astype(BF16)
    ones = jnp.ones((LANES, LANES), BF16)
    return pl.pallas_call(
        functools.partial(_fox_c_kernel, seq=seq),
        grid=(bsz,),
        in_specs=[
            pl.BlockSpec((seq, LANES), lambda b: (b, 3)),
            pl.BlockSpec((1, LANES), lambda b: (0, 0)),
            pl.BlockSpec((LANES, LANES), lambda b: (0, 0)),
            pl.BlockSpec((LANES, LANES), lambda b: (0, 0)),
        ],
        out_specs=pl.BlockSpec((1, N_HEADS_PER_MIXER, seq), lambda b: (b, 0, 0)),
        out_shape=jax.ShapeDtypeStruct((bsz, N_HEADS_PER_MIXER, seq), F32),
        compiler_params=_cparams(("parallel",)),
        name="fox_c",
    )(misc, bias, uinc, ones)


def _mla_prep_kernel(misc_ref, gq_ref, gkv_ref, wq_ref, wk_ref, wv_ref, cos_ref, sin_ref,
                     q_ref, k_ref, v_ref):
    def rms(x, g):
        return x * lax.rsqrt(jnp.mean(x * x, axis=-1, keepdims=True) + RMS_EPS) * g

    cq = rms(misc_ref[:, 0:MLA_Q_LORA], gq_ref[...]).astype(BF16)
    ckv = rms(misc_ref[:, MLA_Q_LORA:MLA_Q_LORA + MLA_KV_LORA], gkv_ref[...]).astype(BF16)
    kr_blk = misc_ref[:, 3 * LANES:4 * LANES]
    lane = _lane_iota(kr_blk.shape)
    in_rope = (lane >= MLA_NOPE) & (lane < MLA_NOPE + MLA_ROPE)
    kr = jnp.where(in_rope, pltpu.roll(kr_blk, MLA_NOPE, 1), 0.0)
    cos = cos_ref[...]
    sin = sin_ref[...]

    def rope(y):
        return y * cos + _rotate_half(y, MLA_ROPE // 2) * sin

    q = _dot(cq, wq_ref[...])
    k = _dot(ckv, wk_ref[...])
    for h in range(N_HEADS_PER_MIXER):
        sl = slice(h * LANES, (h + 1) * LANES)
        q_ref[:, sl] = rope(q[:, sl]).astype(BF16)
        k_ref[:, sl] = rope(k[:, sl] + kr).astype(BF16)
    v_ref[...] = _dot(ckv, wv_ref[...]).astype(BF16)


def _mla_prep_call(misc, g_cq, g_ckv, wq, wk, wv, cos, sin, seq, tm):
    t = misc.shape[0]
    nper = seq // tm
    hw = N_HEADS_PER_MIXER * LANES
    vw = N_HEADS_PER_MIXER * HEAD_DIM
    full = lambda a: pl.BlockSpec(a.shape, lambda i: (0, 0))
    return pl.pallas_call(
        _mla_prep_kernel,
        grid=(t // tm,),
        in_specs=[
            pl.BlockSpec((tm, MISC_W), lambda i: (i, 0)),
            full(g_cq), full(g_ckv), full(wq), full(wk), full(wv),
            pl.BlockSpec((tm, LANES), lambda i: (i % nper, 0)),
            pl.BlockSpec((tm, LANES), lambda i: (i % nper, 0)),
        ],
        out_specs=[
            pl.BlockSpec((tm, hw), lambda i: (i, 0)),
            pl.BlockSpec((tm, hw), lambda i: (i, 0)),
            pl.BlockSpec((tm, vw), lambda i: (i, 0)),
        ],
        out_shape=[
            jax.ShapeDtypeStruct((t, hw), BF16),
            jax.ShapeDtypeStruct((t, hw), BF16),
            jax.ShapeDtypeStruct((t, vw), BF16),
        ],
        compiler_params=_cparams(("parallel",)),
        name="mla_prep",
    )(misc, g_cq, g_ckv, wq, wk, wv, cos, sin)


def _attn_kernel(*refs, mode, tq, tk, scale):
    if mode == "sb":
        q_ref, k_ref, v_ref, g_ref, u_ref, o_ref = refs
    elif mode == "fox":
        q_ref, k_ref, v_ref, g_ref, nc_ref, o_ref = refs
    else:
        q_ref, k_ref, v_ref, g_ref, o_ref = refs
    nh = N_HEADS_PER_MIXER
    i = pl.program_id(1)
    lane = _lane_iota((tq, LANES))
    first = lane < HEAD_DIM
    row = lax.broadcasted_iota(jnp.int32, (tq, tk), 0)
    colm = lax.broadcasted_iota(jnp.int32, (tq, tk), 1)

    q_heads = []
    for h in range(nh):
        if mode == "mla":
            q_heads.append(q_ref[:, h * LANES:(h + 1) * LANES])
        else:
            q2 = q_ref[:, (h // 2) * LANES:(h // 2 + 1) * LANES]
            zero = jnp.zeros_like(q2)
            q_heads.append(jnp.where(first, q2, zero) if h % 2 == 0 else jnp.where(first, zero, q2))

    def k_head(j, h):
        r0 = pl.multiple_of(j * tk, tk)
        kb = h if mode == "mla" else h // 2
        return k_ref[pl.ds(r0, tk), kb * LANES:(kb + 1) * LANES]

    def v_pair(j, p):
        r0 = pl.multiple_of(j * tk, tk)
        return v_ref[pl.ds(r0, tk), p * LANES:(p + 1) * LANES]

    if mode == "sb":
        def step(j, carry, diag):
            accs, rs = carry
            new_accs, new_rs = [], []
            for p in range(nh // 2):
                v2 = v_pair(j, p)
                outs = []
                for h in (2 * p, 2 * p + 1):
                    z2 = _dot_nt(q_heads[h], k_head(j, h)) * LOG2E
                    u = jnp.maximum(z2, 0.0) + jnp.log2(1.0 + jnp.exp2(-jnp.abs(z2)))
                    ls_pos = z2 - u
                    ls_neg = -u
                    if diag:
                        before = colm < row
                        ls_neg = jnp.where(before, ls_neg, 0.0)
                    c = _dot(ls_neg.astype(BF16), u_ref[...])
                    w = jnp.exp2(ls_pos + c + rs[h])
                    if diag:
                        w = jnp.where(before, w, 0.0)
                    outs.append(_dot(w.astype(BF16), v2))
                    new_rs.append(rs[h] + (c[:, 0:1] + ls_neg[:, 0:1]))
                new_accs.append(accs[p] + jnp.where(first, outs[0], outs[1]))
            return tuple(new_accs), tuple(new_rs)

        zacc = jnp.zeros((tq, LANES), F32)
        zr = jnp.zeros((tq, 1), F32)
        carry = step(i, ((zacc,) * (nh // 2), (zr,) * nh), True)
        carry = lax.fori_loop(0, i, lambda n, c: step(i - 1 - n, c, False), carry)
        outs = carry[0]
    else:
        ones = jnp.ones((tk, LANES), BF16)
        jd = (i * tq) // tk
        off = i * tq - jd * tk

        def step(j, carry, diag):
            accs, ms, ls = carry
            new_accs, new_ms, new_ls = [], [], []
            r0 = pl.multiple_of(j * tk, tk)
            for p in range(nh // 2):
                v_aug = jnp.concatenate([v_pair(j, p), ones], axis=1)
                pv, alphas = [], []
                for h in (2 * p, 2 * p + 1):
                    s = _dot_nt(q_heads[h], k_head(j, h))
                    if mode == "mla":
                        s = s * scale
                    else:
                        s = s + nc_ref[0, h:h + 1, pl.ds(r0, tk)]
                    if diag:
                        s = jnp.where(colm <= row + off, s, -jnp.inf)
                    m_new = jnp.maximum(ms[h], jnp.max(s, axis=-1, keepdims=True))
                    alpha = jnp.exp(ms[h] - m_new)
                    pvx = _dot(jnp.exp(s - m_new).astype(BF16), v_aug)
                    new_ls.append(alpha * ls[h] + pvx[:, LANES:2 * LANES])
                    new_ms.append(m_new)
                    pv.append(pvx[:, 0:LANES])
                    alphas.append(alpha)
                new_accs.append(accs[p] * jnp.where(first, alphas[0], alphas[1])
                                + jnp.where(first, pv[0], pv[1]))
            return tuple(new_accs), tuple(new_ms), tuple(new_ls)

        neg = jnp.full((tq, 1), -jnp.inf, F32)
        zacc = jnp.zeros((tq, LANES), F32)
        carry = step(jd, ((zacc,) * (nh // 2), (neg,) * nh, (zacc,) * nh), True)
        carry = lax.fori_loop(0, jd, lambda n, c: step(n, c, False), carry)
        accs, _, ls = carry
        outs = [accs[p] / jnp.where(first, ls[2 * p], ls[2 * p + 1]) for p in range(nh // 2)]
    for p in range(nh // 2):
        sl = slice(p * LANES, (p + 1) * LANES)
        o_ref[:, sl] = _head_rms(outs[p], g_ref[:, sl]).astype(o_ref.dtype)


def _attn_call(mode, q, k, v, qcol, kcol, vcol, g_flat, gcol, bsz, seq, tq, tk, extra=(), scale=1.0):
    t = bsz * seq
    nq = seq // tq
    hq = N_HEADS_PER_MIXER * HEAD_DIM
    qw = N_HEADS_PER_MIXER * LANES if mode == "mla" else hq
    assert tk % tq == 0 and seq % tk == 0 and (mode != "sb" or tk == tq)
    in_specs = [
        pl.BlockSpec((tq, qw), lambda b, i: (b * nq + i, qcol)),
        pl.BlockSpec((seq, qw), lambda b, i: (b, kcol)),
        pl.BlockSpec((seq, hq), lambda b, i: (b, vcol)),
        pl.BlockSpec((1, hq), lambda b, i: (0, gcol)),
    ]
    args = [q, k, v, g_flat]
    if mode == "sb":
        (u,) = extra
        in_specs.append(pl.BlockSpec(u.shape, lambda b, i: (0, 0)))
        args.append(u)
    elif mode == "fox":
        (nc,) = extra
        in_specs.append(pl.BlockSpec((1, N_HEADS_PER_MIXER, seq), lambda b, i: (b, 0, 0)))
        args.append(nc)
    return pl.pallas_call(
        functools.partial(_attn_kernel, mode=mode, tq=tq, tk=tk, scale=scale),
        grid=(bsz, nq),
        in_specs=in_specs,
        out_specs=pl.BlockSpec((tq, hq), lambda b, i: (b * nq + i, 0)),
        out_shape=jax.ShapeDtypeStruct((t, hq), BF16),
        compiler_params=_cparams(("parallel", "arbitrary")),
        name=f"attn_{mode}",
    )(*args)


def _dil_kernel(qkv_ref, o_ref, *, tq, tk, win):
    i = pl.program_id(2)
    lane = _lane_iota((tq, LANES))
    first = lane < HEAD_DIM
    hq = N_HEADS_PER_MIXER * HEAD_DIM
    r0 = pl.multiple_of(i * tq, tq)
    k0 = pl.multiple_of(jnp.maximum(i - 1, 0) * tq, tq)
    row = lax.broadcasted_iota(jnp.int32, (tq, tk), 0)
    colm = lax.broadcasted_iota(jnp.int32, (tq, tk), 1)
    delta = row + (r0 - k0) - colm
    band = jnp.abs(2 * delta - win) <= win
    ones = jnp.ones((tk, LANES), BF16)
    for p in range(N_HEADS_PER_MIXER // 2):
        q2 = qkv_ref[0, 0, pl.ds(r0, tq), p * LANES:(p + 1) * LANES]
        k2 = qkv_ref[0, 0, pl.ds(k0, tk), hq + p * LANES:hq + (p + 1) * LANES]
        v_aug = jnp.concatenate(
            [qkv_ref[0, 0, pl.ds(k0, tk), 2 * hq + p * LANES:2 * hq + (p + 1) * LANES], ones], axis=1)
        zero = jnp.zeros_like(q2)
        outs = []
        lses = []
        for hh in range(2):
            qh = jnp.where(first, q2, zero) if hh == 0 else jnp.where(first, zero, q2)
            s = jnp.where(band, _dot_nt(qh, k2), -jnp.inf)
            m = jnp.max(s, axis=-1, keepdims=True)
            pvx = _dot(jnp.exp(s - m).astype(BF16), v_aug)
            l = pvx[:, LANES:2 * LANES]
            outs.append(pvx[:, 0:LANES] / l)
            lses.append(m + jnp.log(l))
        o_ref[0, :, p * LANES:(p + 1) * LANES] = jnp.where(first, outs[0], outs[1])
        o_ref[0, :, hq + p * LANES:hq + (p + 1) * LANES] = jnp.where(first, lses[0], lses[1])


def _dil_call(qkv, tqs, win):
    bsz, r, n, w = qkv.shape
    hq = N_HEADS_PER_MIXER * HEAD_DIM
    tk = 2 * tqs if n >= 2 * tqs else tqs
    assert tqs >= win and n % tqs == 0 and (tk == 2 * tqs or n == tqs)
    return pl.pallas_call(
        functools.partial(_dil_kernel, tq=tqs, tk=tk, win=win),
        grid=(bsz, r, n // tqs),
        in_specs=[pl.BlockSpec((1, 1, n, w), lambda b, p, c: (b, p, 0, 0))],
        out_specs=pl.BlockSpec((1, tqs, 2 * hq), lambda b, p, c: (b, c, p)),
        out_shape=jax.ShapeDtypeStruct((bsz, n, r * 2 * hq), F32),
        compiler_params=_cparams(("parallel", "parallel", "arbitrary")),
        name=f"dil_r{r}",
    )(qkv)


def _post_kernel(x_ref, osb_ref, ofox_ref, omla_ref, d1_ref, d2_ref, d3_ref, gd_ref, wo_ref,
                 lng_ref, lnb_ref, wr_ref, br_ref, x1_ref, route_ref, *, alpha):
    hq = N_HEADS_PER_MIXER * HEAD_DIM
    h = _dot(osb_ref[...], wo_ref[0:hq, :])
    h += _dot(ofox_ref[...], wo_ref[hq:2 * hq, :])
    h += _dot(omla_ref[...], wo_ref[2 * hq:3 * hq, :])
    for p in range(N_HEADS_PER_MIXER // 2):
        sl = slice(p * LANES, (p + 1) * LANES)
        ll = slice(hq + p * LANES, hq + (p + 1) * LANES)
        l1, l2, l3 = d1_ref[:, ll], d2_ref[:, ll], d3_ref[:, ll]
        m = jnp.maximum(jnp.maximum(l1, l2), l3)
        e1, e2, e3 = jnp.exp(l1 - m), jnp.exp(l2 - m), jnp.exp(l3 - m)
        inv = 1.0 / (e1 + e2 + e3)
        od = (e1 * inv) * d1_ref[:, sl] + (e2 * inv) * d2_ref[:, sl] + (e3 * inv) * d3_ref[:, sl]
        od = _head_rms(od, gd_ref[:, sl]).astype(BF16)
        h += _dot(od, wo_ref[3 * hq + p * LANES:3 * hq + (p + 1) * LANES, :])
    x1 = _layernorm(alpha * x_ref[...] + h, lng_ref[...], lnb_ref[...])
    x1_ref[...] = x1

    logits = _dot_f32(x1, wr_ref[...]) + br_ref[...]
    lane = _lane_iota(logits.shape)
    lanef = lane.astype(F32)
    big = float(LANES)
    ninf = -jnp.inf
    gl = jnp.where(lane < N_GROUPS, logits, ninf)
    gmax = jnp.max(gl, axis=-1, keepdims=True)
    gsel = jnp.min(jnp.where(gl == gmax, lanef, big), axis=-1, keepdims=True)
    gw = 1.0 / jnp.sum(jnp.exp(gl - gmax), axis=-1, keepdims=True)
    e_lo = N_GROUPS + EXPERTS_PER_GROUP * gsel
    in_grp = (lanef >= e_lo) & (lanef < e_lo + EXPERTS_PER_GROUP)
    el = jnp.where(in_grp, logits, ninf)
    t1 = jnp.max(el, axis=-1, keepdims=True)
    i1 = jnp.min(jnp.where(el == t1, lanef, big), axis=-1, keepdims=True)
    el2 = jnp.where(lanef == i1, ninf, el)
    t2 = jnp.max(el2, axis=-1, keepdims=True)
    i2 = jnp.min(jnp.where(el2 == t2, lanef, big), axis=-1, keepdims=True)
    ex = jnp.exp(t2 - t1)
    den = 1.0 + ex
    g1 = gw / den
    g2 = gw * ex / den
    out = jnp.where(lane == 0, i1 - N_GROUPS,
                    jnp.where(lane == 1, i2 - N_GROUPS,
                              jnp.where(lane == 2, g1, jnp.where(lane == 3, g2, 0.0))))
    route_ref[...] = out


def _post_call(x, osb, ofox, omla, d1, d2, d3, g_dil, wo, lng, lnb, wr, br, alpha, tm):
    t, d = x.shape
    hq = N_HEADS_PER_MIXER * HEAD_DIM
    row = lambda w: pl.BlockSpec((tm, w), lambda i: (i, 0))
    full = lambda a: pl.BlockSpec(a.shape, lambda i: (0, 0))
    return pl.pallas_call(
        functools.partial(_post_kernel, alpha=alpha),
        grid=(t // tm,),
        in_specs=[row(d), row(hq), row(hq), row(hq), row(2 * hq), row(2 * hq), row(2 * hq),
                  full(g_dil), full(wo), full(lng), full(lnb), full(wr), full(br)],
        out_specs=[row(d), row(LANES)],
        out_shape=[jax.ShapeDtypeStruct((t, d), F32), jax.ShapeDtypeStruct((t, LANES), F32)],
        compiler_params=_cparams(("parallel",)),
        name="post_mixer",
    )(x, osb, ofox, omla, d1, d2, d3, g_dil, wo, lng, lnb, wr, br)


DMA_UNROLL = 8


def _moe_kernel(ce_ref, nv_ref, src_ref, srcn_ref, dst_ref, x_hbm, w1_ref, w3_ref, w2_ref, y_hbm,
                xs_ref, ys_ref, gsem, ssem):
    c = pl.program_id(0)
    nc = pl.num_programs(0)
    slot = c % 2
    other = 1 - slot
    nv = nv_ref[c]
    nv_next = jnp.where(c + 1 < nc, nv_ref[jnp.minimum(c + 1, nc - 1)], 0)
    nv_prev = jnp.where(c > 0, nv_ref[jnp.maximum(c - 1, 0)], 0)

    def gather_copy(idx_ref, i, s):
        return pltpu.make_async_copy(x_hbm.at[pl.ds(idx_ref[0, 0, i], 1), :],
                                     xs_ref.at[s, pl.ds(i, 1), :], gsem.at[s])

    def scatter_copy(i, s):
        return pltpu.make_async_copy(ys_ref.at[s, pl.ds(i, 1), :],
                                     y_hbm.at[pl.ds(dst_ref[0, 0, i], 1), :], ssem.at[s])

    def start_rows(n, make):
        ngrp = n // DMA_UNROLL

        def grp(g, _):
            for k in range(DMA_UNROLL):
                make(g * DMA_UNROLL + k).start()
            return 0

        def one(i, _):
            make(i).start()
            return 0

        lax.fori_loop(0, ngrp, grp, 0)
        lax.fori_loop(ngrp * DMA_UNROLL, n, one, 0)

    def wait_rows(n, make_row, make_block):
        @pl.when(n == MOE_BLOCK)
        def _():
            make_block().wait()

        @pl.when(n < MOE_BLOCK)
        def _():
            def one(i, _):
                make_row(i).wait()
                return 0
            lax.fori_loop(0, n, one, 0)

    def gather_block(s):
        return pltpu.make_async_copy(x_hbm.at[pl.ds(0, MOE_BLOCK), :], xs_ref.at[s], gsem.at[s])

    def scatter_block(s):
        return pltpu.make_async_copy(ys_ref.at[s], y_hbm.at[pl.ds(0, MOE_BLOCK), :], ssem.at[s])

    @pl.when(c == 0)
    def _():
        xs_ref[...] = jnp.zeros_like(xs_ref)
        start_rows(nv, lambda i: gather_copy(src_ref, i, 0))

    start_rows(nv_next, lambda i: gather_copy(srcn_ref, i, other))
    wait_rows(nv, lambda i: gather_copy(src_ref, i, slot), lambda: gather_block(slot))

    @pl.when(nv > 0)
    def _():
        xb = xs_ref[slot].astype(BF16)
        a = _dot(xb, w1_ref[0])
        b = _dot(xb, w3_ref[0])
        hid = (a / (1.0 + jnp.exp(-a)) * b).astype(BF16)
        ys_ref[slot] = _dot(hid, w2_ref[0])

    wait_rows(nv_prev, lambda i: scatter_copy(i, other), lambda: scatter_block(other))
    start_rows(nv, lambda i: scatter_copy(i, slot))

    @pl.when(c == nc - 1)
    def _():
        wait_rows(nv, lambda i: scatter_copy(i, slot), lambda: scatter_block(slot))


def _moe_call(chunk_expert, n_valid, src, dst, x1, w1, w3, w2, n_rows_out):
    n_chunks = chunk_expert.shape[0]
    t, d = x1.shape
    de = w1.shape[-1]
    grid_spec = pltpu.PrefetchScalarGridSpec(
        num_scalar_prefetch=2,
        grid=(n_chunks,),
        in_specs=[
            pl.BlockSpec((1, 1, MOE_BLOCK), lambda c, ce, nv: (c, 0, 0), memory_space=pltpu.SMEM),
            pl.BlockSpec((1, 1, MOE_BLOCK), lambda c, ce, nv: (jnp.minimum(c + 1, n_chunks - 1), 0, 0),
                         memory_space=pltpu.SMEM),
            pl.BlockSpec((1, 1, MOE_BLOCK), lambda c, ce, nv: (c, 0, 0), memory_space=pltpu.SMEM),
            pl.BlockSpec(memory_space=pl.ANY),
            pl.BlockSpec((1, d, de), lambda c, ce, nv: (ce[c], 0, 0)),
            pl.BlockSpec((1, d, de), lambda c, ce, nv: (ce[c], 0, 0)),
            pl.BlockSpec((1, de, d), lambda c, ce, nv: (ce[c], 0, 0)),
        ],
        out_specs=pl.BlockSpec(memory_space=pl.ANY),
        scratch_shapes=[
            pltpu.VMEM((2, MOE_BLOCK, d), F32),
            pltpu.VMEM((2, MOE_BLOCK, d), F32),
            pltpu.SemaphoreType.DMA((2,)),
            pltpu.SemaphoreType.DMA((2,)),
        ],
    )
    return pl.pallas_call(
        _moe_kernel,
        grid_spec=grid_spec,
        out_shape=jax.ShapeDtypeStruct((n_rows_out, d), F32),
        compiler_params=_cparams(("arbitrary",)),
        name="moe_experts",
    )(chunk_expert, n_valid, src, src, dst, x1, w1, w3, w2)


def _combine_kernel(x1_ref, ya_ref, yb_ref, route_ref, lng_ref, lnb_ref, o_ref, *, alpha):
    g1 = route_ref[:, 2:3]
    g2 = route_ref[:, 3:4]
    m = g1 * ya_ref[...] + g2 * yb_ref[...]
    o_ref[...] = _layernorm(alpha * x1_ref[...] + m, lng_ref[...], lnb_ref[...])


def _combine_call(x1, y, route, lng, lnb, alpha, tm):
    t, d = x1.shape
    nt = t // tm
    full = lambda a: pl.BlockSpec(a.shape, lambda i: (0, 0))
    return pl.pallas_call(
        functools.partial(_combine_kernel, alpha=alpha),
        grid=(nt,),
        in_specs=[pl.BlockSpec((tm, d), lambda i: (i, 0)),
                  pl.BlockSpec((tm, d), lambda i: (i, 0)),
                  pl.BlockSpec((tm, d), lambda i: (i + nt, 0)),
                  pl.BlockSpec((tm, LANES), lambda i: (i, 0)),
                  full(lng), full(lnb)],
        out_specs=pl.BlockSpec((tm, d), lambda i: (i, 0)),
        out_shape=jax.ShapeDtypeStruct((t, d), F32),
        compiler_params=_cparams(("parallel",)),
        name="moe_combine",
    )(x1, y, y, route, lng, lnb)


def _rope_tables(seq, dim, lane_lo):
    half = dim // 2
    inv_freq = ROPE_THETA ** (-jnp.arange(half, dtype=F32) / half)
    ang = jnp.arange(seq, dtype=F32)[:, None] * inv_freq[None, :]
    cos = jnp.concatenate([jnp.cos(ang), jnp.cos(ang)], -1)
    sin = jnp.concatenate([jnp.sin(ang), jnp.sin(ang)], -1)
    if lane_lo == 0:
        reps = LANES // dim
        return jnp.tile(cos, (1, reps)), jnp.tile(sin, (1, reps))
    cos_t = jnp.ones((seq, LANES), F32).at[:, lane_lo:lane_lo + dim].set(cos)
    sin_t = jnp.zeros((seq, LANES), F32).at[:, lane_lo:lane_lo + dim].set(sin)
    return cos_t, sin_t


def _dispatch_tables(route, n_tok):
    expert_id = route[:, 0:TOP_K].astype(jnp.int32).reshape(-1)
    n_assign = n_tok * TOP_K
    n_slots = n_assign + N_EXPERTS * MOE_BLOCK
    n_chunks = n_slots // MOE_BLOCK
    onehot = (expert_id[:, None] == jnp.arange(N_EXPERTS, dtype=jnp.int32)[None, :]).astype(jnp.int32)
    ranks = jnp.cumsum(onehot, axis=0) - onehot
    rank = jnp.sum(ranks * onehot, axis=1)
    counts = jnp.sum(onehot, axis=0)
    padded = (counts + MOE_BLOCK - 1) // MOE_BLOCK * MOE_BLOCK
    pad_end = jnp.cumsum(padded)
    pad_start = pad_end - padded
    dest = pad_start[expert_id] + rank
    assign = jnp.arange(n_assign, dtype=jnp.int32)
    slot_assign = jnp.zeros((n_slots,), jnp.int32).at[dest].set(assign)
    src = slot_assign // TOP_K
    dst = (slot_assign % TOP_K) * n_tok + src
    chunk_start = jnp.arange(n_chunks, dtype=jnp.int32) * MOE_BLOCK
    chunk_expert = jnp.minimum(jnp.searchsorted(pad_end, chunk_start, side="right"),
                               N_EXPERTS - 1).astype(jnp.int32)
    n_valid = jnp.clip(pad_start[chunk_expert] + counts[chunk_expert] - chunk_start,
                       0, MOE_BLOCK).astype(jnp.int32)
    return (chunk_expert, n_valid, src.reshape(n_chunks, 1, MOE_BLOCK),
            dst.reshape(n_chunks, 1, MOE_BLOCK), n_assign)


def _pick_tile(n, pref):
    t = pref
    while n % t:
        t //= 2
    return t


def kernel(x, w_in, b_forget, g_cq, g_ckv, w_uq, w_ukv, g_head, w_out, ln1_g, ln1_b,
           w_group, b_group, w_expert, b_expert, w1, w3, w2, ln2_g, ln2_b):
    bsz, seq, d = x.shape
    depth = w_in.shape[0]
    t = bsz * seq
    alpha = (2.0 * depth) ** 0.25
    hq = N_HEADS_PER_MIXER * HEAD_DIM
    qk_scale = HEAD_DIM ** -0.5
    mla_scale = (MLA_NOPE + MLA_ROPE) ** -0.5
    win = DIL_BRANCHES[0][0]
    assert all(w // r == win for w, r in DIL_BRANCHES)
    assert seq % (DIL_BRANCHES[-1][1] * win) == 0 and d % LANES == 0

    cos64, sin64 = _rope_tables(seq, HEAD_DIM, 0)
    cos_m, sin_m = _rope_tables(seq, MLA_ROPE, MLA_NOPE)
    tq = _pick_tile(seq, 256)
    idx = jnp.arange(tq)
    u_sb = (idx[:, None] > idx[None, :]).astype(BF16)
    tk_sm = _pick_tile(seq, 2 * tq)

    for l in range(depth):
        wl = w_in[l]
        o_fox, o_mla, o_dil = N_SB, N_SB + N_FOX_QKV + N_HEADS_PER_MIXER, N_SB + N_FOX_QKV + N_HEADS_PER_MIXER + N_MLA
        qs = lambda w: w.at[:, 0:hq].multiply(qk_scale)
        w_sb = qs(wl[:, 0:N_SB])
        w_fx = qs(wl[:, o_fox:o_fox + N_FOX_QKV])
        w_f = wl[:, o_fox + N_FOX_QKV:o_mla]
        w_ml = wl[:, o_mla:o_dil]
        wd = wl[:, o_dil:].reshape(d, 3, len(DIL_BRANCHES), hq)
        w_br = [qs(jnp.concatenate([wd[:, 0, g], wd[:, 1, g], wd[:, 2, g]], axis=1)) for g in range(len(DIL_BRANCHES))]
        w_misc = jnp.concatenate([w_ml, w_f, jnp.zeros((d, MISC_W - N_MLA - N_HEADS_PER_MIXER), F32)], axis=1)
        w_tok = jnp.concatenate([w_sb, w_fx, w_misc] + w_br, axis=1).astype(BF16)

        wq = jnp.pad(w_uq[l].reshape(MLA_Q_LORA, N_HEADS_PER_MIXER, MLA_NOPE + MLA_ROPE),
                     ((0, 0), (0, 0), (0, LANES - MLA_NOPE - MLA_ROPE))).reshape(MLA_Q_LORA, -1).astype(BF16)
        wkv = w_ukv[l].reshape(MLA_KV_LORA, N_HEADS_PER_MIXER, MLA_NOPE + HEAD_DIM)
        wk = jnp.pad(wkv[:, :, :MLA_NOPE], ((0, 0), (0, 0), (0, LANES - MLA_NOPE))).reshape(MLA_KV_LORA, -1).astype(BF16)
        wv = wkv[:, :, MLA_NOPE:].reshape(MLA_KV_LORA, -1).astype(BF16)
        g_flat = g_head[l].reshape(1, -1)
        wr = jnp.concatenate([w_group[l], w_expert[l],
                              jnp.zeros((d, LANES - N_GROUPS - N_EXPERTS), F32)], axis=1)
        br = jnp.concatenate([b_group[l], b_expert[l],
                              jnp.zeros((LANES - N_GROUPS - N_EXPERTS,), F32)]).reshape(1, LANES)

        tm = _pick_tile(seq, 512)
        sb, fx, misc, *qkv_br = _proj_call(
            x, w_tok, cos64, sin64,
            ((N_SB, 0, BF16, 1), (N_FOX_QKV, 0, BF16, 1), (MISC_W, 0, F32, 1))
            + tuple((N_BRANCH, 2 * hq, BF16, r) for _, r in DIL_BRANCHES), tm)
        sb = sb.reshape(t, N_SB)
        fx = fx.reshape(t, N_FOX_QKV)
        misc = misc.reshape(t, MISC_W)

        o_sb = _attn_call("sb", sb, sb, sb, 0, 1, 2, g_flat, 0, bsz, seq, tq, tq, extra=(u_sb,))
        neg_c = _fox_c_call(misc, b_forget[l], bsz, seq)
        o_fx = _attn_call("fox", fx, fx, fx, 0, 1, 2, g_flat, 1, bsz, seq, tq, tk_sm, extra=(neg_c,))
        mq, mk, mv = _mla_prep_call(misc, g_cq[l].reshape(1, -1), g_ckv[l].reshape(1, -1), wq, wk, wv,
                                    cos_m, sin_m, seq, tm)
        o_ml = _attn_call("mla", mq, mk, mv, 0, 0, 0, g_flat, 2, bsz, seq, tq, tk_sm, scale=mla_scale)

        dil = []
        for g, (_, r) in enumerate(DIL_BRANCHES):
            n = seq // r
            og = _dil_call(qkv_br[g], max(win, min(tq, n // 2)), win)
            dil.append(og.reshape(t, 2 * hq))

        x1, route = _post_call(
            x.reshape(t, d), o_sb, o_fx, o_ml, dil[0], dil[1], dil[2], g_flat[:, 3 * hq:], w_out[l].astype(BF16),
            ln1_g[l].reshape(1, d), ln1_b[l].reshape(1, d), wr, br, alpha, _pick_tile(t, 256))

        chunk_expert, n_valid, src, dst, n_rows = _dispatch_tables(route, t)
        y = _moe_call(chunk_expert, n_valid, src, dst, x1,
                      w1[l].astype(BF16), w3[l].astype(BF16), w2[l].astype(BF16), n_rows)
        x = _combine_call(x1, y, route,
                          ln2_g[l].reshape(1, d), ln2_b[l].reshape(1, d), alpha, _pick_tile(t, 256)).reshape(bsz, seq, d)
    return x
```

```python
import functools

import jax
import jax.numpy as jnp
import numpy as np
from jax import lax
from jax.experimental import pallas as pl
from jax.experimental.pallas import tpu as pltpu

F32 = jnp.float32
BF16 = jnp.bfloat16

HEAD_DIM = 64
N_HEADS_PER_MIXER = 4
MLA_Q_LORA = 256
MLA_KV_LORA = 128
MLA_NOPE = 64
MLA_ROPE = 32
DIL_BRANCHES = ((128, 1), (512, 4), (2048, 16))
ROPE_THETA = 10000.0
N_GROUPS = 4
EXPERTS_PER_GROUP = 4
N_EXPERTS = N_GROUPS * EXPERTS_PER_GROUP
TOP_K = 2
MOE_BLOCK = 256
LN_EPS = 1e-5
RMS_EPS = 1e-6

LANES = 128
VMEM_LIMIT_BYTES = 56 * 1024 * 1024

N_SB = 3 * N_HEADS_PER_MIXER * HEAD_DIM
N_FOX_QKV = 3 * N_HEADS_PER_MIXER * HEAD_DIM
N_MLA = MLA_Q_LORA + MLA_KV_LORA + MLA_ROPE
N_BRANCH = 3 * N_HEADS_PER_MIXER * HEAD_DIM
MISC_W = 512
F_COL = N_MLA


def _cparams(sem):
    return pltpu.CompilerParams(dimension_semantics=sem, vmem_limit_bytes=VMEM_LIMIT_BYTES)


def _split3(a):
    hi = a.astype(BF16)
    r1 = a - hi.astype(F32)
    mid = r1.astype(BF16)
    lo = (r1 - mid.astype(F32)).astype(BF16)
    return hi, mid, lo


def _dot(a, b):
    return jnp.dot(a, b, preferred_element_type=F32)


def _dot_nt(a, b):
    return lax.dot_general(a, b, (((1,), (1,)), ((), ())), preferred_element_type=F32)


def _dot_exact_rhs(a, u):
    hi, mid, lo = _split3(a)
    return _dot(hi, u) + _dot(mid, u) + _dot(lo, u)


def _dot_f32(a, b):
    ah = a.astype(BF16)
    al = (a - ah.astype(F32)).astype(BF16)
    bh = b.astype(BF16)
    bl = (b - bh.astype(F32)).astype(BF16)
    return _dot(ah, bh) + (_dot(ah, bl) + _dot(al, bh))


def _lane_iota(shape):
    return lax.broadcasted_iota(jnp.int32, shape, len(shape) - 1)


def _rotate_half(y, half):
    lane = _lane_iota(y.shape)
    fwd = pltpu.roll(y, half, 1)
    bwd = pltpu.roll(y, LANES - half, 1)
    return jnp.where((lane % (2 * half)) < half, -bwd, fwd)


def _log_sigmoid_pair(z):
    sp = jnp.log(1.0 + jnp.exp(-jnp.abs(z)))
    return jnp.minimum(z, 0.0) - sp, -jnp.maximum(z, 0.0) - sp


def _head_rms(o, g):
    lane = _lane_iota(o.shape)
    first = lane < HEAD_DIM
    sq = o * o
    ss_a = jnp.sum(jnp.where(first, sq, 0.0), axis=-1, keepdims=True)
    ss_b = jnp.sum(jnp.where(first, 0.0, sq), axis=-1, keepdims=True)
    ms = jnp.where(first, ss_a, ss_b) * (1.0 / HEAD_DIM)
    return o * lax.rsqrt(ms + RMS_EPS) * g


def _layernorm(y, g, b):
    mu = jnp.mean(y, axis=-1, keepdims=True)
    d = y - mu
    var = jnp.mean(d * d, axis=-1, keepdims=True)
    return d * lax.rsqrt(var + LN_EPS) * g + b


def _proj_kernel(x_ref, w_ref, cos_ref, sin_ref, *refs, outs):
    out_refs, stage_ref = refs[:len(outs)], refs[len(outs)]
    tm = x_ref.shape[1]
    xb = x_ref[0].astype(BF16)
    col = 0
    slab = 0
    for o_ref, (width, n_rope, _, r) in zip(out_refs, outs):
        for c in range(0, width, 2 * LANES):
            cw = min(2 * LANES, width - c)
            y = _dot(xb, w_ref[:, col + c:col + c + cw])
            for s in range(0, cw, LANES):
                ys = y[:, s:s + LANES]
                if c + s < n_rope:
                    ys = ys * cos_ref[...] + _rotate_half(ys, HEAD_DIM // 2) * sin_ref[...]
                if r == 1:
                    o_ref[0, 0, :, c + s:c + s + LANES] = ys.astype(o_ref.dtype)
                else:
                    st = stage_ref.at[slab % stage_ref.shape[0]]
                    slab += 1
                    st[...] = ys
                    for p in range(r):
                        o_ref[0, p, :, c + s:c + s + LANES] = (
                            st[pl.ds(p, tm // r, stride=r), :].astype(o_ref.dtype))
        col += width


PROJ_STAGE_SLABS = 4


def _proj_call(x, w, cos, sin, outs, tm):
    bsz, seq, d = x.shape
    nt = seq // tm
    in_specs = [
        pl.BlockSpec((1, tm, d), lambda b, i: (b, i, 0)),
        pl.BlockSpec(w.shape, lambda b, i: (0, 0)),
        pl.BlockSpec((tm, LANES), lambda b, i: (i, 0)),
        pl.BlockSpec((tm, LANES), lambda b, i: (i, 0)),
    ]
    out_specs = [pl.BlockSpec((1, r, tm // r, wd), lambda b, i: (b, 0, i, 0)) for wd, _, _, r in outs]
    out_shape = [jax.ShapeDtypeStruct((bsz, r, seq // r, wd), dt) for wd, _, dt, r in outs]
    return pl.pallas_call(
        functools.partial(_proj_kernel, outs=outs),
        grid=(bsz, nt), in_specs=in_specs, out_specs=out_specs, out_shape=out_shape,
        scratch_shapes=[pltpu.VMEM((PROJ_STAGE_SLABS, tm, LANES), F32)],
        compiler_params=_cparams(("parallel", "parallel")),
        name="proj",
    )(x, w, cos, sin)


def _fox_c_kernel(misc_ref, bias_ref, uinc_ref, ones_ref, out_ref, *, seq):
    lane0 = F_COL - 3 * LANES
    nblk = seq // LANES

    def body(j, carry):
        r0 = pl.multiple_of(j * LANES, LANES)
        f = misc_ref[pl.ds(r0, LANES), :] + bias_ref[...]
        lf, _ = _log_sigmoid_pair(f)
        lft = lf.T
        csum = _dot_exact_rhs(lft, uinc_ref[...]) + carry
        tot = _dot_exact_rhs(lft, ones_ref[...])
        out_ref[0, :, pl.ds(r0, LANES)] = -csum[lane0:lane0 + N_HEADS_PER_MIXER, :]
        return carry + tot

    lax.fori_loop(0, nblk, body, jnp.zeros((LANES, LANES), F32))


def _fox_c_call(misc, b_forget, bsz, seq):
    bias = jnp.zeros((1, LANES), F32).at[0, F_COL - 3 * LANES:F_COL - 3 * LANES + N_HEADS_PER_MIXER].set(b_forget)
    idx = jnp.arange(LANES)
    uinc = (idx[:, None] <= idx[None, :]).astype(BF16)
    ones = jnp.ones((LANES, LANES), BF16)
    return pl.pallas_call(
        functools.partial(_fox_c_kernel, seq=seq),
        grid=(bsz,),
        in_specs=[
            pl.BlockSpec((seq, LANES), lambda b: (b, 3)),
            pl.BlockSpec((1, LANES), lambda b: (0, 0)),
            pl.BlockSpec((LANES, LANES), lambda b: (0, 0)),
            pl.BlockSpec((LANES, LANES), lambda b: (0, 0)),
        ],
        out_specs=pl.BlockSpec((1, N_HEADS_PER_MIXER, seq), lambda b: (b, 0, 0)),
        out_shape=jax.ShapeDtypeStruct((bsz, N_HEADS_PER_MIXER, seq), F32),
        compiler_params=_cparams(("parallel",)),
        name="fox_c",
    )(misc, bias, uinc, ones)


def _mla_prep_kernel(misc_ref, gq_ref, gkv_ref, wq_ref, wk_ref, wv_ref, cos_ref, sin_ref,
                     q_ref, k_ref, v_ref):
    def rms(x, g):
        return x * lax.rsqrt(jnp.mean(x * x, axis=-1, keepdims=True) + RMS_EPS) * g

    cq = rms(misc_ref[:, 0:MLA_Q_LORA], gq_ref[...]).astype(BF16)
    ckv = rms(misc_ref[:, MLA_Q_LORA:MLA_Q_LORA + MLA_KV_LORA], gkv_ref[...]).astype(BF16)
    kr_blk = misc_ref[:, 3 * LANES:4 * LANES]
    lane = _lane_iota(kr_blk.shape)
    in_rope = (lane >= MLA_NOPE) & (lane < MLA_NOPE + MLA_ROPE)
    kr = jnp.where(in_rope, pltpu.roll(kr_blk, MLA_NOPE, 1), 0.0)
    cos = cos_ref[...]
    sin = sin_ref[...]

    def rope(y):
        return y * cos + _rotate_half(y, MLA_ROPE // 2) * sin

    q = _dot(cq, wq_ref[...])
    k = _dot(ckv, wk_ref[...])
    for h in range(N_HEADS_PER_MIXER):
        sl = slice(h * LANES, (h + 1) * LANES)
        q_ref[:, sl] = rope(q[:, sl]).astype(BF16)
        k_ref[:, sl] = rope(k[:, sl] + kr).astype(BF16)
    v_ref[...] = _dot(ckv, wv_ref[...]).astype(BF16)


def _mla_prep_call(misc, g_cq, g_ckv, wq, wk, wv, cos, sin, seq, tm):
    t = misc.shape[0]
    nper = seq // tm
    hw = N_HEADS_PER_MIXER * LANES
    vw = N_HEADS_PER_MIXER * HEAD_DIM
    full = lambda a: pl.BlockSpec(a.shape, lambda i: (0, 0))
    return pl.pallas_call(
        _mla_prep_kernel,
        grid=(t // tm,),
        in_specs=[
            pl.BlockSpec((tm, MISC_W), lambda i: (i, 0)),
            full(g_cq), full(g_ckv), full(wq), full(wk), full(wv),
            pl.BlockSpec((tm, LANES), lambda i: (i % nper, 0)),
            pl.BlockSpec((tm, LANES), lambda i: (i % nper, 0)),
        ],
        out_specs=[
            pl.BlockSpec((tm, hw), lambda i: (i, 0)),
            pl.BlockSpec((tm, hw), lambda i: (i, 0)),
            pl.BlockSpec((tm, vw), lambda i: (i, 0)),
        ],
        out_shape=[
            jax.ShapeDtypeStruct((t, hw), BF16),
            jax.ShapeDtypeStruct((t, hw), BF16),
            jax.ShapeDtypeStruct((t, vw), BF16),
        ],
        compiler_params=_cparams(("parallel",)),
        name="mla_prep",
    )(misc, g_cq, g_ckv, wq, wk, wv, cos, sin)


def _attn_kernel(*refs, mode, tq, tk, scale):
    if mode == "sb":
        q_ref, k_ref, v_ref, g_ref, u_ref, o_ref = refs
    elif mode == "fox":
        q_ref, k_ref, v_ref, g_ref, nc_ref, o_ref = refs
    else:
        q_ref, k_ref, v_ref, g_ref, o_ref = refs
    nh = N_HEADS_PER_MIXER
    i = pl.program_id(1)
    lane = _lane_iota((tq, LANES))
    first = lane < HEAD_DIM
    row = lax.broadcasted_iota(jnp.int32, (tq, tk), 0)
    colm = lax.broadcasted_iota(jnp.int32, (tq, tk), 1)

    q_heads = []
    for h in range(nh):
        if mode == "mla":
            q_heads.append(q_ref[:, h * LANES:(h + 1) * LANES])
        else:
            q2 = q_ref[:, (h // 2) * LANES:(h // 2 + 1) * LANES]
            zero = jnp.zeros_like(q2)
            q_heads.append(jnp.where(first, q2, zero) if h % 2 == 0 else jnp.where(first, zero, q2))

    def k_head(j, h):
        r0 = pl.multiple_of(j * tk, tk)
        kb = h if mode == "mla" else h // 2
        return k_ref[pl.ds(r0, tk), kb * LANES:(kb + 1) * LANES]

    def v_pair(j, p):
        r0 = pl.multiple_of(j * tk, tk)
        return v_ref[pl.ds(r0, tk), p * LANES:(p + 1) * LANES]

    jd = (i * tq) // tk
    off = i * tq - jd * tk

    if mode == "sb":
        tu = u_ref.shape[0]

        def step(j, carry, diag):
            accs, rs = carry
            new_accs, new_rs = [], []
            for p in range(nh // 2):
                v2 = v_pair(j, p)
                outs = []
                for h in (2 * p, 2 * p + 1):
                    z = _dot_nt(q_heads[h], k_head(j, h))
                    ls_pos, ls_neg = _log_sigmoid_pair(z)
                    if diag:
                        before = colm < row + off
                        ls_neg = jnp.where(before, ls_neg, 0.0)
                    lb = ls_neg.astype(BF16)
                    pieces = []
                    later = rs[h]
                    for c0 in range(tk - tu, -1, -tu):
                        c = _dot(lb[:, c0:c0 + tu], u_ref[...]) + later
                        pieces.insert(0, c)
                        later = c[:, 0:1] + ls_neg[:, c0:c0 + 1]
                    w = jnp.exp(ls_pos + jnp.concatenate(pieces, axis=1))
                    if diag:
                        w = jnp.where(before, w, 0.0)
                    outs.append(_dot(w.astype(BF16), v2))
                    new_rs.append(later)
                new_accs.append(accs[p] + jnp.where(first, outs[0], outs[1]))
            return tuple(new_accs), tuple(new_rs)

        zacc = jnp.zeros((tq, LANES), F32)
        zr = jnp.zeros((tq, 1), F32)
        carry = step(jd, ((zacc,) * (nh // 2), (zr,) * nh), True)
        carry = lax.fori_loop(0, jd, lambda n, c: step(jd - 1 - n, c, False), carry)
        outs = carry[0]
    else:
        ones = jnp.ones((tk, LANES), BF16)

        def step(j, carry, diag):
            accs, ms, ls = carry
            new_accs, new_ms, new_ls = [], [], []
            r0 = pl.multiple_of(j * tk, tk)
            for p in range(nh // 2):
                v_aug = jnp.concatenate([v_pair(j, p), ones], axis=1)
                pv, alphas = [], []
                for h in (2 * p, 2 * p + 1):
                    s = _dot_nt(q_heads[h], k_head(j, h))
                    if mode == "mla":
                        s = s * scale
                    else:
                        s = s + nc_ref[0, h:h + 1, pl.ds(r0, tk)]
                    if diag:
                        s = jnp.where(colm <= row + off, s, -jnp.inf)
                    m_new = jnp.maximum(ms[h], jnp.max(s, axis=-1, keepdims=True))
                    alpha = jnp.exp(ms[h] - m_new)
                    pvx = _dot(jnp.exp(s - m_new).astype(BF16), v_aug)
                    new_ls.append(alpha * ls[h] + pvx[:, LANES:2 * LANES])
                    new_ms.append(m_new)
                    pv.append(pvx[:, 0:LANES])
                    alphas.append(alpha)
                new_accs.append(accs[p] * jnp.where(first, alphas[0], alphas[1])
                                + jnp.where(first, pv[0], pv[1]))
            return tuple(new_accs), tuple(new_ms), tuple(new_ls)

        neg = jnp.full((tq, 1), -jnp.inf, F32)
        zacc = jnp.zeros((tq, LANES), F32)
        carry = step(jd, ((zacc,) * (nh // 2), (neg,) * nh, (zacc,) * nh), True)
        carry = lax.fori_loop(0, jd, lambda n, c: step(n, c, False), carry)
        accs, _, ls = carry
        outs = [accs[p] / jnp.where(first, ls[2 * p], ls[2 * p + 1]) for p in range(nh // 2)]
    for p in range(nh // 2):
        sl = slice(p * LANES, (p + 1) * LANES)
        o_ref[:, sl] = _head_rms(outs[p], g_ref[:, sl]).astype(o_ref.dtype)


def _attn_call(mode, q, k, v, qcol, kcol, vcol, g_flat, gcol, bsz, seq, tq, tk, extra=(), scale=1.0):
    t = bsz * seq
    nq = seq // tq
    hq = N_HEADS_PER_MIXER * HEAD_DIM
    qw = N_HEADS_PER_MIXER * LANES if mode == "mla" else hq
    assert tk % tq == 0 and seq % tk == 0
    in_specs = [
        pl.BlockSpec((tq, qw), lambda b, i: (b * nq + i, qcol)),
        pl.BlockSpec((seq, qw), lambda b, i: (b, kcol)),
        pl.BlockSpec((seq, hq), lambda b, i: (b, vcol)),
        pl.BlockSpec((1, hq), lambda b, i: (0, gcol)),
    ]
    args = [q, k, v, g_flat]
    if mode == "sb":
        (u,) = extra
        in_specs.append(pl.BlockSpec(u.shape, lambda b, i: (0, 0)))
        args.append(u)
    elif mode == "fox":
        (nc,) = extra
        in_specs.append(pl.BlockSpec((1, N_HEADS_PER_MIXER, seq), lambda b, i: (b, 0, 0)))
        args.append(nc)
    return pl.pallas_call(
        functools.partial(_attn_kernel, mode=mode, tq=tq, tk=tk, scale=scale),
        grid=(bsz, nq),
        in_specs=in_specs,
        out_specs=pl.BlockSpec((tq, hq), lambda b, i: (b * nq + i, 0)),
        out_shape=jax.ShapeDtypeStruct((t, hq), BF16),
        compiler_params=_cparams(("parallel", "arbitrary")),
        name=f"attn_{mode}",
    )(*args)


def _dil_kernel(qkv_ref, o_ref, *, tq, tk, win):
    i = pl.program_id(2)
    lane = _lane_iota((tq, LANES))
    first = lane < HEAD_DIM
    hq = N_HEADS_PER_MIXER * HEAD_DIM
    r0 = pl.multiple_of(i * tq, tq)
    k0 = pl.multiple_of(jnp.maximum(i - 1, 0) * tq, tq)
    row = lax.broadcasted_iota(jnp.int32, (tq, tk), 0)
    colm = lax.broadcasted_iota(jnp.int32, (tq, tk), 1)
    delta = row + (r0 - k0) - colm
    band = jnp.abs(2 * delta - win) <= win
    ones = jnp.ones((tk, LANES), BF16)
    for p in range(N_HEADS_PER_MIXER // 2):
        q2 = qkv_ref[0, 0, pl.ds(r0, tq), p * LANES:(p + 1) * LANES]
        k2 = qkv_ref[0, 0, pl.ds(k0, tk), hq + p * LANES:hq + (p + 1) * LANES]
        v_aug = jnp.concatenate(
            [qkv_ref[0, 0, pl.ds(k0, tk), 2 * hq + p * LANES:2 * hq + (p + 1) * LANES], ones], axis=1)
        zero = jnp.zeros_like(q2)
        outs = []
        lses = []
        for hh in range(2):
            qh = jnp.where(first, q2, zero) if hh == 0 else jnp.where(first, zero, q2)
            s = jnp.where(band, _dot_nt(qh, k2), -jnp.inf)
            m = jnp.max(s, axis=-1, keepdims=True)
            pvx = _dot(jnp.exp(s - m).astype(BF16), v_aug)
            l = pvx[:, LANES:2 * LANES]
            outs.append(pvx[:, 0:LANES] / l)
            lses.append(m + jnp.log(l))
        o_ref[0, :, p * LANES:(p + 1) * LANES] = jnp.where(first, outs[0], outs[1])
        o_ref[0, :, hq + p * LANES:hq + (p + 1) * LANES] = jnp.where(first, lses[0], lses[1])


def _dil_call(qkv, tqs, win):
    bsz, r, n, w = qkv.shape
    hq = N_HEADS_PER_MIXER * HEAD_DIM
    tk = 2 * tqs if n >= 2 * tqs else tqs
    assert tqs >= win and n % tqs == 0 and (tk == 2 * tqs or n == tqs)
    return pl.pallas_call(
        functools.partial(_dil_kernel, tq=tqs, tk=tk, win=win),
        grid=(bsz, r, n // tqs),
        in_specs=[pl.BlockSpec((1, 1, n, w), lambda b, p, c: (b, p, 0, 0))],
        out_specs=pl.BlockSpec((1, tqs, 2 * hq), lambda b, p, c: (b, c, p)),
        out_shape=jax.ShapeDtypeStruct((bsz, n, r * 2 * hq), F32),
        compiler_params=_cparams(("parallel", "parallel", "arbitrary")),
        name=f"dil_r{r}",
    )(qkv)


def _post_kernel(x_ref, osb_ref, ofox_ref, omla_ref, d1_ref, d2_ref, d3_ref, gd_ref, wo_ref,
                 lng_ref, lnb_ref, wr_ref, br_ref, x1_ref, route_ref, *, alpha):
    hq = N_HEADS_PER_MIXER * HEAD_DIM
    h = _dot(osb_ref[...], wo_ref[0:hq, :])
    h += _dot(ofox_ref[...], wo_ref[hq:2 * hq, :])
    h += _dot(omla_ref[...], wo_ref[2 * hq:3 * hq, :])
    for p in range(N_HEADS_PER_MIXER // 2):
        sl = slice(p * LANES, (p + 1) * LANES)
        ll = slice(hq + p * LANES, hq + (p + 1) * LANES)
        l1, l2, l3 = d1_ref[:, ll], d2_ref[:, ll], d3_ref[:, ll]
        m = jnp.maximum(jnp.maximum(l1, l2), l3)
        e1, e2, e3 = jnp.exp(l1 - m), jnp.exp(l2 - m), jnp.exp(l3 - m)
        inv = 1.0 / (e1 + e2 + e3)
        od = (e1 * inv) * d1_ref[:, sl] + (e2 * inv) * d2_ref[:, sl] + (e3 * inv) * d3_ref[:, sl]
        od = _head_rms(od, gd_ref[:, sl]).astype(BF16)
        h += _dot(od, wo_ref[3 * hq + p * LANES:3 * hq + (p + 1) * LANES, :])
    x1 = _layernorm(alpha * x_ref[...] + h, lng_ref[...], lnb_ref[...])
    x1_ref[...] = x1

    logits = _dot_f32(x1, wr_ref[...]) + br_ref[...]
    lane = _lane_iota(logits.shape)
    lanef = lane.astype(F32)
    big = float(LANES)
    ninf = -jnp.inf
    gl = jnp.where(lane < N_GROUPS, logits, ninf)
    gmax = jnp.max(gl, axis=-1, keepdims=True)
    gsel = jnp.min(jnp.where(gl == gmax, lanef, big), axis=-1, keepdims=True)
    gw = 1.0 / jnp.sum(jnp.exp(gl - gmax), axis=-1, keepdims=True)
    e_lo = N_GROUPS + EXPERTS_PER_GROUP * gsel
    in_grp = (lanef >= e_lo) & (lanef < e_lo + EXPERTS_PER_GROUP)
    el = jnp.where(in_grp, logits, ninf)
    t1 = jnp.max(el, axis=-1, keepdims=True)
    i1 = jnp.min(jnp.where(el == t1, lanef, big), axis=-1, keepdims=True)
    el2 = jnp.where(lanef == i1, ninf, el)
    t2 = jnp.max(el2, axis=-1, keepdims=True)
    i2 = jnp.min(jnp.where(el2 == t2, lanef, big), axis=-1, keepdims=True)
    ex = jnp.exp(t2 - t1)
    den = 1.0 + ex
    g1 = gw / den
    g2 = gw * ex / den
    out = jnp.where(lane == 0, i1 - N_GROUPS,
                    jnp.where(lane == 1, i2 - N_GROUPS,
                              jnp.where(lane == 2, g1, jnp.where(lane == 3, g2, 0.0))))
    route_ref[...] = out


def _post_call(x, osb, ofox, omla, d1, d2, d3, g_dil, wo, lng, lnb, wr, br, alpha, tm):
    t, d = x.shape
    hq = N_HEADS_PER_MIXER * HEAD_DIM
    row = lambda w: pl.BlockSpec((tm, w), lambda i: (i, 0))
    full = lambda a: pl.BlockSpec(a.shape, lambda i: (0, 0))
    return pl.pallas_call(
        functools.partial(_post_kernel, alpha=alpha),
        grid=(t // tm,),
        in_specs=[row(d), row(hq), row(hq), row(hq), row(2 * hq), row(2 * hq), row(2 * hq),
                  full(g_dil), full(wo), full(lng), full(lnb), full(wr), full(br)],
        out_specs=[row(d), row(LANES)],
        out_shape=[jax.ShapeDtypeStruct((t, d), F32), jax.ShapeDtypeStruct((t, LANES), F32)],
        compiler_params=_cparams(("parallel",)),
        name="post_mixer",
    )(x, osb, ofox, omla, d1, d2, d3, g_dil, wo, lng, lnb, wr, br)


DMA_UNROLL = 8


def _moe_kernel(ce_ref, nv_ref, src_ref, srcn_ref, dst_ref, x_hbm, w1_ref, w3_ref, w2_ref, y_hbm,
                xs_ref, ys_ref, gsem, ssem):
    c = pl.program_id(0)
    nc = pl.num_programs(0)
    slot = c % 2
    other = 1 - slot
    nv = nv_ref[c]
    nv_next = jnp.where(c + 1 < nc, nv_ref[jnp.minimum(c + 1, nc - 1)], 0)
    nv_prev = jnp.where(c > 0, nv_ref[jnp.maximum(c - 1, 0)], 0)

    def gather_copy(idx_ref, i, s):
        return pltpu.make_async_copy(x_hbm.at[pl.ds(idx_ref[0, 0, i], 1), :],
                                     xs_ref.at[s, pl.ds(i, 1), :], gsem.at[s])

    def scatter_copy(i, s):
        return pltpu.make_async_copy(ys_ref.at[s, pl.ds(i, 1), :],
                                     y_hbm.at[pl.ds(dst_ref[0, 0, i], 1), :], ssem.at[s])

    def start_rows(n, make):
        ngrp = n // DMA_UNROLL

        def grp(g, _):
            for k in range(DMA_UNROLL):
                make(g * DMA_UNROLL + k).start()
            return 0

        def one(i, _):
            make(i).start()
            return 0

        lax.fori_loop(0, ngrp, grp, 0)
        lax.fori_loop(ngrp * DMA_UNROLL, n, one, 0)

    def wait_rows(n, make_row, make_block):
        @pl.when(n == MOE_BLOCK)
        def _():
            make_block().wait()

        @pl.when(n < MOE_BLOCK)
        def _():
            def one(i, _):
                make_row(i).wait()
                return 0
            lax.fori_loop(0, n, one, 0)

    def gather_block(s):
        return pltpu.make_async_copy(x_hbm.at[pl.ds(0, MOE_BLOCK), :], xs_ref.at[s], gsem.at[s])

    def scatter_block(s):
        return pltpu.make_async_copy(ys_ref.at[s], y_hbm.at[pl.ds(0, MOE_BLOCK), :], ssem.at[s])

    @pl.when(c == 0)
    def _():
        xs_ref[...] = jnp.zeros_like(xs_ref)
        start_rows(nv, lambda i: gather_copy(src_ref, i, 0))

    start_rows(nv_next, lambda i: gather_copy(srcn_ref, i, other))
    wait_rows(nv, lambda i: gather_copy(src_ref, i, slot), lambda: gather_block(slot))

    @pl.when(nv > 0)
    def _():
        xb = xs_ref[slot].astype(BF16)
        a = _dot(xb, w1_ref[0])
        b = _dot(xb, w3_ref[0])
        hid = (a / (1.0 + jnp.exp(-a)) * b).astype(BF16)
        ys_ref[slot] = _dot(hid, w2_ref[0])

    wait_rows(nv_prev, lambda i: scatter_copy(i, other), lambda: scatter_block(other))
    start_rows(nv, lambda i: scatter_copy(i, slot))

    @pl.when(c == nc - 1)
    def _():
        wait_rows(nv, lambda i: scatter_copy(i, slot), lambda: scatter_block(slot))


def _moe_call(chunk_expert, n_valid, src, dst, x1, w1, w3, w2, n_rows_out):
    n_chunks = chunk_expert.shape[0]
    t, d = x1.shape
    de = w1.shape[-1]
    grid_spec = pltpu.PrefetchScalarGridSpec(
        num_scalar_prefetch=2,
        grid=(n_chunks,),
        in_specs=[
            pl.BlockSpec((1, 1, MOE_BLOCK), lambda c, ce, nv: (c, 0, 0), memory_space=pltpu.SMEM),
            pl.BlockSpec((1, 1, MOE_BLOCK), lambda c, ce, nv: (jnp.minimum(c + 1, n_chunks - 1), 0, 0),
                         memory_space=pltpu.SMEM),
            pl.BlockSpec((1, 1, MOE_BLOCK), lambda c, ce, nv: (c, 0, 0), memory_space=pltpu.SMEM),
            pl.BlockSpec(memory_space=pl.ANY),
            pl.BlockSpec((1, d, de), lambda c, ce, nv: (ce[c], 0, 0)),
            pl.BlockSpec((1, d, de), lambda c, ce, nv: (ce[c], 0, 0)),
            pl.BlockSpec((1, de, d), lambda c, ce, nv: (ce[c], 0, 0)),
        ],
        out_specs=pl.BlockSpec(memory_space=pl.ANY),
        scratch_shapes=[
            pltpu.VMEM((2, MOE_BLOCK, d), F32),
            pltpu.VMEM((2, MOE_BLOCK, d), F32),
            pltpu.SemaphoreType.DMA((2,)),
            pltpu.SemaphoreType.DMA((2,)),
        ],
    )
    return pl.pallas_call(
        _moe_kernel,
        grid_spec=grid_spec,
        out_shape=jax.ShapeDtypeStruct((n_rows_out, d), F32),
        compiler_params=_cparams(("arbitrary",)),
        name="moe_experts",
    )(chunk_expert, n_valid, src, src, dst, x1, w1, w3, w2)


def _combine_kernel(x1_ref, ya_ref, yb_ref, route_ref, lng_ref, lnb_ref, o_ref, *, alpha):
    g1 = route_ref[:, 2:3]
    g2 = route_ref[:, 3:4]
    m = g1 * ya_ref[...] + g2 * yb_ref[...]
    o_ref[...] = _layernorm(alpha * x1_ref[...] + m, lng_ref[...], lnb_ref[...])


def _combine_call(x1, y, route, lng, lnb, alpha, tm):
    t, d = x1.shape
    nt = t // tm
    full = lambda a: pl.BlockSpec(a.shape, lambda i: (0, 0))
    return pl.pallas_call(
        functools.partial(_combine_kernel, alpha=alpha),
        grid=(nt,),
        in_specs=[pl.BlockSpec((tm, d), lambda i: (i, 0)),
                  pl.BlockSpec((tm, d), lambda i: (i, 0)),
                  pl.BlockSpec((tm, d), lambda i: (i + nt, 0)),
                  pl.BlockSpec((tm, LANES), lambda i: (i, 0)),
                  full(lng), full(lnb)],
        out_specs=pl.BlockSpec((tm, d), lambda i: (i, 0)),
        out_shape=jax.ShapeDtypeStruct((t, d), F32),
        compiler_params=_cparams(("parallel",)),
        name="moe_combine",
    )(x1, y, y, route, lng, lnb)


def _rope_tables(seq, dim, lane_lo):
    half = dim // 2
    inv_freq = ROPE_THETA ** (-jnp.arange(half, dtype=F32) / half)
    ang = jnp.arange(seq, dtype=F32)[:, None] * inv_freq[None, :]
    cos = jnp.concatenate([jnp.cos(ang), jnp.cos(ang)], -1)
    sin = jnp.concatenate([jnp.sin(ang), jnp.sin(ang)], -1)
    if lane_lo == 0:
        reps = LANES // dim
        return jnp.tile(cos, (1, reps)), jnp.tile(sin, (1, reps))
    cos_t = jnp.ones((seq, LANES), F32).at[:, lane_lo:lane_lo + dim].set(cos)
    sin_t = jnp.zeros((seq, LANES), F32).at[:, lane_lo:lane_lo + dim].set(sin)
    return cos_t, sin_t


def _dispatch_tables(route, n_tok):
    expert_id = route[:, 0:TOP_K].astype(jnp.int32).reshape(-1)
    n_assign = n_tok * TOP_K
    n_slots = n_assign + N_EXPERTS * MOE_BLOCK
    n_chunks = n_slots // MOE_BLOCK
    onehot = (expert_id[:, None] == jnp.arange(N_EXPERTS, dtype=jnp.int32)[None, :]).astype(jnp.int32)
    ranks = jnp.cumsum(onehot, axis=0) - onehot
    rank = jnp.sum(ranks * onehot, axis=1)
    counts = jnp.sum(onehot, axis=0)
    padded = (counts + MOE_BLOCK - 1) // MOE_BLOCK * MOE_BLOCK
    pad_end = jnp.cumsum(padded)
    pad_start = pad_end - padded
    dest = pad_start[expert_id] + rank
    assign = jnp.arange(n_assign, dtype=jnp.int32)
    slot_assign = jnp.zeros((n_slots,), jnp.int32).at[dest].set(assign)
    src = slot_assign // TOP_K
    dst = (slot_assign % TOP_K) * n_tok + src
    chunk_start = jnp.arange(n_chunks, dtype=jnp.int32) * MOE_BLOCK
    chunk_expert = jnp.minimum(jnp.searchsorted(pad_end, chunk_start, side="right"),
                               N_EXPERTS - 1).astype(jnp.int32)
    n_valid = jnp.clip(pad_start[chunk_expert] + counts[chunk_expert] - chunk_start,
                       0, MOE_BLOCK).astype(jnp.int32)
    return (chunk_expert, n_valid, src.reshape(n_chunks, 1, MOE_BLOCK),
            dst.reshape(n_chunks, 1, MOE_BLOCK), n_assign)


def _pick_tile(n, pref):
    t = pref
    while n % t:
        t //= 2
    return t


def kernel(x, w_in, b_forget, g_cq, g_ckv, w_uq, w_ukv, g_head, w_out, ln1_g, ln1_b,
           w_group, b_group, w_expert, b_expert, w1, w3, w2, ln2_g, ln2_b):
    bsz, seq, d = x.shape
    depth = w_in.shape[0]
    t = bsz * seq
    alpha = (2.0 * depth) ** 0.25
    hq = N_HEADS_PER_MIXER * HEAD_DIM
    qk_scale = HEAD_DIM ** -0.5
    mla_scale = (MLA_NOPE + MLA_ROPE) ** -0.5
    win = DIL_BRANCHES[0][0]
    assert all(w // r == win for w, r in DIL_BRANCHES)
    assert seq % (DIL_BRANCHES[-1][1] * win) == 0 and d % LANES == 0

    cos64, sin64 = _rope_tables(seq, HEAD_DIM, 0)
    cos_m, sin_m = _rope_tables(seq, MLA_ROPE, MLA_NOPE)
    tq = _pick_tile(seq, 256)
    idx = jnp.arange(tq)
    u_sb = (idx[:, None] > idx[None, :]).astype(BF16)
    tk_sm = _pick_tile(seq, 2 * tq)

    for l in range(depth):
        wl = w_in[l]
        o_fox, o_mla, o_dil = N_SB, N_SB + N_FOX_QKV + N_HEADS_PER_MIXER, N_SB + N_FOX_QKV + N_HEADS_PER_MIXER + N_MLA
        qs = lambda w: w.at[:, 0:hq].multiply(qk_scale)
        w_sb = qs(wl[:, 0:N_SB])
        w_fx = qs(wl[:, o_fox:o_fox + N_FOX_QKV])
        w_f = wl[:, o_fox + N_FOX_QKV:o_mla]
        w_ml = wl[:, o_mla:o_dil]
        wd = wl[:, o_dil:].reshape(d, 3, len(DIL_BRANCHES), hq)
        w_br = [qs(jnp.concatenate([wd[:, 0, g], wd[:, 1, g], wd[:, 2, g]], axis=1)) for g in range(len(DIL_BRANCHES))]
        w_misc = jnp.concatenate([w_ml, w_f, jnp.zeros((d, MISC_W - N_MLA - N_HEADS_PER_MIXER), F32)], axis=1)
        w_tok = jnp.concatenate([w_sb, w_fx, w_misc] + w_br, axis=1).astype(BF16)

        wq = jnp.pad(w_uq[l].reshape(MLA_Q_LORA, N_HEADS_PER_MIXER, MLA_NOPE + MLA_ROPE),
                     ((0, 0), (0, 0), (0, LANES - MLA_NOPE - MLA_ROPE))).reshape(MLA_Q_LORA, -1).astype(BF16)
        wkv = w_ukv[l].reshape(MLA_KV_LORA, N_HEADS_PER_MIXER, MLA_NOPE + HEAD_DIM)
        wk = jnp.pad(wkv[:, :, :MLA_NOPE], ((0, 0), (0, 0), (0, LANES - MLA_NOPE))).reshape(MLA_KV_LORA, -1).astype(BF16)
        wv = wkv[:, :, MLA_NOPE:].reshape(MLA_KV_LORA, -1).astype(BF16)
        g_flat = g_head[l].reshape(1, -1)
        wr = jnp.concatenate([w_group[l], w_expert[l],
                              jnp.zeros((d, LANES - N_GROUPS - N_EXPERTS), F32)], axis=1)
        br = jnp.concatenate([b_group[l], b_expert[l],
                              jnp.zeros((LANES - N_GROUPS - N_EXPERTS,), F32)]).reshape(1, LANES)

        tm = _pick_tile(seq, 512)
        sb, fx, misc, *qkv_br = _proj_call(
            x, w_tok, cos64, sin64,
            ((N_SB, 0, BF16, 1), (N_FOX_QKV, 0, BF16, 1), (MISC_W, 0, F32, 1))
            + tuple((N_BRANCH, 2 * hq, BF16, r) for _, r in DIL_BRANCHES), tm)
        sb = sb.reshape(t, N_SB)
        fx = fx.reshape(t, N_FOX_QKV)
        misc = misc.reshape(t, MISC_W)

        o_sb = _attn_call("sb", sb, sb, sb, 0, 1, 2, g_flat, 0, bsz, seq, tq, tk_sm, extra=(u_sb,))
        neg_c = _fox_c_call(misc, b_forget[l], bsz, seq)
        o_fx = _attn_call("fox", fx, fx, fx, 0, 1, 2, g_flat, 1, bsz, seq, tq, tk_sm, extra=(neg_c,))
        mq, mk, mv = _mla_prep_call(misc, g_cq[l].reshape(1, -1), g_ckv[l].reshape(1, -1), wq, wk, wv,
                                    cos_m, sin_m, seq, tm)
        o_ml = _attn_call("mla", mq, mk, mv, 0, 0, 0, g_flat, 2, bsz, seq, tq, tk_sm, scale=mla_scale)

        dil = []
        for g, (_, r) in enumerate(DIL_BRANCHES):
            n = seq // r
            og = _dil_call(qkv_br[g], max(win, min(tq, n // 2)), win)
            dil.append(og.reshape(t, 2 * hq))

        x1, route = _post_call(
            x.reshape(t, d), o_sb, o_fx, o_ml, dil[0], dil[1], dil[2], g_flat[:, 3 * hq:], w_out[l].astype(BF16),
            ln1_g[l].reshape(1, d), ln1_b[l].reshape(1, d), wr, br, alpha, _pick_tile(t, 256))

        chunk_expert, n_valid, src, dst, n_rows = _dispatch_tables(route, t)
        y = _moe_call(chunk_expert, n_valid, src, dst, x1,
                      w1[l].astype(BF16), w3[l].astype(BF16), w2[l].astype(BF16), n_rows)
        x = _combine_call(x1, y, route,
                          ln2_g[l].reshape(1, d), ln2_b[l].reshape(1, d), alpha, _pick_tile(t, 256)).reshape(bsz, seq, d)
    return x
```

```python
import functools

import jax
import jax.numpy as jnp
import numpy as np
from jax import lax
from jax.experimental import pallas as pl
from jax.experimental.pallas import tpu as pltpu

F32 = jnp.float32
BF16 = jnp.bfloat16

HEAD_DIM = 64
N_HEADS_PER_MIXER = 4
MLA_Q_LORA = 256
MLA_KV_LORA = 128
MLA_NOPE = 64
MLA_ROPE = 32
DIL_BRANCHES = ((128, 1), (512, 4), (2048, 16))
ROPE_THETA = 10000.0
N_GROUPS = 4
EXPERTS_PER_GROUP = 4
N_EXPERTS = N_GROUPS * EXPERTS_PER_GROUP
TOP_K = 2
MOE_BLOCK = 256
LN_EPS = 1e-5
RMS_EPS = 1e-6

LANES = 128
VMEM_LIMIT_BYTES = 56 * 1024 * 1024

N_SB = 3 * N_HEADS_PER_MIXER * HEAD_DIM
N_FOX_QKV = 3 * N_HEADS_PER_MIXER * HEAD_DIM
N_MLA = MLA_Q_LORA + MLA_KV_LORA + MLA_ROPE
N_BRANCH = 3 * N_HEADS_PER_MIXER * HEAD_DIM
MISC_W = 512
F_COL = N_MLA


def _cparams(sem):
    return pltpu.CompilerParams(dimension_semantics=sem, vmem_limit_bytes=VMEM_LIMIT_BYTES)


def _split3(a):
    hi = a.astype(BF16)
    r1 = a - hi.astype(F32)
    mid = r1.astype(BF16)
    lo = (r1 - mid.astype(F32)).astype(BF16)
    return hi, mid, lo


def _dot(a, b):
    return jnp.dot(a, b, preferred_element_type=F32)


def _dot_nt(a, b):
    return lax.dot_general(a, b, (((1,), (1,)), ((), ())), preferred_element_type=F32)


def _dot_exact_rhs(a, u):
    hi, mid, lo = _split3(a)
    return _dot(hi, u) + _dot(mid, u) + _dot(lo, u)


def _dot_f32(a, b):
    ah = a.astype(BF16)
    al = (a - ah.astype(F32)).astype(BF16)
    bh = b.astype(BF16)
    bl = (b - bh.astype(F32)).astype(BF16)
    return _dot(ah, bh) + (_dot(ah, bl) + _dot(al, bh))


def _lane_iota(shape):
    return lax.broadcasted_iota(jnp.int32, shape, len(shape) - 1)


def _rotate_half(y, half):
    lane = _lane_iota(y.shape)
    fwd = pltpu.roll(y, half, 1)
    bwd = pltpu.roll(y, LANES - half, 1)
    return jnp.where((lane % (2 * half)) < half, -bwd, fwd)


def _log_sigmoid_pair(z):
    sp = jnp.log(1.0 + jnp.exp(-jnp.abs(z)))
    return jnp.minimum(z, 0.0) - sp, -jnp.maximum(z, 0.0) - sp


def _head_rms(o, g):
    lane = _lane_iota(o.shape)
    first = lane < HEAD_DIM
    sq = o * o
    ss_a = jnp.sum(jnp.where(first, sq, 0.0), axis=-1, keepdims=True)
    ss_b = jnp.sum(jnp.where(first, 0.0, sq), axis=-1, keepdims=True)
    ms = jnp.where(first, ss_a, ss_b) * (1.0 / HEAD_DIM)
    return o * lax.rsqrt(ms + RMS_EPS) * g


def _layernorm(y, g, b):
    mu = jnp.mean(y, axis=-1, keepdims=True)
    d = y - mu
    var = jnp.mean(d * d, axis=-1, keepdims=True)
    return d * lax.rsqrt(var + LN_EPS) * g + b


def _proj_kernel(x_ref, w_ref, cos_ref, sin_ref, *refs, outs):
    out_refs, stage_ref = refs[:len(outs)], refs[len(outs)]
    tm = x_ref.shape[1]
    xb = x_ref[0].astype(BF16)
    col = 0
    slab = 0
    for o_ref, (width, n_rope, _, r) in zip(out_refs, outs):
        for c in range(0, width, 2 * LANES):
            cw = min(2 * LANES, width - c)
            y = _dot(xb, w_ref[:, col + c:col + c + cw])
            for s in range(0, cw, LANES):
                ys = y[:, s:s + LANES]
                if c + s < n_rope:
                    ys = ys * cos_ref[...] + _rotate_half(ys, HEAD_DIM // 2) * sin_ref[...]
                if r == 1:
                    o_ref[0, 0, :, c + s:c + s + LANES] = ys.astype(o_ref.dtype)
                else:
                    st = stage_ref.at[slab % stage_ref.shape[0]]
                    slab += 1
                    st[...] = ys
                    for p in range(r):
                        o_ref[0, p, :, c + s:c + s + LANES] = (
                            st[pl.ds(p, tm // r, stride=r), :].astype(o_ref.dtype))
        col += width


PROJ_STAGE_SLABS = 4


def _proj_call(x, w, cos, sin, outs, tm):
    bsz, seq, d = x.shape
    nt = seq // tm
    in_specs = [
        pl.BlockSpec((1, tm, d), lambda b, i: (b, i, 0)),
        pl.BlockSpec(w.shape, lambda b, i: (0, 0)),
        pl.BlockSpec((tm, LANES), lambda b, i: (i, 0)),
        pl.BlockSpec((tm, LANES), lambda b, i: (i, 0)),
    ]
    out_specs = [pl.BlockSpec((1, r, tm // r, wd), lambda b, i: (b, 0, i, 0)) for wd, _, _, r in outs]
    out_shape = [jax.ShapeDtypeStruct((bsz, r, seq // r, wd), dt) for wd, _, dt, r in outs]
    return pl.pallas_call(
        functools.partial(_proj_kernel, outs=outs),
        grid=(bsz, nt), in_specs=in_specs, out_specs=out_specs, out_shape=out_shape,
        scratch_shapes=[pltpu.VMEM((PROJ_STAGE_SLABS, tm, LANES), F32)],
        compiler_params=_cparams(("parallel", "parallel")),
        name="proj",
    )(x, w, cos, sin)


def _fox_c_kernel(misc_ref, bias_ref, uinc_ref, ones_ref, out_ref, *, seq):
    lane0 = F_COL - 3 * LANES
    nblk = seq // LANES

    def body(j, carry):
        r0 = pl.multiple_of(j * LANES, LANES)
        f = misc_ref[pl.ds(r0, LANES), :] + bias_ref[...]
        lf, _ = _log_sigmoid_pair(f)
        lft = lf.T
        csum = _dot_exact_rhs(lft, uinc_ref[...]) + carry
        tot = _dot_exact_rhs(lft, ones_ref[...])
        out_ref[0, :, pl.ds(r0, LANES)] = -csum[lane0:lane0 + N_HEADS_PER_MIXER, :]
        return carry + tot

    lax.fori_loop(0, nblk, body, jnp.zeros((LANES, LANES), F32))


def _fox_c_call(misc, b_forget, bsz, seq):
    bias = jnp.zeros((1, LANES), F32).at[0, F_COL - 3 * LANES:F_COL - 3 * LANES + N_HEADS_PER_MIXER].set(b_forget)
    idx = jnp.arange(LANES)
    uinc = (idx[:, None] <= idx[None, :]).astype(BF16)
    ones = jnp.ones((LANES, LANES), BF16)
    return pl.pallas_call(
        functools.partial(_fox_c_kernel, seq=seq),
        grid=(bsz,),
        in_specs=[
            pl.BlockSpec((seq, LANES), lambda b: (b, 3)),
            pl.BlockSpec((1, LANES), lambda b: (0, 0)),
            pl.BlockSpec((LANES, LANES), lambda b: (0, 0)),
            pl.BlockSpec((LANES, LANES), lambda b: (0, 0)),
        ],
        out_specs=pl.BlockSpec((1, N_HEADS_PER_MIXER, seq), lambda b: (b, 0, 0)),
        out_shape=jax.ShapeDtypeStruct((bsz, N_HEADS_PER_MIXER, seq), F32),
        compiler_params=_cparams(("parallel",)),
        name="fox_c",
    )(misc, bias, uinc, ones)


def _mla_prep_kernel(misc_ref, gq_ref, gkv_ref, wq_ref, wk_ref, wv_ref, cos_ref, sin_ref,
                     q_ref, k_ref, v_ref):
    def rms(x, g):
        return x * lax.rsqrt(jnp.mean(x * x, axis=-1, keepdims=True) + RMS_EPS) * g

    cq = rms(misc_ref[:, 0:MLA_Q_LORA], gq_ref[...]).astype(BF16)
    ckv = rms(misc_ref[:, MLA_Q_LORA:MLA_Q_LORA + MLA_KV_LORA], gkv_ref[...]).astype(BF16)
    kr_blk = misc_ref[:, 3 * LANES:4 * LANES]
    lane = _lane_iota(kr_blk.shape)
    in_rope = (lane >= MLA_NOPE) & (lane < MLA_NOPE + MLA_ROPE)
    kr = jnp.where(in_rope, pltpu.roll(kr_blk, MLA_NOPE, 1), 0.0)
    cos = cos_ref[...]
    sin = sin_ref[...]

    def rope(y):
        return y * cos + _rotate_half(y, MLA_ROPE // 2) * sin

    q = _dot(cq, wq_ref[...])
    k = _dot(ckv, wk_ref[...])
    for h in range(N_HEADS_PER_MIXER):
        sl = slice(h * LANES, (h + 1) * LANES)
        q_ref[:, sl] = rope(q[:, sl]).astype(BF16)
        k_ref[:, sl] = rope(k[:, sl] + kr).astype(BF16)
    v_ref[...] = _dot(ckv, wv_ref[...]).astype(BF16)


def _mla_prep_call(misc, g_cq, g_ckv, wq, wk, wv, cos, sin, seq, tm):
    t = misc.shape[0]
    nper = seq // tm
    hw = N_HEADS_PER_MIXER * LANES
    vw = N_HEADS_PER_MIXER * HEAD_DIM
    full = lambda a: pl.BlockSpec(a.shape, lambda i: (0, 0))
    return pl.pallas_call(
        _mla_prep_kernel,
        grid=(t // tm,),
        in_specs=[
            pl.BlockSpec((tm, MISC_W), lambda i: (i, 0)),
            full(g_cq), full(g_ckv), full(wq), full(wk), full(wv),
            pl.BlockSpec((tm, LANES), lambda i: (i % nper, 0)),
            pl.BlockSpec((tm, LANES), lambda i: (i % nper, 0)),
        ],
        out_specs=[
            pl.BlockSpec((tm, hw), lambda i: (i, 0)),
            pl.BlockSpec((tm, hw), lambda i: (i, 0)),
            pl.BlockSpec((tm, vw), lambda i: (i, 0)),
        ],
        out_shape=[
            jax.ShapeDtypeStruct((t, hw), BF16),
            jax.ShapeDtypeStruct((t, hw), BF16),
            jax.ShapeDtypeStruct((t, vw), BF16),
        ],
        compiler_params=_cparams(("parallel",)),
        name="mla_prep",
    )(misc, g_cq, g_ckv, wq, wk, wv, cos, sin)


def _attn_kernel(*refs, mode, tq, tk, scale):
    if mode == "sb":
        q_ref, k_ref, v_ref, g_ref, u_ref, o_ref = refs
    elif mode == "fox":
        q_ref, k_ref, v_ref, g_ref, nc_ref, o_ref = refs
    else:
        q_ref, k_ref, v_ref, g_ref, o_ref = refs
    nh = N_HEADS_PER_MIXER
    i = pl.program_id(1)
    lane = _lane_iota((tq, LANES))
    first = lane < HEAD_DIM
    row = lax.broadcasted_iota(jnp.int32, (tq, tk), 0)
    colm = lax.broadcasted_iota(jnp.int32, (tq, tk), 1)

    q_heads = []
    for h in range(nh):
        if mode == "mla":
            q_heads.append(q_ref[:, h * LANES:(h + 1) * LANES])
        else:
            q2 = q_ref[:, (h // 2) * LANES:(h // 2 + 1) * LANES]
            zero = jnp.zeros_like(q2)
            q_heads.append(jnp.where(first, q2, zero) if h % 2 == 0 else jnp.where(first, zero, q2))

    def k_head(j, h):
        r0 = pl.multiple_of(j * tk, tk)
        kb = h if mode == "mla" else h // 2
        return k_ref[pl.ds(r0, tk), kb * LANES:(kb + 1) * LANES]

    def v_pair(j, p):
        r0 = pl.multiple_of(j * tk, tk)
        return v_ref[pl.ds(r0, tk), p * LANES:(p + 1) * LANES]

    jd = (i * tq) // tk
    off = i * tq - jd * tk

    if mode == "sb":
        tu = u_ref.shape[0]

        def step(j, carry, diag):
            accs, rs = carry
            new_accs, new_rs = [], []
            for p in range(nh // 2):
                v2 = v_pair(j, p)
                outs = []
                for h in (2 * p, 2 * p + 1):
                    z = _dot_nt(q_heads[h], k_head(j, h))
                    ls_pos, ls_neg = _log_sigmoid_pair(z)
                    if diag:
                        before = colm < row + off
                        ls_neg = jnp.where(before, ls_neg, 0.0)
                    lb = ls_neg.astype(BF16)
                    pieces = []
                    later = rs[h]
                    for c0 in range(tk - tu, -1, -tu):
                        c = _dot(lb[:, c0:c0 + tu], u_ref[...]) + later
                        pieces.insert(0, c)
                        later = c[:, 0:1] + ls_neg[:, c0:c0 + 1]
                    w = jnp.exp(ls_pos + jnp.concatenate(pieces, axis=1))
                    if diag:
                        w = jnp.where(before, w, 0.0)
                    outs.append(_dot(w.astype(BF16), v2))
                    new_rs.append(later)
                new_accs.append(accs[p] + jnp.where(first, outs[0], outs[1]))
            return tuple(new_accs), tuple(new_rs)

        zacc = jnp.zeros((tq, LANES), F32)
        zr = jnp.zeros((tq, 1), F32)
        carry = step(jd, ((zacc,) * (nh // 2), (zr,) * nh), True)
        carry = lax.fori_loop(0, jd, lambda n, c: step(jd - 1 - n, c, False), carry)
        outs = carry[0]
    else:
        ones = jnp.ones((tk, LANES), BF16)

        def step(j, carry, diag):
            accs, ms, ls = carry
            new_accs, new_ms, new_ls = [], [], []
            r0 = pl.multiple_of(j * tk, tk)
            for p in range(nh // 2):
                v_aug = jnp.concatenate([v_pair(j, p), ones], axis=1)
                pv, alphas = [], []
                for h in (2 * p, 2 * p + 1):
                    s = _dot_nt(q_heads[h], k_head(j, h))
                    if mode == "mla":
                        s = s * scale
                    else:
                        s = s + nc_ref[0, h:h + 1, pl.ds(r0, tk)]
                    if diag:
                        s = jnp.where(colm <= row + off, s, -jnp.inf)
                    m_new = jnp.maximum(ms[h], jnp.max(s, axis=-1, keepdims=True))
                    alpha = jnp.exp(ms[h] - m_new)
                    pvx = _dot(jnp.exp(s - m_new).astype(BF16), v_aug)
                    new_ls.append(alpha * ls[h] + pvx[:, LANES:2 * LANES])
                    new_ms.append(m_new)
                    pv.append(pvx[:, 0:LANES])
                    alphas.append(alpha)
                new_accs.append(accs[p] * jnp.where(first, alphas[0], alphas[1])
                                + jnp.where(first, pv[0], pv[1]))
            return tuple(new_accs), tuple(new_ms), tuple(new_ls)

        neg = jnp.full((tq, 1), -jnp.inf, F32)
        zacc = jnp.zeros((tq, LANES), F32)
        carry = step(jd, ((zacc,) * (nh // 2), (neg,) * nh, (zacc,) * nh), True)
        carry = lax.fori_loop(0, jd, lambda n, c: step(n, c, False), carry)
        accs, _, ls = carry
        outs = [accs[p] / jnp.where(first, ls[2 * p], ls[2 * p + 1]) for p in range(nh // 2)]
    for p in range(nh // 2):
        sl = slice(p * LANES, (p + 1) * LANES)
        o_ref[:, sl] = _head_rms(outs[p], g_ref[:, sl]).astype(o_ref.dtype)


def _attn_call(mode, q, k, v, qcol, kcol, vcol, g_flat, gcol, bsz, seq, tq, tk, extra=(), scale=1.0):
    t = bsz * seq
    nq = seq // tq
    hq = N_HEADS_PER_MIXER * HEAD_DIM
    qw = N_HEADS_PER_MIXER * LANES if mode == "mla" else hq
    assert tk % tq == 0 and seq % tk == 0
    in_specs = [
        pl.BlockSpec((tq, qw), lambda b, i: (b * nq + i, qcol)),
        pl.BlockSpec((seq, qw), lambda b, i: (b, kcol)),
        pl.BlockSpec((seq, hq), lambda b, i: (b, vcol)),
        pl.BlockSpec((1, hq), lambda b, i: (0, gcol)),
    ]
    args = [q, k, v, g_flat]
    if mode == "sb":
        (u,) = extra
        in_specs.append(pl.BlockSpec(u.shape, lambda b, i: (0, 0)))
        args.append(u)
    elif mode == "fox":
        (nc,) = extra
        in_specs.append(pl.BlockSpec((1, N_HEADS_PER_MIXER, seq), lambda b, i: (b, 0, 0)))
        args.append(nc)
    return pl.pallas_call(
        functools.partial(_attn_kernel, mode=mode, tq=tq, tk=tk, scale=scale),
        grid=(bsz, nq),
        in_specs=in_specs,
        out_specs=pl.BlockSpec((tq, hq), lambda b, i: (b * nq + i, 0)),
        out_shape=jax.ShapeDtypeStruct((t, hq), BF16),
        compiler_params=_cparams(("parallel", "arbitrary")),
        name=f"attn_{mode}",
    )(*args)


def _dil_kernel(qkv_ref, o_ref, *, tq, tk, win):
    i = pl.program_id(2)
    lane = _lane_iota((tq, LANES))
    first = lane < HEAD_DIM
    hq = N_HEADS_PER_MIXER * HEAD_DIM
    r0 = pl.multiple_of(i * tq, tq)
    k0 = pl.multiple_of(jnp.maximum(i - 1, 0) * tq, tq)
    row = lax.broadcasted_iota(jnp.int32, (tq, tk), 0)
    colm = lax.broadcasted_iota(jnp.int32, (tq, tk), 1)
    delta = row + (r0 - k0) - colm
    band = jnp.abs(2 * delta - win) <= win
    ones = jnp.ones((tk, LANES), BF16)
    for p in range(N_HEADS_PER_MIXER // 2):
        q2 = qkv_ref[0, 0, pl.ds(r0, tq), p * LANES:(p + 1) * LANES]
        k2 = qkv_ref[0, 0, pl.ds(k0, tk), hq + p * LANES:hq + (p + 1) * LANES]
        v_aug = jnp.concatenate(
            [qkv_ref[0, 0, pl.ds(k0, tk), 2 * hq + p * LANES:2 * hq + (p + 1) * LANES], ones], axis=1)
        zero = jnp.zeros_like(q2)
        outs = []
        lses = []
        for hh in range(2):
            qh = jnp.where(first, q2, zero) if hh == 0 else jnp.where(first, zero, q2)
            s = jnp.where(band, _dot_nt(qh, k2), -jnp.inf)
            m = jnp.max(s, axis=-1, keepdims=True)
            pvx = _dot(jnp.exp(s - m).astype(BF16), v_aug)
            l = pvx[:, LANES:2 * LANES]
            outs.append(pvx[:, 0:LANES] / l)
            lses.append(m + jnp.log(l))
        o_ref[0, :, p * LANES:(p + 1) * LANES] = jnp.where(first, outs[0], outs[1])
        o_ref[0, :, hq + p * LANES:hq + (p + 1) * LANES] = jnp.where(first, lses[0], lses[1])


def _dil_call(qkv, tqs, win):
    bsz, r, n, w = qkv.shape
    hq = N_HEADS_PER_MIXER * HEAD_DIM
    tk = 2 * tqs if n >= 2 * tqs else tqs
    assert tqs >= win and n % tqs == 0 and (tk == 2 * tqs or n == tqs)
    return pl.pallas_call(
        functools.partial(_dil_kernel, tq=tqs, tk=tk, win=win),
        grid=(bsz, r, n // tqs),
        in_specs=[pl.BlockSpec((1, 1, n, w), lambda b, p, c: (b, p, 0, 0))],
        out_specs=pl.BlockSpec((1, tqs, 2 * hq), lambda b, p, c: (b, c, p)),
        out_shape=jax.ShapeDtypeStruct((bsz, n, r * 2 * hq), F32),
        compiler_params=_cparams(("parallel", "parallel", "arbitrary")),
        name=f"dil_r{r}",
    )(qkv)


def _post_kernel(x_ref, osb_ref, ofox_ref, omla_ref, d1_ref, d2_ref, d3_ref, gd_ref, wo_ref,
                 lng_ref, lnb_ref, wr_ref, br_ref, x1_ref, route_ref, stage_ref, *, alpha, dils):
    hq = N_HEADS_PER_MIXER * HEAD_DIM
    tm = x_ref.shape[0]
    h = _dot(osb_ref[...], wo_ref[0:hq, :])
    h += _dot(ofox_ref[...], wo_ref[hq:2 * hq, :])
    h += _dot(omla_ref[...], wo_ref[2 * hq:3 * hq, :])

    def token_rows(g, d_ref, r, c0):
        if r == 1:
            return d_ref[:, c0:c0 + LANES]
        st = stage_ref.at[g, c0 // LANES]
        for q in range(r):
            st[pl.ds(q, tm // r, stride=r), :] = d_ref[:, q * 2 * hq + c0:q * 2 * hq + c0 + LANES]
        return st[...]

    d_refs = (d1_ref, d2_ref, d3_ref)
    for p in range(N_HEADS_PER_MIXER // 2):
        sl = slice(p * LANES, (p + 1) * LANES)
        lses = [token_rows(g, d_refs[g], dils[g], hq + p * LANES) for g in range(len(dils))]
        vals = [token_rows(g, d_refs[g], dils[g], p * LANES) for g in range(len(dils))]
        l1, l2, l3 = lses
        m = jnp.maximum(jnp.maximum(l1, l2), l3)
        e1, e2, e3 = jnp.exp(l1 - m), jnp.exp(l2 - m), jnp.exp(l3 - m)
        inv = 1.0 / (e1 + e2 + e3)
        od = (e1 * inv) * vals[0] + (e2 * inv) * vals[1] + (e3 * inv) * vals[2]
        od = _head_rms(od, gd_ref[:, sl]).astype(BF16)
        h += _dot(od, wo_ref[3 * hq + p * LANES:3 * hq + (p + 1) * LANES, :])
    x1 = _layernorm(alpha * x_ref[...] + h, lng_ref[...], lnb_ref[...])
    x1_ref[...] = x1

    logits = _dot_f32(x1, wr_ref[...]) + br_ref[...]
    lane = _lane_iota(logits.shape)
    lanef = lane.astype(F32)
    big = float(LANES)
    ninf = -jnp.inf
    gl = jnp.where(lane < N_GROUPS, logits, ninf)
    gmax = jnp.max(gl, axis=-1, keepdims=True)
    gsel = jnp.min(jnp.where(gl == gmax, lanef, big), axis=-1, keepdims=True)
    gw = 1.0 / jnp.sum(jnp.exp(gl - gmax), axis=-1, keepdims=True)
    e_lo = N_GROUPS + EXPERTS_PER_GROUP * gsel
    in_grp = (lanef >= e_lo) & (lanef < e_lo + EXPERTS_PER_GROUP)
    el = jnp.where(in_grp, logits, ninf)
    t1 = jnp.max(el, axis=-1, keepdims=True)
    i1 = jnp.min(jnp.where(el == t1, lanef, big), axis=-1, keepdims=True)
    el2 = jnp.where(lanef == i1, ninf, el)
    t2 = jnp.max(el2, axis=-1, keepdims=True)
    i2 = jnp.min(jnp.where(el2 == t2, lanef, big), axis=-1, keepdims=True)
    ex = jnp.exp(t2 - t1)
    den = 1.0 + ex
    g1 = gw / den
    g2 = gw * ex / den
    out = jnp.where(lane == 0, i1 - N_GROUPS,
                    jnp.where(lane == 1, i2 - N_GROUPS,
                              jnp.where(lane == 2, g1, jnp.where(lane == 3, g2, 0.0))))
    route_ref[...] = out


def _post_call(x, osb, ofox, omla, dil_outs, dils, g_dil, wo, lng, lnb, wr, br, alpha, tm):
    t, d = x.shape
    hq = N_HEADS_PER_MIXER * HEAD_DIM
    row = lambda w: pl.BlockSpec((tm, w), lambda i: (i, 0))
    full = lambda a: pl.BlockSpec(a.shape, lambda i: (0, 0))
    dil_specs = [pl.BlockSpec((tm // r, r * 2 * hq), lambda i: (i, 0)) for r in dils]
    return pl.pallas_call(
        functools.partial(_post_kernel, alpha=alpha, dils=dils),
        grid=(t // tm,),
        in_specs=[row(d), row(hq), row(hq), row(hq)] + dil_specs
                 + [full(g_dil), full(wo), full(lng), full(lnb), full(wr), full(br)],
        out_specs=[row(d), row(LANES)],
        out_shape=[jax.ShapeDtypeStruct((t, d), F32), jax.ShapeDtypeStruct((t, LANES), F32)],
        scratch_shapes=[pltpu.VMEM((len(dils), 2 * hq // LANES, tm, LANES), F32)],
        compiler_params=_cparams(("parallel",)),
        name="post_mixer",
    )(x, osb, ofox, omla, *dil_outs, g_dil, wo, lng, lnb, wr, br)


DMA_UNROLL = 8


def _moe_kernel(ce_ref, nv_ref, src_ref, srcn_ref, dst_ref, x_hbm, w1_ref, w3_ref, w2_ref, y_hbm,
                xs_ref, ys_ref, gsem, ssem):
    c = pl.program_id(0)
    nc = pl.num_programs(0)
    slot = c % 2
    other = 1 - slot
    nv = nv_ref[c]
    nv_next = jnp.where(c + 1 < nc, nv_ref[jnp.minimum(c + 1, nc - 1)], 0)
    nv_prev = jnp.where(c > 0, nv_ref[jnp.maximum(c - 1, 0)], 0)

    def gather_copy(idx_ref, i, s):
        return pltpu.make_async_copy(x_hbm.at[pl.ds(idx_ref[0, 0, i], 1), :],
                                     xs_ref.at[s, pl.ds(i, 1), :], gsem.at[s])

    def scatter_copy(i, s):
        return pltpu.make_async_copy(ys_ref.at[s, pl.ds(i, 1), :],
                                     y_hbm.at[pl.ds(dst_ref[0, 0, i], 1), :], ssem.at[s])

    def start_rows(n, make):
        ngrp = n // DMA_UNROLL

        def grp(g, _):
            for k in range(DMA_UNROLL):
                make(g * DMA_UNROLL + k).start()
            return 0

        def one(i, _):
            make(i).start()
            return 0

        lax.fori_loop(0, ngrp, grp, 0)
        lax.fori_loop(ngrp * DMA_UNROLL, n, one, 0)

    def wait_rows(n, make_row, make_block):
        @pl.when(n == MOE_BLOCK)
        def _():
            make_block().wait()

        @pl.when(n < MOE_BLOCK)
        def _():
            def one(i, _):
                make_row(i).wait()
                return 0
            lax.fori_loop(0, n, one, 0)

    def gather_block(s):
        return pltpu.make_async_copy(x_hbm.at[pl.ds(0, MOE_BLOCK), :], xs_ref.at[s], gsem.at[s])

    def scatter_block(s):
        return pltpu.make_async_copy(ys_ref.at[s], y_hbm.at[pl.ds(0, MOE_BLOCK), :], ssem.at[s])

    @pl.when(c == 0)
    def _():
        xs_ref[...] = jnp.zeros_like(xs_ref)
        start_rows(nv, lambda i: gather_copy(src_ref, i, 0))

    start_rows(nv_next, lambda i: gather_copy(srcn_ref, i, other))
    wait_rows(nv, lambda i: gather_copy(src_ref, i, slot), lambda: gather_block(slot))

    @pl.when(nv > 0)
    def _():
        xb = xs_ref[slot].astype(BF16)
        a = _dot(xb, w1_ref[0])
        b = _dot(xb, w3_ref[0])
        hid = (a / (1.0 + jnp.exp(-a)) * b).astype(BF16)
        ys_ref[slot] = _dot(hid, w2_ref[0])

    wait_rows(nv_prev, lambda i: scatter_copy(i, other), lambda: scatter_block(other))
    start_rows(nv, lambda i: scatter_copy(i, slot))

    @pl.when(c == nc - 1)
    def _():
        wait_rows(nv, lambda i: scatter_copy(i, slot), lambda: scatter_block(slot))


def _moe_call(chunk_expert, n_valid, src, dst, x1, w1, w3, w2, n_rows_out):
    n_chunks = chunk_expert.shape[0]
    t, d = x1.shape
    de = w1.shape[-1]
    grid_spec = pltpu.PrefetchScalarGridSpec(
        num_scalar_prefetch=2,
        grid=(n_chunks,),
        in_specs=[
            pl.BlockSpec((1, 1, MOE_BLOCK), lambda c, ce, nv: (c, 0, 0), memory_space=pltpu.SMEM),
            pl.BlockSpec((1, 1, MOE_BLOCK), lambda c, ce, nv: (jnp.minimum(c + 1, n_chunks - 1), 0, 0),
                         memory_space=pltpu.SMEM),
            pl.BlockSpec((1, 1, MOE_BLOCK), lambda c, ce, nv: (c, 0, 0), memory_space=pltpu.SMEM),
            pl.BlockSpec(memory_space=pl.ANY),
            pl.BlockSpec((1, d, de), lambda c, ce, nv: (ce[c], 0, 0)),
            pl.BlockSpec((1, d, de), lambda c, ce, nv: (ce[c], 0, 0)),
            pl.BlockSpec((1, de, d), lambda c, ce, nv: (ce[c], 0, 0)),
        ],
        out_specs=pl.BlockSpec(memory_space=pl.ANY),
        scratch_shapes=[
            pltpu.VMEM((2, MOE_BLOCK, d), F32),
            pltpu.VMEM((2, MOE_BLOCK, d), F32),
            pltpu.SemaphoreType.DMA((2,)),
            pltpu.SemaphoreType.DMA((2,)),
        ],
    )
    return pl.pallas_call(
        _moe_kernel,
        grid_spec=grid_spec,
        out_shape=jax.ShapeDtypeStruct((n_rows_out, d), F32),
        compiler_params=_cparams(("arbitrary",)),
        name="moe_experts",
    )(chunk_expert, n_valid, src, src, dst, x1, w1, w3, w2)


def _combine_kernel(x1_ref, ya_ref, yb_ref, route_ref, lng_ref, lnb_ref, o_ref, *, alpha):
    g1 = route_ref[:, 2:3]
    g2 = route_ref[:, 3:4]
    m = g1 * ya_ref[...] + g2 * yb_ref[...]
    o_ref[...] = _layernorm(alpha * x1_ref[...] + m, lng_ref[...], lnb_ref[...])


def _combine_call(x1, y, route, lng, lnb, alpha, tm):
    t, d = x1.shape
    nt = t // tm
    full = lambda a: pl.BlockSpec(a.shape, lambda i: (0, 0))
    return pl.pallas_call(
        functools.partial(_combine_kernel, alpha=alpha),
        grid=(nt,),
        in_specs=[pl.BlockSpec((tm, d), lambda i: (i, 0)),
                  pl.BlockSpec((tm, d), lambda i: (i, 0)),
                  pl.BlockSpec((tm, d), lambda i: (i + nt, 0)),
                  pl.BlockSpec((tm, LANES), lambda i: (i, 0)),
                  full(lng), full(lnb)],
        out_specs=pl.BlockSpec((tm, d), lambda i: (i, 0)),
        out_shape=jax.ShapeDtypeStruct((t, d), F32),
        compiler_params=_cparams(("parallel",)),
        name="moe_combine",
    )(x1, y, y, route, lng, lnb)


def _rope_tables(seq, dim, lane_lo):
    half = dim // 2
    inv_freq = ROPE_THETA ** (-jnp.arange(half, dtype=F32) / half)
    ang = jnp.arange(seq, dtype=F32)[:, None] * inv_freq[None, :]
    cos = jnp.concatenate([jnp.cos(ang), jnp.cos(ang)], -1)
    sin = jnp.concatenate([jnp.sin(ang), jnp.sin(ang)], -1)
    if lane_lo == 0:
        reps = LANES // dim
        return jnp.tile(cos, (1, reps)), jnp.tile(sin, (1, reps))
    cos_t = jnp.ones((seq, LANES), F32).at[:, lane_lo:lane_lo + dim].set(cos)
    sin_t = jnp.zeros((seq, LANES), F32).at[:, lane_lo:lane_lo + dim].set(sin)
    return cos_t, sin_t


def _dispatch_tables(route, n_tok):
    expert_id = route[:, 0:TOP_K].astype(jnp.int32).reshape(-1)
    n_assign = n_tok * TOP_K
    n_slots = n_assign + N_EXPERTS * MOE_BLOCK
    n_chunks = n_slots // MOE_BLOCK
    onehot = (expert_id[:, None] == jnp.arange(N_EXPERTS, dtype=jnp.int32)[None, :]).astype(jnp.int32)
    ranks = jnp.cumsum(onehot, axis=0) - onehot
    rank = jnp.sum(ranks * onehot, axis=1)
    counts = jnp.sum(onehot, axis=0)
    padded = (counts + MOE_BLOCK - 1) // MOE_BLOCK * MOE_BLOCK
    pad_end = jnp.cumsum(padded)
    pad_start = pad_end - padded
    dest = pad_start[expert_id] + rank
    assign = jnp.arange(n_assign, dtype=jnp.int32)
    slot_assign = jnp.zeros((n_slots,), jnp.int32).at[dest].set(assign)
    src = slot_assign // TOP_K
    dst = (slot_assign % TOP_K) * n_tok + src
    chunk_start = jnp.arange(n_chunks, dtype=jnp.int32) * MOE_BLOCK
    chunk_expert = jnp.minimum(jnp.searchsorted(pad_end, chunk_start, side="right"),
                               N_EXPERTS - 1).astype(jnp.int32)
    n_valid = jnp.clip(pad_start[chunk_expert] + counts[chunk_expert] - chunk_start,
                       0, MOE_BLOCK).astype(jnp.int32)
    return (chunk_expert, n_valid, src.reshape(n_chunks, 1, MOE_BLOCK),
            dst.reshape(n_chunks, 1, MOE_BLOCK), n_assign)


def _pick_tile(n, pref):
    t = pref
    while n % t:
        t //= 2
    return t


def kernel(x, w_in, b_forget, g_cq, g_ckv, w_uq, w_ukv, g_head, w_out, ln1_g, ln1_b,
           w_group, b_group, w_expert, b_expert, w1, w3, w2, ln2_g, ln2_b):
    bsz, seq, d = x.shape
    depth = w_in.shape[0]
    t = bsz * seq
    alpha = (2.0 * depth) ** 0.25
    hq = N_HEADS_PER_MIXER * HEAD_DIM
    qk_scale = HEAD_DIM ** -0.5
    mla_scale = (MLA_NOPE + MLA_ROPE) ** -0.5
    win = DIL_BRANCHES[0][0]
    assert all(w // r == win for w, r in DIL_BRANCHES)
    assert seq % (DIL_BRANCHES[-1][1] * win) == 0 and d % LANES == 0

    cos64, sin64 = _rope_tables(seq, HEAD_DIM, 0)
    cos_m, sin_m = _rope_tables(seq, MLA_ROPE, MLA_NOPE)
    tq = _pick_tile(seq, 256)
    idx = jnp.arange(tq)
    u_sb = (idx[:, None] > idx[None, :]).astype(BF16)
    tk_sm = _pick_tile(seq, 2 * tq)

    for l in range(depth):
        wl = w_in[l]
        o_fox, o_mla, o_dil = N_SB, N_SB + N_FOX_QKV + N_HEADS_PER_MIXER, N_SB + N_FOX_QKV + N_HEADS_PER_MIXER + N_MLA
        qs = lambda w: w.at[:, 0:hq].multiply(qk_scale)
        w_sb = qs(wl[:, 0:N_SB])
        w_fx = qs(wl[:, o_fox:o_fox + N_FOX_QKV])
        w_f = wl[:, o_fox + N_FOX_QKV:o_mla]
        w_ml = wl[:, o_mla:o_dil]
        wd = wl[:, o_dil:].reshape(d, 3, len(DIL_BRANCHES), hq)
        w_br = [qs(jnp.concatenate([wd[:, 0, g], wd[:, 1, g], wd[:, 2, g]], axis=1)) for g in range(len(DIL_BRANCHES))]
        w_misc = jnp.concatenate([w_ml, w_f, jnp.zeros((d, MISC_W - N_MLA - N_HEADS_PER_MIXER), F32)], axis=1)
        w_tok = jnp.concatenate([w_sb, w_fx, w_misc] + w_br, axis=1).astype(BF16)

        wq = jnp.pad(w_uq[l].reshape(MLA_Q_LORA, N_HEADS_PER_MIXER, MLA_NOPE + MLA_ROPE),
                     ((0, 0), (0, 0), (0, LANES - MLA_NOPE - MLA_ROPE))).reshape(MLA_Q_LORA, -1).astype(BF16)
        wkv = w_ukv[l].reshape(MLA_KV_LORA, N_HEADS_PER_MIXER, MLA_NOPE + HEAD_DIM)
        wk = jnp.pad(wkv[:, :, :MLA_NOPE], ((0, 0), (0, 0), (0, LANES - MLA_NOPE))).reshape(MLA_KV_LORA, -1).astype(BF16)
        wv = wkv[:, :, MLA_NOPE:].reshape(MLA_KV_LORA, -1).astype(BF16)
        g_flat = g_head[l].reshape(1, -1)
        wr = jnp.concatenate([w_group[l], w_expert[l],
                              jnp.zeros((d, LANES - N_GROUPS - N_EXPERTS), F32)], axis=1)
        br = jnp.concatenate([b_group[l], b_expert[l],
                              jnp.zeros((LANES - N_GROUPS - N_EXPERTS,), F32)]).reshape(1, LANES)

        tm = _pick_tile(seq, 512)
        sb, fx, misc, *qkv_br = _proj_call(
            x, w_tok, cos64, sin64,
            ((N_SB, 0, BF16, 1), (N_FOX_QKV, 0, BF16, 1), (MISC_W, 0, F32, 1))
            + tuple((N_BRANCH, 2 * hq, BF16, r) for _, r in DIL_BRANCHES), tm)
        sb = sb.reshape(t, N_SB)
        fx = fx.reshape(t, N_FOX_QKV)
        misc = misc.reshape(t, MISC_W)

        o_sb = _attn_call("sb", sb, sb, sb, 0, 1, 2, g_flat, 0, bsz, seq, tq, tk_sm, extra=(u_sb,))
        neg_c = _fox_c_call(misc, b_forget[l], bsz, seq)
        o_fx = _attn_call("fox", fx, fx, fx, 0, 1, 2, g_flat, 1, bsz, seq, tq, tk_sm, extra=(neg_c,))
        mq, mk, mv = _mla_prep_call(misc, g_cq[l].reshape(1, -1), g_ckv[l].reshape(1, -1), wq, wk, wv,
                                    cos_m, sin_m, seq, tm)
        o_ml = _attn_call("mla", mq, mk, mv, 0, 0, 0, g_flat, 2, bsz, seq, tq, tk_sm, scale=mla_scale)

        dil = []
        for g, (_, r) in enumerate(DIL_BRANCHES):
            n = seq // r
            og = _dil_call(qkv_br[g], max(win, min(tq, n // 2)), win)
            dil.append(og.reshape(t // r, r * 2 * hq))

        x1, route = _post_call(
            x.reshape(t, d), o_sb, o_fx, o_ml, dil, tuple(r for _, r in DIL_BRANCHES),
            g_flat[:, 3 * hq:], w_out[l].astype(BF16),
            ln1_g[l].reshape(1, d), ln1_b[l].reshape(1, d), wr, br, alpha, _pick_tile(t, 256))

        chunk_expert, n_valid, src, dst, n_rows = _dispatch_tables(route, t)
        y = _moe_call(chunk_expert, n_valid, src, dst, x1,
                      w1[l].astype(BF16), w3[l].astype(BF16), w2[l].astype(BF16), n_rows)
        x = _combine_call(x1, y, route,
                          ln2_g[l].reshape(1, d), ln2_b[l].reshape(1, d), alpha, _pick_tile(t, 256)).reshape(bsz, seq, d)
    return x
```

```python
import functools

import jax
import jax.numpy as jnp
import numpy as np
from jax import lax
from jax.experimental import pallas as pl
from jax.experimental.pallas import tpu as pltpu

F32 = jnp.float32
BF16 = jnp.bfloat16

HEAD_DIM = 64
N_HEADS_PER_MIXER = 4
MLA_Q_LORA = 256
MLA_KV_LORA = 128
MLA_NOPE = 64
MLA_ROPE = 32
DIL_BRANCHES = ((128, 1), (512, 4), (2048, 16))
ROPE_THETA = 10000.0
N_GROUPS = 4
EXPERTS_PER_GROUP = 4
N_EXPERTS = N_GROUPS * EXPERTS_PER_GROUP
TOP_K = 2
MOE_BLOCK = 256
LN_EPS = 1e-5
RMS_EPS = 1e-6

LANES = 128
VMEM_LIMIT_BYTES = 56 * 1024 * 1024

N_SB = 3 * N_HEADS_PER_MIXER * HEAD_DIM
N_FOX_QKV = 3 * N_HEADS_PER_MIXER * HEAD_DIM
N_MLA = MLA_Q_LORA + MLA_KV_LORA + MLA_ROPE
N_BRANCH = 3 * N_HEADS_PER_MIXER * HEAD_DIM
MISC_W = 512
F_COL = N_MLA


def _cparams(sem):
    return pltpu.CompilerParams(dimension_semantics=sem, vmem_limit_bytes=VMEM_LIMIT_BYTES)


def _split3(a):
    hi = a.astype(BF16)
    r1 = a - hi.astype(F32)
    mid = r1.astype(BF16)
    lo = (r1 - mid.astype(F32)).astype(BF16)
    return hi, mid, lo


def _dot(a, b):
    return jnp.dot(a, b, preferred_element_type=F32)


def _dot_nt(a, b):
    return lax.dot_general(a, b, (((1,), (1,)), ((), ())), preferred_element_type=F32)


def _dot_exact_rhs(a, u):
    hi, mid, lo = _split3(a)
    return _dot(hi, u) + _dot(mid, u) + _dot(lo, u)


def _dot_f32(a, b):
    ah = a.astype(BF16)
    al = (a - ah.astype(F32)).astype(BF16)
    bh = b.astype(BF16)
    bl = (b - bh.astype(F32)).astype(BF16)
    return _dot(ah, bh) + (_dot(ah, bl) + _dot(al, bh))


def _lane_iota(shape):
    return lax.broadcasted_iota(jnp.int32, shape, len(shape) - 1)


def _rotate_half(y, half):
    lane = _lane_iota(y.shape)
    fwd = pltpu.roll(y, half, 1)
    bwd = pltpu.roll(y, LANES - half, 1)
    return jnp.where((lane % (2 * half)) < half, -bwd, fwd)


def _log_sigmoid_pair(z):
    sp = jnp.log(1.0 + jnp.exp(-jnp.abs(z)))
    return jnp.minimum(z, 0.0) - sp, -jnp.maximum(z, 0.0) - sp


def _head_rms(o, g):
    lane = _lane_iota(o.shape)
    first = lane < HEAD_DIM
    sq = o * o
    ss_a = jnp.sum(jnp.where(first, sq, 0.0), axis=-1, keepdims=True)
    ss_b = jnp.sum(jnp.where(first, 0.0, sq), axis=-1, keepdims=True)
    ms = jnp.where(first, ss_a, ss_b) * (1.0 / HEAD_DIM)
    return o * lax.rsqrt(ms + RMS_EPS) * g


def _layernorm(y, g, b):
    mu = jnp.mean(y, axis=-1, keepdims=True)
    d = y - mu
    var = jnp.mean(d * d, axis=-1, keepdims=True)
    return d * lax.rsqrt(var + LN_EPS) * g + b


def _proj_kernel(x_ref, w_ref, cos_ref, sin_ref, *refs, outs):
    out_refs, stage_ref = refs[:len(outs)], refs[len(outs)]
    tm = x_ref.shape[1]
    xb = x_ref[0].astype(BF16)
    col = 0
    slab = 0
    for o_ref, (width, n_rope, _, r) in zip(out_refs, outs):
        for c in range(0, width, 2 * LANES):
            cw = min(2 * LANES, width - c)
            y = _dot(xb, w_ref[:, col + c:col + c + cw])
            for s in range(0, cw, LANES):
                ys = y[:, s:s + LANES]
                if c + s < n_rope:
                    ys = ys * cos_ref[...] + _rotate_half(ys, HEAD_DIM // 2) * sin_ref[...]
                if r == 1:
                    o_ref[0, 0, :, c + s:c + s + LANES] = ys.astype(o_ref.dtype)
                else:
                    st = stage_ref.at[slab % stage_ref.shape[0]]
                    slab += 1
                    st[...] = ys
                    for p in range(r):
                        o_ref[0, p, :, c + s:c + s + LANES] = (
                            st[pl.ds(p, tm // r, stride=r), :].astype(o_ref.dtype))
        col += width


PROJ_STAGE_SLABS = 4


def _proj_call(x, w, cos, sin, outs, tm):
    bsz, seq, d = x.shape
    nt = seq // tm
    in_specs = [
        pl.BlockSpec((1, tm, d), lambda b, i: (b, i, 0)),
        pl.BlockSpec(w.shape, lambda b, i: (0, 0)),
        pl.BlockSpec((tm, LANES), lambda b, i: (i, 0)),
        pl.BlockSpec((tm, LANES), lambda b, i: (i, 0)),
    ]
    out_specs = [pl.BlockSpec((1, r, tm // r, wd), lambda b, i: (b, 0, i, 0)) for wd, _, _, r in outs]
    out_shape = [jax.ShapeDtypeStruct((bsz, r, seq // r, wd), dt) for wd, _, dt, r in outs]
    return pl.pallas_call(
        functools.partial(_proj_kernel, outs=outs),
        grid=(bsz, nt), in_specs=in_specs, out_specs=out_specs, out_shape=out_shape,
        scratch_shapes=[pltpu.VMEM((PROJ_STAGE_SLABS, tm, LANES), F32)],
        compiler_params=_cparams(("parallel", "parallel")),
        name="proj",
    )(x, w, cos, sin)


def _fox_c_kernel(misc_ref, bias_ref, uinc_ref, ones_ref, out_ref, *, seq):
    lane0 = F_COL - 3 * LANES
    nblk = seq // LANES

    def body(j, carry):
        r0 = pl.multiple_of(j * LANES, LANES)
        f = misc_ref[pl.ds(r0, LANES), :] + bias_ref[...]
        lf, _ = _log_sigmoid_pair(f)
        lft = lf.T
        csum = _dot_exact_rhs(lft, uinc_ref[...]) + carry
        tot = _dot_exact_rhs(lft, ones_ref[...])
        out_ref[0, :, pl.ds(r0, LANES)] = -csum[lane0:lane0 + N_HEADS_PER_MIXER, :]
        return carry + tot

    lax.fori_loop(0, nblk, body, jnp.zeros((LANES, LANES), F32))


def _fox_c_call(misc, b_forget, bsz, seq):
    bias = jnp.zeros((1, LANES), F32).at[0, F_COL - 3 * LANES:F_COL - 3 * LANES + N_HEADS_PER_MIXER].set(b_forget)
    idx = jnp.arange(LANES)
    uinc = (idx[:, None] <= idx[None, :]).astype(BF16)
    ones = jnp.ones((LANES, LANES), BF16)
    return pl.pallas_call(
        functools.partial(_fox_c_kernel, seq=seq),
        grid=(bsz,),
        in_specs=[
            pl.BlockSpec((seq, LANES), lambda b: (b, 3)),
            pl.BlockSpec((1, LANES), lambda b: (0, 0)),
            pl.BlockSpec((LANES, LANES), lambda b: (0, 0)),
            pl.BlockSpec((LANES, LANES), lambda b: (0, 0)),
        ],
        out_specs=pl.BlockSpec((1, N_HEADS_PER_MIXER, seq), lambda b: (b, 0, 0)),
        out_shape=jax.ShapeDtypeStruct((bsz, N_HEADS_PER_MIXER, seq), F32),
        compiler_params=_cparams(("parallel",)),
        name="fox_c",
    )(misc, bias, uinc, ones)


def _mla_prep_kernel(misc_ref, gq_ref, gkv_ref, wq_ref, wk_ref, wv_ref, cos_ref, sin_ref,
                     q_ref, k_ref, v_ref):
    def rms(x, g):
        return x * lax.rsqrt(jnp.mean(x * x, axis=-1, keepdims=True) + RMS_EPS) * g

    cq = rms(misc_ref[:, 0:MLA_Q_LORA], gq_ref[...]).astype(BF16)
    ckv = rms(misc_ref[:, MLA_Q_LORA:MLA_Q_LORA + MLA_KV_LORA], gkv_ref[...]).astype(BF16)
    kr_blk = misc_ref[:, 3 * LANES:4 * LANES]
    lane = _lane_iota(kr_blk.shape)
    in_rope = (lane >= MLA_NOPE) & (lane < MLA_NOPE + MLA_ROPE)
    kr = jnp.where(in_rope, pltpu.roll(kr_blk, MLA_NOPE, 1), 0.0)
    cos = cos_ref[...]
    sin = sin_ref[...]

    def rope(y):
        return y * cos + _rotate_half(y, MLA_ROPE // 2) * sin

    q = _dot(cq, wq_ref[...])
    k = _dot(ckv, wk_ref[...])
    for h in range(N_HEADS_PER_MIXER):
        sl = slice(h * LANES, (h + 1) * LANES)
        q_ref[:, sl] = rope(q[:, sl]).astype(BF16)
        k_ref[:, sl] = rope(k[:, sl] + kr).astype(BF16)
    v_ref[...] = _dot(ckv, wv_ref[...]).astype(BF16)


def _mla_prep_call(misc, g_cq, g_ckv, wq, wk, wv, cos, sin, seq, tm):
    t = misc.shape[0]
    nper = seq // tm
    hw = N_HEADS_PER_MIXER * LANES
    vw = N_HEADS_PER_MIXER * HEAD_DIM
    full = lambda a: pl.BlockSpec(a.shape, lambda i: (0, 0))
    return pl.pallas_call(
        _mla_prep_kernel,
        grid=(t // tm,),
        in_specs=[
            pl.BlockSpec((tm, MISC_W), lambda i: (i, 0)),
            full(g_cq), full(g_ckv), full(wq), full(wk), full(wv),
            pl.BlockSpec((tm, LANES), lambda i: (i % nper, 0)),
            pl.BlockSpec((tm, LANES), lambda i: (i % nper, 0)),
        ],
        out_specs=[
            pl.BlockSpec((tm, hw), lambda i: (i, 0)),
            pl.BlockSpec((tm, hw), lambda i: (i, 0)),
            pl.BlockSpec((tm, vw), lambda i: (i, 0)),
        ],
        out_shape=[
            jax.ShapeDtypeStruct((t, hw), BF16),
            jax.ShapeDtypeStruct((t, hw), BF16),
            jax.ShapeDtypeStruct((t, vw), BF16),
        ],
        compiler_params=_cparams(("parallel",)),
        name="mla_prep",
    )(misc, g_cq, g_ckv, wq, wk, wv, cos, sin)


def _attn_kernel(*refs, mode, tq, tk, scale):
    if mode == "sb":
        q_ref, k_ref, v_ref, g_ref, u_ref, o_ref = refs
    elif mode == "fox":
        q_ref, k_ref, v_ref, g_ref, nc_ref, o_ref = refs
    else:
        q_ref, k_ref, v_ref, g_ref, o_ref = refs
    nh = N_HEADS_PER_MIXER
    i = pl.program_id(1)
    lane = _lane_iota((tq, LANES))
    first = lane < HEAD_DIM
    row = lax.broadcasted_iota(jnp.int32, (tq, tk), 0)
    colm = lax.broadcasted_iota(jnp.int32, (tq, tk), 1)

    q_heads = []
    for h in range(nh):
        if mode == "mla":
            q_heads.append(q_ref[:, h * LANES:(h + 1) * LANES])
        else:
            q2 = q_ref[:, (h // 2) * LANES:(h // 2 + 1) * LANES]
            zero = jnp.zeros_like(q2)
            q_heads.append(jnp.where(first, q2, zero) if h % 2 == 0 else jnp.where(first, zero, q2))

    def k_head(j, h):
        r0 = pl.multiple_of(j * tk, tk)
        kb = h if mode == "mla" else h // 2
        return k_ref[pl.ds(r0, tk), kb * LANES:(kb + 1) * LANES]

    def v_pair(j, p):
        r0 = pl.multiple_of(j * tk, tk)
        return v_ref[pl.ds(r0, tk), p * LANES:(p + 1) * LANES]

    jd = (i * tq) // tk
    off = i * tq - jd * tk

    if mode == "sb":
        tu = u_ref.shape[0]

        def step(j, carry, diag):
            accs, rs = carry
            new_accs, new_rs = [], []
            for p in range(nh // 2):
                v2 = v_pair(j, p)
                outs = []
                for h in (2 * p, 2 * p + 1):
                    z = _dot_nt(q_heads[h], k_head(j, h))
                    ls_pos, ls_neg = _log_sigmoid_pair(z)
                    if diag:
                        before = colm < row + off
                        ls_neg = jnp.where(before, ls_neg, 0.0)
                    lb = ls_neg.astype(BF16)
                    pieces = []
                    later = rs[h]
                    for c0 in range(tk - tu, -1, -tu):
                        c = _dot(lb[:, c0:c0 + tu], u_ref[...]) + later
                        pieces.insert(0, c)
                        later = c[:, 0:1] + ls_neg[:, c0:c0 + 1]
                    w = jnp.exp(ls_pos + jnp.concatenate(pieces, axis=1))
                    if diag:
                        w = jnp.where(before, w, 0.0)
                    outs.append(_dot(w.astype(BF16), v2))
                    new_rs.append(later)
                new_accs.append(accs[p] + jnp.where(first, outs[0], outs[1]))
            return tuple(new_accs), tuple(new_rs)

        zacc = jnp.zeros((tq, LANES), F32)
        zr = jnp.zeros((tq, 1), F32)
        carry = step(jd, ((zacc,) * (nh // 2), (zr,) * nh), True)
        carry = lax.fori_loop(0, jd, lambda n, c: step(jd - 1 - n, c, False), carry)
        outs = carry[0]
    else:
        ones = jnp.ones((tk, LANES), BF16)

        def step(j, carry, diag):
            accs, ms, ls = carry
            new_accs, new_ms, new_ls = [], [], []
            r0 = pl.multiple_of(j * tk, tk)
            for p in range(nh // 2):
                v_aug = jnp.concatenate([v_pair(j, p), ones], axis=1)
                pv, alphas = [], []
                for h in (2 * p, 2 * p + 1):
                    s = _dot_nt(q_heads[h], k_head(j, h))
                    if mode == "mla":
                        s = s * scale
                    else:
                        s = s + nc_ref[0, h:h + 1, pl.ds(r0, tk)]
                    if diag:
                        s = jnp.where(colm <= row + off, s, -jnp.inf)
                    m_new = jnp.maximum(ms[h], jnp.max(s, axis=-1, keepdims=True))
                    alpha = jnp.exp(ms[h] - m_new)
                    pvx = _dot(jnp.exp(s - m_new).astype(BF16), v_aug)
                    new_ls.append(alpha * ls[h] + pvx[:, LANES:2 * LANES])
                    new_ms.append(m_new)
                    pv.append(pvx[:, 0:LANES])
                    alphas.append(alpha)
                new_accs.append(accs[p] * jnp.where(first, alphas[0], alphas[1])
                                + jnp.where(first, pv[0], pv[1]))
            return tuple(new_accs), tuple(new_ms), tuple(new_ls)

        neg = jnp.full((tq, 1), -jnp.inf, F32)
        zacc = jnp.zeros((tq, LANES), F32)
        carry = step(jd, ((zacc,) * (nh // 2), (neg,) * nh, (zacc,) * nh), True)
        carry = lax.fori_loop(0, jd, lambda n, c: step(n, c, False), carry)
        accs, _, ls = carry
        outs = [accs[p] / jnp.where(first, ls[2 * p], ls[2 * p + 1]) for p in range(nh // 2)]
    for p in range(nh // 2):
        sl = slice(p * LANES, (p + 1) * LANES)
        o_ref[:, sl] = _head_rms(outs[p], g_ref[:, sl]).astype(o_ref.dtype)


def _attn_call(mode, q, k, v, qcol, kcol, vcol, g_flat, gcol, bsz, seq, tq, tk, extra=(), scale=1.0):
    t = bsz * seq
    nq = seq // tq
    hq = N_HEADS_PER_MIXER * HEAD_DIM
    qw = N_HEADS_PER_MIXER * LANES if mode == "mla" else hq
    assert tk % tq == 0 and seq % tk == 0
    in_specs = [
        pl.BlockSpec((tq, qw), lambda b, i: (b * nq + i, qcol)),
        pl.BlockSpec((seq, qw), lambda b, i: (b, kcol)),
        pl.BlockSpec((seq, hq), lambda b, i: (b, vcol)),
        pl.BlockSpec((1, hq), lambda b, i: (0, gcol)),
    ]
    args = [q, k, v, g_flat]
    if mode == "sb":
        (u,) = extra
        in_specs.append(pl.BlockSpec(u.shape, lambda b, i: (0, 0)))
        args.append(u)
    elif mode == "fox":
        (nc,) = extra
        in_specs.append(pl.BlockSpec((1, N_HEADS_PER_MIXER, seq), lambda b, i: (b, 0, 0)))
        args.append(nc)
    return pl.pallas_call(
        functools.partial(_attn_kernel, mode=mode, tq=tq, tk=tk, scale=scale),
        grid=(bsz, nq),
        in_specs=in_specs,
        out_specs=pl.BlockSpec((tq, hq), lambda b, i: (b * nq + i, 0)),
        out_shape=jax.ShapeDtypeStruct((t, hq), BF16),
        compiler_params=_cparams(("parallel", "arbitrary")),
        name=f"attn_{mode}",
    )(*args)


def _dil_kernel(qkv_ref, o_ref, *, tq, tk, win):
    i = pl.program_id(2)
    lane = _lane_iota((tq, LANES))
    first = lane < HEAD_DIM
    hq = N_HEADS_PER_MIXER * HEAD_DIM
    r0 = pl.multiple_of(i * tq, tq)
    k0 = pl.multiple_of(jnp.maximum(i - 1, 0) * tq, tq)
    row = lax.broadcasted_iota(jnp.int32, (tq, tk), 0)
    colm = lax.broadcasted_iota(jnp.int32, (tq, tk), 1)
    delta = row + (r0 - k0) - colm
    band = jnp.abs(2 * delta - win) <= win
    ones = jnp.ones((tk, LANES), BF16)
    for p in range(N_HEADS_PER_MIXER // 2):
        q2 = qkv_ref[0, 0, pl.ds(r0, tq), p * LANES:(p + 1) * LANES]
        k2 = qkv_ref[0, 0, pl.ds(k0, tk), hq + p * LANES:hq + (p + 1) * LANES]
        v_aug = jnp.concatenate(
            [qkv_ref[0, 0, pl.ds(k0, tk), 2 * hq + p * LANES:2 * hq + (p + 1) * LANES], ones], axis=1)
        zero = jnp.zeros_like(q2)
        outs = []
        lses = []
        for hh in range(2):
            qh = jnp.where(first, q2, zero) if hh == 0 else jnp.where(first, zero, q2)
            s = jnp.where(band, _dot_nt(qh, k2), -jnp.inf)
            m = jnp.max(s, axis=-1, keepdims=True)
            pvx = _dot(jnp.exp(s - m).astype(BF16), v_aug)
            l = pvx[:, LANES:2 * LANES]
            outs.append(pvx[:, 0:LANES] / l)
            lses.append(m + jnp.log(l))
        o_ref[0, :, p * LANES:(p + 1) * LANES] = jnp.where(first, outs[0], outs[1])
        o_ref[0, :, hq + p * LANES:hq + (p + 1) * LANES] = jnp.where(first, lses[0], lses[1])


def _dil_call(qkv, tqs, win):
    bsz, r, n, w = qkv.shape
    hq = N_HEADS_PER_MIXER * HEAD_DIM
    tk = 2 * tqs if n >= 2 * tqs else tqs
    assert tqs >= win and n % tqs == 0 and (tk == 2 * tqs or n == tqs)
    return pl.pallas_call(
        functools.partial(_dil_kernel, tq=tqs, tk=tk, win=win),
        grid=(bsz, r, n // tqs),
        in_specs=[pl.BlockSpec((1, 1, n, w), lambda b, p, c: (b, p, 0, 0))],
        out_specs=pl.BlockSpec((1, tqs, 2 * hq), lambda b, p, c: (b, c, p)),
        out_shape=jax.ShapeDtypeStruct((bsz, n, r * 2 * hq), F32),
        compiler_params=_cparams(("parallel", "parallel", "arbitrary")),
        name=f"dil_r{r}",
    )(qkv)


def _post_kernel(x_ref, osb_ref, ofox_ref, omla_ref, d1_ref, d2_ref, d3_ref, gd_ref, wo_ref,
                 lng_ref, lnb_ref, wr_ref, br_ref, x1_ref, route_ref, stage_ref, *, alpha, dils):
    hq = N_HEADS_PER_MIXER * HEAD_DIM
    tm = x_ref.shape[0]
    h = _dot(osb_ref[...], wo_ref[0:hq, :])
    h += _dot(ofox_ref[...], wo_ref[hq:2 * hq, :])
    h += _dot(omla_ref[...], wo_ref[2 * hq:3 * hq, :])

    def token_rows(g, d_ref, r, c0):
        if r == 1:
            return d_ref[:, c0:c0 + LANES]
        st = stage_ref.at[g, c0 // LANES]
        for q in range(r):
            st[pl.ds(q, tm // r, stride=r), :] = d_ref[:, q * 2 * hq + c0:q * 2 * hq + c0 + LANES]
        return st[...]

    d_refs = (d1_ref, d2_ref, d3_ref)
    for p in range(N_HEADS_PER_MIXER // 2):
        sl = slice(p * LANES, (p + 1) * LANES)
        lses = [token_rows(g, d_refs[g], dils[g], hq + p * LANES) for g in range(len(dils))]
        vals = [token_rows(g, d_refs[g], dils[g], p * LANES) for g in range(len(dils))]
        l1, l2, l3 = lses
        m = jnp.maximum(jnp.maximum(l1, l2), l3)
        e1, e2, e3 = jnp.exp(l1 - m), jnp.exp(l2 - m), jnp.exp(l3 - m)
        inv = 1.0 / (e1 + e2 + e3)
        od = (e1 * inv) * vals[0] + (e2 * inv) * vals[1] + (e3 * inv) * vals[2]
        od = _head_rms(od, gd_ref[:, sl]).astype(BF16)
        h += _dot(od, wo_ref[3 * hq + p * LANES:3 * hq + (p + 1) * LANES, :])
    x1 = _layernorm(alpha * x_ref[...] + h, lng_ref[...], lnb_ref[...])
    x1_ref[...] = x1

    logits = _dot_f32(x1, wr_ref[...]) + br_ref[...]
    lane = _lane_iota(logits.shape)
    lanef = lane.astype(F32)
    big = float(LANES)
    ninf = -jnp.inf
    gl = jnp.where(lane < N_GROUPS, logits, ninf)
    gmax = jnp.max(gl, axis=-1, keepdims=True)
    gsel = jnp.min(jnp.where(gl == gmax, lanef, big), axis=-1, keepdims=True)
    gw = 1.0 / jnp.sum(jnp.exp(gl - gmax), axis=-1, keepdims=True)
    e_lo = N_GROUPS + EXPERTS_PER_GROUP * gsel
    in_grp = (lanef >= e_lo) & (lanef < e_lo + EXPERTS_PER_GROUP)
    el = jnp.where(in_grp, logits, ninf)
    t1 = jnp.max(el, axis=-1, keepdims=True)
    i1 = jnp.min(jnp.where(el == t1, lanef, big), axis=-1, keepdims=True)
    el2 = jnp.where(lanef == i1, ninf, el)
    t2 = jnp.max(el2, axis=-1, keepdims=True)
    i2 = jnp.min(jnp.where(el2 == t2, lanef, big), axis=-1, keepdims=True)
    ex = jnp.exp(t2 - t1)
    den = 1.0 + ex
    g1 = gw / den
    g2 = gw * ex / den
    out = jnp.where(lane == 0, i1 - N_GROUPS,
                    jnp.where(lane == 1, i2 - N_GROUPS,
                              jnp.where(lane == 2, g1, jnp.where(lane == 3, g2, 0.0))))
    route_ref[...] = out


def _post_call(x, osb, ofox, omla, dil_outs, dils, g_dil, wo, lng, lnb, wr, br, alpha, tm):
    t, d = x.shape
    hq = N_HEADS_PER_MIXER * HEAD_DIM
    row = lambda w: pl.BlockSpec((tm, w), lambda i: (i, 0))
    full = lambda a: pl.BlockSpec(a.shape, lambda i: (0, 0))
    dil_specs = [pl.BlockSpec((tm // r, r * 2 * hq), lambda i: (i, 0)) for r in dils]
    return pl.pallas_call(
        functools.partial(_post_kernel, alpha=alpha, dils=dils),
        grid=(t // tm,),
        in_specs=[row(d), row(hq), row(hq), row(hq)] + dil_specs
                 + [full(g_dil), full(wo), full(lng), full(lnb), full(wr), full(br)],
        out_specs=[row(d), row(LANES)],
        out_shape=[jax.ShapeDtypeStruct((t, d), F32), jax.ShapeDtypeStruct((t, LANES), F32)],
        scratch_shapes=[pltpu.VMEM((len(dils), 2 * hq // LANES, tm, LANES), F32)],
        compiler_params=_cparams(("parallel",)),
        name="post_mixer",
    )(x, osb, ofox, omla, *dil_outs, g_dil, wo, lng, lnb, wr, br)


DMA_UNROLL = 8


def _moe_kernel(ce_ref, nv_ref, src_ref, srcn_ref, dst_ref, x_hbm, w1_ref, w3_ref, w2_ref, y_hbm,
                xs_ref, ys_ref, gsem, ssem):
    c = pl.program_id(0)
    nc = pl.num_programs(0)
    slot = c % 2
    other = 1 - slot
    nv = nv_ref[c]
    nv_next = jnp.where(c + 1 < nc, nv_ref[jnp.minimum(c + 1, nc - 1)], 0)
    nv_prev = jnp.where(c > 0, nv_ref[jnp.maximum(c - 1, 0)], 0)

    def gather_copy(idx_ref, i, s):
        return pltpu.make_async_copy(x_hbm.at[pl.ds(idx_ref[0, 0, i], 1), :],
                                     xs_ref.at[s, pl.ds(i, 1), :], gsem.at[s])

    def scatter_copy(i, s):
        return pltpu.make_async_copy(ys_ref.at[s, pl.ds(i, 1), :],
                                     y_hbm.at[pl.ds(dst_ref[0, 0, i], 1), :], ssem.at[s])

    def start_rows(n, make):
        ngrp = n // DMA_UNROLL

        def grp(g, _):
            for k in range(DMA_UNROLL):
                make(g * DMA_UNROLL + k).start()
            return 0

        def one(i, _):
            make(i).start()
            return 0

        lax.fori_loop(0, ngrp, grp, 0)
        lax.fori_loop(ngrp * DMA_UNROLL, n, one, 0)

    def wait_rows(n, make_row, make_block):
        @pl.when(n == MOE_BLOCK)
        def _():
            make_block().wait()

        @pl.when(n < MOE_BLOCK)
        def _():
            def one(i, _):
                make_row(i).wait()
                return 0
            lax.fori_loop(0, n, one, 0)

    def gather_block(s):
        return pltpu.make_async_copy(x_hbm.at[pl.ds(0, MOE_BLOCK), :], xs_ref.at[s], gsem.at[s])

    def scatter_block(s):
        return pltpu.make_async_copy(ys_ref.at[s], y_hbm.at[pl.ds(0, MOE_BLOCK), :], ssem.at[s])

    @pl.when(c == 0)
    def _():
        xs_ref[...] = jnp.zeros_like(xs_ref)
        start_rows(nv, lambda i: gather_copy(src_ref, i, 0))

    start_rows(nv_next, lambda i: gather_copy(srcn_ref, i, other))
    wait_rows(nv, lambda i: gather_copy(src_ref, i, slot), lambda: gather_block(slot))

    @pl.when(nv > 0)
    def _():
        xb = xs_ref[slot].astype(BF16)
        a = _dot(xb, w1_ref[0])
        b = _dot(xb, w3_ref[0])
        hid = (a / (1.0 + jnp.exp(-a)) * b).astype(BF16)
        ys_ref[slot] = _dot(hid, w2_ref[0])

    wait_rows(nv_prev, lambda i: scatter_copy(i, other), lambda: scatter_block(other))
    start_rows(nv, lambda i: scatter_copy(i, slot))

    @pl.when(c == nc - 1)
    def _():
        wait_rows(nv, lambda i: scatter_copy(i, slot), lambda: scatter_block(slot))


def _moe_call(chunk_expert, n_valid, src, dst, x1, w1, w3, w2, n_rows_out):
    n_chunks = chunk_expert.shape[0]
    t, d = x1.shape
    de = w1.shape[-1]
    grid_spec = pltpu.PrefetchScalarGridSpec(
        num_scalar_prefetch=2,
        grid=(n_chunks,),
        in_specs=[
            pl.BlockSpec((1, 1, MOE_BLOCK), lambda c, ce, nv: (c, 0, 0), memory_space=pltpu.SMEM),
            pl.BlockSpec((1, 1, MOE_BLOCK), lambda c, ce, nv: (jnp.minimum(c + 1, n_chunks - 1), 0, 0),
                         memory_space=pltpu.SMEM),
            pl.BlockSpec((1, 1, MOE_BLOCK), lambda c, ce, nv: (c, 0, 0), memory_space=pltpu.SMEM),
            pl.BlockSpec(memory_space=pl.ANY),
            pl.BlockSpec((1, d, de), lambda c, ce, nv: (ce[c], 0, 0)),
            pl.BlockSpec((1, d, de), lambda c, ce, nv: (ce[c], 0, 0)),
            pl.BlockSpec((1, de, d), lambda c, ce, nv: (ce[c], 0, 0)),
        ],
        out_specs=pl.BlockSpec(memory_space=pl.ANY),
        scratch_shapes=[
            pltpu.VMEM((2, MOE_BLOCK, d), F32),
            pltpu.VMEM((2, MOE_BLOCK, d), F32),
            pltpu.SemaphoreType.DMA((2,)),
            pltpu.SemaphoreType.DMA((2,)),
        ],
    )
    return pl.pallas_call(
        _moe_kernel,
        grid_spec=grid_spec,
        out_shape=jax.ShapeDtypeStruct((n_rows_out, d), F32),
        compiler_params=_cparams(("arbitrary",)),
        name="moe_experts",
    )(chunk_expert, n_valid, src, src, dst, x1, w1, w3, w2)


def _combine_kernel(x1_ref, ya_ref, yb_ref, route_ref, lng_ref, lnb_ref, o_ref, *, alpha):
    g1 = route_ref[:, 2:3]
    g2 = route_ref[:, 3:4]
    m = g1 * ya_ref[...] + g2 * yb_ref[...]
    o_ref[...] = _layernorm(alpha * x1_ref[...] + m, lng_ref[...], lnb_ref[...])


def _combine_call(x1, y, route, lng, lnb, alpha, tm):
    t, d = x1.shape
    nt = t // tm
    full = lambda a: pl.BlockSpec(a.shape, lambda i: (0, 0))
    return pl.pallas_call(
        functools.partial(_combine_kernel, alpha=alpha),
        grid=(nt,),
        in_specs=[pl.BlockSpec((tm, d), lambda i: (i, 0)),
                  pl.BlockSpec((tm, d), lambda i: (i, 0)),
                  pl.BlockSpec((tm, d), lambda i: (i + nt, 0)),
                  pl.BlockSpec((tm, LANES), lambda i: (i, 0)),
                  full(lng), full(lnb)],
        out_specs=pl.BlockSpec((tm, d), lambda i: (i, 0)),
        out_shape=jax.ShapeDtypeStruct((t, d), F32),
        compiler_params=_cparams(("parallel",)),
        name="moe_combine",
    )(x1, y, y, route, lng, lnb)


def _rope_tables(seq, dim, lane_lo):
    half = dim // 2
    inv_freq = ROPE_THETA ** (-jnp.arange(half, dtype=F32) / half)
    ang = jnp.arange(seq, dtype=F32)[:, None] * inv_freq[None, :]
    cos = jnp.concatenate([jnp.cos(ang), jnp.cos(ang)], -1)
    sin = jnp.concatenate([jnp.sin(ang), jnp.sin(ang)], -1)
    if lane_lo == 0:
        reps = LANES // dim
        return jnp.tile(cos, (1, reps)), jnp.tile(sin, (1, reps))
    cos_t = jnp.ones((seq, LANES), F32).at[:, lane_lo:lane_lo + dim].set(cos)
    sin_t = jnp.zeros((seq, LANES), F32).at[:, lane_lo:lane_lo + dim].set(sin)
    return cos_t, sin_t


def _dispatch_tables(route, n_tok):
    expert_id = route[:, 0:TOP_K].astype(jnp.int32).reshape(-1)
    n_assign = n_tok * TOP_K
    n_slots = n_assign + N_EXPERTS * MOE_BLOCK
    n_chunks = n_slots // MOE_BLOCK
    onehot = (expert_id[:, None] == jnp.arange(N_EXPERTS, dtype=jnp.int32)[None, :]).astype(jnp.int32)
    ranks = jnp.cumsum(onehot, axis=0) - onehot
    rank = jnp.sum(ranks * onehot, axis=1)
    counts = jnp.sum(onehot, axis=0)
    padded = (counts + MOE_BLOCK - 1) // MOE_BLOCK * MOE_BLOCK
    pad_end = jnp.cumsum(padded)
    pad_start = pad_end - padded
    dest = pad_start[expert_id] + rank
    assign = jnp.arange(n_assign, dtype=jnp.int32)
    slot_assign = jnp.zeros((n_slots,), jnp.int32).at[dest].set(assign)
    src = slot_assign // TOP_K
    dst = (slot_assign % TOP_K) * n_tok + src
    chunk_start = jnp.arange(n_chunks, dtype=jnp.int32) * MOE_BLOCK
    chunk_expert = jnp.minimum(jnp.searchsorted(pad_end, chunk_start, side="right"),
                               N_EXPERTS - 1).astype(jnp.int32)
    n_valid = jnp.clip(pad_start[chunk_expert] + counts[chunk_expert] - chunk_start,
                       0, MOE_BLOCK).astype(jnp.int32)
    return (chunk_expert, n_valid, src.reshape(n_chunks, 1, MOE_BLOCK),
            dst.reshape(n_chunks, 1, MOE_BLOCK), n_assign)


def _pick_tile(n, pref):
    t = pref
    while n % t:
        t //= 2
    return t


def kernel(x, w_in, b_forget, g_cq, g_ckv, w_uq, w_ukv, g_head, w_out, ln1_g, ln1_b,
           w_group, b_group, w_expert, b_expert, w1, w3, w2, ln2_g, ln2_b):
    bsz, seq, d = x.shape
    depth = w_in.shape[0]
    t = bsz * seq
    alpha = (2.0 * depth) ** 0.25
    hq = N_HEADS_PER_MIXER * HEAD_DIM
    qk_scale = HEAD_DIM ** -0.5
    mla_scale = (MLA_NOPE + MLA_ROPE) ** -0.5
    win = DIL_BRANCHES[0][0]
    assert all(w // r == win for w, r in DIL_BRANCHES)
    assert seq % (DIL_BRANCHES[-1][1] * win) == 0 and d % LANES == 0

    cos64, sin64 = _rope_tables(seq, HEAD_DIM, 0)
    cos_m, sin_m = _rope_tables(seq, MLA_ROPE, MLA_NOPE)
    tq = _pick_tile(seq, 256)
    idx = jnp.arange(tq)
    u_sb = (idx[:, None] > idx[None, :]).astype(BF16)
    tk_sm = _pick_tile(seq, 4 * tq)

    for l in range(depth):
        wl = w_in[l]
        o_fox, o_mla, o_dil = N_SB, N_SB + N_FOX_QKV + N_HEADS_PER_MIXER, N_SB + N_FOX_QKV + N_HEADS_PER_MIXER + N_MLA
        qs = lambda w: w.at[:, 0:hq].multiply(qk_scale)
        w_sb = qs(wl[:, 0:N_SB])
        w_fx = qs(wl[:, o_fox:o_fox + N_FOX_QKV])
        w_f = wl[:, o_fox + N_FOX_QKV:o_mla]
        w_ml = wl[:, o_mla:o_dil]
        wd = wl[:, o_dil:].reshape(d, 3, len(DIL_BRANCHES), hq)
        w_br = [qs(jnp.concatenate([wd[:, 0, g], wd[:, 1, g], wd[:, 2, g]], axis=1)) for g in range(len(DIL_BRANCHES))]
        w_misc = jnp.concatenate([w_ml, w_f, jnp.zeros((d, MISC_W - N_MLA - N_HEADS_PER_MIXER), F32)], axis=1)
        w_tok = jnp.concatenate([w_sb, w_fx, w_misc] + w_br, axis=1).astype(BF16)

        wq = jnp.pad(w_uq[l].reshape(MLA_Q_LORA, N_HEADS_PER_MIXER, MLA_NOPE + MLA_ROPE),
                     ((0, 0), (0, 0), (0, LANES - MLA_NOPE - MLA_ROPE))).reshape(MLA_Q_LORA, -1).astype(BF16)
        wkv = w_ukv[l].reshape(MLA_KV_LORA, N_HEADS_PER_MIXER, MLA_NOPE + HEAD_DIM)
        wk = jnp.pad(wkv[:, :, :MLA_NOPE], ((0, 0), (0, 0), (0, LANES - MLA_NOPE))).reshape(MLA_KV_LORA, -1).astype(BF16)
        wv = wkv[:, :, MLA_NOPE:].reshape(MLA_KV_LORA, -1).astype(BF16)
        g_flat = g_head[l].reshape(1, -1)
        wr = jnp.concatenate([w_group[l], w_expert[l],
                              jnp.zeros((d, LANES - N_GROUPS - N_EXPERTS), F32)], axis=1)
        br = jnp.concatenate([b_group[l], b_expert[l],
                              jnp.zeros((LANES - N_GROUPS - N_EXPERTS,), F32)]).reshape(1, LANES)

        tm = _pick_tile(seq, 512)
        sb, fx, misc, *qkv_br = _proj_call(
            x, w_tok, cos64, sin64,
            ((N_SB, 0, BF16, 1), (N_FOX_QKV, 0, BF16, 1), (MISC_W, 0, F32, 1))
            + tuple((N_BRANCH, 2 * hq, BF16, r) for _, r in DIL_BRANCHES), tm)
        sb = sb.reshape(t, N_SB)
        fx = fx.reshape(t, N_FOX_QKV)
        misc = misc.reshape(t, MISC_W)

        o_sb = _attn_call("sb", sb, sb, sb, 0, 1, 2, g_flat, 0, bsz, seq, tq, tk_sm, extra=(u_sb,))
        neg_c = _fox_c_call(misc, b_forget[l], bsz, seq)
        o_fx = _attn_call("fox", fx, fx, fx, 0, 1, 2, g_flat, 1, bsz, seq, tq, tk_sm, extra=(neg_c,))
        mq, mk, mv = _mla_prep_call(misc, g_cq[l].reshape(1, -1), g_ckv[l].reshape(1, -1), wq, wk, wv,
                                    cos_m, sin_m, seq, tm)
        o_ml = _attn_call("mla", mq, mk, mv, 0, 0, 0, g_flat, 2, bsz, seq, tq, tk_sm, scale=mla_scale)

        dil = []
        for g, (_, r) in enumerate(DIL_BRANCHES):
            n = seq // r
            og = _dil_call(qkv_br[g], max(win, min(tq, n // 2)), win)
            dil.append(og.reshape(t // r, r * 2 * hq))

        x1, route = _post_call(
            x.reshape(t, d), o_sb, o_fx, o_ml, dil, tuple(r for _, r in DIL_BRANCHES),
            g_flat[:, 3 * hq:], w_out[l].astype(BF16),
            ln1_g[l].reshape(1, d), ln1_b[l].reshape(1, d), wr, br, alpha, _pick_tile(t, 256))

        chunk_expert, n_valid, src, dst, n_rows = _dispatch_tables(route, t)
        y = _moe_call(chunk_expert, n_valid, src, dst, x1,
                      w1[l].astype(BF16), w3[l].astype(BF16), w2[l].astype(BF16), n_rows)
        x = _combine_call(x1, y, route,
                          ln2_g[l].reshape(1, d), ln2_b[l].reshape(1, d), alpha, _pick_tile(t, 256)).reshape(bsz, seq, d)
    return x
```

```python
import functools

import jax
import jax.numpy as jnp
import numpy as np
from jax import lax
from jax.experimental import pallas as pl
from jax.experimental.pallas import tpu as pltpu

F32 = jnp.float32
BF16 = jnp.bfloat16

HEAD_DIM = 64
N_HEADS_PER_MIXER = 4
MLA_Q_LORA = 256
MLA_KV_LORA = 128
MLA_NOPE = 64
MLA_ROPE = 32
DIL_BRANCHES = ((128, 1), (512, 4), (2048, 16))
ROPE_THETA = 10000.0
N_GROUPS = 4
EXPERTS_PER_GROUP = 4
N_EXPERTS = N_GROUPS * EXPERTS_PER_GROUP
TOP_K = 2
MOE_BLOCK = 256
LN_EPS = 1e-5
RMS_EPS = 1e-6

LANES = 128
VMEM_LIMIT_BYTES = 56 * 1024 * 1024

N_SB = 3 * N_HEADS_PER_MIXER * HEAD_DIM
N_FOX_QKV = 3 * N_HEADS_PER_MIXER * HEAD_DIM
N_MLA = MLA_Q_LORA + MLA_KV_LORA + MLA_ROPE
N_BRANCH = 3 * N_HEADS_PER_MIXER * HEAD_DIM
MISC_W = 512
F_COL = N_MLA


def _cparams(sem):
    return pltpu.CompilerParams(dimension_semantics=sem, vmem_limit_bytes=VMEM_LIMIT_BYTES)


def _split3(a):
    hi = a.astype(BF16)
    r1 = a - hi.astype(F32)
    mid = r1.astype(BF16)
    lo = (r1 - mid.astype(F32)).astype(BF16)
    return hi, mid, lo


def _dot(a, b):
    return jnp.dot(a, b, preferred_element_type=F32)


def _dot_nt(a, b):
    return lax.dot_general(a, b, (((1,), (1,)), ((), ())), preferred_element_type=F32)


def _dot_exact_rhs(a, u):
    hi, mid, lo = _split3(a)
    return _dot(hi, u) + _dot(mid, u) + _dot(lo, u)


def _dot_f32(a, b):
    ah = a.astype(BF16)
    al = (a - ah.astype(F32)).astype(BF16)
    bh = b.astype(BF16)
    bl = (b - bh.astype(F32)).astype(BF16)
    return _dot(ah, bh) + (_dot(ah, bl) + _dot(al, bh))


def _lane_iota(shape):
    return lax.broadcasted_iota(jnp.int32, shape, len(shape) - 1)


def _rotate_half(y, half):
    lane = _lane_iota(y.shape)
    fwd = pltpu.roll(y, half, 1)
    bwd = pltpu.roll(y, LANES - half, 1)
    return jnp.where((lane % (2 * half)) < half, -bwd, fwd)


def _log_sigmoid_pair(z):
    sp = jnp.log(1.0 + jnp.exp(-jnp.abs(z)))
    return jnp.minimum(z, 0.0) - sp, -jnp.maximum(z, 0.0) - sp


def _head_rms(o, g):
    lane = _lane_iota(o.shape)
    first = lane < HEAD_DIM
    sq = o * o
    ss_a = jnp.sum(jnp.where(first, sq, 0.0), axis=-1, keepdims=True)
    ss_b = jnp.sum(jnp.where(first, 0.0, sq), axis=-1, keepdims=True)
    ms = jnp.where(first, ss_a, ss_b) * (1.0 / HEAD_DIM)
    return o * lax.rsqrt(ms + RMS_EPS) * g


def _layernorm(y, g, b):
    mu = jnp.mean(y, axis=-1, keepdims=True)
    d = y - mu
    var = jnp.mean(d * d, axis=-1, keepdims=True)
    return d * lax.rsqrt(var + LN_EPS) * g + b


def _proj_kernel(x_ref, w_ref, cos_ref, sin_ref, *refs, outs):
    out_refs, stage_ref = refs[:len(outs)], refs[len(outs)]
    tm = x_ref.shape[1]
    xb = x_ref[0].astype(BF16)
    col = 0
    slab = 0
    for o_ref, (width, n_rope, _, r) in zip(out_refs, outs):
        for c in range(0, width, 2 * LANES):
            cw = min(2 * LANES, width - c)
            y = _dot(xb, w_ref[:, col + c:col + c + cw])
            for s in range(0, cw, LANES):
                ys = y[:, s:s + LANES]
                if c + s < n_rope:
                    ys = ys * cos_ref[...] + _rotate_half(ys, HEAD_DIM // 2) * sin_ref[...]
                if r == 1:
                    o_ref[0, 0, :, c + s:c + s + LANES] = ys.astype(o_ref.dtype)
                else:
                    st = stage_ref.at[slab % stage_ref.shape[0]]
                    slab += 1
                    st[...] = ys
                    for p in range(r):
                        o_ref[0, p, :, c + s:c + s + LANES] = (
                            st[pl.ds(p, tm // r, stride=r), :].astype(o_ref.dtype))
        col += width


PROJ_STAGE_SLABS = 4


def _proj_call(x, w, cos, sin, outs, tm):
    bsz, seq, d = x.shape
    nt = seq // tm
    in_specs = [
        pl.BlockSpec((1, tm, d), lambda b, i: (b, i, 0)),
        pl.BlockSpec(w.shape, lambda b, i: (0, 0)),
        pl.BlockSpec((tm, LANES), lambda b, i: (i, 0)),
        pl.BlockSpec((tm, LANES), lambda b, i: (i, 0)),
    ]
    out_specs = [pl.BlockSpec((1, r, tm // r, wd), lambda b, i: (b, 0, i, 0)) for wd, _, _, r in outs]
    out_shape = [jax.ShapeDtypeStruct((bsz, r, seq // r, wd), dt) for wd, _, dt, r in outs]
    return pl.pallas_call(
        functools.partial(_proj_kernel, outs=outs),
        grid=(bsz, nt), in_specs=in_specs, out_specs=out_specs, out_shape=out_shape,
        scratch_shapes=[pltpu.VMEM((PROJ_STAGE_SLABS, tm, LANES), F32)],
        compiler_params=_cparams(("parallel", "parallel")),
        name="proj",
    )(x, w, cos, sin)


def _fox_c_kernel(misc_ref, bias_ref, uinc_ref, ones_ref, out_ref, *, seq):
    lane0 = F_COL - 3 * LANES
    nblk = seq // LANES

    def body(j, carry):
        r0 = pl.multiple_of(j * LANES, LANES)
        f = misc_ref[pl.ds(r0, LANES), :] + bias_ref[...]
        lf, _ = _log_sigmoid_pair(f)
        lft = lf.T
        csum = _dot_exact_rhs(lft, uinc_ref[...]) + carry
        tot = _dot_exact_rhs(lft, ones_ref[...])
        out_ref[0, :, pl.ds(r0, LANES)] = -csum[lane0:lane0 + N_HEADS_PER_MIXER, :]
        return carry + tot

    lax.fori_loop(0, nblk, body, jnp.zeros((LANES, LANES), F32))


def _fox_c_call(misc, b_forget, bsz, seq):
    bias = jnp.zeros((1, LANES), F32).at[0, F_COL - 3 * LANES:F_COL - 3 * LANES + N_HEADS_PER_MIXER].set(b_forget)
    idx = jnp.arange(LANES)
    uinc = (idx[:, None] <= idx[None, :]).astype(BF16)
    ones = jnp.ones((LANES, LANES), BF16)
    return pl.pallas_call(
        functools.partial(_fox_c_kernel, seq=seq),
        grid=(bsz,),
        in_specs=[
            pl.BlockSpec((seq, LANES), lambda b: (b, 3)),
            pl.BlockSpec((1, LANES), lambda b: (0, 0)),
            pl.BlockSpec((LANES, LANES), lambda b: (0, 0)),
            pl.BlockSpec((LANES, LANES), lambda b: (0, 0)),
        ],
        out_specs=pl.BlockSpec((1, N_HEADS_PER_MIXER, seq), lambda b: (b, 0, 0)),
        out_shape=jax.ShapeDtypeStruct((bsz, N_HEADS_PER_MIXER, seq), F32),
        compiler_params=_cparams(("parallel",)),
        name="fox_c",
    )(misc, bias, uinc, ones)


def _mla_prep_kernel(misc_ref, gq_ref, gkv_ref, wq_ref, wk_ref, wv_ref, cos_ref, sin_ref,
                     q_ref, k_ref, v_ref):
    def rms(x, g):
        return x * lax.rsqrt(jnp.mean(x * x, axis=-1, keepdims=True) + RMS_EPS) * g

    cq = rms(misc_ref[:, 0:MLA_Q_LORA], gq_ref[...]).astype(BF16)
    ckv = rms(misc_ref[:, MLA_Q_LORA:MLA_Q_LORA + MLA_KV_LORA], gkv_ref[...]).astype(BF16)
    kr_blk = misc_ref[:, 3 * LANES:4 * LANES]
    lane = _lane_iota(kr_blk.shape)
    in_rope = (lane >= MLA_NOPE) & (lane < MLA_NOPE + MLA_ROPE)
    kr = jnp.where(in_rope, pltpu.roll(kr_blk, MLA_NOPE, 1), 0.0)
    cos = cos_ref[...]
    sin = sin_ref[...]

    def rope(y):
        return y * cos + _rotate_half(y, MLA_ROPE // 2) * sin

    q = _dot(cq, wq_ref[...])
    k = _dot(ckv, wk_ref[...])
    for h in range(N_HEADS_PER_MIXER):
        sl = slice(h * LANES, (h + 1) * LANES)
        q_ref[:, sl] = rope(q[:, sl]).astype(BF16)
        k_ref[:, sl] = rope(k[:, sl] + kr).astype(BF16)
    v_ref[...] = _dot(ckv, wv_ref[...]).astype(BF16)


def _mla_prep_call(misc, g_cq, g_ckv, wq, wk, wv, cos, sin, seq, tm):
    t = misc.shape[0]
    nper = seq // tm
    hw = N_HEADS_PER_MIXER * LANES
    vw = N_HEADS_PER_MIXER * HEAD_DIM
    full = lambda a: pl.BlockSpec(a.shape, lambda i: (0, 0))
    return pl.pallas_call(
        _mla_prep_kernel,
        grid=(t // tm,),
        in_specs=[
            pl.BlockSpec((tm, MISC_W), lambda i: (i, 0)),
            full(g_cq), full(g_ckv), full(wq), full(wk), full(wv),
            pl.BlockSpec((tm, LANES), lambda i: (i % nper, 0)),
            pl.BlockSpec((tm, LANES), lambda i: (i % nper, 0)),
        ],
        out_specs=[
            pl.BlockSpec((tm, hw), lambda i: (i, 0)),
            pl.BlockSpec((tm, hw), lambda i: (i, 0)),
            pl.BlockSpec((tm, vw), lambda i: (i, 0)),
        ],
        out_shape=[
            jax.ShapeDtypeStruct((t, hw), BF16),
            jax.ShapeDtypeStruct((t, hw), BF16),
            jax.ShapeDtypeStruct((t, vw), BF16),
        ],
        compiler_params=_cparams(("parallel",)),
        name="mla_prep",
    )(misc, g_cq, g_ckv, wq, wk, wv, cos, sin)


def _attn_kernel(*refs, mode, tq, tk, scale):
    if mode == "sb":
        q_ref, k_ref, v_ref, g_ref, u_ref, o_ref = refs
    elif mode == "fox":
        q_ref, k_ref, v_ref, g_ref, nc_ref, o_ref = refs
    else:
        q_ref, k_ref, v_ref, g_ref, o_ref = refs
    nh = N_HEADS_PER_MIXER
    i = pl.program_id(1)
    lane = _lane_iota((tq, LANES))
    first = lane < HEAD_DIM
    row = lax.broadcasted_iota(jnp.int32, (tq, tk), 0)
    colm = lax.broadcasted_iota(jnp.int32, (tq, tk), 1)

    q_heads = []
    for h in range(nh):
        if mode == "mla":
            q_heads.append(q_ref[:, h * LANES:(h + 1) * LANES])
        else:
            q2 = q_ref[:, (h // 2) * LANES:(h // 2 + 1) * LANES]
            zero = jnp.zeros_like(q2)
            q_heads.append(jnp.where(first, q2, zero) if h % 2 == 0 else jnp.where(first, zero, q2))

    def k_head(j, h):
        r0 = pl.multiple_of(j * tk, tk)
        kb = h if mode == "mla" else h // 2
        return k_ref[pl.ds(r0, tk), kb * LANES:(kb + 1) * LANES]

    def v_pair(j, p):
        r0 = pl.multiple_of(j * tk, tk)
        return v_ref[pl.ds(r0, tk), p * LANES:(p + 1) * LANES]

    jd = (i * tq) // tk
    off = i * tq - jd * tk

    if mode == "sb":
        tu = u_ref.shape[0]

        def step(j, carry, diag):
            accs, rs = carry
            new_accs, new_rs = [], []
            for p in range(nh // 2):
                v2 = v_pair(j, p)
                outs = []
                for h in (2 * p, 2 * p + 1):
                    z = _dot_nt(q_heads[h], k_head(j, h))
                    ls_pos, ls_neg = _log_sigmoid_pair(z)
                    if diag:
                        before = colm < row + off
                        ls_neg = jnp.where(before, ls_neg, 0.0)
                    lb = ls_neg.astype(BF16)
                    pieces = []
                    later = rs[h]
                    for c0 in range(tk - tu, -1, -tu):
                        c = _dot(lb[:, c0:c0 + tu], u_ref[...]) + later
                        pieces.insert(0, c)
                        later = c[:, 0:1] + ls_neg[:, c0:c0 + 1]
                    w = jnp.exp(ls_pos + jnp.concatenate(pieces, axis=1))
                    if diag:
                        w = jnp.where(before, w, 0.0)
                    outs.append(_dot(w.astype(BF16), v2))
                    new_rs.append(later)
                new_accs.append(accs[p] + jnp.where(first, outs[0], outs[1]))
            return tuple(new_accs), tuple(new_rs)

        zacc = jnp.zeros((tq, LANES), F32)
        zr = jnp.zeros((tq, 1), F32)
        carry = step(jd, ((zacc,) * (nh // 2), (zr,) * nh), True)
        carry = lax.fori_loop(0, jd, lambda n, c: step(jd - 1 - n, c, False), carry)
        outs = carry[0]
    else:
        ones = jnp.ones((tk, LANES), BF16)

        def step(j, carry, diag):
            accs, ms, ls = carry
            new_accs, new_ms, new_ls = [], [], []
            r0 = pl.multiple_of(j * tk, tk)
            for p in range(nh // 2):
                v_aug = jnp.concatenate([v_pair(j, p), ones], axis=1)
                pv, alphas = [], []
                for h in (2 * p, 2 * p + 1):
                    s = _dot_nt(q_heads[h], k_head(j, h))
                    if mode == "mla":
                        s = s * scale
                    else:
                        s = s + nc_ref[0, h:h + 1, pl.ds(r0, tk)]
                    if diag:
                        s = jnp.where(colm <= row + off, s, -jnp.inf)
                    m_new = jnp.maximum(ms[h], jnp.max(s, axis=-1, keepdims=True))
                    alpha = jnp.exp(ms[h] - m_new)
                    pvx = _dot(jnp.exp(s - m_new).astype(BF16), v_aug)
                    new_ls.append(alpha * ls[h] + pvx[:, LANES:2 * LANES])
                    new_ms.append(m_new)
                    pv.append(pvx[:, 0:LANES])
                    alphas.append(alpha)
                new_accs.append(accs[p] * jnp.where(first, alphas[0], alphas[1])
                                + jnp.where(first, pv[0], pv[1]))
            return tuple(new_accs), tuple(new_ms), tuple(new_ls)

        neg = jnp.full((tq, 1), -jnp.inf, F32)
        zacc = jnp.zeros((tq, LANES), F32)
        carry = step(jd, ((zacc,) * (nh // 2), (neg,) * nh, (zacc,) * nh), True)
        carry = lax.fori_loop(0, jd, lambda n, c: step(n, c, False), carry)
        accs, _, ls = carry
        outs = [accs[p] / jnp.where(first, ls[2 * p], ls[2 * p + 1]) for p in range(nh // 2)]
    for p in range(nh // 2):
        sl = slice(p * LANES, (p + 1) * LANES)
        o_ref[:, sl] = _head_rms(outs[p], g_ref[:, sl]).astype(o_ref.dtype)


def _attn_call(mode, q, k, v, qcol, kcol, vcol, g_flat, gcol, bsz, seq, tq, tk, extra=(), scale=1.0):
    t = bsz * seq
    nq = seq // tq
    hq = N_HEADS_PER_MIXER * HEAD_DIM
    qw = N_HEADS_PER_MIXER * LANES if mode == "mla" else hq
    assert tk % tq == 0 and seq % tk == 0
    in_specs = [
        pl.BlockSpec((tq, qw), lambda b, i: (b * nq + i, qcol)),
        pl.BlockSpec((seq, qw), lambda b, i: (b, kcol)),
        pl.BlockSpec((seq, hq), lambda b, i: (b, vcol)),
        pl.BlockSpec((1, hq), lambda b, i: (0, gcol)),
    ]
    args = [q, k, v, g_flat]
    if mode == "sb":
        (u,) = extra
        in_specs.append(pl.BlockSpec(u.shape, lambda b, i: (0, 0)))
        args.append(u)
    elif mode == "fox":
        (nc,) = extra
        in_specs.append(pl.BlockSpec((1, N_HEADS_PER_MIXER, seq), lambda b, i: (b, 0, 0)))
        args.append(nc)
    return pl.pallas_call(
        functools.partial(_attn_kernel, mode=mode, tq=tq, tk=tk, scale=scale),
        grid=(bsz, nq),
        in_specs=in_specs,
        out_specs=pl.BlockSpec((tq, hq), lambda b, i: (b * nq + i, 0)),
        out_shape=jax.ShapeDtypeStruct((t, hq), BF16),
        compiler_params=_cparams(("parallel", "arbitrary")),
        name=f"attn_{mode}",
    )(*args)


def _dil_kernel(qkv_ref, o_ref, *, tq, tk, win):
    i = pl.program_id(2)
    lane = _lane_iota((tq, LANES))
    first = lane < HEAD_DIM
    hq = N_HEADS_PER_MIXER * HEAD_DIM
    r0 = pl.multiple_of(i * tq, tq)
    k0 = pl.multiple_of(jnp.maximum(i - 1, 0) * tq, tq)
    row = lax.broadcasted_iota(jnp.int32, (tq, tk), 0)
    colm = lax.broadcasted_iota(jnp.int32, (tq, tk), 1)
    delta = row + (r0 - k0) - colm
    band = jnp.abs(2 * delta - win) <= win
    ones = jnp.ones((tk, LANES), BF16)
    for p in range(N_HEADS_PER_MIXER // 2):
        q2 = qkv_ref[0, 0, pl.ds(r0, tq), p * LANES:(p + 1) * LANES]
        k2 = qkv_ref[0, 0, pl.ds(k0, tk), hq + p * LANES:hq + (p + 1) * LANES]
        v_aug = jnp.concatenate(
            [qkv_ref[0, 0, pl.ds(k0, tk), 2 * hq + p * LANES:2 * hq + (p + 1) * LANES], ones], axis=1)
        zero = jnp.zeros_like(q2)
        outs = []
        lses = []
        for hh in range(2):
            qh = jnp.where(first, q2, zero) if hh == 0 else jnp.where(first, zero, q2)
            s = jnp.where(band, _dot_nt(qh, k2), -jnp.inf)
            m = jnp.max(s, axis=-1, keepdims=True)
            pvx = _dot(jnp.exp(s - m).astype(BF16), v_aug)
            l = pvx[:, LANES:2 * LANES]
            outs.append(pvx[:, 0:LANES] / l)
            lses.append(m + jnp.log(l))
        o_ref[0, :, p * LANES:(p + 1) * LANES] = jnp.where(first, outs[0], outs[1])
        o_ref[0, :, hq + p * LANES:hq + (p + 1) * LANES] = jnp.where(first, lses[0], lses[1])


def _dil_call(qkv, tqs, win):
    bsz, r, n, w = qkv.shape
    hq = N_HEADS_PER_MIXER * HEAD_DIM
    tk = 2 * tqs if n >= 2 * tqs else tqs
    assert tqs >= win and n % tqs == 0 and (tk == 2 * tqs or n == tqs)
    return pl.pallas_call(
        functools.partial(_dil_kernel, tq=tqs, tk=tk, win=win),
        grid=(bsz, r, n // tqs),
        in_specs=[pl.BlockSpec((1, 1, n, w), lambda b, p, c: (b, p, 0, 0))],
        out_specs=pl.BlockSpec((1, tqs, 2 * hq), lambda b, p, c: (b, c, p)),
        out_shape=jax.ShapeDtypeStruct((bsz, n, r * 2 * hq), F32),
        compiler_params=_cparams(("parallel", "parallel", "arbitrary")),
        name=f"dil_r{r}",
    )(qkv)


def _post_kernel(x_ref, osb_ref, ofox_ref, omla_ref, d1_ref, d2_ref, d3_ref, gd_ref, wo_ref,
                 lng_ref, lnb_ref, wr_ref, br_ref, x1_ref, route_ref, stage_ref, *, alpha, dils):
    hq = N_HEADS_PER_MIXER * HEAD_DIM
    tm = x_ref.shape[0]
    h = _dot(osb_ref[...], wo_ref[0:hq, :])
    h += _dot(ofox_ref[...], wo_ref[hq:2 * hq, :])
    h += _dot(omla_ref[...], wo_ref[2 * hq:3 * hq, :])

    def token_rows(g, d_ref, r, c0):
        if r == 1:
            return d_ref[:, c0:c0 + LANES]
        st = stage_ref.at[g, c0 // LANES]
        for q in range(r):
            st[pl.ds(q, tm // r, stride=r), :] = d_ref[:, q * 2 * hq + c0:q * 2 * hq + c0 + LANES]
        return st[...]

    d_refs = (d1_ref, d2_ref, d3_ref)
    for p in range(N_HEADS_PER_MIXER // 2):
        sl = slice(p * LANES, (p + 1) * LANES)
        lses = [token_rows(g, d_refs[g], dils[g], hq + p * LANES) for g in range(len(dils))]
        vals = [token_rows(g, d_refs[g], dils[g], p * LANES) for g in range(len(dils))]
        l1, l2, l3 = lses
        m = jnp.maximum(jnp.maximum(l1, l2), l3)
        e1, e2, e3 = jnp.exp(l1 - m), jnp.exp(l2 - m), jnp.exp(l3 - m)
        inv = 1.0 / (e1 + e2 + e3)
        od = (e1 * inv) * vals[0] + (e2 * inv) * vals[1] + (e3 * inv) * vals[2]
        od = _head_rms(od, gd_ref[:, sl]).astype(BF16)
        h += _dot(od, wo_ref[3 * hq + p * LANES:3 * hq + (p + 1) * LANES, :])
    x1 = _layernorm(alpha * x_ref[...] + h, lng_ref[...], lnb_ref[...])
    x1_ref[...] = x1

    logits = _dot_f32(x1, wr_ref[...]) + br_ref[...]
    lane = _lane_iota(logits.shape)
    lanef = lane.astype(F32)
    big = float(LANES)
    ninf = -jnp.inf
    gl = jnp.where(lane < N_GROUPS, logits, ninf)
    gmax = jnp.max(gl, axis=-1, keepdims=True)
    gsel = jnp.min(jnp.where(gl == gmax, lanef, big), axis=-1, keepdims=True)
    gw = 1.0 / jnp.sum(jnp.exp(gl - gmax), axis=-1, keepdims=True)
    e_lo = N_GROUPS + EXPERTS_PER_GROUP * gsel
    in_grp = (lanef >= e_lo) & (lanef < e_lo + EXPERTS_PER_GROUP)
    el = jnp.where(in_grp, logits, ninf)
    t1 = jnp.max(el, axis=-1, keepdims=True)
    i1 = jnp.min(jnp.where(el == t1, lanef, big), axis=-1, keepdims=True)
    el2 = jnp.where(lanef == i1, ninf, el)
    t2 = jnp.max(el2, axis=-1, keepdims=True)
    i2 = jnp.min(jnp.where(el2 == t2, lanef, big), axis=-1, keepdims=True)
    ex = jnp.exp(t2 - t1)
    den = 1.0 + ex
    g1 = gw / den
    g2 = gw * ex / den
    out = jnp.where(lane == 0, i1 - N_GROUPS,
                    jnp.where(lane == 1, i2 - N_GROUPS,
                              jnp.where(lane == 2, g1, jnp.where(lane == 3, g2, 0.0))))
    route_ref[...] = out


def _post_call(x, osb, ofox, omla, dil_outs, dils, g_dil, wo, lng, lnb, wr, br, alpha, tm):
    t, d = x.shape
    hq = N_HEADS_PER_MIXER * HEAD_DIM
    row = lambda w: pl.BlockSpec((tm, w), lambda i: (i, 0))
    full = lambda a: pl.BlockSpec(a.shape, lambda i: (0, 0))
    dil_specs = [pl.BlockSpec((tm // r, r * 2 * hq), lambda i: (i, 0)) for r in dils]
    return pl.pallas_call(
        functools.partial(_post_kernel, alpha=alpha, dils=dils),
        grid=(t // tm,),
        in_specs=[row(d), row(hq), row(hq), row(hq)] + dil_specs
                 + [full(g_dil), full(wo), full(lng), full(lnb), full(wr), full(br)],
        out_specs=[row(d), row(LANES)],
        out_shape=[jax.ShapeDtypeStruct((t, d), F32), jax.ShapeDtypeStruct((t, LANES), F32)],
        scratch_shapes=[pltpu.VMEM((len(dils), 2 * hq // LANES, tm, LANES), F32)],
        compiler_params=_cparams(("parallel",)),
        name="post_mixer",
    )(x, osb, ofox, omla, *dil_outs, g_dil, wo, lng, lnb, wr, br)


DMA_UNROLL = 8


def _moe_kernel(ce_ref, nv_ref, src_ref, srcn_ref, dst_ref, x_hbm, w1_ref, w3_ref, w2_ref, y_hbm,
                xs_ref, ys_ref, gsem, ssem):
    c = pl.program_id(0)
    nc = pl.num_programs(0)
    slot = c % 2
    other = 1 - slot
    nv = nv_ref[c]
    nv_next = jnp.where(c + 1 < nc, nv_ref[jnp.minimum(c + 1, nc - 1)], 0)
    nv_prev = jnp.where(c > 0, nv_ref[jnp.maximum(c - 1, 0)], 0)

    def gather_copy(idx_ref, i, s):
        return pltpu.make_async_copy(x_hbm.at[pl.ds(idx_ref[0, 0, i], 1), :],
                                     xs_ref.at[s, pl.ds(i, 1), :], gsem.at[s])

    def scatter_copy(i, s):
        return pltpu.make_async_copy(ys_ref.at[s, pl.ds(i, 1), :],
                                     y_hbm.at[pl.ds(dst_ref[0, 0, i], 1), :], ssem.at[s])

    def start_rows(n, make):
        ngrp = n // DMA_UNROLL

        def grp(g, _):
            for k in range(DMA_UNROLL):
                make(g * DMA_UNROLL + k).start()
            return 0

        def one(i, _):
            make(i).start()
            return 0

        lax.fori_loop(0, ngrp, grp, 0)
        lax.fori_loop(ngrp * DMA_UNROLL, n, one, 0)

    def wait_rows(n, make_row, make_block):
        @pl.when(n == MOE_BLOCK)
        def _():
            make_block().wait()

        @pl.when(n < MOE_BLOCK)
        def _():
            def one(i, _):
                make_row(i).wait()
                return 0
            lax.fori_loop(0, n, one, 0)

    def gather_block(s):
        return pltpu.make_async_copy(x_hbm.at[pl.ds(0, MOE_BLOCK), :], xs_ref.at[s], gsem.at[s])

    def scatter_block(s):
        return pltpu.make_async_copy(ys_ref.at[s], y_hbm.at[pl.ds(0, MOE_BLOCK), :], ssem.at[s])

    @pl.when(c == 0)
    def _():
        xs_ref[...] = jnp.zeros_like(xs_ref)
        start_rows(nv, lambda i: gather_copy(src_ref, i, 0))

    start_rows(nv_next, lambda i: gather_copy(srcn_ref, i, other))
    wait_rows(nv, lambda i: gather_copy(src_ref, i, slot), lambda: gather_block(slot))

    @pl.when(nv > 0)
    def _():
        xb = xs_ref[slot].astype(BF16)
        a = _dot(xb, w1_ref[0])
        b = _dot(xb, w3_ref[0])
        hid = (a / (1.0 + jnp.exp(-a)) * b).astype(BF16)
        ys_ref[slot] = _dot(hid, w2_ref[0])

    wait_rows(nv_prev, lambda i: scatter_copy(i, other), lambda: scatter_block(other))
    start_rows(nv, lambda i: scatter_copy(i, slot))

    @pl.when(c == nc - 1)
    def _():
        wait_rows(nv, lambda i: scatter_copy(i, slot), lambda: scatter_block(slot))


def _moe_call(chunk_expert, n_valid, src, dst, x1, w1, w3, w2, n_rows_out):
    n_chunks = chunk_expert.shape[0]
    t, d = x1.shape
    de = w1.shape[-1]
    grid_spec = pltpu.PrefetchScalarGridSpec(
        num_scalar_prefetch=2,
        grid=(n_chunks,),
        in_specs=[
            pl.BlockSpec((1, 1, MOE_BLOCK), lambda c, ce, nv: (c, 0, 0), memory_space=pltpu.SMEM),
            pl.BlockSpec((1, 1, MOE_BLOCK), lambda c, ce, nv: (jnp.minimum(c + 1, n_chunks - 1), 0, 0),
                         memory_space=pltpu.SMEM),
            pl.BlockSpec((1, 1, MOE_BLOCK), lambda c, ce, nv: (c, 0, 0), memory_space=pltpu.SMEM),
            pl.BlockSpec(memory_space=pl.ANY),
            pl.BlockSpec((1, d, de), lambda c, ce, nv: (ce[c], 0, 0)),
            pl.BlockSpec((1, d, de), lambda c, ce, nv: (ce[c], 0, 0)),
            pl.BlockSpec((1, de, d), lambda c, ce, nv: (ce[c], 0, 0)),
        ],
        out_specs=pl.BlockSpec(memory_space=pl.ANY),
        scratch_shapes=[
            pltpu.VMEM((2, MOE_BLOCK, d), F32),
            pltpu.VMEM((2, MOE_BLOCK, d), F32),
            pltpu.SemaphoreType.DMA((2,)),
            pltpu.SemaphoreType.DMA((2,)),
        ],
    )
    return pl.pallas_call(
        _moe_kernel,
        grid_spec=grid_spec,
        out_shape=jax.ShapeDtypeStruct((n_rows_out, d), F32),
        compiler_params=_cparams(("arbitrary",)),
        name="moe_experts",
    )(chunk_expert, n_valid, src, src, dst, x1, w1, w3, w2)


def _combine_kernel(x1_ref, ya_ref, yb_ref, route_ref, lng_ref, lnb_ref, o_ref, *, alpha):
    g1 = route_ref[:, 2:3]
    g2 = route_ref[:, 3:4]
    m = g1 * ya_ref[...] + g2 * yb_ref[...]
    o_ref[...] = _layernorm(alpha * x1_ref[...] + m, lng_ref[...], lnb_ref[...])


def _combine_call(x1, y, route, lng, lnb, alpha, tm):
    t, d = x1.shape
    nt = t // tm
    full = lambda a: pl.BlockSpec(a.shape, lambda i: (0, 0))
    return pl.pallas_call(
        functools.partial(_combine_kernel, alpha=alpha),
        grid=(nt,),
        in_specs=[pl.BlockSpec((tm, d), lambda i: (i, 0)),
                  pl.BlockSpec((tm, d), lambda i: (i, 0)),
                  pl.BlockSpec((tm, d), lambda i: (i + nt, 0)),
                  pl.BlockSpec((tm, LANES), lambda i: (i, 0)),
                  full(lng), full(lnb)],
        out_specs=pl.BlockSpec((tm, d), lambda i: (i, 0)),
        out_shape=jax.ShapeDtypeStruct((t, d), F32),
        compiler_params=_cparams(("parallel",)),
        name="moe_combine",
    )(x1, y, y, route, lng, lnb)


def _rope_tables(seq, dim, lane_lo):
    half = dim // 2
    inv_freq = ROPE_THETA ** (-jnp.arange(half, dtype=F32) / half)
    ang = jnp.arange(seq, dtype=F32)[:, None] * inv_freq[None, :]
    cos = jnp.concatenate([jnp.cos(ang), jnp.cos(ang)], -1)
    sin = jnp.concatenate([jnp.sin(ang), jnp.sin(ang)], -1)
    if lane_lo == 0:
        reps = LANES // dim
        return jnp.tile(cos, (1, reps)), jnp.tile(sin, (1, reps))
    cos_t = jnp.ones((seq, LANES), F32).at[:, lane_lo:lane_lo + dim].set(cos)
    sin_t = jnp.zeros((seq, LANES), F32).at[:, lane_lo:lane_lo + dim].set(sin)
    return cos_t, sin_t


def _dispatch_tables(route, n_tok):
    expert_id = route[:, 0:TOP_K].astype(jnp.int32).reshape(-1)
    n_assign = n_tok * TOP_K
    n_slots = n_assign + N_EXPERTS * MOE_BLOCK
    n_chunks = n_slots // MOE_BLOCK
    onehot = (expert_id[:, None] == jnp.arange(N_EXPERTS, dtype=jnp.int32)[None, :]).astype(jnp.int32)
    ranks = jnp.cumsum(onehot, axis=0) - onehot
    rank = jnp.sum(ranks * onehot, axis=1)
    counts = jnp.sum(onehot, axis=0)
    padded = (counts + MOE_BLOCK - 1) // MOE_BLOCK * MOE_BLOCK
    pad_end = jnp.cumsum(padded)
    pad_start = pad_end - padded
    dest = pad_start[expert_id] + rank
    assign = jnp.arange(n_assign, dtype=jnp.int32)
    slot_assign = jnp.zeros((n_slots,), jnp.int32).at[dest].set(assign)
    src = slot_assign // TOP_K
    dst = (slot_assign % TOP_K) * n_tok + src
    chunk_start = jnp.arange(n_chunks, dtype=jnp.int32) * MOE_BLOCK
    chunk_expert = jnp.minimum(jnp.searchsorted(pad_end, chunk_start, side="right"),
                               N_EXPERTS - 1).astype(jnp.int32)
    n_valid = jnp.clip(pad_start[chunk_expert] + counts[chunk_expert] - chunk_start,
                       0, MOE_BLOCK).astype(jnp.int32)
    return (chunk_expert, n_valid, src.reshape(n_chunks, 1, MOE_BLOCK),
            dst.reshape(n_chunks, 1, MOE_BLOCK), n_assign)


def _pick_tile(n, pref):
    t = pref
    while n % t:
        t //= 2
    return t


def kernel(x, w_in, b_forget, g_cq, g_ckv, w_uq, w_ukv, g_head, w_out, ln1_g, ln1_b,
           w_group, b_group, w_expert, b_expert, w1, w3, w2, ln2_g, ln2_b):
    bsz, seq, d = x.shape
    depth = w_in.shape[0]
    t = bsz * seq
    alpha = (2.0 * depth) ** 0.25
    hq = N_HEADS_PER_MIXER * HEAD_DIM
    qk_scale = HEAD_DIM ** -0.5
    mla_scale = (MLA_NOPE + MLA_ROPE) ** -0.5
    win = DIL_BRANCHES[0][0]
    assert all(w // r == win for w, r in DIL_BRANCHES)
    assert seq % (DIL_BRANCHES[-1][1] * win) == 0 and d % LANES == 0

    cos64, sin64 = _rope_tables(seq, HEAD_DIM, 0)
    cos_m, sin_m = _rope_tables(seq, MLA_ROPE, MLA_NOPE)
    tq = _pick_tile(seq, 256)
    idx = jnp.arange(tq)
    u_sb = (idx[:, None] > idx[None, :]).astype(BF16)
    tk_sb = _pick_tile(seq, 2 * tq)
    tk_sm = _pick_tile(seq, 8 * tq)

    for l in range(depth):
        wl = w_in[l]
        o_fox, o_mla, o_dil = N_SB, N_SB + N_FOX_QKV + N_HEADS_PER_MIXER, N_SB + N_FOX_QKV + N_HEADS_PER_MIXER + N_MLA
        qs = lambda w: w.at[:, 0:hq].multiply(qk_scale)
        w_sb = qs(wl[:, 0:N_SB])
        w_fx = qs(wl[:, o_fox:o_fox + N_FOX_QKV])
        w_f = wl[:, o_fox + N_FOX_QKV:o_mla]
        w_ml = wl[:, o_mla:o_dil]
        wd = wl[:, o_dil:].reshape(d, 3, len(DIL_BRANCHES), hq)
        w_br = [qs(jnp.concatenate([wd[:, 0, g], wd[:, 1, g], wd[:, 2, g]], axis=1)) for g in range(len(DIL_BRANCHES))]
        w_misc = jnp.concatenate([w_ml, w_f, jnp.zeros((d, MISC_W - N_MLA - N_HEADS_PER_MIXER), F32)], axis=1)
        w_tok = jnp.concatenate([w_sb, w_fx, w_misc] + w_br, axis=1).astype(BF16)

        wq = jnp.pad(w_uq[l].reshape(MLA_Q_LORA, N_HEADS_PER_MIXER, MLA_NOPE + MLA_ROPE),
                     ((0, 0), (0, 0), (0, LANES - MLA_NOPE - MLA_ROPE))).reshape(MLA_Q_LORA, -1).astype(BF16)
        wkv = w_ukv[l].reshape(MLA_KV_LORA, N_HEADS_PER_MIXER, MLA_NOPE + HEAD_DIM)
        wk = jnp.pad(wkv[:, :, :MLA_NOPE], ((0, 0), (0, 0), (0, LANES - MLA_NOPE))).reshape(MLA_KV_LORA, -1).astype(BF16)
        wv = wkv[:, :, MLA_NOPE:].reshape(MLA_KV_LORA, -1).astype(BF16)
        g_flat = g_head[l].reshape(1, -1)
        wr = jnp.concatenate([w_group[l], w_expert[l],
                              jnp.zeros((d, LANES - N_GROUPS - N_EXPERTS), F32)], axis=1)
        br = jnp.concatenate([b_group[l], b_expert[l],
                              jnp.zeros((LANES - N_GROUPS - N_EXPERTS,), F32)]).reshape(1, LANES)

        tm = _pick_tile(seq, 512)
        sb, fx, misc, *qkv_br = _proj_call(
            x, w_tok, cos64, sin64,
            ((N_SB, 0, BF16, 1), (N_FOX_QKV, 0, BF16, 1), (MISC_W, 0, F32, 1))
            + tuple((N_BRANCH, 2 * hq, BF16, r) for _, r in DIL_BRANCHES), tm)
        sb = sb.reshape(t, N_SB)
        fx = fx.reshape(t, N_FOX_QKV)
        misc = misc.reshape(t, MISC_W)

        o_sb = _attn_call("sb", sb, sb, sb, 0, 1, 2, g_flat, 0, bsz, seq, tq, tk_sb, extra=(u_sb,))
        neg_c = _fox_c_call(misc, b_forget[l], bsz, seq)
        o_fx = _attn_call("fox", fx, fx, fx, 0, 1, 2, g_flat, 1, bsz, seq, tq, tk_sm, extra=(neg_c,))
        mq, mk, mv = _mla_prep_call(misc, g_cq[l].reshape(1, -1), g_ckv[l].reshape(1, -1), wq, wk, wv,
                                    cos_m, sin_m, seq, tm)
        o_ml = _attn_call("mla", mq, mk, mv, 0, 0, 0, g_flat, 2, bsz, seq, tq, tk_sm, scale=mla_scale)

        dil = []
        for g, (_, r) in enumerate(DIL_BRANCHES):
            n = seq // r
            og = _dil_call(qkv_br[g], max(win, min(tq, n // 2)), win)
            dil.append(og.reshape(t // r, r * 2 * hq))

        x1, route = _post_call(
            x.reshape(t, d), o_sb, o_fx, o_ml, dil, tuple(r for _, r in DIL_BRANCHES),
            g_flat[:, 3 * hq:], w_out[l].astype(BF16),
            ln1_g[l].reshape(1, d), ln1_b[l].reshape(1, d), wr, br, alpha, _pick_tile(t, 256))

        chunk_expert, n_valid, src, dst, n_rows = _dispatch_tables(route, t)
        y = _moe_call(chunk_expert, n_valid, src, dst, x1,
                      w1[l].astype(BF16), w3[l].astype(BF16), w2[l].astype(BF16), n_rows)
        x = _combine_call(x1, y, route,
                          ln2_g[l].reshape(1, d), ln2_b[l].reshape(1, d), alpha, _pick_tile(t, 256)).reshape(bsz, seq, d)
    return x
```

```python
import functools

import jax
import jax.numpy as jnp
import numpy as np
from jax import lax
from jax.experimental import pallas as pl
from jax.experimental.pallas import tpu as pltpu

F32 = jnp.float32
BF16 = jnp.bfloat16

HEAD_DIM = 64
N_HEADS_PER_MIXER = 4
MLA_Q_LORA = 256
MLA_KV_LORA = 128
MLA_NOPE = 64
MLA_ROPE = 32
DIL_BRANCHES = ((128, 1), (512, 4), (2048, 16))
ROPE_THETA = 10000.0
N_GROUPS = 4
EXPERTS_PER_GROUP = 4
N_EXPERTS = N_GROUPS * EXPERTS_PER_GROUP
TOP_K = 2
MOE_BLOCK = 256
LN_EPS = 1e-5
RMS_EPS = 1e-6

LANES = 128
VMEM_LIMIT_BYTES = 56 * 1024 * 1024

N_SB = 3 * N_HEADS_PER_MIXER * HEAD_DIM
N_FOX_QKV = 3 * N_HEADS_PER_MIXER * HEAD_DIM
N_MLA = MLA_Q_LORA + MLA_KV_LORA + MLA_ROPE
N_BRANCH = 3 * N_HEADS_PER_MIXER * HEAD_DIM
MISC_W = 512
F_COL = N_MLA


def _cparams(sem):
    return pltpu.CompilerParams(dimension_semantics=sem, vmem_limit_bytes=VMEM_LIMIT_BYTES)


def _split3(a):
    hi = a.astype(BF16)
    r1 = a - hi.astype(F32)
    mid = r1.astype(BF16)
    lo = (r1 - mid.astype(F32)).astype(BF16)
    return hi, mid, lo


def _dot(a, b):
    return jnp.dot(a, b, preferred_element_type=F32)


def _dot_nt(a, b):
    return lax.dot_general(a, b, (((1,), (1,)), ((), ())), preferred_element_type=F32)


def _dot_exact_rhs(a, u):
    hi, mid, lo = _split3(a)
    return _dot(hi, u) + _dot(mid, u) + _dot(lo, u)


def _dot_f32(a, b):
    ah = a.astype(BF16)
    al = (a - ah.astype(F32)).astype(BF16)
    bh = b.astype(BF16)
    bl = (b - bh.astype(F32)).astype(BF16)
    return _dot(ah, bh) + (_dot(ah, bl) + _dot(al, bh))


def _lane_iota(shape):
    return lax.broadcasted_iota(jnp.int32, shape, len(shape) - 1)


def _rotate_half(y, half):
    lane = _lane_iota(y.shape)
    fwd = pltpu.roll(y, half, 1)
    bwd = pltpu.roll(y, LANES - half, 1)
    return jnp.where((lane % (2 * half)) < half, -bwd, fwd)


def _log_sigmoid_pair(z):
    sp = jnp.log(1.0 + jnp.exp(-jnp.abs(z)))
    return jnp.minimum(z, 0.0) - sp, -jnp.maximum(z, 0.0) - sp


def _head_rms(o, g):
    lane = _lane_iota(o.shape)
    first = lane < HEAD_DIM
    sq = o * o
    ss_a = jnp.sum(jnp.where(first, sq, 0.0), axis=-1, keepdims=True)
    ss_b = jnp.sum(jnp.where(first, 0.0, sq), axis=-1, keepdims=True)
    ms = jnp.where(first, ss_a, ss_b) * (1.0 / HEAD_DIM)
    return o * lax.rsqrt(ms + RMS_EPS) * g


def _layernorm(y, g, b):
    mu = jnp.mean(y, axis=-1, keepdims=True)
    d = y - mu
    var = jnp.mean(d * d, axis=-1, keepdims=True)
    return d * lax.rsqrt(var + LN_EPS) * g + b


def _proj_kernel(x_ref, w_ref, cos_ref, sin_ref, *refs, outs):
    out_refs, stage_ref = refs[:len(outs)], refs[len(outs)]
    tm = x_ref.shape[1]
    xb = x_ref[0].astype(BF16)
    col = 0
    slab = 0
    for o_ref, (width, n_rope, _, r) in zip(out_refs, outs):
        for c in range(0, width, 2 * LANES):
            cw = min(2 * LANES, width - c)
            y = _dot(xb, w_ref[:, col + c:col + c + cw])
            for s in range(0, cw, LANES):
                ys = y[:, s:s + LANES]
                if c + s < n_rope:
                    ys = ys * cos_ref[...] + _rotate_half(ys, HEAD_DIM // 2) * sin_ref[...]
                if r == 1:
                    o_ref[0, 0, :, c + s:c + s + LANES] = ys.astype(o_ref.dtype)
                else:
                    st = stage_ref.at[slab % stage_ref.shape[0]]
                    slab += 1
                    st[...] = ys
                    for p in range(r):
                        o_ref[0, p, :, c + s:c + s + LANES] = (
                            st[pl.ds(p, tm // r, stride=r), :].astype(o_ref.dtype))
        col += width


PROJ_STAGE_SLABS = 4


def _proj_call(x, w, cos, sin, outs, tm):
    bsz, seq, d = x.shape
    nt = seq // tm
    in_specs = [
        pl.BlockSpec((1, tm, d), lambda b, i: (b, i, 0)),
        pl.BlockSpec(w.shape, lambda b, i: (0, 0)),
        pl.BlockSpec((tm, LANES), lambda b, i: (i, 0)),
        pl.BlockSpec((tm, LANES), lambda b, i: (i, 0)),
    ]
    out_specs = [pl.BlockSpec((1, r, tm // r, wd), lambda b, i: (b, 0, i, 0)) for wd, _, _, r in outs]
    out_shape = [jax.ShapeDtypeStruct((bsz, r, seq // r, wd), dt) for wd, _, dt, r in outs]
    return pl.pallas_call(
        functools.partial(_proj_kernel, outs=outs),
        grid=(bsz, nt), in_specs=in_specs, out_specs=out_specs, out_shape=out_shape,
        scratch_shapes=[pltpu.VMEM((PROJ_STAGE_SLABS, tm, LANES), F32)],
        compiler_params=_cparams(("parallel", "parallel")),
        name="proj",
    )(x, w, cos, sin)


def _fox_c_kernel(misc_ref, bias_ref, uinc_ref, ones_ref, out_ref, *, seq):
    lane0 = F_COL - 3 * LANES
    nblk = seq // LANES

    def body(j, carry):
        r0 = pl.multiple_of(j * LANES, LANES)
        f = misc_ref[pl.ds(r0, LANES), :] + bias_ref[...]
        lf, _ = _log_sigmoid_pair(f)
        lft = lf.T
        csum = _dot_exact_rhs(lft, uinc_ref[...]) + carry
        tot = _dot_exact_rhs(lft, ones_ref[...])
        out_ref[0, :, pl.ds(r0, LANES)] = -csum[lane0:lane0 + N_HEADS_PER_MIXER, :]
        return carry + tot

    lax.fori_loop(0, nblk, body, jnp.zeros((LANES, LANES), F32))


def _fox_c_call(misc, b_forget, bsz, seq):
    bias = jnp.zeros((1, LANES), F32).at[0, F_COL - 3 * LANES:F_COL - 3 * LANES + N_HEADS_PER_MIXER].set(b_forget)
    idx = jnp.arange(LANES)
    uinc = (idx[:, None] <= idx[None, :]).astype(BF16)
    ones = jnp.ones((LANES, LANES), BF16)
    return pl.pallas_call(
        functools.partial(_fox_c_kernel, seq=seq),
        grid=(bsz,),
        in_specs=[
            pl.BlockSpec((seq, LANES), lambda b: (b, 3)),
            pl.BlockSpec((1, LANES), lambda b: (0, 0)),
            pl.BlockSpec((LANES, LANES), lambda b: (0, 0)),
            pl.BlockSpec((LANES, LANES), lambda b: (0, 0)),
        ],
        out_specs=pl.BlockSpec((1, N_HEADS_PER_MIXER, seq), lambda b: (b, 0, 0)),
        out_shape=jax.ShapeDtypeStruct((bsz, N_HEADS_PER_MIXER, seq), F32),
        compiler_params=_cparams(("parallel",)),
        name="fox_c",
    )(misc, bias, uinc, ones)


def _mla_prep_kernel(misc_ref, gq_ref, gkv_ref, wq_ref, wk_ref, wv_ref, cos_ref, sin_ref,
                     q_ref, k_ref, v_ref):
    def rms(x, g):
        return x * lax.rsqrt(jnp.mean(x * x, axis=-1, keepdims=True) + RMS_EPS) * g

    cq = rms(misc_ref[:, 0:MLA_Q_LORA], gq_ref[...]).astype(BF16)
    ckv = rms(misc_ref[:, MLA_Q_LORA:MLA_Q_LORA + MLA_KV_LORA], gkv_ref[...]).astype(BF16)
    kr_blk = misc_ref[:, 3 * LANES:4 * LANES]
    lane = _lane_iota(kr_blk.shape)
    in_rope = (lane >= MLA_NOPE) & (lane < MLA_NOPE + MLA_ROPE)
    kr = jnp.where(in_rope, pltpu.roll(kr_blk, MLA_NOPE, 1), 0.0)
    cos = cos_ref[...]
    sin = sin_ref[...]

    def rope(y):
        return y * cos + _rotate_half(y, MLA_ROPE // 2) * sin

    q = _dot(cq, wq_ref[...])
    k = _dot(ckv, wk_ref[...])
    for h in range(N_HEADS_PER_MIXER):
        sl = slice(h * LANES, (h + 1) * LANES)
        q_ref[:, sl] = rope(q[:, sl]).astype(BF16)
        k_ref[:, sl] = rope(k[:, sl] + kr).astype(BF16)
    v_ref[...] = _dot(ckv, wv_ref[...]).astype(BF16)


def _mla_prep_call(misc, g_cq, g_ckv, wq, wk, wv, cos, sin, seq, tm):
    t = misc.shape[0]
    nper = seq // tm
    hw = N_HEADS_PER_MIXER * LANES
    vw = N_HEADS_PER_MIXER * HEAD_DIM
    full = lambda a: pl.BlockSpec(a.shape, lambda i: (0, 0))
    return pl.pallas_call(
        _mla_prep_kernel,
        grid=(t // tm,),
        in_specs=[
            pl.BlockSpec((tm, MISC_W), lambda i: (i, 0)),
            full(g_cq), full(g_ckv), full(wq), full(wk), full(wv),
            pl.BlockSpec((tm, LANES), lambda i: (i % nper, 0)),
            pl.BlockSpec((tm, LANES), lambda i: (i % nper, 0)),
        ],
        out_specs=[
            pl.BlockSpec((tm, hw), lambda i: (i, 0)),
            pl.BlockSpec((tm, hw), lambda i: (i, 0)),
            pl.BlockSpec((tm, vw), lambda i: (i, 0)),
        ],
        out_shape=[
            jax.ShapeDtypeStruct((t, hw), BF16),
            jax.ShapeDtypeStruct((t, hw), BF16),
            jax.ShapeDtypeStruct((t, vw), BF16),
        ],
        compiler_params=_cparams(("parallel",)),
        name="mla_prep",
    )(misc, g_cq, g_ckv, wq, wk, wv, cos, sin)


def _attn_kernel(*refs, mode, tq, tk, scale):
    if mode == "sb":
        q_ref, k_ref, v_ref, g_ref, u_ref, o_ref = refs
    elif mode == "fox":
        q_ref, k_ref, v_ref, g_ref, nc_ref, o_ref = refs
    else:
        q_ref, k_ref, v_ref, g_ref, o_ref = refs
    nh = N_HEADS_PER_MIXER
    i = pl.program_id(1)
    lane = _lane_iota((tq, LANES))
    first = lane < HEAD_DIM
    row = lax.broadcasted_iota(jnp.int32, (tq, tk), 0)
    colm = lax.broadcasted_iota(jnp.int32, (tq, tk), 1)

    q_heads = []
    for h in range(nh):
        if mode == "mla":
            q_heads.append(q_ref[:, h * LANES:(h + 1) * LANES])
        else:
            q2 = q_ref[:, (h // 2) * LANES:(h // 2 + 1) * LANES]
            zero = jnp.zeros_like(q2)
            q_heads.append(jnp.where(first, q2, zero) if h % 2 == 0 else jnp.where(first, zero, q2))

    def k_head(j, h):
        r0 = pl.multiple_of(j * tk, tk)
        kb = h if mode == "mla" else h // 2
        return k_ref[pl.ds(r0, tk), kb * LANES:(kb + 1) * LANES]

    def v_pair(j, p):
        r0 = pl.multiple_of(j * tk, tk)
        return v_ref[pl.ds(r0, tk), p * LANES:(p + 1) * LANES]

    jd = (i * tq) // tk
    off = i * tq - jd * tk

    if mode == "sb":
        tu = u_ref.shape[0]

        def step(j, carry, diag):
            accs, rs = carry
            new_accs, new_rs = [], []
            for p in range(nh // 2):
                v2 = v_pair(j, p)
                outs = []
                for h in (2 * p, 2 * p + 1):
                    z = _dot_nt(q_heads[h], k_head(j, h))
                    ls_pos, ls_neg = _log_sigmoid_pair(z)
                    if diag:
                        before = colm < row + off
                        ls_neg = jnp.where(before, ls_neg, 0.0)
                    lb = ls_neg.astype(BF16)
                    pieces = []
                    later = rs[h]
                    for c0 in range(tk - tu, -1, -tu):
                        c = _dot(lb[:, c0:c0 + tu], u_ref[...]) + later
                        pieces.insert(0, c)
                        later = c[:, 0:1] + ls_neg[:, c0:c0 + 1]
                    w = jnp.exp(ls_pos + jnp.concatenate(pieces, axis=1))
                    if diag:
                        w = jnp.where(before, w, 0.0)
                    outs.append(_dot(w.astype(BF16), v2))
                    new_rs.append(later)
                new_accs.append(accs[p] + jnp.where(first, outs[0], outs[1]))
            return tuple(new_accs), tuple(new_rs)

        zacc = jnp.zeros((tq, LANES), F32)
        zr = jnp.zeros((tq, 1), F32)
        carry = step(jd, ((zacc,) * (nh // 2), (zr,) * nh), True)
        carry = lax.fori_loop(0, jd, lambda n, c: step(jd - 1 - n, c, False), carry)
        outs = carry[0]
    else:
        ones = jnp.ones((tk, LANES), BF16)

        def step(j, carry, diag):
            accs, ms, ls = carry
            new_accs, new_ms, new_ls = [], [], []
            r0 = pl.multiple_of(j * tk, tk)
            for p in range(nh // 2):
                v_aug = jnp.concatenate([v_pair(j, p), ones], axis=1)
                pv, alphas = [], []
                for h in (2 * p, 2 * p + 1):
                    s = _dot_nt(q_heads[h], k_head(j, h))
                    if mode == "mla":
                        s = s * scale
                    else:
                        s = s + nc_ref[0, h:h + 1, pl.ds(r0, tk)]
                    if diag:
                        s = jnp.where(colm <= row + off, s, -jnp.inf)
                    m_new = jnp.maximum(ms[h], jnp.max(s, axis=-1, keepdims=True))
                    alpha = jnp.exp(ms[h] - m_new)
                    pvx = _dot(jnp.exp(s - m_new).astype(BF16), v_aug)
                    new_ls.append(alpha * ls[h] + pvx[:, LANES:2 * LANES])
                    new_ms.append(m_new)
                    pv.append(pvx[:, 0:LANES])
                    alphas.append(alpha)
                new_accs.append(accs[p] * jnp.where(first, alphas[0], alphas[1])
                                + jnp.where(first, pv[0], pv[1]))
            return tuple(new_accs), tuple(new_ms), tuple(new_ls)

        neg = jnp.full((tq, 1), -jnp.inf, F32)
        zacc = jnp.zeros((tq, LANES), F32)
        carry = step(jd, ((zacc,) * (nh // 2), (neg,) * nh, (zacc,) * nh), True)
        carry = lax.fori_loop(0, jd, lambda n, c: step(n, c, False), carry)
        accs, _, ls = carry
        outs = [accs[p] / jnp.where(first, ls[2 * p], ls[2 * p + 1]) for p in range(nh // 2)]
    for p in range(nh // 2):
        sl = slice(p * LANES, (p + 1) * LANES)
        o_ref[:, sl] = _head_rms(outs[p], g_ref[:, sl]).astype(o_ref.dtype)


def _attn_call(mode, q, k, v, qcol, kcol, vcol, g_flat, gcol, bsz, seq, tq, tk, extra=(), scale=1.0):
    t = bsz * seq
    nq = seq // tq
    hq = N_HEADS_PER_MIXER * HEAD_DIM
    qw = N_HEADS_PER_MIXER * LANES if mode == "mla" else hq
    assert tk % tq == 0 and seq % tk == 0
    in_specs = [
        pl.BlockSpec((tq, qw), lambda b, i: (b * nq + i, qcol)),
        pl.BlockSpec((seq, qw), lambda b, i: (b, kcol)),
        pl.BlockSpec((seq, hq), lambda b, i: (b, vcol)),
        pl.BlockSpec((1, hq), lambda b, i: (0, gcol)),
    ]
    args = [q, k, v, g_flat]
    if mode == "sb":
        (u,) = extra
        in_specs.append(pl.BlockSpec(u.shape, lambda b, i: (0, 0)))
        args.append(u)
    elif mode == "fox":
        (nc,) = extra
        in_specs.append(pl.BlockSpec((1, N_HEADS_PER_MIXER, seq), lambda b, i: (b, 0, 0)))
        args.append(nc)
    return pl.pallas_call(
        functools.partial(_attn_kernel, mode=mode, tq=tq, tk=tk, scale=scale),
        grid=(bsz, nq),
        in_specs=in_specs,
        out_specs=pl.BlockSpec((tq, hq), lambda b, i: (b * nq + i, 0)),
        out_shape=jax.ShapeDtypeStruct((t, hq), BF16),
        compiler_params=_cparams(("parallel", "arbitrary")),
        name=f"attn_{mode}",
    )(*args)


def _dil_kernel(qkv_ref, o_ref, *, tq, tk, win):
    i = pl.program_id(2)
    lane = _lane_iota((tq, LANES))
    first = lane < HEAD_DIM
    hq = N_HEADS_PER_MIXER * HEAD_DIM
    r0 = pl.multiple_of(i * tq, tq)
    k0 = pl.multiple_of(jnp.maximum(i - 1, 0) * tq, tq)
    row = lax.broadcasted_iota(jnp.int32, (tq, tk), 0)
    colm = lax.broadcasted_iota(jnp.int32, (tq, tk), 1)
    delta = row + (r0 - k0) - colm
    band = jnp.abs(2 * delta - win) <= win
    ones = jnp.ones((tk, LANES), BF16)
    for p in range(N_HEADS_PER_MIXER // 2):
        q2 = qkv_ref[0, 0, pl.ds(r0, tq), p * LANES:(p + 1) * LANES]
        k2 = qkv_ref[0, 0, pl.ds(k0, tk), hq + p * LANES:hq + (p + 1) * LANES]
        v_aug = jnp.concatenate(
            [qkv_ref[0, 0, pl.ds(k0, tk), 2 * hq + p * LANES:2 * hq + (p + 1) * LANES], ones], axis=1)
        zero = jnp.zeros_like(q2)
        outs = []
        lses = []
        for hh in range(2):
            qh = jnp.where(first, q2, zero) if hh == 0 else jnp.where(first, zero, q2)
            s = jnp.where(band, _dot_nt(qh, k2), -jnp.inf)
            m = jnp.max(s, axis=-1, keepdims=True)
            pvx = _dot(jnp.exp(s - m).astype(BF16), v_aug)
            l = pvx[:, LANES:2 * LANES]
            outs.append(pvx[:, 0:LANES] / l)
            lses.append(m + jnp.log(l))
        o_ref[0, :, p * LANES:(p + 1) * LANES] = jnp.where(first, outs[0], outs[1])
        o_ref[0, :, hq + p * LANES:hq + (p + 1) * LANES] = jnp.where(first, lses[0], lses[1])


def _dil_call(qkv, tqs, win):
    bsz, r, n, w = qkv.shape
    hq = N_HEADS_PER_MIXER * HEAD_DIM
    tk = 2 * tqs if n >= 2 * tqs else tqs
    assert tqs >= win and n % tqs == 0 and (tk == 2 * tqs or n == tqs)
    return pl.pallas_call(
        functools.partial(_dil_kernel, tq=tqs, tk=tk, win=win),
        grid=(bsz, r, n // tqs),
        in_specs=[pl.BlockSpec((1, 1, n, w), lambda b, p, c: (b, p, 0, 0))],
        out_specs=pl.BlockSpec((1, tqs, 2 * hq), lambda b, p, c: (b, c, p)),
        out_shape=jax.ShapeDtypeStruct((bsz, n, r * 2 * hq), F32),
        compiler_params=_cparams(("parallel", "parallel", "arbitrary")),
        name=f"dil_r{r}",
    )(qkv)


def _post_kernel(x_ref, osb_ref, ofox_ref, omla_ref, d1_ref, d2_ref, d3_ref, gd_ref, wo_ref,
                 lng_ref, lnb_ref, wr_ref, br_ref, x1_ref, route_ref, stage_ref, *, alpha, dils):
    hq = N_HEADS_PER_MIXER * HEAD_DIM
    tm = x_ref.shape[0]
    h = _dot(osb_ref[...], wo_ref[0:hq, :])
    h += _dot(ofox_ref[...], wo_ref[hq:2 * hq, :])
    h += _dot(omla_ref[...], wo_ref[2 * hq:3 * hq, :])

    def token_rows(g, d_ref, r, c0):
        if r == 1:
            return d_ref[:, c0:c0 + LANES]
        st = stage_ref.at[g, c0 // LANES]
        for q in range(r):
            st[pl.ds(q, tm // r, stride=r), :] = d_ref[:, q * 2 * hq + c0:q * 2 * hq + c0 + LANES]
        return st[...]

    d_refs = (d1_ref, d2_ref, d3_ref)
    for p in range(N_HEADS_PER_MIXER // 2):
        sl = slice(p * LANES, (p + 1) * LANES)
        lses = [token_rows(g, d_refs[g], dils[g], hq + p * LANES) for g in range(len(dils))]
        vals = [token_rows(g, d_refs[g], dils[g], p * LANES) for g in range(len(dils))]
        l1, l2, l3 = lses
        m = jnp.maximum(jnp.maximum(l1, l2), l3)
        e1, e2, e3 = jnp.exp(l1 - m), jnp.exp(l2 - m), jnp.exp(l3 - m)
        inv = 1.0 / (e1 + e2 + e3)
        od = (e1 * inv) * vals[0] + (e2 * inv) * vals[1] + (e3 * inv) * vals[2]
        od = _head_rms(od, gd_ref[:, sl]).astype(BF16)
        h += _dot(od, wo_ref[3 * hq + p * LANES:3 * hq + (p + 1) * LANES, :])
    x1 = _layernorm(alpha * x_ref[...] + h, lng_ref[...], lnb_ref[...])
    x1_ref[...] = x1

    logits = _dot_f32(x1, wr_ref[...]) + br_ref[...]
    lane = _lane_iota(logits.shape)
    lanef = lane.astype(F32)
    big = float(LANES)
    ninf = -jnp.inf
    gl = jnp.where(lane < N_GROUPS, logits, ninf)
    gmax = jnp.max(gl, axis=-1, keepdims=True)
    gsel = jnp.min(jnp.where(gl == gmax, lanef, big), axis=-1, keepdims=True)
    gw = 1.0 / jnp.sum(jnp.exp(gl - gmax), axis=-1, keepdims=True)
    e_lo = N_GROUPS + EXPERTS_PER_GROUP * gsel
    in_grp = (lanef >= e_lo) & (lanef < e_lo + EXPERTS_PER_GROUP)
    el = jnp.where(in_grp, logits, ninf)
    t1 = jnp.max(el, axis=-1, keepdims=True)
    i1 = jnp.min(jnp.where(el == t1, lanef, big), axis=-1, keepdims=True)
    el2 = jnp.where(lanef == i1, ninf, el)
    t2 = jnp.max(el2, axis=-1, keepdims=True)
    i2 = jnp.min(jnp.where(el2 == t2, lanef, big), axis=-1, keepdims=True)
    ex = jnp.exp(t2 - t1)
    den = 1.0 + ex
    g1 = gw / den
    g2 = gw * ex / den
    out = jnp.where(lane == 0, i1 - N_GROUPS,
                    jnp.where(lane == 1, i2 - N_GROUPS,
                              jnp.where(lane == 2, g1, jnp.where(lane == 3, g2, 0.0))))
    route_ref[...] = out


def _post_call(x, osb, ofox, omla, dil_outs, dils, g_dil, wo, lng, lnb, wr, br, alpha, tm):
    t, d = x.shape
    hq = N_HEADS_PER_MIXER * HEAD_DIM
    row = lambda w: pl.BlockSpec((tm, w), lambda i: (i, 0))
    full = lambda a: pl.BlockSpec(a.shape, lambda i: (0, 0))
    dil_specs = [pl.BlockSpec((tm // r, r * 2 * hq), lambda i: (i, 0)) for r in dils]
    return pl.pallas_call(
        functools.partial(_post_kernel, alpha=alpha, dils=dils),
        grid=(t // tm,),
        in_specs=[row(d), row(hq), row(hq), row(hq)] + dil_specs
                 + [full(g_dil), full(wo), full(lng), full(lnb), full(wr), full(br)],
        out_specs=[row(d), row(LANES)],
        out_shape=[jax.ShapeDtypeStruct((t, d), F32), jax.ShapeDtypeStruct((t, LANES), F32)],
        scratch_shapes=[pltpu.VMEM((len(dils), 2 * hq // LANES, tm, LANES), F32)],
        compiler_params=_cparams(("parallel",)),
        name="post_mixer",
    )(x, osb, ofox, omla, *dil_outs, g_dil, wo, lng, lnb, wr, br)


DMA_UNROLL = 8


def _moe_kernel(ce_ref, nv_ref, src_ref, srcn_ref, dst_ref, x_hbm, w1_ref, w3_ref, w2_ref, y_hbm,
                xs_ref, ys_ref, gsem, ssem):
    c = pl.program_id(0)
    nc = pl.num_programs(0)
    slot = c % 2
    other = 1 - slot
    nv = nv_ref[c]
    nv_next = jnp.where(c + 1 < nc, nv_ref[jnp.minimum(c + 1, nc - 1)], 0)
    nv_prev = jnp.where(c > 0, nv_ref[jnp.maximum(c - 1, 0)], 0)

    def gather_copy(idx_ref, i, s):
        return pltpu.make_async_copy(x_hbm.at[pl.ds(idx_ref[0, 0, i], 1), :],
                                     xs_ref.at[s, pl.ds(i, 1), :], gsem.at[s])

    def scatter_copy(i, s):
        return pltpu.make_async_copy(ys_ref.at[s, pl.ds(i, 1), :],
                                     y_hbm.at[pl.ds(dst_ref[0, 0, i], 1), :], ssem.at[s])

    def start_rows(n, make):
        ngrp = n // DMA_UNROLL

        def grp(g, _):
            base = pl.multiple_of(g * DMA_UNROLL, DMA_UNROLL)
            for k in range(DMA_UNROLL):
                make(base + k).start()
            return 0

        def one(i, _):
            make(i).start()
            return 0

        lax.fori_loop(0, ngrp, grp, 0)
        lax.fori_loop(ngrp * DMA_UNROLL, n, one, 0)

    def wait_rows(n, make_row, make_block):
        @pl.when(n == MOE_BLOCK)
        def _():
            make_block().wait()

        @pl.when(n < MOE_BLOCK)
        def _():
            def one(i, _):
                make_row(i).wait()
                return 0
            lax.fori_loop(0, n, one, 0)

    def gather_block(s):
        return pltpu.make_async_copy(x_hbm.at[pl.ds(0, MOE_BLOCK), :], xs_ref.at[s], gsem.at[s])

    def scatter_block(s):
        return pltpu.make_async_copy(ys_ref.at[s], y_hbm.at[pl.ds(0, MOE_BLOCK), :], ssem.at[s])

    @pl.when(c == 0)
    def _():
        xs_ref[...] = jnp.zeros_like(xs_ref)
        start_rows(nv, lambda i: gather_copy(src_ref, i, 0))

    start_rows(nv_next, lambda i: gather_copy(srcn_ref, i, other))
    wait_rows(nv, lambda i: gather_copy(src_ref, i, slot), lambda: gather_block(slot))

    @pl.when(nv > 0)
    def _():
        xb = xs_ref[slot].astype(BF16)
        a = _dot(xb, w1_ref[0])
        b = _dot(xb, w3_ref[0])
        hid = (a / (1.0 + jnp.exp(-a)) * b).astype(BF16)
        ys_ref[slot] = _dot(hid, w2_ref[0])

    wait_rows(nv_prev, lambda i: scatter_copy(i, other), lambda: scatter_block(other))
    start_rows(nv, lambda i: scatter_copy(i, slot))

    @pl.when(c == nc - 1)
    def _():
        wait_rows(nv, lambda i: scatter_copy(i, slot), lambda: scatter_block(slot))


def _moe_call(chunk_expert, n_valid, src, dst, x1, w1, w3, w2, n_rows_out):
    n_chunks = chunk_expert.shape[0]
    t, d = x1.shape
    de = w1.shape[-1]
    grid_spec = pltpu.PrefetchScalarGridSpec(
        num_scalar_prefetch=2,
        grid=(n_chunks,),
        in_specs=[
            pl.BlockSpec((1, 1, MOE_BLOCK), lambda c, ce, nv: (c, 0, 0), memory_space=pltpu.SMEM),
            pl.BlockSpec((1, 1, MOE_BLOCK), lambda c, ce, nv: (jnp.minimum(c + 1, n_chunks - 1), 0, 0),
                         memory_space=pltpu.SMEM),
            pl.BlockSpec((1, 1, MOE_BLOCK), lambda c, ce, nv: (c, 0, 0), memory_space=pltpu.SMEM),
            pl.BlockSpec(memory_space=pl.ANY),
            pl.BlockSpec((1, d, de), lambda c, ce, nv: (ce[c], 0, 0)),
            pl.BlockSpec((1, d, de), lambda c, ce, nv: (ce[c], 0, 0)),
            pl.BlockSpec((1, de, d), lambda c, ce, nv: (ce[c], 0, 0)),
        ],
        out_specs=pl.BlockSpec(memory_space=pl.ANY),
        scratch_shapes=[
            pltpu.VMEM((2, MOE_BLOCK, d), F32),
            pltpu.VMEM((2, MOE_BLOCK, d), F32),
            pltpu.SemaphoreType.DMA((2,)),
            pltpu.SemaphoreType.DMA((2,)),
        ],
    )
    return pl.pallas_call(
        _moe_kernel,
        grid_spec=grid_spec,
        out_shape=jax.ShapeDtypeStruct((n_rows_out, d), F32),
        compiler_params=_cparams(("arbitrary",)),
        name="moe_experts",
    )(chunk_expert, n_valid, src, src, dst, x1, w1, w3, w2)


def _combine_kernel(x1_ref, ya_ref, yb_ref, route_ref, lng_ref, lnb_ref, o_ref, *, alpha):
    g1 = route_ref[:, 2:3]
    g2 = route_ref[:, 3:4]
    m = g1 * ya_ref[...] + g2 * yb_ref[...]
    o_ref[...] = _layernorm(alpha * x1_ref[...] + m, lng_ref[...], lnb_ref[...])


def _combine_call(x1, y, route, lng, lnb, alpha, tm):
    t, d = x1.shape
    nt = t // tm
    full = lambda a: pl.BlockSpec(a.shape, lambda i: (0, 0))
    return pl.pallas_call(
        functools.partial(_combine_kernel, alpha=alpha),
        grid=(nt,),
        in_specs=[pl.BlockSpec((tm, d), lambda i: (i, 0)),
                  pl.BlockSpec((tm, d), lambda i: (i, 0)),
                  pl.BlockSpec((tm, d), lambda i: (i + nt, 0)),
                  pl.BlockSpec((tm, LANES), lambda i: (i, 0)),
                  full(lng), full(lnb)],
        out_specs=pl.BlockSpec((tm, d), lambda i: (i, 0)),
        out_shape=jax.ShapeDtypeStruct((t, d), F32),
        compiler_params=_cparams(("parallel",)),
        name="moe_combine",
    )(x1, y, y, route, lng, lnb)


def _rope_tables(seq, dim, lane_lo):
    half = dim // 2
    inv_freq = ROPE_THETA ** (-jnp.arange(half, dtype=F32) / half)
    ang = jnp.arange(seq, dtype=F32)[:, None] * inv_freq[None, :]
    cos = jnp.concatenate([jnp.cos(ang), jnp.cos(ang)], -1)
    sin = jnp.concatenate([jnp.sin(ang), jnp.sin(ang)], -1)
    if lane_lo == 0:
        reps = LANES // dim
        return jnp.tile(cos, (1, reps)), jnp.tile(sin, (1, reps))
    cos_t = jnp.ones((seq, LANES), F32).at[:, lane_lo:lane_lo + dim].set(cos)
    sin_t = jnp.zeros((seq, LANES), F32).at[:, lane_lo:lane_lo + dim].set(sin)
    return cos_t, sin_t


def _dispatch_tables(route, n_tok):
    expert_id = route[:, 0:TOP_K].astype(jnp.int32).reshape(-1)
    n_assign = n_tok * TOP_K
    n_slots = n_assign + N_EXPERTS * MOE_BLOCK
    n_chunks = n_slots // MOE_BLOCK
    onehot = (expert_id[:, None] == jnp.arange(N_EXPERTS, dtype=jnp.int32)[None, :]).astype(jnp.int32)
    ranks = jnp.cumsum(onehot, axis=0) - onehot
    rank = jnp.sum(ranks * onehot, axis=1)
    counts = jnp.sum(onehot, axis=0)
    padded = (counts + MOE_BLOCK - 1) // MOE_BLOCK * MOE_BLOCK
    pad_end = jnp.cumsum(padded)
    pad_start = pad_end - padded
    dest = pad_start[expert_id] + rank
    assign = jnp.arange(n_assign, dtype=jnp.int32)
    slot_assign = jnp.zeros((n_slots,), jnp.int32).at[dest].set(assign)
    src = slot_assign // TOP_K
    dst = (slot_assign % TOP_K) * n_tok + src
    chunk_start = jnp.arange(n_chunks, dtype=jnp.int32) * MOE_BLOCK
    chunk_expert = jnp.minimum(jnp.searchsorted(pad_end, chunk_start, side="right"),
                               N_EXPERTS - 1).astype(jnp.int32)
    n_valid = jnp.clip(pad_start[chunk_expert] + counts[chunk_expert] - chunk_start,
                       0, MOE_BLOCK).astype(jnp.int32)
    return (chunk_expert, n_valid, src.reshape(n_chunks, 1, MOE_BLOCK),
            dst.reshape(n_chunks, 1, MOE_BLOCK), n_assign)


def _pick_tile(n, pref):
    t = pref
    while n % t:
        t //= 2
    return t


def kernel(x, w_in, b_forget, g_cq, g_ckv, w_uq, w_ukv, g_head, w_out, ln1_g, ln1_b,
           w_group, b_group, w_expert, b_expert, w1, w3, w2, ln2_g, ln2_b):
    bsz, seq, d = x.shape
    depth = w_in.shape[0]
    t = bsz * seq
    alpha = (2.0 * depth) ** 0.25
    hq = N_HEADS_PER_MIXER * HEAD_DIM
    qk_scale = HEAD_DIM ** -0.5
    mla_scale = (MLA_NOPE + MLA_ROPE) ** -0.5
    win = DIL_BRANCHES[0][0]
    assert all(w // r == win for w, r in DIL_BRANCHES)
    assert seq % (DIL_BRANCHES[-1][1] * win) == 0 and d % LANES == 0

    cos64, sin64 = _rope_tables(seq, HEAD_DIM, 0)
    cos_m, sin_m = _rope_tables(seq, MLA_ROPE, MLA_NOPE)
    tq = _pick_tile(seq, 256)
    idx = jnp.arange(tq)
    u_sb = (idx[:, None] > idx[None, :]).astype(BF16)
    tk_sb = _pick_tile(seq, 2 * tq)
    tk_sm = _pick_tile(seq, 4 * tq)

    for l in range(depth):
        wl = w_in[l]
        o_fox, o_mla, o_dil = N_SB, N_SB + N_FOX_QKV + N_HEADS_PER_MIXER, N_SB + N_FOX_QKV + N_HEADS_PER_MIXER + N_MLA
        qs = lambda w: w.at[:, 0:hq].multiply(qk_scale)
        w_sb = qs(wl[:, 0:N_SB])
        w_fx = qs(wl[:, o_fox:o_fox + N_FOX_QKV])
        w_f = wl[:, o_fox + N_FOX_QKV:o_mla]
        w_ml = wl[:, o_mla:o_dil]
        wd = wl[:, o_dil:].reshape(d, 3, len(DIL_BRANCHES), hq)
        w_br = [qs(jnp.concatenate([wd[:, 0, g], wd[:, 1, g], wd[:, 2, g]], axis=1)) for g in range(len(DIL_BRANCHES))]
        w_misc = jnp.concatenate([w_ml, w_f, jnp.zeros((d, MISC_W - N_MLA - N_HEADS_PER_MIXER), F32)], axis=1)
        w_tok = jnp.concatenate([w_sb, w_fx, w_misc] + w_br, axis=1).astype(BF16)

        wq = jnp.pad(w_uq[l].reshape(MLA_Q_LORA, N_HEADS_PER_MIXER, MLA_NOPE + MLA_ROPE),
                     ((0, 0), (0, 0), (0, LANES - MLA_NOPE - MLA_ROPE))).reshape(MLA_Q_LORA, -1).astype(BF16)
        wkv = w_ukv[l].reshape(MLA_KV_LORA, N_HEADS_PER_MIXER, MLA_NOPE + HEAD_DIM)
        wk = jnp.pad(wkv[:, :, :MLA_NOPE], ((0, 0), (0, 0), (0, LANES - MLA_NOPE))).reshape(MLA_KV_LORA, -1).astype(BF16)
        wv = wkv[:, :, MLA_NOPE:].reshape(MLA_KV_LORA, -1).astype(BF16)
        g_flat = g_head[l].reshape(1, -1)
        wr = jnp.concatenate([w_group[l], w_expert[l],
                              jnp.zeros((d, LANES - N_GROUPS - N_EXPERTS), F32)], axis=1)
        br = jnp.concatenate([b_group[l], b_expert[l],
                              jnp.zeros((LANES - N_GROUPS - N_EXPERTS,), F32)]).reshape(1, LANES)

        tm = _pick_tile(seq, 512)
        sb, fx, misc, *qkv_br = _proj_call(
            x, w_tok, cos64, sin64,
            ((N_SB, 0, BF16, 1), (N_FOX_QKV, 0, BF16, 1), (MISC_W, 0, F32, 1))
            + tuple((N_BRANCH, 2 * hq, BF16, r) for _, r in DIL_BRANCHES), tm)
        sb = sb.reshape(t, N_SB)
        fx = fx.reshape(t, N_FOX_QKV)
        misc = misc.reshape(t, MISC_W)

        o_sb = _attn_call("sb", sb, sb, sb, 0, 1, 2, g_flat, 0, bsz, seq, tq, tk_sb, extra=(u_sb,))
        neg_c = _fox_c_call(misc, b_forget[l], bsz, seq)
        o_fx = _attn_call("fox", fx, fx, fx, 0, 1, 2, g_flat, 1, bsz, seq, tq, tk_sm, extra=(neg_c,))
        mq, mk, mv = _mla_prep_call(misc, g_cq[l].reshape(1, -1), g_ckv[l].reshape(1, -1), wq, wk, wv,
                                    cos_m, sin_m, seq, tm)
        o_ml = _attn_call("mla", mq, mk, mv, 0, 0, 0, g_flat, 2, bsz, seq, tq, tk_sm, scale=mla_scale)

        dil = []
        for g, (_, r) in enumerate(DIL_BRANCHES):
            n = seq // r
            og = _dil_call(qkv_br[g], max(win, min(tq, n // 2)), win)
            dil.append(og.reshape(t // r, r * 2 * hq))

        x1, route = _post_call(
            x.reshape(t, d), o_sb, o_fx, o_ml, dil, tuple(r for _, r in DIL_BRANCHES),
            g_flat[:, 3 * hq:], w_out[l].astype(BF16),
            ln1_g[l].reshape(1, d), ln1_b[l].reshape(1, d), wr, br, alpha, _pick_tile(t, 256))

        chunk_expert, n_valid, src, dst, n_rows = _dispatch_tables(route, t)
        y = _moe_call(chunk_expert, n_valid, src, dst, x1,
                      w1[l].astype(BF16), w3[l].astype(BF16), w2[l].astype(BF16), n_rows)
        x = _combine_call(x1, y, route,
                          ln2_g[l].reshape(1, d), ln2_b[l].reshape(1, d), alpha, _pick_tile(t, 256)).reshape(bsz, seq, d)
    return x
```

```python
import functools

import jax
import jax.numpy as jnp
import numpy as np
from jax import lax
from jax.experimental import pallas as pl
from jax.experimental.pallas import tpu as pltpu

F32 = jnp.float32
BF16 = jnp.bfloat16

HEAD_DIM = 64
N_HEADS_PER_MIXER = 4
MLA_Q_LORA = 256
MLA_KV_LORA = 128
MLA_NOPE = 64
MLA_ROPE = 32
DIL_BRANCHES = ((128, 1), (512, 4), (2048, 16))
ROPE_THETA = 10000.0
N_GROUPS = 4
EXPERTS_PER_GROUP = 4
N_EXPERTS = N_GROUPS * EXPERTS_PER_GROUP
TOP_K = 2
MOE_BLOCK = 256
LN_EPS = 1e-5
RMS_EPS = 1e-6

LANES = 128
SUBLANES = 8
VMEM_LIMIT_BYTES = 56 * 1024 * 1024

N_SB = 3 * N_HEADS_PER_MIXER * HEAD_DIM
N_FOX_QKV = 3 * N_HEADS_PER_MIXER * HEAD_DIM
N_MLA = MLA_Q_LORA + MLA_KV_LORA + MLA_ROPE
N_BRANCH = 3 * N_HEADS_PER_MIXER * HEAD_DIM
MISC_W = 512
F_COL = N_MLA


def _cparams(sem):
    return pltpu.CompilerParams(dimension_semantics=sem, vmem_limit_bytes=VMEM_LIMIT_BYTES)


def _split3(a):
    hi = a.astype(BF16)
    r1 = a - hi.astype(F32)
    mid = r1.astype(BF16)
    lo = (r1 - mid.astype(F32)).astype(BF16)
    return hi, mid, lo


def _dot(a, b):
    return jnp.dot(a, b, preferred_element_type=F32)


def _dot_nt(a, b):
    return lax.dot_general(a, b, (((1,), (1,)), ((), ())), preferred_element_type=F32)


def _dot_exact_rhs(a, u):
    hi, mid, lo = _split3(a)
    return _dot(hi, u) + _dot(mid, u) + _dot(lo, u)


def _dot_f32(a, b):
    ah = a.astype(BF16)
    al = (a - ah.astype(F32)).astype(BF16)
    bh = b.astype(BF16)
    bl = (b - bh.astype(F32)).astype(BF16)
    return _dot(ah, bh) + (_dot(ah, bl) + _dot(al, bh))


def _lane_iota(shape):
    return lax.broadcasted_iota(jnp.int32, shape, len(shape) - 1)


def _rotate_half(y, half):
    lane = _lane_iota(y.shape)
    fwd = pltpu.roll(y, half, 1)
    bwd = pltpu.roll(y, LANES - half, 1)
    return jnp.where((lane % (2 * half)) < half, -bwd, fwd)


def _log_sigmoid_pair(z):
    sp = jnp.log(1.0 + jnp.exp(-jnp.abs(z)))
    return jnp.minimum(z, 0.0) - sp, -jnp.maximum(z, 0.0) - sp


def _head_rms(o, g):
    lane = _lane_iota(o.shape)
    first = lane < HEAD_DIM
    sq = o * o
    ss_a = jnp.sum(jnp.where(first, sq, 0.0), axis=-1, keepdims=True)
    ss_b = jnp.sum(jnp.where(first, 0.0, sq), axis=-1, keepdims=True)
    ms = jnp.where(first, ss_a, ss_b) * (1.0 / HEAD_DIM)
    return o * lax.rsqrt(ms + RMS_EPS) * g


def _layernorm(y, g, b):
    mu = jnp.mean(y, axis=-1, keepdims=True)
    d = y - mu
    var = jnp.mean(d * d, axis=-1, keepdims=True)
    return d * lax.rsqrt(var + LN_EPS) * g + b


def _proj_kernel(x_ref, w_ref, cos_ref, sin_ref, *refs, outs):
    out_refs, stage_ref = refs[:len(outs)], refs[len(outs)]
    tm = x_ref.shape[1]
    xb = x_ref[0].astype(BF16)
    col = 0
    slab = 0
    for o_ref, (width, n_rope, _, r) in zip(out_refs, outs):
        for c in range(0, width, 2 * LANES):
            cw = min(2 * LANES, width - c)
            y = _dot(xb, w_ref[:, col + c:col + c + cw])
            for s in range(0, cw, LANES):
                ys = y[:, s:s + LANES]
                if c + s < n_rope:
                    ys = ys * cos_ref[...] + _rotate_half(ys, HEAD_DIM // 2) * sin_ref[...]
                if r == 1:
                    o_ref[0, 0, :, c + s:c + s + LANES] = ys.astype(o_ref.dtype)
                else:
                    st = stage_ref.at[slab % stage_ref.shape[0]]
                    slab += 1
                    st[...] = ys
                    for p in range(r):
                        o_ref[0, p, :, c + s:c + s + LANES] = (
                            st[pl.ds(p, tm // r, stride=r), :].astype(o_ref.dtype))
        col += width


PROJ_STAGE_SLABS = 4


def _proj_call(x, w, cos, sin, outs, tm):
    bsz, seq, d = x.shape
    nt = seq // tm
    in_specs = [
        pl.BlockSpec((1, tm, d), lambda b, i: (b, i, 0)),
        pl.BlockSpec(w.shape, lambda b, i: (0, 0)),
        pl.BlockSpec((tm, LANES), lambda b, i: (i, 0)),
        pl.BlockSpec((tm, LANES), lambda b, i: (i, 0)),
    ]
    out_specs = [pl.BlockSpec((1, r, tm // r, wd), lambda b, i: (b, 0, i, 0)) for wd, _, _, r in outs]
    out_shape = [jax.ShapeDtypeStruct((bsz, r, seq // r, wd), dt) for wd, _, dt, r in outs]
    return pl.pallas_call(
        functools.partial(_proj_kernel, outs=outs),
        grid=(bsz, nt), in_specs=in_specs, out_specs=out_specs, out_shape=out_shape,
        scratch_shapes=[pltpu.VMEM((PROJ_STAGE_SLABS, tm, LANES), F32)],
        compiler_params=_cparams(("parallel", "parallel")),
        name="proj",
    )(x, w, cos, sin)


def _fox_c_kernel(misc_ref, bias_ref, uinc_ref, ones_ref, out_ref, *, seq):
    lane0 = F_COL - 3 * LANES
    nblk = seq // LANES

    def body(j, carry):
        r0 = pl.multiple_of(j * LANES, LANES)
        f = misc_ref[pl.ds(r0, LANES), :] + bias_ref[...]
        lf, _ = _log_sigmoid_pair(f)
        lft = lf.T
        csum = _dot_exact_rhs(lft, uinc_ref[...]) + carry
        tot = _dot_exact_rhs(lft, ones_ref[...])
        out_ref[0, :, pl.ds(r0, LANES)] = -csum[lane0:lane0 + N_HEADS_PER_MIXER, :]
        return carry + tot

    lax.fori_loop(0, nblk, body, jnp.zeros((LANES, LANES), F32))


def _fox_c_call(misc, b_forget, bsz, seq):
    bias = jnp.zeros((1, LANES), F32).at[0, F_COL - 3 * LANES:F_COL - 3 * LANES + N_HEADS_PER_MIXER].set(b_forget)
    idx = jnp.arange(LANES)
    uinc = (idx[:, None] <= idx[None, :]).astype(BF16)
    ones = jnp.ones((LANES, LANES), BF16)
    return pl.pallas_call(
        functools.partial(_fox_c_kernel, seq=seq),
        grid=(bsz,),
        in_specs=[
            pl.BlockSpec((seq, LANES), lambda b: (b, 3)),
            pl.BlockSpec((1, LANES), lambda b: (0, 0)),
            pl.BlockSpec((LANES, LANES), lambda b: (0, 0)),
            pl.BlockSpec((LANES, LANES), lambda b: (0, 0)),
        ],
        out_specs=pl.BlockSpec((1, N_HEADS_PER_MIXER, seq), lambda b: (b, 0, 0)),
        out_shape=jax.ShapeDtypeStruct((bsz, N_HEADS_PER_MIXER, seq), F32),
        compiler_params=_cparams(("parallel",)),
        name="fox_c",
    )(misc, bias, uinc, ones)


def _mla_prep_kernel(misc_ref, gq_ref, gkv_ref, wq_ref, wk_ref, wv_ref, cos_ref, sin_ref,
                     q_ref, k_ref, v_ref):
    def rms(x, g):
        return x * lax.rsqrt(jnp.mean(x * x, axis=-1, keepdims=True) + RMS_EPS) * g

    cq = rms(misc_ref[:, 0:MLA_Q_LORA], gq_ref[...]).astype(BF16)
    ckv = rms(misc_ref[:, MLA_Q_LORA:MLA_Q_LORA + MLA_KV_LORA], gkv_ref[...]).astype(BF16)
    kr_blk = misc_ref[:, 3 * LANES:4 * LANES]
    lane = _lane_iota(kr_blk.shape)
    in_rope = (lane >= MLA_NOPE) & (lane < MLA_NOPE + MLA_ROPE)
    kr = jnp.where(in_rope, pltpu.roll(kr_blk, MLA_NOPE, 1), 0.0)
    cos = cos_ref[...]
    sin = sin_ref[...]

    def rope(y):
        return y * cos + _rotate_half(y, MLA_ROPE // 2) * sin

    q = _dot(cq, wq_ref[...])
    k = _dot(ckv, wk_ref[...])
    for h in range(N_HEADS_PER_MIXER):
        sl = slice(h * LANES, (h + 1) * LANES)
        q_ref[:, sl] = rope(q[:, sl]).astype(BF16)
        k_ref[:, sl] = rope(k[:, sl] + kr).astype(BF16)
    v_ref[...] = _dot(ckv, wv_ref[...]).astype(BF16)


def _mla_prep_call(misc, g_cq, g_ckv, wq, wk, wv, cos, sin, seq, tm):
    t = misc.shape[0]
    nper = seq // tm
    hw = N_HEADS_PER_MIXER * LANES
    vw = N_HEADS_PER_MIXER * HEAD_DIM
    full = lambda a: pl.BlockSpec(a.shape, lambda i: (0, 0))
    return pl.pallas_call(
        _mla_prep_kernel,
        grid=(t // tm,),
        in_specs=[
            pl.BlockSpec((tm, MISC_W), lambda i: (i, 0)),
            full(g_cq), full(g_ckv), full(wq), full(wk), full(wv),
            pl.BlockSpec((tm, LANES), lambda i: (i % nper, 0)),
            pl.BlockSpec((tm, LANES), lambda i: (i % nper, 0)),
        ],
        out_specs=[
            pl.BlockSpec((tm, hw), lambda i: (i, 0)),
            pl.BlockSpec((tm, hw), lambda i: (i, 0)),
            pl.BlockSpec((tm, vw), lambda i: (i, 0)),
        ],
        out_shape=[
            jax.ShapeDtypeStruct((t, hw), BF16),
            jax.ShapeDtypeStruct((t, hw), BF16),
            jax.ShapeDtypeStruct((t, vw), BF16),
        ],
        compiler_params=_cparams(("parallel",)),
        name="mla_prep",
    )(misc, g_cq, g_ckv, wq, wk, wv, cos, sin)


def _attn_kernel(*refs, mode, tq, tk, scale):
    if mode == "sb":
        q_ref, k_ref, v_ref, g_ref, u_ref, o_ref = refs
    elif mode == "fox":
        q_ref, k_ref, v_ref, g_ref, nc_ref, o_ref = refs
    else:
        q_ref, k_ref, v_ref, g_ref, o_ref = refs
    nh = N_HEADS_PER_MIXER
    i = pl.program_id(1)
    lane = _lane_iota((tq, LANES))
    first = lane < HEAD_DIM
    row = lax.broadcasted_iota(jnp.int32, (tq, tk), 0)
    colm = lax.broadcasted_iota(jnp.int32, (tq, tk), 1)

    q_heads = []
    for h in range(nh):
        if mode == "mla":
            q_heads.append(q_ref[:, h * LANES:(h + 1) * LANES])
        else:
            q2 = q_ref[:, (h // 2) * LANES:(h // 2 + 1) * LANES]
            zero = jnp.zeros_like(q2)
            q_heads.append(jnp.where(first, q2, zero) if h % 2 == 0 else jnp.where(first, zero, q2))

    def k_head(j, h):
        r0 = pl.multiple_of(j * tk, tk)
        kb = h if mode == "mla" else h // 2
        return k_ref[pl.ds(r0, tk), kb * LANES:(kb + 1) * LANES]

    def v_pair(j, p):
        r0 = pl.multiple_of(j * tk, tk)
        return v_ref[pl.ds(r0, tk), p * LANES:(p + 1) * LANES]

    jd = (i * tq) // tk
    off = i * tq - jd * tk

    if mode == "sb":
        tu = u_ref.shape[0]

        def step(j, carry, diag):
            accs, rs = carry
            new_accs, new_rs = [], []
            for p in range(nh // 2):
                v2 = v_pair(j, p)
                outs = []
                for h in (2 * p, 2 * p + 1):
                    z = _dot_nt(q_heads[h], k_head(j, h))
                    ls_pos, ls_neg = _log_sigmoid_pair(z)
                    if diag:
                        before = colm < row + off
                        ls_neg = jnp.where(before, ls_neg, 0.0)
                    lb = ls_neg.astype(BF16)
                    pieces = []
                    later = rs[h]
                    for c0 in range(tk - tu, -1, -tu):
                        c = _dot(lb[:, c0:c0 + tu], u_ref[...]) + later
                        pieces.insert(0, c)
                        later = c[:, 0:1] + ls_neg[:, c0:c0 + 1]
                    w = jnp.exp(ls_pos + jnp.concatenate(pieces, axis=1))
                    if diag:
                        w = jnp.where(before, w, 0.0)
                    outs.append(_dot(w.astype(BF16), v2))
                    new_rs.append(later)
                new_accs.append(accs[p] + jnp.where(first, outs[0], outs[1]))
            return tuple(new_accs), tuple(new_rs)

        zacc = jnp.zeros((tq, LANES), F32)
        zr = jnp.zeros((tq, 1), F32)
        carry = step(jd, ((zacc,) * (nh // 2), (zr,) * nh), True)
        carry = lax.fori_loop(0, jd, lambda n, c: step(jd - 1 - n, c, False), carry)
        outs = carry[0]
    else:
        ones = jnp.ones((tk, LANES), BF16)

        def step(j, carry, diag):
            accs, ms, ls = carry
            new_accs, new_ms, new_ls = [], [], []
            r0 = pl.multiple_of(j * tk, tk)
            for p in range(nh // 2):
                v_aug = jnp.concatenate([v_pair(j, p), ones], axis=1)
                pv, alphas = [], []
                for h in (2 * p, 2 * p + 1):
                    s = _dot_nt(q_heads[h], k_head(j, h))
                    if mode == "mla":
                        s = s * scale
                    else:
                        s = s + nc_ref[0, h:h + 1, pl.ds(r0, tk)]
                    if diag:
                        s = jnp.where(colm <= row + off, s, -jnp.inf)
                    m_new = jnp.maximum(ms[h], jnp.max(s, axis=-1, keepdims=True))
                    alpha = jnp.exp(ms[h] - m_new)
                    pvx = _dot(jnp.exp(s - m_new).astype(BF16), v_aug)
                    new_ls.append(alpha * ls[h] + pvx[:, LANES:2 * LANES])
                    new_ms.append(m_new)
                    pv.append(pvx[:, 0:LANES])
                    alphas.append(alpha)
                new_accs.append(accs[p] * jnp.where(first, alphas[0], alphas[1])
                                + jnp.where(first, pv[0], pv[1]))
            return tuple(new_accs), tuple(new_ms), tuple(new_ls)

        neg = jnp.full((tq, 1), -jnp.inf, F32)
        zacc = jnp.zeros((tq, LANES), F32)
        carry = step(jd, ((zacc,) * (nh // 2), (neg,) * nh, (zacc,) * nh), True)
        carry = lax.fori_loop(0, jd, lambda n, c: step(n, c, False), carry)
        accs, _, ls = carry
        outs = [accs[p] / jnp.where(first, ls[2 * p], ls[2 * p + 1]) for p in range(nh // 2)]
    for p in range(nh // 2):
        sl = slice(p * LANES, (p + 1) * LANES)
        o_ref[:, sl] = _head_rms(outs[p], g_ref[:, sl]).astype(o_ref.dtype)


def _attn_call(mode, q, k, v, qcol, kcol, vcol, g_flat, gcol, bsz, seq, tq, tk, extra=(), scale=1.0):
    t = bsz * seq
    nq = seq // tq
    hq = N_HEADS_PER_MIXER * HEAD_DIM
    qw = N_HEADS_PER_MIXER * LANES if mode == "mla" else hq
    assert tk % tq == 0 and seq % tk == 0
    in_specs = [
        pl.BlockSpec((tq, qw), lambda b, i: (b * nq + i, qcol)),
        pl.BlockSpec((seq, qw), lambda b, i: (b, kcol)),
        pl.BlockSpec((seq, hq), lambda b, i: (b, vcol)),
        pl.BlockSpec((1, hq), lambda b, i: (0, gcol)),
    ]
    args = [q, k, v, g_flat]
    if mode == "sb":
        (u,) = extra
        in_specs.append(pl.BlockSpec(u.shape, lambda b, i: (0, 0)))
        args.append(u)
    elif mode == "fox":
        (nc,) = extra
        in_specs.append(pl.BlockSpec((1, N_HEADS_PER_MIXER, seq), lambda b, i: (b, 0, 0)))
        args.append(nc)
    return pl.pallas_call(
        functools.partial(_attn_kernel, mode=mode, tq=tq, tk=tk, scale=scale),
        grid=(bsz, nq),
        in_specs=in_specs,
        out_specs=pl.BlockSpec((tq, hq), lambda b, i: (b * nq + i, 0)),
        out_shape=jax.ShapeDtypeStruct((t, hq), BF16),
        compiler_params=_cparams(("parallel", "arbitrary")),
        name=f"attn_{mode}",
    )(*args)


def _dil_kernel(qkv_ref, o_ref, *, tq, tk, win):
    i = pl.program_id(2)
    lane = _lane_iota((tq, LANES))
    first = lane < HEAD_DIM
    hq = N_HEADS_PER_MIXER * HEAD_DIM
    r0 = pl.multiple_of(i * tq, tq)
    k0 = pl.multiple_of(jnp.maximum(i - 1, 0) * tq, tq)
    row = lax.broadcasted_iota(jnp.int32, (tq, tk), 0)
    colm = lax.broadcasted_iota(jnp.int32, (tq, tk), 1)
    delta = row + (r0 - k0) - colm
    band = jnp.abs(2 * delta - win) <= win
    ones = jnp.ones((tk, LANES), BF16)
    for p in range(N_HEADS_PER_MIXER // 2):
        q2 = qkv_ref[0, 0, pl.ds(r0, tq), p * LANES:(p + 1) * LANES]
        k2 = qkv_ref[0, 0, pl.ds(k0, tk), hq + p * LANES:hq + (p + 1) * LANES]
        v_aug = jnp.concatenate(
            [qkv_ref[0, 0, pl.ds(k0, tk), 2 * hq + p * LANES:2 * hq + (p + 1) * LANES], ones], axis=1)
        zero = jnp.zeros_like(q2)
        outs = []
        lses = []
        for hh in range(2):
            qh = jnp.where(first, q2, zero) if hh == 0 else jnp.where(first, zero, q2)
            s = jnp.where(band, _dot_nt(qh, k2), -jnp.inf)
            m = jnp.max(s, axis=-1, keepdims=True)
            pvx = _dot(jnp.exp(s - m).astype(BF16), v_aug)
            l = pvx[:, LANES:2 * LANES]
            outs.append(pvx[:, 0:LANES] / l)
            lses.append(m + jnp.log(l))
        o_ref[0, :, p * LANES:(p + 1) * LANES] = jnp.where(first, outs[0], outs[1])
        o_ref[0, :, hq + p * LANES:hq + (p + 1) * LANES] = jnp.where(first, lses[0], lses[1])


def _dil_call(qkv, tqs, win):
    bsz, r, n, w = qkv.shape
    hq = N_HEADS_PER_MIXER * HEAD_DIM
    tk = 2 * tqs if n >= 2 * tqs else tqs
    assert tqs >= win and n % tqs == 0 and (tk == 2 * tqs or n == tqs)
    return pl.pallas_call(
        functools.partial(_dil_kernel, tq=tqs, tk=tk, win=win),
        grid=(bsz, r, n // tqs),
        in_specs=[pl.BlockSpec((1, 1, n, w), lambda b, p, c: (b, p, 0, 0))],
        out_specs=pl.BlockSpec((1, tqs, 2 * hq), lambda b, p, c: (b, c, p)),
        out_shape=jax.ShapeDtypeStruct((bsz, n, r * 2 * hq), F32),
        compiler_params=_cparams(("parallel", "parallel", "arbitrary")),
        name=f"dil_r{r}",
    )(qkv)


def _post_kernel(x_ref, osb_ref, ofox_ref, omla_ref, d1_ref, d2_ref, d3_ref, gd_ref, wo_ref,
                 lng_ref, lnb_ref, wr_ref, br_ref, x1_ref, x1t_ref, route_ref, stage_ref, *, alpha, dils):
    hq = N_HEADS_PER_MIXER * HEAD_DIM
    tm = x_ref.shape[0]
    h = _dot(osb_ref[...], wo_ref[0:hq, :])
    h += _dot(ofox_ref[...], wo_ref[hq:2 * hq, :])
    h += _dot(omla_ref[...], wo_ref[2 * hq:3 * hq, :])

    def token_rows(g, d_ref, r, c0):
        if r == 1:
            return d_ref[:, c0:c0 + LANES]
        st = stage_ref.at[g, c0 // LANES]
        for q in range(r):
            st[pl.ds(q, tm // r, stride=r), :] = d_ref[:, q * 2 * hq + c0:q * 2 * hq + c0 + LANES]
        return st[...]

    d_refs = (d1_ref, d2_ref, d3_ref)
    for p in range(N_HEADS_PER_MIXER // 2):
        sl = slice(p * LANES, (p + 1) * LANES)
        lses = [token_rows(g, d_refs[g], dils[g], hq + p * LANES) for g in range(len(dils))]
        vals = [token_rows(g, d_refs[g], dils[g], p * LANES) for g in range(len(dils))]
        l1, l2, l3 = lses
        m = jnp.maximum(jnp.maximum(l1, l2), l3)
        e1, e2, e3 = jnp.exp(l1 - m), jnp.exp(l2 - m), jnp.exp(l3 - m)
        inv = 1.0 / (e1 + e2 + e3)
        od = (e1 * inv) * vals[0] + (e2 * inv) * vals[1] + (e3 * inv) * vals[2]
        od = _head_rms(od, gd_ref[:, sl]).astype(BF16)
        h += _dot(od, wo_ref[3 * hq + p * LANES:3 * hq + (p + 1) * LANES, :])
    x1 = _layernorm(alpha * x_ref[...] + h, lng_ref[...], lnb_ref[...])
    x1_ref[...] = x1
    for j in range(x1.shape[1] // LANES):
        x1t_ref[pl.ds(j, tm, stride=SUBLANES), :] = x1[:, j * LANES:(j + 1) * LANES]

    logits = _dot_f32(x1, wr_ref[...]) + br_ref[...]
    lane = _lane_iota(logits.shape)
    lanef = lane.astype(F32)
    big = float(LANES)
    ninf = -jnp.inf
    gl = jnp.where(lane < N_GROUPS, logits, ninf)
    gmax = jnp.max(gl, axis=-1, keepdims=True)
    gsel = jnp.min(jnp.where(gl == gmax, lanef, big), axis=-1, keepdims=True)
    gw = 1.0 / jnp.sum(jnp.exp(gl - gmax), axis=-1, keepdims=True)
    e_lo = N_GROUPS + EXPERTS_PER_GROUP * gsel
    in_grp = (lanef >= e_lo) & (lanef < e_lo + EXPERTS_PER_GROUP)
    el = jnp.where(in_grp, logits, ninf)
    t1 = jnp.max(el, axis=-1, keepdims=True)
    i1 = jnp.min(jnp.where(el == t1, lanef, big), axis=-1, keepdims=True)
    el2 = jnp.where(lanef == i1, ninf, el)
    t2 = jnp.max(el2, axis=-1, keepdims=True)
    i2 = jnp.min(jnp.where(el2 == t2, lanef, big), axis=-1, keepdims=True)
    ex = jnp.exp(t2 - t1)
    den = 1.0 + ex
    g1 = gw / den
    g2 = gw * ex / den
    out = jnp.where(lane == 0, i1 - N_GROUPS,
                    jnp.where(lane == 1, i2 - N_GROUPS,
                              jnp.where(lane == 2, g1, jnp.where(lane == 3, g2, 0.0))))
    route_ref[...] = out


def _post_call(x, osb, ofox, omla, dil_outs, dils, g_dil, wo, lng, lnb, wr, br, alpha, tm):
    t, d = x.shape
    hq = N_HEADS_PER_MIXER * HEAD_DIM
    row = lambda w: pl.BlockSpec((tm, w), lambda i: (i, 0))
    full = lambda a: pl.BlockSpec(a.shape, lambda i: (0, 0))
    dil_specs = [pl.BlockSpec((tm // r, r * 2 * hq), lambda i: (i, 0)) for r in dils]
    return pl.pallas_call(
        functools.partial(_post_kernel, alpha=alpha, dils=dils),
        grid=(t // tm,),
        in_specs=[row(d), row(hq), row(hq), row(hq)] + dil_specs
                 + [full(g_dil), full(wo), full(lng), full(lnb), full(wr), full(br)],
        out_specs=[row(d), pl.BlockSpec((tm * SUBLANES, LANES), lambda i: (i, 0)), row(LANES)],
        out_shape=[jax.ShapeDtypeStruct((t, d), F32), jax.ShapeDtypeStruct((t * SUBLANES, LANES), F32),
                   jax.ShapeDtypeStruct((t, LANES), F32)],
        scratch_shapes=[pltpu.VMEM((len(dils), 2 * hq // LANES, tm, LANES), F32)],
        compiler_params=_cparams(("parallel",)),
        name="post_mixer",
    )(x, osb, ofox, omla, *dil_outs, g_dil, wo, lng, lnb, wr, br)


DMA_UNROLL = 8


def _moe_kernel(ce_ref, nv_ref, src_ref, srcn_ref, dst_ref, x_hbm, w1_ref, w3_ref, w2_ref, y_hbm,
                xs_ref, ys_ref, gsem, ssem):
    c = pl.program_id(0)
    nc = pl.num_programs(0)
    slot = c % 2
    other = 1 - slot
    nv = nv_ref[c]
    nv_next = jnp.where(c + 1 < nc, nv_ref[jnp.minimum(c + 1, nc - 1)], 0)
    nv_prev = jnp.where(c > 0, nv_ref[jnp.maximum(c - 1, 0)], 0)

    def gather_copy(idx_ref, i, s):
        return pltpu.make_async_copy(
            x_hbm.at[pl.ds(pl.multiple_of(idx_ref[0, 0, i] * SUBLANES, SUBLANES), SUBLANES), :],
            xs_ref.at[s, pl.ds(pl.multiple_of(i * SUBLANES, SUBLANES), SUBLANES), :], gsem.at[s])

    def scatter_copy(i, s):
        return pltpu.make_async_copy(ys_ref.at[s, pl.ds(i, 1), :],
                                     y_hbm.at[pl.ds(dst_ref[0, 0, i], 1), :], ssem.at[s])

    def start_rows(n, make):
        ngrp = n // DMA_UNROLL

        def grp(g, _):
            base = pl.multiple_of(g * DMA_UNROLL, DMA_UNROLL)
            for k in range(DMA_UNROLL):
                make(base + k).start()
            return 0

        def one(i, _):
            make(i).start()
            return 0

        lax.fori_loop(0, ngrp, grp, 0)
        lax.fori_loop(ngrp * DMA_UNROLL, n, one, 0)

    def wait_rows(n, make_row, make_block):
        @pl.when(n == MOE_BLOCK)
        def _():
            make_block().wait()

        @pl.when(n < MOE_BLOCK)
        def _():
            def one(i, _):
                make_row(i).wait()
                return 0
            lax.fori_loop(0, n, one, 0)

    def gather_block(s):
        return pltpu.make_async_copy(x_hbm.at[pl.ds(0, MOE_BLOCK * SUBLANES), :], xs_ref.at[s], gsem.at[s])

    def scatter_block(s):
        return pltpu.make_async_copy(ys_ref.at[s], y_hbm.at[pl.ds(0, MOE_BLOCK), :], ssem.at[s])

    @pl.when(c == 0)
    def _():
        xs_ref[...] = jnp.zeros_like(xs_ref)
        start_rows(nv, lambda i: gather_copy(src_ref, i, 0))

    start_rows(nv_next, lambda i: gather_copy(srcn_ref, i, other))
    wait_rows(nv, lambda i: gather_copy(src_ref, i, slot), lambda: gather_block(slot))

    @pl.when(nv > 0)
    def _():
        xt = xs_ref.at[slot]
        a = b = None
        for jj in range(0, xt.shape[0] // MOE_BLOCK, 2):
            xb = jnp.concatenate([xt[pl.ds(jj, MOE_BLOCK, stride=SUBLANES), :],
                                  xt[pl.ds(jj + 1, MOE_BLOCK, stride=SUBLANES), :]], axis=1).astype(BF16)
            ws = slice(jj * LANES, (jj + 2) * LANES)
            da, db = _dot(xb, w1_ref[0, ws, :]), _dot(xb, w3_ref[0, ws, :])
            a, b = (da, db) if a is None else (a + da, b + db)
        hid = (a / (1.0 + jnp.exp(-a)) * b).astype(BF16)
        ys_ref[slot] = _dot(hid, w2_ref[0])

    wait_rows(nv_prev, lambda i: scatter_copy(i, other), lambda: scatter_block(other))
    start_rows(nv, lambda i: scatter_copy(i, slot))

    @pl.when(c == nc - 1)
    def _():
        wait_rows(nv, lambda i: scatter_copy(i, slot), lambda: scatter_block(slot))


def _moe_call(chunk_expert, n_valid, src, dst, x1t, w1, w3, w2, n_rows_out):
    n_chunks = chunk_expert.shape[0]
    d, de = w1.shape[1], w1.shape[2]
    assert d == SUBLANES * LANES and x1t.shape[1] == LANES
    grid_spec = pltpu.PrefetchScalarGridSpec(
        num_scalar_prefetch=2,
        grid=(n_chunks,),
        in_specs=[
            pl.BlockSpec((1, 1, MOE_BLOCK), lambda c, ce, nv: (c, 0, 0), memory_space=pltpu.SMEM),
            pl.BlockSpec((1, 1, MOE_BLOCK), lambda c, ce, nv: (jnp.minimum(c + 1, n_chunks - 1), 0, 0),
                         memory_space=pltpu.SMEM),
            pl.BlockSpec((1, 1, MOE_BLOCK), lambda c, ce, nv: (c, 0, 0), memory_space=pltpu.SMEM),
            pl.BlockSpec(memory_space=pl.ANY),
            pl.BlockSpec((1, d, de), lambda c, ce, nv: (ce[c], 0, 0)),
            pl.BlockSpec((1, d, de), lambda c, ce, nv: (ce[c], 0, 0)),
            pl.BlockSpec((1, de, d), lambda c, ce, nv: (ce[c], 0, 0)),
        ],
        out_specs=pl.BlockSpec(memory_space=pl.ANY),
        scratch_shapes=[
            pltpu.VMEM((2, MOE_BLOCK * SUBLANES, LANES), F32),
            pltpu.VMEM((2, MOE_BLOCK, d), F32),
            pltpu.SemaphoreType.DMA((2,)),
            pltpu.SemaphoreType.DMA((2,)),
        ],
    )
    return pl.pallas_call(
        _moe_kernel,
        grid_spec=grid_spec,
        out_shape=jax.ShapeDtypeStruct((n_rows_out, d), F32),
        compiler_params=_cparams(("arbitrary",)),
        name="moe_experts",
    )(chunk_expert, n_valid, src, src, dst, x1t, w1, w3, w2)


def _combine_kernel(x1_ref, ya_ref, yb_ref, route_ref, lng_ref, lnb_ref, o_ref, *, alpha):
    g1 = route_ref[:, 2:3]
    g2 = route_ref[:, 3:4]
    m = g1 * ya_ref[...] + g2 * yb_ref[...]
    o_ref[...] = _layernorm(alpha * x1_ref[...] + m, lng_ref[...], lnb_ref[...])


def _combine_call(x1, y, route, lng, lnb, alpha, tm):
    t, d = x1.shape
    nt = t // tm
    full = lambda a: pl.BlockSpec(a.shape, lambda i: (0, 0))
    return pl.pallas_call(
        functools.partial(_combine_kernel, alpha=alpha),
        grid=(nt,),
        in_specs=[pl.BlockSpec((tm, d), lambda i: (i, 0)),
                  pl.BlockSpec((tm, d), lambda i: (i, 0)),
                  pl.BlockSpec((tm, d), lambda i: (i + nt, 0)),
                  pl.BlockSpec((tm, LANES), lambda i: (i, 0)),
                  full(lng), full(lnb)],
        out_specs=pl.BlockSpec((tm, d), lambda i: (i, 0)),
        out_shape=jax.ShapeDtypeStruct((t, d), F32),
        compiler_params=_cparams(("parallel",)),
        name="moe_combine",
    )(x1, y, y, route, lng, lnb)


def _rope_tables(seq, dim, lane_lo):
    half = dim // 2
    inv_freq = ROPE_THETA ** (-jnp.arange(half, dtype=F32) / half)
    ang = jnp.arange(seq, dtype=F32)[:, None] * inv_freq[None, :]
    cos = jnp.concatenate([jnp.cos(ang), jnp.cos(ang)], -1)
    sin = jnp.concatenate([jnp.sin(ang), jnp.sin(ang)], -1)
    if lane_lo == 0:
        reps = LANES // dim
        return jnp.tile(cos, (1, reps)), jnp.tile(sin, (1, reps))
    cos_t = jnp.ones((seq, LANES), F32).at[:, lane_lo:lane_lo + dim].set(cos)
    sin_t = jnp.zeros((seq, LANES), F32).at[:, lane_lo:lane_lo + dim].set(sin)
    return cos_t, sin_t


def _dispatch_tables(route, n_tok):
    expert_id = route[:, 0:TOP_K].astype(jnp.int32).reshape(-1)
    n_assign = n_tok * TOP_K
    n_slots = n_assign + N_EXPERTS * MOE_BLOCK
    n_chunks = n_slots // MOE_BLOCK
    onehot = (expert_id[:, None] == jnp.arange(N_EXPERTS, dtype=jnp.int32)[None, :]).astype(jnp.int32)
    ranks = jnp.cumsum(onehot, axis=0) - onehot
    rank = jnp.sum(ranks * onehot, axis=1)
    counts = jnp.sum(onehot, axis=0)
    padded = (counts + MOE_BLOCK - 1) // MOE_BLOCK * MOE_BLOCK
    pad_end = jnp.cumsum(padded)
    pad_start = pad_end - padded
    dest = pad_start[expert_id] + rank
    assign = jnp.arange(n_assign, dtype=jnp.int32)
    slot_assign = jnp.zeros((n_slots,), jnp.int32).at[dest].set(assign)
    src = slot_assign // TOP_K
    dst = (slot_assign % TOP_K) * n_tok + src
    chunk_start = jnp.arange(n_chunks, dtype=jnp.int32) * MOE_BLOCK
    chunk_expert = jnp.minimum(jnp.searchsorted(pad_end, chunk_start, side="right"),
                               N_EXPERTS - 1).astype(jnp.int32)
    n_valid = jnp.clip(pad_start[chunk_expert] + counts[chunk_expert] - chunk_start,
                       0, MOE_BLOCK).astype(jnp.int32)
    return (chunk_expert, n_valid, src.reshape(n_chunks, 1, MOE_BLOCK),
            dst.reshape(n_chunks, 1, MOE_BLOCK), n_assign)


def _pick_tile(n, pref):
    t = pref
    while n % t:
        t //= 2
    return t


def kernel(x, w_in, b_forget, g_cq, g_ckv, w_uq, w_ukv, g_head, w_out, ln1_g, ln1_b,
           w_group, b_group, w_expert, b_expert, w1, w3, w2, ln2_g, ln2_b):
    bsz, seq, d = x.shape
    depth = w_in.shape[0]
    t = bsz * seq
    alpha = (2.0 * depth) ** 0.25
    hq = N_HEADS_PER_MIXER * HEAD_DIM
    qk_scale = HEAD_DIM ** -0.5
    mla_scale = (MLA_NOPE + MLA_ROPE) ** -0.5
    win = DIL_BRANCHES[0][0]
    assert all(w // r == win for w, r in DIL_BRANCHES)
    assert seq % (DIL_BRANCHES[-1][1] * win) == 0 and d % LANES == 0

    cos64, sin64 = _rope_tables(seq, HEAD_DIM, 0)
    cos_m, sin_m = _rope_tables(seq, MLA_ROPE, MLA_NOPE)
    tq = _pick_tile(seq, 256)
    idx = jnp.arange(tq)
    u_sb = (idx[:, None] > idx[None, :]).astype(BF16)
    tk_sb = _pick_tile(seq, 2 * tq)
    tk_sm = _pick_tile(seq, 4 * tq)

    for l in range(depth):
        wl = w_in[l]
        o_fox, o_mla, o_dil = N_SB, N_SB + N_FOX_QKV + N_HEADS_PER_MIXER, N_SB + N_FOX_QKV + N_HEADS_PER_MIXER + N_MLA
        qs = lambda w: w.at[:, 0:hq].multiply(qk_scale)
        w_sb = qs(wl[:, 0:N_SB])
        w_fx = qs(wl[:, o_fox:o_fox + N_FOX_QKV])
        w_f = wl[:, o_fox + N_FOX_QKV:o_mla]
        w_ml = wl[:, o_mla:o_dil]
        wd = wl[:, o_dil:].reshape(d, 3, len(DIL_BRANCHES), hq)
        w_br = [qs(jnp.concatenate([wd[:, 0, g], wd[:, 1, g], wd[:, 2, g]], axis=1)) for g in range(len(DIL_BRANCHES))]
        w_misc = jnp.concatenate([w_ml, w_f, jnp.zeros((d, MISC_W - N_MLA - N_HEADS_PER_MIXER), F32)], axis=1)
        w_tok = jnp.concatenate([w_sb, w_fx, w_misc] + w_br, axis=1).astype(BF16)

        wq = jnp.pad(w_uq[l].reshape(MLA_Q_LORA, N_HEADS_PER_MIXER, MLA_NOPE + MLA_ROPE),
                     ((0, 0), (0, 0), (0, LANES - MLA_NOPE - MLA_ROPE))).reshape(MLA_Q_LORA, -1).astype(BF16)
        wkv = w_ukv[l].reshape(MLA_KV_LORA, N_HEADS_PER_MIXER, MLA_NOPE + HEAD_DIM)
        wk = jnp.pad(wkv[:, :, :MLA_NOPE], ((0, 0), (0, 0), (0, LANES - MLA_NOPE))).reshape(MLA_KV_LORA, -1).astype(BF16)
        wv = wkv[:, :, MLA_NOPE:].reshape(MLA_KV_LORA, -1).astype(BF16)
        g_flat = g_head[l].reshape(1, -1)
        wr = jnp.concatenate([w_group[l], w_expert[l],
                              jnp.zeros((d, LANES - N_GROUPS - N_EXPERTS), F32)], axis=1)
        br = jnp.concatenate([b_group[l], b_expert[l],
                              jnp.zeros((LANES - N_GROUPS - N_EXPERTS,), F32)]).reshape(1, LANES)

        tm = _pick_tile(seq, 512)
        sb, fx, misc, *qkv_br = _proj_call(
            x, w_tok, cos64, sin64,
            ((N_SB, 0, BF16, 1), (N_FOX_QKV, 0, BF16, 1), (MISC_W, 0, F32, 1))
            + tuple((N_BRANCH, 2 * hq, BF16, r) for _, r in DIL_BRANCHES), tm)
        sb = sb.reshape(t, N_SB)
        fx = fx.reshape(t, N_FOX_QKV)
        misc = misc.reshape(t, MISC_W)

        o_sb = _attn_call("sb", sb, sb, sb, 0, 1, 2, g_flat, 0, bsz, seq, tq, tk_sb, extra=(u_sb,))
        neg_c = _fox_c_call(misc, b_forget[l], bsz, seq)
        o_fx = _attn_call("fox", fx, fx, fx, 0, 1, 2, g_flat, 1, bsz, seq, tq, tk_sm, extra=(neg_c,))
        mq, mk, mv = _mla_prep_call(misc, g_cq[l].reshape(1, -1), g_ckv[l].reshape(1, -1), wq, wk, wv,
                                    cos_m, sin_m, seq, tm)
        o_ml = _attn_call("mla", mq, mk, mv, 0, 0, 0, g_flat, 2, bsz, seq, tq, tk_sm, scale=mla_scale)

        dil = []
        for g, (_, r) in enumerate(DIL_BRANCHES):
            n = seq // r
            og = _dil_call(qkv_br[g], max(win, min(tq, n // 2)), win)
            dil.append(og.reshape(t // r, r * 2 * hq))

        x1, x1t, route = _post_call(
            x.reshape(t, d), o_sb, o_fx, o_ml, dil, tuple(r for _, r in DIL_BRANCHES),
            g_flat[:, 3 * hq:], w_out[l].astype(BF16),
            ln1_g[l].reshape(1, d), ln1_b[l].reshape(1, d), wr, br, alpha, _pick_tile(t, 256))

        chunk_expert, n_valid, src, dst, n_rows = _dispatch_tables(route, t)
        y = _moe_call(chunk_expert, n_valid, src, dst, x1t,
                      w1[l].astype(BF16), w3[l].astype(BF16), w2[l].astype(BF16), n_rows)
        x = _combine_call(x1, y, route,
                          ln2_g[l].reshape(1, d), ln2_b[l].reshape(1, d), alpha, _pick_tile(t, 256)).reshape(bsz, seq, d)
    return x
```

```python
import functools

import jax
import jax.numpy as jnp
import numpy as np
from jax import lax
from jax.experimental import pallas as pl
from jax.experimental.pallas import tpu as pltpu

F32 = jnp.float32
BF16 = jnp.bfloat16

HEAD_DIM = 64
N_HEADS_PER_MIXER = 4
MLA_Q_LORA = 256
MLA_KV_LORA = 128
MLA_NOPE = 64
MLA_ROPE = 32
DIL_BRANCHES = ((128, 1), (512, 4), (2048, 16))
ROPE_THETA = 10000.0
N_GROUPS = 4
EXPERTS_PER_GROUP = 4
N_EXPERTS = N_GROUPS * EXPERTS_PER_GROUP
TOP_K = 2
MOE_BLOCK = 256
LN_EPS = 1e-5
RMS_EPS = 1e-6

LANES = 128
SUBLANES = 8
VMEM_LIMIT_BYTES = 56 * 1024 * 1024

N_SB = 3 * N_HEADS_PER_MIXER * HEAD_DIM
N_FOX_QKV = 3 * N_HEADS_PER_MIXER * HEAD_DIM
N_MLA = MLA_Q_LORA + MLA_KV_LORA + MLA_ROPE
N_BRANCH = 3 * N_HEADS_PER_MIXER * HEAD_DIM
MISC_W = 512
F_COL = N_MLA


def _cparams(sem):
    return pltpu.CompilerParams(dimension_semantics=sem, vmem_limit_bytes=VMEM_LIMIT_BYTES)


def _split3(a):
    hi = a.astype(BF16)
    r1 = a - hi.astype(F32)
    mid = r1.astype(BF16)
    lo = (r1 - mid.astype(F32)).astype(BF16)
    return hi, mid, lo


def _dot(a, b):
    return jnp.dot(a, b, preferred_element_type=F32)


def _dot_nt(a, b):
    return lax.dot_general(a, b, (((1,), (1,)), ((), ())), preferred_element_type=F32)


def _dot_exact_rhs(a, u):
    hi, mid, lo = _split3(a)
    return _dot(hi, u) + _dot(mid, u) + _dot(lo, u)


def _dot_f32(a, b):
    ah = a.astype(BF16)
    al = (a - ah.astype(F32)).astype(BF16)
    bh = b.astype(BF16)
    bl = (b - bh.astype(F32)).astype(BF16)
    return _dot(ah, bh) + (_dot(ah, bl) + _dot(al, bh))


def _lane_iota(shape):
    return lax.broadcasted_iota(jnp.int32, shape, len(shape) - 1)


def _rotate_half(y, half):
    lane = _lane_iota(y.shape)
    fwd = pltpu.roll(y, half, 1)
    bwd = pltpu.roll(y, LANES - half, 1)
    return jnp.where((lane % (2 * half)) < half, -bwd, fwd)


def _log_sigmoid_pair(z):
    sp = jnp.log(1.0 + jnp.exp(-jnp.abs(z)))
    return jnp.minimum(z, 0.0) - sp, -jnp.maximum(z, 0.0) - sp


def _head_rms(o, g):
    lane = _lane_iota(o.shape)
    first = lane < HEAD_DIM
    sq = o * o
    ss_a = jnp.sum(jnp.where(first, sq, 0.0), axis=-1, keepdims=True)
    ss_b = jnp.sum(jnp.where(first, 0.0, sq), axis=-1, keepdims=True)
    ms = jnp.where(first, ss_a, ss_b) * (1.0 / HEAD_DIM)
    return o * lax.rsqrt(ms + RMS_EPS) * g


def _layernorm(y, g, b):
    mu = jnp.mean(y, axis=-1, keepdims=True)
    d = y - mu
    var = jnp.mean(d * d, axis=-1, keepdims=True)
    return d * lax.rsqrt(var + LN_EPS) * g + b


def _proj_kernel(x_ref, w_ref, cos_ref, sin_ref, *refs, outs):
    out_refs, stage_ref = refs[:len(outs)], refs[len(outs)]
    tm = x_ref.shape[1]
    xb = x_ref[0].astype(BF16)
    col = 0
    slab = 0
    for o_ref, (width, n_rope, _, r) in zip(out_refs, outs):
        for c in range(0, width, 2 * LANES):
            cw = min(2 * LANES, width - c)
            y = _dot(xb, w_ref[:, col + c:col + c + cw])
            for s in range(0, cw, LANES):
                ys = y[:, s:s + LANES]
                if c + s < n_rope:
                    ys = ys * cos_ref[...] + _rotate_half(ys, HEAD_DIM // 2) * sin_ref[...]
                if r == 1:
                    o_ref[0, 0, :, c + s:c + s + LANES] = ys.astype(o_ref.dtype)
                else:
                    st = stage_ref.at[slab % stage_ref.shape[0]]
                    slab += 1
                    st[...] = ys
                    for p in range(r):
                        o_ref[0, p, :, c + s:c + s + LANES] = (
                            st[pl.ds(p, tm // r, stride=r), :].astype(o_ref.dtype))
        col += width


PROJ_STAGE_SLABS = 4


def _proj_call(x, w, cos, sin, outs, tm):
    bsz, seq, d = x.shape
    nt = seq // tm
    in_specs = [
        pl.BlockSpec((1, tm, d), lambda b, i: (b, i, 0)),
        pl.BlockSpec(w.shape, lambda b, i: (0, 0)),
        pl.BlockSpec((tm, LANES), lambda b, i: (i, 0)),
        pl.BlockSpec((tm, LANES), lambda b, i: (i, 0)),
    ]
    out_specs = [pl.BlockSpec((1, r, tm // r, wd), lambda b, i: (b, 0, i, 0)) for wd, _, _, r in outs]
    out_shape = [jax.ShapeDtypeStruct((bsz, r, seq // r, wd), dt) for wd, _, dt, r in outs]
    return pl.pallas_call(
        functools.partial(_proj_kernel, outs=outs),
        grid=(bsz, nt), in_specs=in_specs, out_specs=out_specs, out_shape=out_shape,
        scratch_shapes=[pltpu.VMEM((PROJ_STAGE_SLABS, tm, LANES), F32)],
        compiler_params=_cparams(("parallel", "parallel")),
        name="proj",
    )(x, w, cos, sin)


def _fox_c_kernel(misc_ref, bias_ref, uinc_ref, ones_ref, out_ref, *, seq):
    lane0 = F_COL - 3 * LANES
    nblk = seq // LANES

    def body(j, carry):
        r0 = pl.multiple_of(j * LANES, LANES)
        f = misc_ref[pl.ds(r0, LANES), :] + bias_ref[...]
        lf, _ = _log_sigmoid_pair(f)
        lft = lf.T
        csum = _dot_exact_rhs(lft, uinc_ref[...]) + carry
        tot = _dot_exact_rhs(lft, ones_ref[...])
        out_ref[0, :, pl.ds(r0, LANES)] = -csum[lane0:lane0 + N_HEADS_PER_MIXER, :]
        return carry + tot

    lax.fori_loop(0, nblk, body, jnp.zeros((LANES, LANES), F32))


def _fox_c_call(misc, b_forget, bsz, seq):
    bias = jnp.zeros((1, LANES), F32).at[0, F_COL - 3 * LANES:F_COL - 3 * LANES + N_HEADS_PER_MIXER].set(b_forget)
    idx = jnp.arange(LANES)
    uinc = (idx[:, None] <= idx[None, :]).astype(BF16)
    ones = jnp.ones((LANES, LANES), BF16)
    return pl.pallas_call(
        functools.partial(_fox_c_kernel, seq=seq),
        grid=(bsz,),
        in_specs=[
            pl.BlockSpec((seq, LANES), lambda b: (b, 3)),
            pl.BlockSpec((1, LANES), lambda b: (0, 0)),
            pl.BlockSpec((LANES, LANES), lambda b: (0, 0)),
            pl.BlockSpec((LANES, LANES), lambda b: (0, 0)),
        ],
        out_specs=pl.BlockSpec((1, N_HEADS_PER_MIXER, seq), lambda b: (b, 0, 0)),
        out_shape=jax.ShapeDtypeStruct((bsz, N_HEADS_PER_MIXER, seq), F32),
        compiler_params=_cparams(("parallel",)),
        name="fox_c",
    )(misc, bias, uinc, ones)


def _mla_prep_kernel(misc_ref, gq_ref, gkv_ref, wq_ref, wk_ref, wv_ref, cos_ref, sin_ref,
                     q_ref, k_ref, v_ref):
    def rms(x, g):
        return x * lax.rsqrt(jnp.mean(x * x, axis=-1, keepdims=True) + RMS_EPS) * g

    cq = rms(misc_ref[:, 0:MLA_Q_LORA], gq_ref[...]).astype(BF16)
    ckv = rms(misc_ref[:, MLA_Q_LORA:MLA_Q_LORA + MLA_KV_LORA], gkv_ref[...]).astype(BF16)
    kr_blk = misc_ref[:, 3 * LANES:4 * LANES]
    lane = _lane_iota(kr_blk.shape)
    in_rope = (lane >= MLA_NOPE) & (lane < MLA_NOPE + MLA_ROPE)
    kr = jnp.where(in_rope, pltpu.roll(kr_blk, MLA_NOPE, 1), 0.0)
    cos = cos_ref[...]
    sin = sin_ref[...]

    def rope(y):
        return y * cos + _rotate_half(y, MLA_ROPE // 2) * sin

    q = _dot(cq, wq_ref[...])
    k = _dot(ckv, wk_ref[...])
    for h in range(N_HEADS_PER_MIXER):
        sl = slice(h * LANES, (h + 1) * LANES)
        q_ref[:, sl] = rope(q[:, sl]).astype(BF16)
        k_ref[:, sl] = rope(k[:, sl] + kr).astype(BF16)
    v_ref[...] = _dot(ckv, wv_ref[...]).astype(BF16)


def _mla_prep_call(misc, g_cq, g_ckv, wq, wk, wv, cos, sin, seq, tm):
    t = misc.shape[0]
    nper = seq // tm
    hw = N_HEADS_PER_MIXER * LANES
    vw = N_HEADS_PER_MIXER * HEAD_DIM
    full = lambda a: pl.BlockSpec(a.shape, lambda i: (0, 0))
    return pl.pallas_call(
        _mla_prep_kernel,
        grid=(t // tm,),
        in_specs=[
            pl.BlockSpec((tm, MISC_W), lambda i: (i, 0)),
            full(g_cq), full(g_ckv), full(wq), full(wk), full(wv),
            pl.BlockSpec((tm, LANES), lambda i: (i % nper, 0)),
            pl.BlockSpec((tm, LANES), lambda i: (i % nper, 0)),
        ],
        out_specs=[
            pl.BlockSpec((tm, hw), lambda i: (i, 0)),
            pl.BlockSpec((tm, hw), lambda i: (i, 0)),
            pl.BlockSpec((tm, vw), lambda i: (i, 0)),
        ],
        out_shape=[
            jax.ShapeDtypeStruct((t, hw), BF16),
            jax.ShapeDtypeStruct((t, hw), BF16),
            jax.ShapeDtypeStruct((t, vw), BF16),
        ],
        compiler_params=_cparams(("parallel",)),
        name="mla_prep",
    )(misc, g_cq, g_ckv, wq, wk, wv, cos, sin)


def _attn_kernel(*refs, mode, tq, tk, scale):
    if mode == "sb":
        q_ref, k_ref, v_ref, g_ref, u_ref, o_ref = refs
    elif mode == "fox":
        q_ref, k_ref, v_ref, g_ref, nc_ref, o_ref = refs
    else:
        q_ref, k_ref, v_ref, g_ref, o_ref = refs
    nh = N_HEADS_PER_MIXER
    i = pl.program_id(1)
    lane = _lane_iota((tq, LANES))
    first = lane < HEAD_DIM
    row = lax.broadcasted_iota(jnp.int32, (tq, tk), 0)
    colm = lax.broadcasted_iota(jnp.int32, (tq, tk), 1)

    q_heads = []
    for h in range(nh):
        if mode == "mla":
            q_heads.append(q_ref[:, h * LANES:(h + 1) * LANES])
        else:
            q2 = q_ref[:, (h // 2) * LANES:(h // 2 + 1) * LANES]
            zero = jnp.zeros_like(q2)
            q_heads.append(jnp.where(first, q2, zero) if h % 2 == 0 else jnp.where(first, zero, q2))

    def k_head(j, h):
        r0 = pl.multiple_of(j * tk, tk)
        kb = h if mode == "mla" else h // 2
        return k_ref[pl.ds(r0, tk), kb * LANES:(kb + 1) * LANES]

    def v_pair(j, p):
        r0 = pl.multiple_of(j * tk, tk)
        return v_ref[pl.ds(r0, tk), p * LANES:(p + 1) * LANES]

    jd = (i * tq) // tk
    off = i * tq - jd * tk

    if mode == "sb":
        tu = u_ref.shape[0]

        def step(j, carry, diag):
            accs, rs = carry
            new_accs, new_rs = [], []
            for p in range(nh // 2):
                v2 = v_pair(j, p)
                outs = []
                for h in (2 * p, 2 * p + 1):
                    z = _dot_nt(q_heads[h], k_head(j, h))
                    ls_pos, ls_neg = _log_sigmoid_pair(z)
                    if diag:
                        before = colm < row + off
                        ls_neg = jnp.where(before, ls_neg, 0.0)
                    lb = ls_neg.astype(BF16)
                    pieces = []
                    later = rs[h]
                    for c0 in range(tk - tu, -1, -tu):
                        c = _dot(lb[:, c0:c0 + tu], u_ref[...]) + later
                        pieces.insert(0, c)
                        later = c[:, 0:1] + ls_neg[:, c0:c0 + 1]
                    w = jnp.exp(ls_pos + jnp.concatenate(pieces, axis=1))
                    if diag:
                        w = jnp.where(before, w, 0.0)
                    outs.append(_dot(w.astype(BF16), v2))
                    new_rs.append(later)
                new_accs.append(accs[p] + jnp.where(first, outs[0], outs[1]))
            return tuple(new_accs), tuple(new_rs)

        zacc = jnp.zeros((tq, LANES), F32)
        zr = jnp.zeros((tq, 1), F32)
        carry = step(jd, ((zacc,) * (nh // 2), (zr,) * nh), True)
        carry = lax.fori_loop(0, jd, lambda n, c: step(jd - 1 - n, c, False), carry)
        outs = carry[0]
    else:
        ones = jnp.ones((tk, LANES), BF16)

        def step(j, carry, diag):
            accs, ms, ls = carry
            new_accs, new_ms, new_ls = [], [], []
            r0 = pl.multiple_of(j * tk, tk)
            for p in range(nh // 2):
                v_aug = jnp.concatenate([v_pair(j, p), ones], axis=1)
                pv, alphas = [], []
                for h in (2 * p, 2 * p + 1):
                    s = _dot_nt(q_heads[h], k_head(j, h))
                    if mode == "mla":
                        s = s * scale
                    else:
                        s = s + nc_ref[0, h:h + 1, pl.ds(r0, tk)]
                    if diag:
                        s = jnp.where(colm <= row + off, s, -jnp.inf)
                    m_new = jnp.maximum(ms[h], jnp.max(s, axis=-1, keepdims=True))
                    alpha = jnp.exp(ms[h] - m_new)
                    pvx = _dot(jnp.exp(s - m_new).astype(BF16), v_aug)
                    new_ls.append(alpha * ls[h] + pvx[:, LANES:2 * LANES])
                    new_ms.append(m_new)
                    pv.append(pvx[:, 0:LANES])
                    alphas.append(alpha)
                new_accs.append(accs[p] * jnp.where(first, alphas[0], alphas[1])
                                + jnp.where(first, pv[0], pv[1]))
            return tuple(new_accs), tuple(new_ms), tuple(new_ls)

        neg = jnp.full((tq, 1), -jnp.inf, F32)
        zacc = jnp.zeros((tq, LANES), F32)
        carry = step(jd, ((zacc,) * (nh // 2), (neg,) * nh, (zacc,) * nh), True)
        carry = lax.fori_loop(0, jd, lambda n, c: step(n, c, False), carry)
        accs, _, ls = carry
        outs = [accs[p] / jnp.where(first, ls[2 * p], ls[2 * p + 1]) for p in range(nh // 2)]
    for p in range(nh // 2):
        sl = slice(p * LANES, (p + 1) * LANES)
        o_ref[:, sl] = _head_rms(outs[p], g_ref[:, sl]).astype(o_ref.dtype)


def _attn_call(mode, q, k, v, qcol, kcol, vcol, g_flat, gcol, bsz, seq, tq, tk, extra=(), scale=1.0):
    t = bsz * seq
    nq = seq // tq
    hq = N_HEADS_PER_MIXER * HEAD_DIM
    qw = N_HEADS_PER_MIXER * LANES if mode == "mla" else hq
    assert tk % tq == 0 and seq % tk == 0
    in_specs = [
        pl.BlockSpec((tq, qw), lambda b, i: (b * nq + i, qcol)),
        pl.BlockSpec((seq, qw), lambda b, i: (b, kcol)),
        pl.BlockSpec((seq, hq), lambda b, i: (b, vcol)),
        pl.BlockSpec((1, hq), lambda b, i: (0, gcol)),
    ]
    args = [q, k, v, g_flat]
    if mode == "sb":
        (u,) = extra
        in_specs.append(pl.BlockSpec(u.shape, lambda b, i: (0, 0)))
        args.append(u)
    elif mode == "fox":
        (nc,) = extra
        in_specs.append(pl.BlockSpec((1, N_HEADS_PER_MIXER, seq), lambda b, i: (b, 0, 0)))
        args.append(nc)
    return pl.pallas_call(
        functools.partial(_attn_kernel, mode=mode, tq=tq, tk=tk, scale=scale),
        grid=(bsz, nq),
        in_specs=in_specs,
        out_specs=pl.BlockSpec((tq, hq), lambda b, i: (b * nq + i, 0)),
        out_shape=jax.ShapeDtypeStruct((t, hq), BF16),
        compiler_params=_cparams(("parallel", "arbitrary")),
        name=f"attn_{mode}",
    )(*args)


def _dil_kernel(qkv_ref, o_ref, *, tq, tk, win):
    i = pl.program_id(2)
    lane = _lane_iota((tq, LANES))
    first = lane < HEAD_DIM
    hq = N_HEADS_PER_MIXER * HEAD_DIM
    r0 = pl.multiple_of(i * tq, tq)
    k0 = pl.multiple_of(jnp.maximum(i - 1, 0) * tq, tq)
    row = lax.broadcasted_iota(jnp.int32, (tq, tk), 0)
    colm = lax.broadcasted_iota(jnp.int32, (tq, tk), 1)
    delta = row + (r0 - k0) - colm
    band = jnp.abs(2 * delta - win) <= win
    ones = jnp.ones((tk, LANES), BF16)
    for p in range(N_HEADS_PER_MIXER // 2):
        q2 = qkv_ref[0, 0, pl.ds(r0, tq), p * LANES:(p + 1) * LANES]
        k2 = qkv_ref[0, 0, pl.ds(k0, tk), hq + p * LANES:hq + (p + 1) * LANES]
        v_aug = jnp.concatenate(
            [qkv_ref[0, 0, pl.ds(k0, tk), 2 * hq + p * LANES:2 * hq + (p + 1) * LANES], ones], axis=1)
        zero = jnp.zeros_like(q2)
        outs = []
        lses = []
        for hh in range(2):
            qh = jnp.where(first, q2, zero) if hh == 0 else jnp.where(first, zero, q2)
            s = jnp.where(band, _dot_nt(qh, k2), -jnp.inf)
            m = jnp.max(s, axis=-1, keepdims=True)
            pvx = _dot(jnp.exp(s - m).astype(BF16), v_aug)
            l = pvx[:, LANES:2 * LANES]
            outs.append(pvx[:, 0:LANES] / l)
            lses.append(m + jnp.log(l))
        o_ref[0, :, p * LANES:(p + 1) * LANES] = jnp.where(first, outs[0], outs[1])
        o_ref[0, :, hq + p * LANES:hq + (p + 1) * LANES] = jnp.where(first, lses[0], lses[1])


def _dil_call(qkv, tqs, win):
    bsz, r, n, w = qkv.shape
    hq = N_HEADS_PER_MIXER * HEAD_DIM
    tk = 2 * tqs if n >= 2 * tqs else tqs
    assert tqs >= win and n % tqs == 0 and (tk == 2 * tqs or n == tqs)
    return pl.pallas_call(
        functools.partial(_dil_kernel, tq=tqs, tk=tk, win=win),
        grid=(bsz, r, n // tqs),
        in_specs=[pl.BlockSpec((1, 1, n, w), lambda b, p, c: (b, p, 0, 0))],
        out_specs=pl.BlockSpec((1, tqs, 2 * hq), lambda b, p, c: (b, c, p)),
        out_shape=jax.ShapeDtypeStruct((bsz, n, r * 2 * hq), F32),
        compiler_params=_cparams(("parallel", "parallel", "arbitrary")),
        name=f"dil_r{r}",
    )(qkv)


def _post_kernel(x_ref, osb_ref, ofox_ref, omla_ref, d1_ref, d2_ref, d3_ref, gd_ref, wo_ref,
                 lng_ref, lnb_ref, wr_ref, br_ref, x1_ref, x1t_ref, route_ref, stage_ref, *, alpha, dils):
    hq = N_HEADS_PER_MIXER * HEAD_DIM
    tm = x_ref.shape[0]
    h = _dot(osb_ref[...], wo_ref[0:hq, :])
    h += _dot(ofox_ref[...], wo_ref[hq:2 * hq, :])
    h += _dot(omla_ref[...], wo_ref[2 * hq:3 * hq, :])

    def token_rows(g, d_ref, r, c0):
        if r == 1:
            return d_ref[:, c0:c0 + LANES]
        st = stage_ref.at[g, c0 // LANES]
        for q in range(r):
            st[pl.ds(q, tm // r, stride=r), :] = d_ref[:, q * 2 * hq + c0:q * 2 * hq + c0 + LANES]
        return st[...]

    d_refs = (d1_ref, d2_ref, d3_ref)
    for p in range(N_HEADS_PER_MIXER // 2):
        sl = slice(p * LANES, (p + 1) * LANES)
        lses = [token_rows(g, d_refs[g], dils[g], hq + p * LANES) for g in range(len(dils))]
        vals = [token_rows(g, d_refs[g], dils[g], p * LANES) for g in range(len(dils))]
        l1, l2, l3 = lses
        m = jnp.maximum(jnp.maximum(l1, l2), l3)
        e1, e2, e3 = jnp.exp(l1 - m), jnp.exp(l2 - m), jnp.exp(l3 - m)
        inv = 1.0 / (e1 + e2 + e3)
        od = (e1 * inv) * vals[0] + (e2 * inv) * vals[1] + (e3 * inv) * vals[2]
        od = _head_rms(od, gd_ref[:, sl]).astype(BF16)
        h += _dot(od, wo_ref[3 * hq + p * LANES:3 * hq + (p + 1) * LANES, :])
    x1 = _layernorm(alpha * x_ref[...] + h, lng_ref[...], lnb_ref[...])
    x1_ref[...] = x1
    for j in range(x1.shape[1] // LANES):
        x1t_ref[pl.ds(j, tm, stride=SUBLANES), :] = x1[:, j * LANES:(j + 1) * LANES]

    logits = _dot_f32(x1, wr_ref[...]) + br_ref[...]
    lane = _lane_iota(logits.shape)
    lanef = lane.astype(F32)
    big = float(LANES)
    ninf = -jnp.inf
    gl = jnp.where(lane < N_GROUPS, logits, ninf)
    gmax = jnp.max(gl, axis=-1, keepdims=True)
    gsel = jnp.min(jnp.where(gl == gmax, lanef, big), axis=-1, keepdims=True)
    gw = 1.0 / jnp.sum(jnp.exp(gl - gmax), axis=-1, keepdims=True)
    e_lo = N_GROUPS + EXPERTS_PER_GROUP * gsel
    in_grp = (lanef >= e_lo) & (lanef < e_lo + EXPERTS_PER_GROUP)
    el = jnp.where(in_grp, logits, ninf)
    t1 = jnp.max(el, axis=-1, keepdims=True)
    i1 = jnp.min(jnp.where(el == t1, lanef, big), axis=-1, keepdims=True)
    el2 = jnp.where(lanef == i1, ninf, el)
    t2 = jnp.max(el2, axis=-1, keepdims=True)
    i2 = jnp.min(jnp.where(el2 == t2, lanef, big), axis=-1, keepdims=True)
    ex = jnp.exp(t2 - t1)
    den = 1.0 + ex
    g1 = gw / den
    g2 = gw * ex / den
    out = jnp.where(lane == 0, i1 - N_GROUPS,
                    jnp.where(lane == 1, i2 - N_GROUPS,
                              jnp.where(lane == 2, g1, jnp.where(lane == 3, g2, 0.0))))
    route_ref[...] = out


def _post_call(x, osb, ofox, omla, dil_outs, dils, g_dil, wo, lng, lnb, wr, br, alpha, tm):
    t, d = x.shape
    hq = N_HEADS_PER_MIXER * HEAD_DIM
    row = lambda w: pl.BlockSpec((tm, w), lambda i: (i, 0))
    full = lambda a: pl.BlockSpec(a.shape, lambda i: (0, 0))
    dil_specs = [pl.BlockSpec((tm // r, r * 2 * hq), lambda i: (i, 0)) for r in dils]
    return pl.pallas_call(
        functools.partial(_post_kernel, alpha=alpha, dils=dils),
        grid=(t // tm,),
        in_specs=[row(d), row(hq), row(hq), row(hq)] + dil_specs
                 + [full(g_dil), full(wo), full(lng), full(lnb), full(wr), full(br)],
        out_specs=[row(d), pl.BlockSpec((tm * SUBLANES, LANES), lambda i: (i, 0)), row(LANES)],
        out_shape=[jax.ShapeDtypeStruct((t, d), F32), jax.ShapeDtypeStruct((t * SUBLANES, LANES), F32),
                   jax.ShapeDtypeStruct((t, LANES), F32)],
        scratch_shapes=[pltpu.VMEM((len(dils), 2 * hq // LANES, tm, LANES), F32)],
        compiler_params=_cparams(("parallel",)),
        name="post_mixer",
    )(x, osb, ofox, omla, *dil_outs, g_dil, wo, lng, lnb, wr, br)


DMA_UNROLL = 8


def _moe_kernel(ce_ref, nv_ref, src_ref, srcn_ref, dst_ref, x_hbm, w1_ref, w3_ref, w2_ref, y_hbm,
                xs_ref, ys_ref, gsem, ssem):
    c = pl.program_id(0)
    nc = pl.num_programs(0)
    slot = c % 2
    other = 1 - slot
    nv = nv_ref[c]
    nv_next = jnp.where(c + 1 < nc, nv_ref[jnp.minimum(c + 1, nc - 1)], 0)
    nv_prev = jnp.where(c > 0, nv_ref[jnp.maximum(c - 1, 0)], 0)

    def gather_copy(idx_ref, i, s):
        return pltpu.make_async_copy(
            x_hbm.at[pl.ds(pl.multiple_of(idx_ref[0, 0, i] * SUBLANES, SUBLANES), SUBLANES), :],
            xs_ref.at[s, pl.ds(pl.multiple_of(i * SUBLANES, SUBLANES), SUBLANES), :], gsem.at[s])

    def scatter_copy(i, s):
        return pltpu.make_async_copy(
            ys_ref.at[s, pl.ds(pl.multiple_of(i * SUBLANES, SUBLANES), SUBLANES), :],
            y_hbm.at[pl.ds(pl.multiple_of(dst_ref[0, 0, i] * SUBLANES, SUBLANES), SUBLANES), :], ssem.at[s])

    def start_rows(n, make):
        ngrp = n // DMA_UNROLL

        def grp(g, _):
            base = pl.multiple_of(g * DMA_UNROLL, DMA_UNROLL)
            for k in range(DMA_UNROLL):
                make(base + k).start()
            return 0

        def one(i, _):
            make(i).start()
            return 0

        lax.fori_loop(0, ngrp, grp, 0)
        lax.fori_loop(ngrp * DMA_UNROLL, n, one, 0)

    def wait_rows(n, make_row, make_block):
        @pl.when(n == MOE_BLOCK)
        def _():
            make_block().wait()

        @pl.when(n < MOE_BLOCK)
        def _():
            def one(i, _):
                make_row(i).wait()
                return 0
            lax.fori_loop(0, n, one, 0)

    def gather_block(s):
        return pltpu.make_async_copy(x_hbm.at[pl.ds(0, MOE_BLOCK * SUBLANES), :], xs_ref.at[s], gsem.at[s])

    def scatter_block(s):
        return pltpu.make_async_copy(ys_ref.at[s], y_hbm.at[pl.ds(0, MOE_BLOCK * SUBLANES), :], ssem.at[s])

    @pl.when(c == 0)
    def _():
        xs_ref[...] = jnp.zeros_like(xs_ref)
        start_rows(nv, lambda i: gather_copy(src_ref, i, 0))

    start_rows(nv_next, lambda i: gather_copy(srcn_ref, i, other))
    wait_rows(nv, lambda i: gather_copy(src_ref, i, slot), lambda: gather_block(slot))

    @pl.when(nv > 0)
    def _():
        xt = xs_ref.at[slot]
        a = b = None
        for jj in range(0, xt.shape[0] // MOE_BLOCK, 2):
            xb = jnp.concatenate([xt[pl.ds(jj, MOE_BLOCK, stride=SUBLANES), :],
                                  xt[pl.ds(jj + 1, MOE_BLOCK, stride=SUBLANES), :]], axis=1).astype(BF16)
            ws = slice(jj * LANES, (jj + 2) * LANES)
            da, db = _dot(xb, w1_ref[0, ws, :]), _dot(xb, w3_ref[0, ws, :])
            a, b = (da, db) if a is None else (a + da, b + db)
        hid = (a / (1.0 + jnp.exp(-a)) * b).astype(BF16)
        y = _dot(hid, w2_ref[0])
        yt = ys_ref.at[slot]
        for j in range(y.shape[1] // LANES):
            yt[pl.ds(j, MOE_BLOCK, stride=SUBLANES), :] = y[:, j * LANES:(j + 1) * LANES]

    wait_rows(nv_prev, lambda i: scatter_copy(i, other), lambda: scatter_block(other))
    start_rows(nv, lambda i: scatter_copy(i, slot))

    @pl.when(c == nc - 1)
    def _():
        wait_rows(nv, lambda i: scatter_copy(i, slot), lambda: scatter_block(slot))


def _moe_call(chunk_expert, n_valid, src, dst, x1t, w1, w3, w2, n_rows_out):
    n_chunks = chunk_expert.shape[0]
    d, de = w1.shape[1], w1.shape[2]
    assert d == SUBLANES * LANES and x1t.shape[1] == LANES
    grid_spec = pltpu.PrefetchScalarGridSpec(
        num_scalar_prefetch=2,
        grid=(n_chunks,),
        in_specs=[
            pl.BlockSpec((1, 1, MOE_BLOCK), lambda c, ce, nv: (c, 0, 0), memory_space=pltpu.SMEM),
            pl.BlockSpec((1, 1, MOE_BLOCK), lambda c, ce, nv: (jnp.minimum(c + 1, n_chunks - 1), 0, 0),
                         memory_space=pltpu.SMEM),
            pl.BlockSpec((1, 1, MOE_BLOCK), lambda c, ce, nv: (c, 0, 0), memory_space=pltpu.SMEM),
            pl.BlockSpec(memory_space=pl.ANY),
            pl.BlockSpec((1, d, de), lambda c, ce, nv: (ce[c], 0, 0)),
            pl.BlockSpec((1, d, de), lambda c, ce, nv: (ce[c], 0, 0)),
            pl.BlockSpec((1, de, d), lambda c, ce, nv: (ce[c], 0, 0)),
        ],
        out_specs=pl.BlockSpec(memory_space=pl.ANY),
        scratch_shapes=[
            pltpu.VMEM((2, MOE_BLOCK * SUBLANES, LANES), F32),
            pltpu.VMEM((2, MOE_BLOCK * SUBLANES, LANES), F32),
            pltpu.SemaphoreType.DMA((2,)),
            pltpu.SemaphoreType.DMA((2,)),
        ],
    )
    return pl.pallas_call(
        _moe_kernel,
        grid_spec=grid_spec,
        out_shape=jax.ShapeDtypeStruct((n_rows_out * SUBLANES, LANES), F32),
        compiler_params=_cparams(("arbitrary",)),
        name="moe_experts",
    )(chunk_expert, n_valid, src, src, dst, x1t, w1, w3, w2)


def _combine_kernel(x1_ref, ya_ref, yb_ref, route_ref, lng_ref, lnb_ref, o_ref, *, alpha):
    tm, d = x1_ref.shape
    g1 = route_ref[:, 2:3]
    g2 = route_ref[:, 3:4]

    def rows(y_ref):
        return jnp.concatenate([y_ref[pl.ds(j, tm, stride=SUBLANES), :] for j in range(d // LANES)], axis=1)

    m = g1 * rows(ya_ref) + g2 * rows(yb_ref)
    o_ref[...] = _layernorm(alpha * x1_ref[...] + m, lng_ref[...], lnb_ref[...])


def _combine_call(x1, y, route, lng, lnb, alpha, tm):
    t, d = x1.shape
    nt = t // tm
    full = lambda a: pl.BlockSpec(a.shape, lambda i: (0, 0))
    return pl.pallas_call(
        functools.partial(_combine_kernel, alpha=alpha),
        grid=(nt,),
        in_specs=[pl.BlockSpec((tm, d), lambda i: (i, 0)),
                  pl.BlockSpec((tm * SUBLANES, LANES), lambda i: (i, 0)),
                  pl.BlockSpec((tm * SUBLANES, LANES), lambda i: (i + nt, 0)),
                  pl.BlockSpec((tm, LANES), lambda i: (i, 0)),
                  full(lng), full(lnb)],
        out_specs=pl.BlockSpec((tm, d), lambda i: (i, 0)),
        out_shape=jax.ShapeDtypeStruct((t, d), F32),
        compiler_params=_cparams(("parallel",)),
        name="moe_combine",
    )(x1, y, y, route, lng, lnb)


def _rope_tables(seq, dim, lane_lo):
    half = dim // 2
    inv_freq = ROPE_THETA ** (-jnp.arange(half, dtype=F32) / half)
    ang = jnp.arange(seq, dtype=F32)[:, None] * inv_freq[None, :]
    cos = jnp.concatenate([jnp.cos(ang), jnp.cos(ang)], -1)
    sin = jnp.concatenate([jnp.sin(ang), jnp.sin(ang)], -1)
    if lane_lo == 0:
        reps = LANES // dim
        return jnp.tile(cos, (1, reps)), jnp.tile(sin, (1, reps))
    cos_t = jnp.ones((seq, LANES), F32).at[:, lane_lo:lane_lo + dim].set(cos)
    sin_t = jnp.zeros((seq, LANES), F32).at[:, lane_lo:lane_lo + dim].set(sin)
    return cos_t, sin_t


def _dispatch_tables(route, n_tok):
    expert_id = route[:, 0:TOP_K].astype(jnp.int32).reshape(-1)
    n_assign = n_tok * TOP_K
    n_slots = n_assign + N_EXPERTS * MOE_BLOCK
    n_chunks = n_slots // MOE_BLOCK
    onehot = (expert_id[:, None] == jnp.arange(N_EXPERTS, dtype=jnp.int32)[None, :]).astype(jnp.int32)
    ranks = jnp.cumsum(onehot, axis=0) - onehot
    rank = jnp.sum(ranks * onehot, axis=1)
    counts = jnp.sum(onehot, axis=0)
    padded = (counts + MOE_BLOCK - 1) // MOE_BLOCK * MOE_BLOCK
    pad_end = jnp.cumsum(padded)
    pad_start = pad_end - padded
    dest = pad_start[expert_id] + rank
    assign = jnp.arange(n_assign, dtype=jnp.int32)
    slot_assign = jnp.zeros((n_slots,), jnp.int32).at[dest].set(assign)
    src = slot_assign // TOP_K
    dst = (slot_assign % TOP_K) * n_tok + src
    chunk_start = jnp.arange(n_chunks, dtype=jnp.int32) * MOE_BLOCK
    chunk_expert = jnp.minimum(jnp.searchsorted(pad_end, chunk_start, side="right"),
                               N_EXPERTS - 1).astype(jnp.int32)
    n_valid = jnp.clip(pad_start[chunk_expert] + counts[chunk_expert] - chunk_start,
                       0, MOE_BLOCK).astype(jnp.int32)
    return (chunk_expert, n_valid, src.reshape(n_chunks, 1, MOE_BLOCK),
            dst.reshape(n_chunks, 1, MOE_BLOCK), n_assign)


def _pick_tile(n, pref):
    t = pref
    while n % t:
        t //= 2
    return t


def kernel(x, w_in, b_forget, g_cq, g_ckv, w_uq, w_ukv, g_head, w_out, ln1_g, ln1_b,
           w_group, b_group, w_expert, b_expert, w1, w3, w2, ln2_g, ln2_b):
    bsz, seq, d = x.shape
    depth = w_in.shape[0]
    t = bsz * seq
    alpha = (2.0 * depth) ** 0.25
    hq = N_HEADS_PER_MIXER * HEAD_DIM
    qk_scale = HEAD_DIM ** -0.5
    mla_scale = (MLA_NOPE + MLA_ROPE) ** -0.5
    win = DIL_BRANCHES[0][0]
    assert all(w // r == win for w, r in DIL_BRANCHES)
    assert seq % (DIL_BRANCHES[-1][1] * win) == 0 and d % LANES == 0

    cos64, sin64 = _rope_tables(seq, HEAD_DIM, 0)
    cos_m, sin_m = _rope_tables(seq, MLA_ROPE, MLA_NOPE)
    tq = _pick_tile(seq, 256)
    idx = jnp.arange(tq)
    u_sb = (idx[:, None] > idx[None, :]).astype(BF16)
    tk_sb = _pick_tile(seq, 2 * tq)
    tk_sm = _pick_tile(seq, 4 * tq)

    for l in range(depth):
        wl = w_in[l]
        o_fox, o_mla, o_dil = N_SB, N_SB + N_FOX_QKV + N_HEADS_PER_MIXER, N_SB + N_FOX_QKV + N_HEADS_PER_MIXER + N_MLA
        qs = lambda w: w.at[:, 0:hq].multiply(qk_scale)
        w_sb = qs(wl[:, 0:N_SB])
        w_fx = qs(wl[:, o_fox:o_fox + N_FOX_QKV])
        w_f = wl[:, o_fox + N_FOX_QKV:o_mla]
        w_ml = wl[:, o_mla:o_dil]
        wd = wl[:, o_dil:].reshape(d, 3, len(DIL_BRANCHES), hq)
        w_br = [qs(jnp.concatenate([wd[:, 0, g], wd[:, 1, g], wd[:, 2, g]], axis=1)) for g in range(len(DIL_BRANCHES))]
        w_misc = jnp.concatenate([w_ml, w_f, jnp.zeros((d, MISC_W - N_MLA - N_HEADS_PER_MIXER), F32)], axis=1)
        w_tok = jnp.concatenate([w_sb, w_fx, w_misc] + w_br, axis=1).astype(BF16)

        wq = jnp.pad(w_uq[l].reshape(MLA_Q_LORA, N_HEADS_PER_MIXER, MLA_NOPE + MLA_ROPE),
                     ((0, 0), (0, 0), (0, LANES - MLA_NOPE - MLA_ROPE))).reshape(MLA_Q_LORA, -1).astype(BF16)
        wkv = w_ukv[l].reshape(MLA_KV_LORA, N_HEADS_PER_MIXER, MLA_NOPE + HEAD_DIM)
        wk = jnp.pad(wkv[:, :, :MLA_NOPE], ((0, 0), (0, 0), (0, LANES - MLA_NOPE))).reshape(MLA_KV_LORA, -1).astype(BF16)
        wv = wkv[:, :, MLA_NOPE:].reshape(MLA_KV_LORA, -1).astype(BF16)
        g_flat = g_head[l].reshape(1, -1)
        wr = jnp.concatenate([w_group[l], w_expert[l],
                              jnp.zeros((d, LANES - N_GROUPS - N_EXPERTS), F32)], axis=1)
        br = jnp.concatenate([b_group[l], b_expert[l],
                              jnp.zeros((LANES - N_GROUPS - N_EXPERTS,), F32)]).reshape(1, LANES)

        tm = _pick_tile(seq, 512)
        sb, fx, misc, *qkv_br = _proj_call(
            x, w_tok, cos64, sin64,
            ((N_SB, 0, BF16, 1), (N_FOX_QKV, 0, BF16, 1), (MISC_W, 0, F32, 1))
            + tuple((N_BRANCH, 2 * hq, BF16, r) for _, r in DIL_BRANCHES), tm)
        sb = sb.reshape(t, N_SB)
        fx = fx.reshape(t, N_FOX_QKV)
        misc = misc.reshape(t, MISC_W)

        o_sb = _attn_call("sb", sb, sb, sb, 0, 1, 2, g_flat, 0, bsz, seq, tq, tk_sb, extra=(u_sb,))
        neg_c = _fox_c_call(misc, b_forget[l], bsz, seq)
        o_fx = _attn_call("fox", fx, fx, fx, 0, 1, 2, g_flat, 1, bsz, seq, tq, tk_sm, extra=(neg_c,))
        mq, mk, mv = _mla_prep_call(misc, g_cq[l].reshape(1, -1), g_ckv[l].reshape(1, -1), wq, wk, wv,
                                    cos_m, sin_m, seq, tm)
        o_ml = _attn_call("mla", mq, mk, mv, 0, 0, 0, g_flat, 2, bsz, seq, tq, tk_sm, scale=mla_scale)

        dil = []
        for g, (_, r) in enumerate(DIL_BRANCHES):
            n = seq // r
            og = _dil_call(qkv_br[g], max(win, min(tq, n // 2)), win)
            dil.append(og.reshape(t // r, r * 2 * hq))

        x1, x1t, route = _post_call(
            x.reshape(t, d), o_sb, o_fx, o_ml, dil, tuple(r for _, r in DIL_BRANCHES),
            g_flat[:, 3 * hq:], w_out[l].astype(BF16),
            ln1_g[l].reshape(1, d), ln1_b[l].reshape(1, d), wr, br, alpha, _pick_tile(t, 256))

        chunk_expert, n_valid, src, dst, n_rows = _dispatch_tables(route, t)
        y = _moe_call(chunk_expert, n_valid, src, dst, x1t,
                      w1[l].astype(BF16), w3[l].astype(BF16), w2[l].astype(BF16), n_rows)
        x = _combine_call(x1, y, route,
                          ln2_g[l].reshape(1, d), ln2_b[l].reshape(1, d), alpha, _pick_tile(t, 256)).reshape(bsz, seq, d)
    return x
```

```python
import functools

import jax
import jax.numpy as jnp
from jax import lax
from jax.experimental import pallas as pl
from jax.experimental.pallas import tpu as pltpu

F32 = jnp.float32
BF16 = jnp.bfloat16

HEAD_DIM = 64
N_HEADS_PER_MIXER = 4
MLA_Q_LORA = 256
MLA_KV_LORA = 128
MLA_NOPE = 64
MLA_ROPE = 32
DIL_BRANCHES = ((128, 1), (512, 4), (2048, 16))
ROPE_THETA = 10000.0
N_GROUPS = 4
EXPERTS_PER_GROUP = 4
N_EXPERTS = N_GROUPS * EXPERTS_PER_GROUP
TOP_K = 2
MOE_BLOCK = 256
LN_EPS = 1e-5
RMS_EPS = 1e-6

LANES = 128
SUBLANES = 8
VMEM_LIMIT_BYTES = 56 * 1024 * 1024

N_SB = 3 * N_HEADS_PER_MIXER * HEAD_DIM
N_FOX_QKV = 3 * N_HEADS_PER_MIXER * HEAD_DIM
N_MLA = MLA_Q_LORA + MLA_KV_LORA + MLA_ROPE
N_BRANCH = 3 * N_HEADS_PER_MIXER * HEAD_DIM
MISC_W = 512
F_COL = N_MLA


def _cparams(sem):
    return pltpu.CompilerParams(dimension_semantics=sem, vmem_limit_bytes=VMEM_LIMIT_BYTES)


def _split3(a):
    hi = a.astype(BF16)
    r1 = a - hi.astype(F32)
    mid = r1.astype(BF16)
    lo = (r1 - mid.astype(F32)).astype(BF16)
    return hi, mid, lo


def _dot(a, b):
    return jnp.dot(a, b, preferred_element_type=F32)


def _dot_nt(a, b):
    return lax.dot_general(a, b, (((1,), (1,)), ((), ())), preferred_element_type=F32)


def _dot_exact_rhs(a, u):
    hi, mid, lo = _split3(a)
    return _dot(hi, u) + _dot(mid, u) + _dot(lo, u)


def _dot_f32(a, b):
    ah = a.astype(BF16)
    al = (a - ah.astype(F32)).astype(BF16)
    bh = b.astype(BF16)
    bl = (b - bh.astype(F32)).astype(BF16)
    return _dot(ah, bh) + (_dot(ah, bl) + _dot(al, bh))


def _lane_iota(shape):
    return lax.broadcasted_iota(jnp.int32, shape, len(shape) - 1)


def _rotate_half(y, half):
    lane = _lane_iota(y.shape)
    fwd = pltpu.roll(y, half, 1)
    bwd = pltpu.roll(y, LANES - half, 1)
    return jnp.where((lane % (2 * half)) < half, -bwd, fwd)


def _log_sigmoid_pair(z):
    sp = jnp.log(1.0 + jnp.exp(-jnp.abs(z)))
    return jnp.minimum(z, 0.0) - sp, -jnp.maximum(z, 0.0) - sp


def _head_rms(o, g):
    lane = _lane_iota(o.shape)
    first = lane < HEAD_DIM
    sq = o * o
    ss_a = jnp.sum(jnp.where(first, sq, 0.0), axis=-1, keepdims=True)
    ss_b = jnp.sum(jnp.where(first, 0.0, sq), axis=-1, keepdims=True)
    ms = jnp.where(first, ss_a, ss_b) * (1.0 / HEAD_DIM)
    return o * lax.rsqrt(ms + RMS_EPS) * g


def _layernorm(y, g, b):
    mu = jnp.mean(y, axis=-1, keepdims=True)
    d = y - mu
    var = jnp.mean(d * d, axis=-1, keepdims=True)
    return d * lax.rsqrt(var + LN_EPS) * g + b


def _proj_kernel(x_ref, w_ref, cos_ref, sin_ref, *refs, outs):
    out_refs, stage_ref = refs[:len(outs)], refs[len(outs)]
    tm = x_ref.shape[1]
    xb = x_ref[0].astype(BF16)
    col = 0
    slab = 0
    for o_ref, (width, n_rope, _, r) in zip(out_refs, outs):
        for c in range(0, width, 2 * LANES):
            cw = min(2 * LANES, width - c)
            y = _dot(xb, w_ref[:, col + c:col + c + cw])
            for s in range(0, cw, LANES):
                ys = y[:, s:s + LANES]
                if c + s < n_rope:
                    ys = ys * cos_ref[...] + _rotate_half(ys, HEAD_DIM // 2) * sin_ref[...]
                if r == 1:
                    o_ref[0, 0, :, c + s:c + s + LANES] = ys.astype(o_ref.dtype)
                else:
                    st = stage_ref.at[slab % stage_ref.shape[0]]
                    slab += 1
                    st[...] = ys
                    for p in range(r):
                        o_ref[0, p, :, c + s:c + s + LANES] = (
                            st[pl.ds(p, tm // r, stride=r), :].astype(o_ref.dtype))
        col += width


PROJ_STAGE_SLABS = 4


def _proj_call(x, w, cos, sin, outs, tm):
    bsz, seq, d = x.shape
    nt = seq // tm
    in_specs = [
        pl.BlockSpec((1, tm, d), lambda b, i: (b, i, 0)),
        pl.BlockSpec(w.shape, lambda b, i: (0, 0)),
        pl.BlockSpec((tm, LANES), lambda b, i: (i, 0)),
        pl.BlockSpec((tm, LANES), lambda b, i: (i, 0)),
    ]
    out_specs = [pl.BlockSpec((1, r, tm // r, wd), lambda b, i: (b, 0, i, 0)) for wd, _, _, r in outs]
    out_shape = [jax.ShapeDtypeStruct((bsz, r, seq // r, wd), dt) for wd, _, dt, r in outs]
    return pl.pallas_call(
        functools.partial(_proj_kernel, outs=outs),
        grid=(bsz, nt), in_specs=in_specs, out_specs=out_specs, out_shape=out_shape,
        scratch_shapes=[pltpu.VMEM((PROJ_STAGE_SLABS, tm, LANES), F32)],
        compiler_params=_cparams(("parallel", "parallel")),
        name="proj",
    )(x, w, cos, sin)


def _fox_c_kernel(misc_ref, bias_ref, uinc_ref, ones_ref, out_ref, *, seq):
    lane0 = F_COL - 3 * LANES
    nblk = seq // LANES

    def body(j, carry):
        r0 = pl.multiple_of(j * LANES, LANES)
        f = misc_ref[pl.ds(r0, LANES), :] + bias_ref[...]
        lf, _ = _log_sigmoid_pair(f)
        lft = lf.T
        csum = _dot_exact_rhs(lft, uinc_ref[...]) + carry
        tot = _dot_exact_rhs(lft, ones_ref[...])
        out_ref[0, :, pl.ds(r0, LANES)] = -csum[lane0:lane0 + N_HEADS_PER_MIXER, :]
        return carry + tot

    lax.fori_loop(0, nblk, body, jnp.zeros((LANES, LANES), F32))


def _fox_c_call(misc, b_forget, bsz, seq):
    bias = jnp.zeros((1, LANES), F32).at[0, F_COL - 3 * LANES:F_COL - 3 * LANES + N_HEADS_PER_MIXER].set(b_forget)
    idx = jnp.arange(LANES)
    uinc = (idx[:, None] <= idx[None, :]).astype(BF16)
    ones = jnp.ones((LANES, LANES), BF16)
    return pl.pallas_call(
        functools.partial(_fox_c_kernel, seq=seq),
        grid=(bsz,),
        in_specs=[
            pl.BlockSpec((seq, LANES), lambda b: (b, 3)),
            pl.BlockSpec((1, LANES), lambda b: (0, 0)),
            pl.BlockSpec((LANES, LANES), lambda b: (0, 0)),
            pl.BlockSpec((LANES, LANES), lambda b: (0, 0)),
        ],
        out_specs=pl.BlockSpec((1, N_HEADS_PER_MIXER, seq), lambda b: (b, 0, 0)),
        out_shape=jax.ShapeDtypeStruct((bsz, N_HEADS_PER_MIXER, seq), F32),
        compiler_params=_cparams(("parallel",)),
        name="fox_c",
    )(misc, bias, uinc, ones)


def _mla_prep_kernel(misc_ref, gq_ref, gkv_ref, wq_ref, wk_ref, wv_ref, cos_ref, sin_ref,
                     q_ref, k_ref, v_ref):
    def rms(x, g):
        return x * lax.rsqrt(jnp.mean(x * x, axis=-1, keepdims=True) + RMS_EPS) * g

    cq = rms(misc_ref[:, 0:MLA_Q_LORA], gq_ref[...]).astype(BF16)
    ckv = rms(misc_ref[:, MLA_Q_LORA:MLA_Q_LORA + MLA_KV_LORA], gkv_ref[...]).astype(BF16)
    kr_blk = misc_ref[:, 3 * LANES:4 * LANES]
    lane = _lane_iota(kr_blk.shape)
    in_rope = (lane >= MLA_NOPE) & (lane < MLA_NOPE + MLA_ROPE)
    kr = jnp.where(in_rope, pltpu.roll(kr_blk, MLA_NOPE, 1), 0.0)
    cos = cos_ref[...]
    sin = sin_ref[...]

    def rope(y):
        return y * cos + _rotate_half(y, MLA_ROPE // 2) * sin

    q = _dot(cq, wq_ref[...])
    k = _dot(ckv, wk_ref[...])
    for h in range(N_HEADS_PER_MIXER):
        sl = slice(h * LANES, (h + 1) * LANES)
        q_ref[:, sl] = rope(q[:, sl]).astype(BF16)
        k_ref[:, sl] = rope(k[:, sl] + kr).astype(BF16)
    v_ref[...] = _dot(ckv, wv_ref[...]).astype(BF16)


def _mla_prep_call(misc, g_cq, g_ckv, wq, wk, wv, cos, sin, seq, tm):
    t = misc.shape[0]
    nper = seq // tm
    hw = N_HEADS_PER_MIXER * LANES
    vw = N_HEADS_PER_MIXER * HEAD_DIM
    full = lambda a: pl.BlockSpec(a.shape, lambda i: (0, 0))
    return pl.pallas_call(
        _mla_prep_kernel,
        grid=(t // tm,),
        in_specs=[
            pl.BlockSpec((tm, MISC_W), lambda i: (i, 0)),
            full(g_cq), full(g_ckv), full(wq), full(wk), full(wv),
            pl.BlockSpec((tm, LANES), lambda i: (i % nper, 0)),
            pl.BlockSpec((tm, LANES), lambda i: (i % nper, 0)),
        ],
        out_specs=[
            pl.BlockSpec((tm, hw), lambda i: (i, 0)),
            pl.BlockSpec((tm, hw), lambda i: (i, 0)),
            pl.BlockSpec((tm, vw), lambda i: (i, 0)),
        ],
        out_shape=[
            jax.ShapeDtypeStruct((t, hw), BF16),
            jax.ShapeDtypeStruct((t, hw), BF16),
            jax.ShapeDtypeStruct((t, vw), BF16),
        ],
        compiler_params=_cparams(("parallel",)),
        name="mla_prep",
    )(misc, g_cq, g_ckv, wq, wk, wv, cos, sin)


def _attn_kernel(*refs, mode, tq, tk, scale):
    if mode == "sb":
        q_ref, k_ref, v_ref, g_ref, u_ref, o_ref = refs
    elif mode == "fox":
        q_ref, k_ref, v_ref, g_ref, nc_ref, o_ref = refs
    else:
        q_ref, k_ref, v_ref, g_ref, o_ref = refs
    nh = N_HEADS_PER_MIXER
    i = pl.program_id(1)
    lane = _lane_iota((tq, LANES))
    first = lane < HEAD_DIM
    row = lax.broadcasted_iota(jnp.int32, (tq, tk), 0)
    colm = lax.broadcasted_iota(jnp.int32, (tq, tk), 1)

    q_heads = []
    for h in range(nh):
        if mode == "mla":
            q_heads.append(q_ref[:, h * LANES:(h + 1) * LANES])
        else:
            q2 = q_ref[:, (h // 2) * LANES:(h // 2 + 1) * LANES]
            zero = jnp.zeros_like(q2)
            q_heads.append(jnp.where(first, q2, zero) if h % 2 == 0 else jnp.where(first, zero, q2))

    def k_head(j, h):
        r0 = pl.multiple_of(j * tk, tk)
        kb = h if mode == "mla" else h // 2
        return k_ref[pl.ds(r0, tk), kb * LANES:(kb + 1) * LANES]

    def v_pair(j, p):
        r0 = pl.multiple_of(j * tk, tk)
        return v_ref[pl.ds(r0, tk), p * LANES:(p + 1) * LANES]

    jd = (i * tq) // tk
    off = i * tq - jd * tk

    if mode == "sb":
        tu = u_ref.shape[0]

        def step(j, carry, diag):
            accs, rs = carry
            new_accs, new_rs = [], []
            for p in range(nh // 2):
                v2 = v_pair(j, p)
                outs = []
                for h in (2 * p, 2 * p + 1):
                    z = _dot_nt(q_heads[h], k_head(j, h))
                    ls_pos, ls_neg = _log_sigmoid_pair(z)
                    if diag:
                        before = colm < row + off
                        ls_neg = jnp.where(before, ls_neg, 0.0)
                    lb = ls_neg.astype(BF16)
                    pieces = []
                    later = rs[h]
                    for c0 in range(tk - tu, -1, -tu):
                        c = _dot(lb[:, c0:c0 + tu], u_ref[...]) + later
                        pieces.insert(0, c)
                        later = c[:, 0:1] + ls_neg[:, c0:c0 + 1]
                    w = jnp.exp(ls_pos + jnp.concatenate(pieces, axis=1))
                    if diag:
                        w = jnp.where(before, w, 0.0)
                    outs.append(_dot(w.astype(BF16), v2))
                    new_rs.append(later)
                new_accs.append(accs[p] + jnp.where(first, outs[0], outs[1]))
            return tuple(new_accs), tuple(new_rs)

        zacc = jnp.zeros((tq, LANES), F32)
        zr = jnp.zeros((tq, 1), F32)
        carry = step(jd, ((zacc,) * (nh // 2), (zr,) * nh), True)
        carry = lax.fori_loop(0, jd, lambda n, c: step(jd - 1 - n, c, False), carry)
        outs = carry[0]
    else:
        ones = jnp.ones((tk, LANES), BF16)

        def step(j, carry, diag):
            accs, ms, ls = carry
            new_accs, new_ms, new_ls = [], [], []
            r0 = pl.multiple_of(j * tk, tk)
            for p in range(nh // 2):
                v_aug = jnp.concatenate([v_pair(j, p), ones], axis=1)
                pv, alphas = [], []
                for h in (2 * p, 2 * p + 1):
                    s = _dot_nt(q_heads[h], k_head(j, h))
                    if mode == "mla":
                        s = s * scale
                    else:
                        s = s + nc_ref[0, h:h + 1, pl.ds(r0, tk)]
                    if diag:
                        s = jnp.where(colm <= row + off, s, -jnp.inf)
                    m_new = jnp.maximum(ms[h], jnp.max(s, axis=-1, keepdims=True))
                    alpha = jnp.exp(ms[h] - m_new)
                    pvx = _dot(jnp.exp(s - m_new).astype(BF16), v_aug)
                    new_ls.append(alpha * ls[h] + pvx[:, LANES:2 * LANES])
                    new_ms.append(m_new)
                    pv.append(pvx[:, 0:LANES])
                    alphas.append(alpha)
                new_accs.append(accs[p] * jnp.where(first, alphas[0], alphas[1])
                                + jnp.where(first, pv[0], pv[1]))
            return tuple(new_accs), tuple(new_ms), tuple(new_ls)

        neg = jnp.full((tq, 1), -jnp.inf, F32)
        zacc = jnp.zeros((tq, LANES), F32)
        carry = step(jd, ((zacc,) * (nh // 2), (neg,) * nh, (zacc,) * nh), True)
        carry = lax.fori_loop(0, jd, lambda n, c: step(n, c, False), carry)
        accs, _, ls = carry
        outs = [accs[p] / jnp.where(first, ls[2 * p], ls[2 * p + 1]) for p in range(nh // 2)]
    for p in range(nh // 2):
        sl = slice(p * LANES, (p + 1) * LANES)
        o_ref[:, sl] = _head_rms(outs[p], g_ref[:, sl]).astype(o_ref.dtype)


def _attn_call(mode, q, k, v, qcol, kcol, vcol, g_flat, gcol, bsz, seq, tq, tk, extra=(), scale=1.0):
    t = bsz * seq
    nq = seq // tq
    hq = N_HEADS_PER_MIXER * HEAD_DIM
    qw = N_HEADS_PER_MIXER * LANES if mode == "mla" else hq
    assert tk % tq == 0 and seq % tk == 0
    in_specs = [
        pl.BlockSpec((tq, qw), lambda b, i: (b * nq + i, qcol)),
        pl.BlockSpec((seq, qw), lambda b, i: (b, kcol)),
        pl.BlockSpec((seq, hq), lambda b, i: (b, vcol)),
        pl.BlockSpec((1, hq), lambda b, i: (0, gcol)),
    ]
    args = [q, k, v, g_flat]
    if mode == "sb":
        (u,) = extra
        in_specs.append(pl.BlockSpec(u.shape, lambda b, i: (0, 0)))
        args.append(u)
    elif mode == "fox":
        (nc,) = extra
        in_specs.append(pl.BlockSpec((1, N_HEADS_PER_MIXER, seq), lambda b, i: (b, 0, 0)))
        args.append(nc)
    return pl.pallas_call(
        functools.partial(_attn_kernel, mode=mode, tq=tq, tk=tk, scale=scale),
        grid=(bsz, nq),
        in_specs=in_specs,
        out_specs=pl.BlockSpec((tq, hq), lambda b, i: (b * nq + i, 0)),
        out_shape=jax.ShapeDtypeStruct((t, hq), BF16),
        compiler_params=_cparams(("parallel", "arbitrary")),
        name=f"attn_{mode}",
    )(*args)


def _dil_kernel(qkv_ref, o_ref, *, tq, tk, win):
    i = pl.program_id(2)
    lane = _lane_iota((tq, LANES))
    first = lane < HEAD_DIM
    hq = N_HEADS_PER_MIXER * HEAD_DIM
    r0 = pl.multiple_of(i * tq, tq)
    k0 = pl.multiple_of(jnp.maximum(i - 1, 0) * tq, tq)
    row = lax.broadcasted_iota(jnp.int32, (tq, tk), 0)
    colm = lax.broadcasted_iota(jnp.int32, (tq, tk), 1)
    delta = row + (r0 - k0) - colm
    band = jnp.abs(2 * delta - win) <= win
    ones = jnp.ones((tk, LANES), BF16)
    for p in range(N_HEADS_PER_MIXER // 2):
        q2 = qkv_ref[0, 0, pl.ds(r0, tq), p * LANES:(p + 1) * LANES]
        k2 = qkv_ref[0, 0, pl.ds(k0, tk), hq + p * LANES:hq + (p + 1) * LANES]
        v_aug = jnp.concatenate(
            [qkv_ref[0, 0, pl.ds(k0, tk), 2 * hq + p * LANES:2 * hq + (p + 1) * LANES], ones], axis=1)
        zero = jnp.zeros_like(q2)
        outs = []
        lses = []
        for hh in range(2):
            qh = jnp.where(first, q2, zero) if hh == 0 else jnp.where(first, zero, q2)
            s = jnp.where(band, _dot_nt(qh, k2), -jnp.inf)
            m = jnp.max(s, axis=-1, keepdims=True)
            pvx = _dot(jnp.exp(s - m).astype(BF16), v_aug)
            l = pvx[:, LANES:2 * LANES]
            outs.append(pvx[:, 0:LANES] / l)
            lses.append(m + jnp.log(l))
        o_ref[0, :, p * LANES:(p + 1) * LANES] = jnp.where(first, outs[0], outs[1])
        o_ref[0, :, hq + p * LANES:hq + (p + 1) * LANES] = jnp.where(first, lses[0], lses[1])


def _dil_call(qkv, tqs, win):
    bsz, r, n, w = qkv.shape
    hq = N_HEADS_PER_MIXER * HEAD_DIM
    tk = 2 * tqs if n >= 2 * tqs else tqs
    assert tqs >= win and n % tqs == 0 and (tk == 2 * tqs or n == tqs)
    return pl.pallas_call(
        functools.partial(_dil_kernel, tq=tqs, tk=tk, win=win),
        grid=(bsz, r, n // tqs),
        in_specs=[pl.BlockSpec((1, 1, n, w), lambda b, p, c: (b, p, 0, 0))],
        out_specs=pl.BlockSpec((1, tqs, 2 * hq), lambda b, p, c: (b, c, p)),
        out_shape=jax.ShapeDtypeStruct((bsz, n, r * 2 * hq), F32),
        compiler_params=_cparams(("parallel", "parallel", "arbitrary")),
        name=f"dil_r{r}",
    )(qkv)


def _post_kernel(x_ref, osb_ref, ofox_ref, omla_ref, d1_ref, d2_ref, d3_ref, gd_ref, wo_ref,
                 lng_ref, lnb_ref, wr_ref, br_ref, x1_ref, x1t_ref, route_ref, stage_ref, *, alpha, dils):
    hq = N_HEADS_PER_MIXER * HEAD_DIM
    tm = x_ref.shape[0]
    h = _dot(osb_ref[...], wo_ref[0:hq, :])
    h += _dot(ofox_ref[...], wo_ref[hq:2 * hq, :])
    h += _dot(omla_ref[...], wo_ref[2 * hq:3 * hq, :])

    def token_rows(g, d_ref, r, c0):
        if r == 1:
            return d_ref[:, c0:c0 + LANES]
        st = stage_ref.at[g, c0 // LANES]
        for q in range(r):
            st[pl.ds(q, tm // r, stride=r), :] = d_ref[:, q * 2 * hq + c0:q * 2 * hq + c0 + LANES]
        return st[...]

    d_refs = (d1_ref, d2_ref, d3_ref)
    for p in range(N_HEADS_PER_MIXER // 2):
        sl = slice(p * LANES, (p + 1) * LANES)
        lses = [token_rows(g, d_refs[g], dils[g], hq + p * LANES) for g in range(len(dils))]
        vals = [token_rows(g, d_refs[g], dils[g], p * LANES) for g in range(len(dils))]
        l1, l2, l3 = lses
        m = jnp.maximum(jnp.maximum(l1, l2), l3)
        e1, e2, e3 = jnp.exp(l1 - m), jnp.exp(l2 - m), jnp.exp(l3 - m)
        inv = 1.0 / (e1 + e2 + e3)
        od = (e1 * inv) * vals[0] + (e2 * inv) * vals[1] + (e3 * inv) * vals[2]
        od = _head_rms(od, gd_ref[:, sl]).astype(BF16)
        h += _dot(od, wo_ref[3 * hq + p * LANES:3 * hq + (p + 1) * LANES, :])
    x1 = _layernorm(alpha * x_ref[...] + h, lng_ref[...], lnb_ref[...])
    x1_ref[...] = x1
    for j in range(x1.shape[1] // LANES):
        x1t_ref[pl.ds(j, tm, stride=SUBLANES), :] = x1[:, j * LANES:(j + 1) * LANES]

    logits = _dot_f32(x1, wr_ref[...]) + br_ref[...]
    lane = _lane_iota(logits.shape)
    lanef = lane.astype(F32)
    big = float(LANES)
    ninf = -jnp.inf
    gl = jnp.where(lane < N_GROUPS, logits, ninf)
    gmax = jnp.max(gl, axis=-1, keepdims=True)
    gsel = jnp.min(jnp.where(gl == gmax, lanef, big), axis=-1, keepdims=True)
    gw = 1.0 / jnp.sum(jnp.exp(gl - gmax), axis=-1, keepdims=True)
    e_lo = N_GROUPS + EXPERTS_PER_GROUP * gsel
    in_grp = (lanef >= e_lo) & (lanef < e_lo + EXPERTS_PER_GROUP)
    el = jnp.where(in_grp, logits, ninf)
    t1 = jnp.max(el, axis=-1, keepdims=True)
    i1 = jnp.min(jnp.where(el == t1, lanef, big), axis=-1, keepdims=True)
    el2 = jnp.where(lanef == i1, ninf, el)
    t2 = jnp.max(el2, axis=-1, keepdims=True)
    i2 = jnp.min(jnp.where(el2 == t2, lanef, big), axis=-1, keepdims=True)
    ex = jnp.exp(t2 - t1)
    den = 1.0 + ex
    g1 = gw / den
    g2 = gw * ex / den
    out = jnp.where(lane == 0, i1 - N_GROUPS,
                    jnp.where(lane == 1, i2 - N_GROUPS,
                              jnp.where(lane == 2, g1, jnp.where(lane == 3, g2, 0.0))))
    route_ref[...] = out


def _post_call(x, osb, ofox, omla, dil_outs, dils, g_dil, wo, lng, lnb, wr, br, alpha, tm):
    t, d = x.shape
    hq = N_HEADS_PER_MIXER * HEAD_DIM
    row = lambda w: pl.BlockSpec((tm, w), lambda i: (i, 0))
    full = lambda a: pl.BlockSpec(a.shape, lambda i: (0, 0))
    dil_specs = [pl.BlockSpec((tm // r, r * 2 * hq), lambda i: (i, 0)) for r in dils]
    return pl.pallas_call(
        functools.partial(_post_kernel, alpha=alpha, dils=dils),
        grid=(t // tm,),
        in_specs=[row(d), row(hq), row(hq), row(hq)] + dil_specs
                 + [full(g_dil), full(wo), full(lng), full(lnb), full(wr), full(br)],
        out_specs=[row(d), pl.BlockSpec((tm * SUBLANES, LANES), lambda i: (i, 0)), row(LANES)],
        out_shape=[jax.ShapeDtypeStruct((t, d), F32), jax.ShapeDtypeStruct((t * SUBLANES, LANES), F32),
                   jax.ShapeDtypeStruct((t, LANES), F32)],
        scratch_shapes=[pltpu.VMEM((len(dils), 2 * hq // LANES, tm, LANES), F32)],
        compiler_params=_cparams(("parallel",)),
        name="post_mixer",
    )(x, osb, ofox, omla, *dil_outs, g_dil, wo, lng, lnb, wr, br)


DMA_UNROLL = 8


def _moe_kernel(ce_ref, nv_ref, src_ref, srcn_ref, dst_ref, x_hbm, w1_ref, w3_ref, w2_ref, y_hbm,
                xs_ref, ys_ref, gsem, ssem):
    c = pl.program_id(0)
    nc = pl.num_programs(0)
    slot = c % 2
    other = 1 - slot
    nv = nv_ref[c]
    nv_next = jnp.where(c + 1 < nc, nv_ref[jnp.minimum(c + 1, nc - 1)], 0)
    nv_prev = jnp.where(c > 0, nv_ref[jnp.maximum(c - 1, 0)], 0)

    def gather_copy(idx_ref, i, s):
        return pltpu.make_async_copy(
            x_hbm.at[pl.ds(pl.multiple_of(idx_ref[0, 0, i] * SUBLANES, SUBLANES), SUBLANES), :],
            xs_ref.at[s, pl.ds(pl.multiple_of(i * SUBLANES, SUBLANES), SUBLANES), :], gsem.at[s])

    def scatter_copy(i, s):
        return pltpu.make_async_copy(
            ys_ref.at[s, pl.ds(pl.multiple_of(i * SUBLANES, SUBLANES), SUBLANES), :],
            y_hbm.at[pl.ds(pl.multiple_of(dst_ref[0, 0, i] * SUBLANES, SUBLANES), SUBLANES), :], ssem.at[s])

    def start_rows(n, make):
        ngrp = n // DMA_UNROLL

        def grp(g, _):
            base = pl.multiple_of(g * DMA_UNROLL, DMA_UNROLL)
            for k in range(DMA_UNROLL):
                make(base + k).start()
            return 0

        def one(i, _):
            make(i).start()
            return 0

        lax.fori_loop(0, ngrp, grp, 0)
        lax.fori_loop(ngrp * DMA_UNROLL, n, one, 0)

    def wait_rows(n, make_row, make_block):
        @pl.when(n == MOE_BLOCK)
        def _():
            make_block().wait()

        @pl.when(n < MOE_BLOCK)
        def _():
            def one(i, _):
                make_row(i).wait()
                return 0
            lax.fori_loop(0, n, one, 0)

    def gather_block(s):
        return pltpu.make_async_copy(x_hbm.at[pl.ds(0, MOE_BLOCK * SUBLANES), :], xs_ref.at[s], gsem.at[s])

    def scatter_block(s):
        return pltpu.make_async_copy(ys_ref.at[s], y_hbm.at[pl.ds(0, MOE_BLOCK * SUBLANES), :], ssem.at[s])

    @pl.when(c == 0)
    def _():
        xs_ref[...] = jnp.zeros_like(xs_ref)
        start_rows(nv, lambda i: gather_copy(src_ref, i, 0))

    start_rows(nv_next, lambda i: gather_copy(srcn_ref, i, other))
    wait_rows(nv, lambda i: gather_copy(src_ref, i, slot), lambda: gather_block(slot))

    @pl.when(nv > 0)
    def _():
        xt = xs_ref.at[slot]
        a = b = None
        for jj in range(0, xt.shape[0] // MOE_BLOCK, 2):
            xb = jnp.concatenate([xt[pl.ds(jj, MOE_BLOCK, stride=SUBLANES), :],
                                  xt[pl.ds(jj + 1, MOE_BLOCK, stride=SUBLANES), :]], axis=1).astype(BF16)
            ws = slice(jj * LANES, (jj + 2) * LANES)
            da, db = _dot(xb, w1_ref[0, ws, :]), _dot(xb, w3_ref[0, ws, :])
            a, b = (da, db) if a is None else (a + da, b + db)
        hid = (a / (1.0 + jnp.exp(-a)) * b).astype(BF16)
        y = _dot(hid, w2_ref[0])
        yt = ys_ref.at[slot]
        for j in range(y.shape[1] // LANES):
            yt[pl.ds(j, MOE_BLOCK, stride=SUBLANES), :] = y[:, j * LANES:(j + 1) * LANES]

    wait_rows(nv_prev, lambda i: scatter_copy(i, other), lambda: scatter_block(other))
    start_rows(nv, lambda i: scatter_copy(i, slot))

    @pl.when(c == nc - 1)
    def _():
        wait_rows(nv, lambda i: scatter_copy(i, slot), lambda: scatter_block(slot))


def _moe_call(chunk_expert, n_valid, src, dst, x1t, w1, w3, w2, n_rows_out):
    n_chunks = chunk_expert.shape[0]
    d, de = w1.shape[1], w1.shape[2]
    assert d == SUBLANES * LANES and x1t.shape[1] == LANES
    grid_spec = pltpu.PrefetchScalarGridSpec(
        num_scalar_prefetch=2,
        grid=(n_chunks,),
        in_specs=[
            pl.BlockSpec((1, 1, MOE_BLOCK), lambda c, ce, nv: (c, 0, 0), memory_space=pltpu.SMEM),
            pl.BlockSpec((1, 1, MOE_BLOCK), lambda c, ce, nv: (jnp.minimum(c + 1, n_chunks - 1), 0, 0),
                         memory_space=pltpu.SMEM),
            pl.BlockSpec((1, 1, MOE_BLOCK), lambda c, ce, nv: (c, 0, 0), memory_space=pltpu.SMEM),
            pl.BlockSpec(memory_space=pl.ANY),
            pl.BlockSpec((1, d, de), lambda c, ce, nv: (ce[c], 0, 0)),
            pl.BlockSpec((1, d, de), lambda c, ce, nv: (ce[c], 0, 0)),
            pl.BlockSpec((1, de, d), lambda c, ce, nv: (ce[c], 0, 0)),
        ],
        out_specs=pl.BlockSpec(memory_space=pl.ANY),
        scratch_shapes=[
            pltpu.VMEM((2, MOE_BLOCK * SUBLANES, LANES), F32),
            pltpu.VMEM((2, MOE_BLOCK * SUBLANES, LANES), F32),
            pltpu.SemaphoreType.DMA((2,)),
            pltpu.SemaphoreType.DMA((2,)),
        ],
    )
    return pl.pallas_call(
        _moe_kernel,
        grid_spec=grid_spec,
        out_shape=jax.ShapeDtypeStruct((n_rows_out * SUBLANES, LANES), F32),
        compiler_params=_cparams(("arbitrary",)),
        name="moe_experts",
    )(chunk_expert, n_valid, src, src, dst, x1t, w1, w3, w2)


def _combine_kernel(x1_ref, ya_ref, yb_ref, route_ref, lng_ref, lnb_ref, o_ref, *, alpha):
    tm, d = x1_ref.shape
    g1 = route_ref[:, 2:3]
    g2 = route_ref[:, 3:4]

    def rows(y_ref):
        return jnp.concatenate([y_ref[pl.ds(j, tm, stride=SUBLANES), :] for j in range(d // LANES)], axis=1)

    m = g1 * rows(ya_ref) + g2 * rows(yb_ref)
    o_ref[...] = _layernorm(alpha * x1_ref[...] + m, lng_ref[...], lnb_ref[...])


def _combine_call(x1, y, route, lng, lnb, alpha, tm):
    t, d = x1.shape
    nt = t // tm
    full = lambda a: pl.BlockSpec(a.shape, lambda i: (0, 0))
    return pl.pallas_call(
        functools.partial(_combine_kernel, alpha=alpha),
        grid=(nt,),
        in_specs=[pl.BlockSpec((tm, d), lambda i: (i, 0)),
                  pl.BlockSpec((tm * SUBLANES, LANES), lambda i: (i, 0)),
                  pl.BlockSpec((tm * SUBLANES, LANES), lambda i: (i + nt, 0)),
                  pl.BlockSpec((tm, LANES), lambda i: (i, 0)),
                  full(lng), full(lnb)],
        out_specs=pl.BlockSpec((tm, d), lambda i: (i, 0)),
        out_shape=jax.ShapeDtypeStruct((t, d), F32),
        compiler_params=_cparams(("parallel",)),
        name="moe_combine",
    )(x1, y, y, route, lng, lnb)


def _rope_tables(seq, dim, lane_lo):
    half = dim // 2
    inv_freq = ROPE_THETA ** (-jnp.arange(half, dtype=F32) / half)
    ang = jnp.arange(seq, dtype=F32)[:, None] * inv_freq[None, :]
    cos = jnp.concatenate([jnp.cos(ang), jnp.cos(ang)], -1)
    sin = jnp.concatenate([jnp.sin(ang), jnp.sin(ang)], -1)
    if lane_lo == 0:
        reps = LANES // dim
        return jnp.tile(cos, (1, reps)), jnp.tile(sin, (1, reps))
    cos_t = jnp.ones((seq, LANES), F32).at[:, lane_lo:lane_lo + dim].set(cos)
    sin_t = jnp.zeros((seq, LANES), F32).at[:, lane_lo:lane_lo + dim].set(sin)
    return cos_t, sin_t


def _dispatch_tables(route, n_tok):
    expert_id = route[:, 0:TOP_K].astype(jnp.int32).reshape(-1)
    n_assign = n_tok * TOP_K
    n_slots = n_assign + N_EXPERTS * MOE_BLOCK
    n_chunks = n_slots // MOE_BLOCK
    onehot = (expert_id[:, None] == jnp.arange(N_EXPERTS, dtype=jnp.int32)[None, :]).astype(jnp.int32)
    ranks = jnp.cumsum(onehot, axis=0) - onehot
    rank = jnp.sum(ranks * onehot, axis=1)
    counts = jnp.sum(onehot, axis=0)
    padded = (counts + MOE_BLOCK - 1) // MOE_BLOCK * MOE_BLOCK
    pad_end = jnp.cumsum(padded)
    pad_start = pad_end - padded
    dest = pad_start[expert_id] + rank
    assign = jnp.arange(n_assign, dtype=jnp.int32)
    slot_assign = jnp.zeros((n_slots,), jnp.int32).at[dest].set(assign)
    src = slot_assign // TOP_K
    dst = (slot_assign % TOP_K) * n_tok + src
    chunk_start = jnp.arange(n_chunks, dtype=jnp.int32) * MOE_BLOCK
    chunk_expert = jnp.minimum(jnp.searchsorted(pad_end, chunk_start, side="right"),
                               N_EXPERTS - 1).astype(jnp.int32)
    n_valid = jnp.clip(pad_start[chunk_expert] + counts[chunk_expert] - chunk_start,
                       0, MOE_BLOCK).astype(jnp.int32)
    return (chunk_expert, n_valid, src.reshape(n_chunks, 1, MOE_BLOCK),
            dst.reshape(n_chunks, 1, MOE_BLOCK), n_assign)


def _pick_tile(n, pref):
    t = pref
    while n % t:
        t //= 2
    return t


def kernel(x, w_in, b_forget, g_cq, g_ckv, w_uq, w_ukv, g_head, w_out, ln1_g, ln1_b,
           w_group, b_group, w_expert, b_expert, w1, w3, w2, ln2_g, ln2_b):
    bsz, seq, d = x.shape
    depth = w_in.shape[0]
    t = bsz * seq
    alpha = (2.0 * depth) ** 0.25
    hq = N_HEADS_PER_MIXER * HEAD_DIM
    qk_scale = HEAD_DIM ** -0.5
    mla_scale = (MLA_NOPE + MLA_ROPE) ** -0.5
    win = DIL_BRANCHES[0][0]
    assert all(w // r == win for w, r in DIL_BRANCHES)
    assert seq % (DIL_BRANCHES[-1][1] * win) == 0 and d % LANES == 0

    cos64, sin64 = _rope_tables(seq, HEAD_DIM, 0)
    cos_m, sin_m = _rope_tables(seq, MLA_ROPE, MLA_NOPE)
    tq = _pick_tile(seq, 256)
    idx = jnp.arange(tq)
    u_sb = (idx[:, None] > idx[None, :]).astype(BF16)
    tk_sb = _pick_tile(seq, 2 * tq)
    tk_sm = _pick_tile(seq, 4 * tq)

    for l in range(depth):
        wl = w_in[l]
        o_fox, o_mla, o_dil = N_SB, N_SB + N_FOX_QKV + N_HEADS_PER_MIXER, N_SB + N_FOX_QKV + N_HEADS_PER_MIXER + N_MLA
        qs = lambda w: w.at[:, 0:hq].multiply(qk_scale)
        w_sb = qs(wl[:, 0:N_SB])
        w_fx = qs(wl[:, o_fox:o_fox + N_FOX_QKV])
        w_f = wl[:, o_fox + N_FOX_QKV:o_mla]
        w_ml = wl[:, o_mla:o_dil]
        wd = wl[:, o_dil:].reshape(d, 3, len(DIL_BRANCHES), hq)
        w_br = [qs(jnp.concatenate([wd[:, 0, g], wd[:, 1, g], wd[:, 2, g]], axis=1)) for g in range(len(DIL_BRANCHES))]
        w_misc = jnp.concatenate([w_ml, w_f, jnp.zeros((d, MISC_W - N_MLA - N_HEADS_PER_MIXER), F32)], axis=1)
        w_tok = jnp.concatenate([w_sb, w_fx, w_misc] + w_br, axis=1).astype(BF16)

        wq = jnp.pad(w_uq[l].reshape(MLA_Q_LORA, N_HEADS_PER_MIXER, MLA_NOPE + MLA_ROPE),
                     ((0, 0), (0, 0), (0, LANES - MLA_NOPE - MLA_ROPE))).reshape(MLA_Q_LORA, -1).astype(BF16)
        wkv = w_ukv[l].reshape(MLA_KV_LORA, N_HEADS_PER_MIXER, MLA_NOPE + HEAD_DIM)
        wk = jnp.pad(wkv[:, :, :MLA_NOPE], ((0, 0), (0, 0), (0, LANES - MLA_NOPE))).reshape(MLA_KV_LORA, -1).astype(BF16)
        wv = wkv[:, :, MLA_NOPE:].reshape(MLA_KV_LORA, -1).astype(BF16)
        g_flat = g_head[l].reshape(1, -1)
        wr = jnp.concatenate([w_group[l], w_expert[l],
                              jnp.zeros((d, LANES - N_GROUPS - N_EXPERTS), F32)], axis=1)
        br = jnp.concatenate([b_group[l], b_expert[l],
                              jnp.zeros((LANES - N_GROUPS - N_EXPERTS,), F32)]).reshape(1, LANES)

        tm = _pick_tile(seq, 512)
        sb, fx, misc, *qkv_br = _proj_call(
            x, w_tok, cos64, sin64,
            ((N_SB, 0, BF16, 1), (N_FOX_QKV, 0, BF16, 1), (MISC_W, 0, F32, 1))
            + tuple((N_BRANCH, 2 * hq, BF16, r) for _, r in DIL_BRANCHES), tm)
        sb = sb.reshape(t, N_SB)
        fx = fx.reshape(t, N_FOX_QKV)
        misc = misc.reshape(t, MISC_W)

        o_sb = _attn_call("sb", sb, sb, sb, 0, 1, 2, g_flat, 0, bsz, seq, tq, tk_sb, extra=(u_sb,))
        neg_c = _fox_c_call(misc, b_forget[l], bsz, seq)
        o_fx = _attn_call("fox", fx, fx, fx, 0, 1, 2, g_flat, 1, bsz, seq, tq, tk_sm, extra=(neg_c,))
        mq, mk, mv = _mla_prep_call(misc, g_cq[l].reshape(1, -1), g_ckv[l].reshape(1, -1), wq, wk, wv,
                                    cos_m, sin_m, seq, tm)
        o_ml = _attn_call("mla", mq, mk, mv, 0, 0, 0, g_flat, 2, bsz, seq, tq, tk_sm, scale=mla_scale)

        dil = []
        for g, (_, r) in enumerate(DIL_BRANCHES):
            n = seq // r
            og = _dil_call(qkv_br[g], max(win, min(tq, n // 2)), win)
            dil.append(og.reshape(t // r, r * 2 * hq))

        x1, x1t, route = _post_call(
            x.reshape(t, d), o_sb, o_fx, o_ml, dil, tuple(r for _, r in DIL_BRANCHES),
            g_flat[:, 3 * hq:], w_out[l].astype(BF16),
            ln1_g[l].reshape(1, d), ln1_b[l].reshape(1, d), wr, br, alpha, _pick_tile(t, 256))

        chunk_expert, n_valid, src, dst, n_rows = _dispatch_tables(route, t)
        y = _moe_call(chunk_expert, n_valid, src, dst, x1t,
                      w1[l].astype(BF16), w3[l].astype(BF16), w2[l].astype(BF16), n_rows)
        x = _combine_call(x1, y, route,
                          ln2_g[l].reshape(1, d), ln2_b[l].reshape(1, d), alpha, _pick_tile(t, 256)).reshape(bsz, seq, d)
    return x
```

```python
import functools

import jax
import jax.numpy as jnp
from jax import lax
from jax.experimental import pallas as pl
from jax.experimental.pallas import tpu as pltpu

F32 = jnp.float32
BF16 = jnp.bfloat16

HEAD_DIM = 64
N_HEADS_PER_MIXER = 4
MLA_Q_LORA = 256
MLA_KV_LORA = 128
MLA_NOPE = 64
MLA_ROPE = 32
DIL_BRANCHES = ((128, 1), (512, 4), (2048, 16))
ROPE_THETA = 10000.0
N_GROUPS = 4
EXPERTS_PER_GROUP = 4
N_EXPERTS = N_GROUPS * EXPERTS_PER_GROUP
TOP_K = 2
MOE_BLOCK = 256
LN_EPS = 1e-5
RMS_EPS = 1e-6

LANES = 128
SUBLANES = 8
VMEM_LIMIT_BYTES = 56 * 1024 * 1024

N_SB = 3 * N_HEADS_PER_MIXER * HEAD_DIM
N_FOX_QKV = 3 * N_HEADS_PER_MIXER * HEAD_DIM
N_MLA = MLA_Q_LORA + MLA_KV_LORA + MLA_ROPE
N_BRANCH = 3 * N_HEADS_PER_MIXER * HEAD_DIM
MISC_W = 512
F_COL = N_MLA


def _cparams(sem):
    return pltpu.CompilerParams(dimension_semantics=sem, vmem_limit_bytes=VMEM_LIMIT_BYTES)


def _split3(a):
    hi = a.astype(BF16)
    r1 = a - hi.astype(F32)
    mid = r1.astype(BF16)
    lo = (r1 - mid.astype(F32)).astype(BF16)
    return hi, mid, lo


def _dot(a, b):
    return jnp.dot(a, b, preferred_element_type=F32)


def _dot_nt(a, b):
    return lax.dot_general(a, b, (((1,), (1,)), ((), ())), preferred_element_type=F32)


def _dot_exact_rhs(a, u):
    hi, mid, lo = _split3(a)
    return _dot(hi, u) + _dot(mid, u) + _dot(lo, u)


def _dot_f32(a, b):
    ah = a.astype(BF16)
    al = (a - ah.astype(F32)).astype(BF16)
    bh = b.astype(BF16)
    bl = (b - bh.astype(F32)).astype(BF16)
    return _dot(ah, bh) + (_dot(ah, bl) + _dot(al, bh))


def _lane_iota(shape):
    return lax.broadcasted_iota(jnp.int32, shape, len(shape) - 1)


def _rotate_half(y, half):
    lane = _lane_iota(y.shape)
    fwd = pltpu.roll(y, half, 1)
    bwd = pltpu.roll(y, LANES - half, 1)
    return jnp.where((lane % (2 * half)) < half, -bwd, fwd)


def _log_sigmoid_pair(z):
    sp = jnp.log(1.0 + jnp.exp(-jnp.abs(z)))
    return jnp.minimum(z, 0.0) - sp, -jnp.maximum(z, 0.0) - sp


def _head_rms(o, g):
    lane = _lane_iota(o.shape)
    first = lane < HEAD_DIM
    sq = o * o
    ss_a = jnp.sum(jnp.where(first, sq, 0.0), axis=-1, keepdims=True)
    ss_b = jnp.sum(jnp.where(first, 0.0, sq), axis=-1, keepdims=True)
    ms = jnp.where(first, ss_a, ss_b) * (1.0 / HEAD_DIM)
    return o * lax.rsqrt(ms + RMS_EPS) * g


def _layernorm(y, g, b):
    mu = jnp.mean(y, axis=-1, keepdims=True)
    d = y - mu
    var = jnp.mean(d * d, axis=-1, keepdims=True)
    return d * lax.rsqrt(var + LN_EPS) * g + b


def _proj_kernel(x_ref, w_ref, cos_ref, sin_ref, *refs, outs):
    out_refs, stage_ref = refs[:len(outs)], refs[len(outs)]
    tm = x_ref.shape[1]
    xb = x_ref[0].astype(BF16)
    col = 0
    slab = 0
    for o_ref, (width, n_rope, _, r) in zip(out_refs, outs):
        for c in range(0, width, 2 * LANES):
            cw = min(2 * LANES, width - c)
            y = _dot(xb, w_ref[:, col + c:col + c + cw])
            for s in range(0, cw, LANES):
                ys = y[:, s:s + LANES]
                if c + s < n_rope:
                    ys = ys * cos_ref[...] + _rotate_half(ys, HEAD_DIM // 2) * sin_ref[...]
                if r == 1:
                    o_ref[0, 0, :, c + s:c + s + LANES] = ys.astype(o_ref.dtype)
                else:
                    st = stage_ref.at[slab % stage_ref.shape[0]]
                    slab += 1
                    st[...] = ys
                    for p in range(r):
                        o_ref[0, p, :, c + s:c + s + LANES] = (
                            st[pl.ds(p, tm // r, stride=r), :].astype(o_ref.dtype))
        col += width


PROJ_STAGE_SLABS = 4


def _proj_call(x, w, cos, sin, outs, tm):
    bsz, seq, d = x.shape
    nt = seq // tm
    in_specs = [
        pl.BlockSpec((1, tm, d), lambda b, i: (b, i, 0)),
        pl.BlockSpec(w.shape, lambda b, i: (0, 0)),
        pl.BlockSpec((tm, LANES), lambda b, i: (i, 0)),
        pl.BlockSpec((tm, LANES), lambda b, i: (i, 0)),
    ]
    out_specs = [pl.BlockSpec((1, r, tm // r, wd), lambda b, i: (b, 0, i, 0)) for wd, _, _, r in outs]
    out_shape = [jax.ShapeDtypeStruct((bsz, r, seq // r, wd), dt) for wd, _, dt, r in outs]
    return pl.pallas_call(
        functools.partial(_proj_kernel, outs=outs),
        grid=(bsz, nt), in_specs=in_specs, out_specs=out_specs, out_shape=out_shape,
        scratch_shapes=[pltpu.VMEM((PROJ_STAGE_SLABS, tm, LANES), F32)],
        compiler_params=_cparams(("parallel", "parallel")),
        name="proj",
    )(x, w, cos, sin)


def _fox_c_kernel(misc_ref, bias_ref, uinc_ref, ones_ref, out_ref, *, seq):
    lane0 = F_COL - 3 * LANES
    nblk = seq // LANES

    def body(j, carry):
        r0 = pl.multiple_of(j * LANES, LANES)
        f = misc_ref[pl.ds(r0, LANES), :] + bias_ref[...]
        lf, _ = _log_sigmoid_pair(f)
        lft = lf.T
        csum = _dot_exact_rhs(lft, uinc_ref[...]) + carry
        tot = _dot_exact_rhs(lft, ones_ref[...])
        out_ref[0, :, pl.ds(r0, LANES)] = -csum[lane0:lane0 + N_HEADS_PER_MIXER, :]
        return carry + tot

    lax.fori_loop(0, nblk, body, jnp.zeros((LANES, LANES), F32))


def _fox_c_call(misc, b_forget, bsz, seq):
    bias = jnp.zeros((1, LANES), F32).at[0, F_COL - 3 * LANES:F_COL - 3 * LANES + N_HEADS_PER_MIXER].set(b_forget)
    idx = jnp.arange(LANES)
    uinc = (idx[:, None] <= idx[None, :]).astype(BF16)
    ones = jnp.ones((LANES, LANES), BF16)
    return pl.pallas_call(
        functools.partial(_fox_c_kernel, seq=seq),
        grid=(bsz,),
        in_specs=[
            pl.BlockSpec((seq, LANES), lambda b: (b, 3)),
            pl.BlockSpec((1, LANES), lambda b: (0, 0)),
            pl.BlockSpec((LANES, LANES), lambda b: (0, 0)),
            pl.BlockSpec((LANES, LANES), lambda b: (0, 0)),
        ],
        out_specs=pl.BlockSpec((1, N_HEADS_PER_MIXER, seq), lambda b: (b, 0, 0)),
        out_shape=jax.ShapeDtypeStruct((bsz, N_HEADS_PER_MIXER, seq), F32),
        compiler_params=_cparams(("parallel",)),
        name="fox_c",
    )(misc, bias, uinc, ones)


def _mla_prep_kernel(misc_ref, gq_ref, gkv_ref, wq_ref, wk_ref, wv_ref, cos_ref, sin_ref,
                     q_ref, k_ref, v_ref):
    def rms(x, g):
        return x * lax.rsqrt(jnp.mean(x * x, axis=-1, keepdims=True) + RMS_EPS) * g

    cq = rms(misc_ref[:, 0:MLA_Q_LORA], gq_ref[...]).astype(BF16)
    ckv = rms(misc_ref[:, MLA_Q_LORA:MLA_Q_LORA + MLA_KV_LORA], gkv_ref[...]).astype(BF16)
    kr_blk = misc_ref[:, 3 * LANES:4 * LANES]
    lane = _lane_iota(kr_blk.shape)
    in_rope = (lane >= MLA_NOPE) & (lane < MLA_NOPE + MLA_ROPE)
    kr = jnp.where(in_rope, pltpu.roll(kr_blk, MLA_NOPE, 1), 0.0)
    cos = cos_ref[...]
    sin = sin_ref[...]

    def rope(y):
        return y * cos + _rotate_half(y, MLA_ROPE // 2) * sin

    q = _dot(cq, wq_ref[...])
    k = _dot(ckv, wk_ref[...])
    for h in range(N_HEADS_PER_MIXER):
        sl = slice(h * LANES, (h + 1) * LANES)
        q_ref[:, sl] = rope(q[:, sl]).astype(BF16)
        k_ref[:, sl] = rope(k[:, sl] + kr).astype(BF16)
    v_ref[...] = _dot(ckv, wv_ref[...]).astype(BF16)


def _mla_prep_call(misc, g_cq, g_ckv, wq, wk, wv, cos, sin, seq, tm):
    t = misc.shape[0]
    nper = seq // tm
    hw = N_HEADS_PER_MIXER * LANES
    vw = N_HEADS_PER_MIXER * HEAD_DIM
    full = lambda a: pl.BlockSpec(a.shape, lambda i: (0, 0))
    return pl.pallas_call(
        _mla_prep_kernel,
        grid=(t // tm,),
        in_specs=[
            pl.BlockSpec((tm, MISC_W), lambda i: (i, 0)),
            full(g_cq), full(g_ckv), full(wq), full(wk), full(wv),
            pl.BlockSpec((tm, LANES), lambda i: (i % nper, 0)),
            pl.BlockSpec((tm, LANES), lambda i: (i % nper, 0)),
        ],
        out_specs=[
            pl.BlockSpec((tm, hw), lambda i: (i, 0)),
            pl.BlockSpec((tm, hw), lambda i: (i, 0)),
            pl.BlockSpec((tm, vw), lambda i: (i, 0)),
        ],
        out_shape=[
            jax.ShapeDtypeStruct((t, hw), BF16),
            jax.ShapeDtypeStruct((t, hw), BF16),
            jax.ShapeDtypeStruct((t, vw), BF16),
        ],
        compiler_params=_cparams(("parallel",)),
        name="mla_prep",
    )(misc, g_cq, g_ckv, wq, wk, wv, cos, sin)


def _attn_kernel(*refs, mode, tq, tk, scale):
    if mode == "sb":
        q_ref, k_ref, v_ref, g_ref, u_ref, o_ref = refs
    elif mode == "fox":
        q_ref, k_ref, v_ref, g_ref, nc_ref, o_ref = refs
    else:
        q_ref, k_ref, v_ref, g_ref, o_ref = refs
    nh = N_HEADS_PER_MIXER
    i = pl.program_id(1)
    lane = _lane_iota((tq, LANES))
    first = lane < HEAD_DIM
    row = lax.broadcasted_iota(jnp.int32, (tq, tk), 0)
    colm = lax.broadcasted_iota(jnp.int32, (tq, tk), 1)

    q_heads = []
    for h in range(nh):
        if mode == "mla":
            q_heads.append(q_ref[:, h * LANES:(h + 1) * LANES])
        else:
            q2 = q_ref[:, (h // 2) * LANES:(h // 2 + 1) * LANES]
            zero = jnp.zeros_like(q2)
            q_heads.append(jnp.where(first, q2, zero) if h % 2 == 0 else jnp.where(first, zero, q2))

    def k_head(j, h):
        r0 = pl.multiple_of(j * tk, tk)
        kb = h if mode == "mla" else h // 2
        return k_ref[pl.ds(r0, tk), kb * LANES:(kb + 1) * LANES]

    def v_pair(j, p):
        r0 = pl.multiple_of(j * tk, tk)
        return v_ref[pl.ds(r0, tk), p * LANES:(p + 1) * LANES]

    jd = (i * tq) // tk
    off = i * tq - jd * tk

    if mode == "sb":
        tu = u_ref.shape[0]

        def step(j, carry, diag):
            accs, rs = carry
            new_accs, new_rs = [], []
            for p in range(nh // 2):
                v2 = v_pair(j, p)
                outs = []
                for h in (2 * p, 2 * p + 1):
                    z = _dot_nt(q_heads[h], k_head(j, h))
                    ls_pos, ls_neg = _log_sigmoid_pair(z)
                    if diag:
                        before = colm < row + off
                        ls_neg = jnp.where(before, ls_neg, 0.0)
                    lb = ls_neg.astype(BF16)
                    pieces = []
                    later = rs[h]
                    for c0 in range(tk - tu, -1, -tu):
                        c = _dot(lb[:, c0:c0 + tu], u_ref[...]) + later
                        pieces.insert(0, c)
                        later = c[:, 0:1] + ls_neg[:, c0:c0 + 1]
                    w = jnp.exp(ls_pos + jnp.concatenate(pieces, axis=1))
                    if diag:
                        w = jnp.where(before, w, 0.0)
                    outs.append(_dot(w.astype(BF16), v2))
                    new_rs.append(later)
                new_accs.append(accs[p] + jnp.where(first, outs[0], outs[1]))
            return tuple(new_accs), tuple(new_rs)

        zacc = jnp.zeros((tq, LANES), F32)
        zr = jnp.zeros((tq, 1), F32)
        carry = step(jd, ((zacc,) * (nh // 2), (zr,) * nh), True)
        carry = lax.fori_loop(0, jd, lambda n, c: step(jd - 1 - n, c, False), carry)
        outs = carry[0]
    else:
        ones = jnp.ones((tk, LANES), BF16)

        def step(j, carry, diag):
            accs, ms, ls = carry
            new_accs, new_ms, new_ls = [], [], []
            r0 = pl.multiple_of(j * tk, tk)
            for p in range(nh // 2):
                v_aug = jnp.concatenate([v_pair(j, p), ones], axis=1)
                pv, alphas = [], []
                for h in (2 * p, 2 * p + 1):
                    s = _dot_nt(q_heads[h], k_head(j, h))
                    if mode == "mla":
                        s = s * scale
                    else:
                        s = s + nc_ref[0, h:h + 1, pl.ds(r0, tk)]
                    if diag:
                        s = jnp.where(colm <= row + off, s, -jnp.inf)
                    m_new = jnp.maximum(ms[h], jnp.max(s, axis=-1, keepdims=True))
                    alpha = jnp.exp(ms[h] - m_new)
                    pvx = _dot(jnp.exp(s - m_new).astype(BF16), v_aug)
                    new_ls.append(alpha * ls[h] + pvx[:, LANES:2 * LANES])
                    new_ms.append(m_new)
                    pv.append(pvx[:, 0:LANES])
                    alphas.append(alpha)
                new_accs.append(accs[p] * jnp.where(first, alphas[0], alphas[1])
                                + jnp.where(first, pv[0], pv[1]))
            return tuple(new_accs), tuple(new_ms), tuple(new_ls)

        neg = jnp.full((tq, 1), -jnp.inf, F32)
        zacc = jnp.zeros((tq, LANES), F32)
        carry = step(jd, ((zacc,) * (nh // 2), (neg,) * nh, (zacc,) * nh), True)
        carry = lax.fori_loop(0, jd, lambda n, c: step(n, c, False), carry)
        accs, _, ls = carry
        outs = [accs[p] / jnp.where(first, ls[2 * p], ls[2 * p + 1]) for p in range(nh // 2)]
    for p in range(nh // 2):
        sl = slice(p * LANES, (p + 1) * LANES)
        o_ref[:, sl] = _head_rms(outs[p], g_ref[:, sl]).astype(o_ref.dtype)


def _attn_call(mode, q, k, v, qcol, kcol, vcol, g_flat, gcol, bsz, seq, tq, tk, extra=(), scale=1.0):
    t = bsz * seq
    nq = seq // tq
    hq = N_HEADS_PER_MIXER * HEAD_DIM
    qw = N_HEADS_PER_MIXER * LANES if mode == "mla" else hq
    assert tk % tq == 0 and seq % tk == 0
    in_specs = [
        pl.BlockSpec((tq, qw), lambda b, i: (b * nq + i, qcol)),
        pl.BlockSpec((seq, qw), lambda b, i: (b, kcol)),
        pl.BlockSpec((seq, hq), lambda b, i: (b, vcol)),
        pl.BlockSpec((1, hq), lambda b, i: (0, gcol)),
    ]
    args = [q, k, v, g_flat]
    if mode == "sb":
        (u,) = extra
        in_specs.append(pl.BlockSpec(u.shape, lambda b, i: (0, 0)))
        args.append(u)
    elif mode == "fox":
        (nc,) = extra
        in_specs.append(pl.BlockSpec((1, N_HEADS_PER_MIXER, seq), lambda b, i: (b, 0, 0)))
        args.append(nc)
    return pl.pallas_call(
        functools.partial(_attn_kernel, mode=mode, tq=tq, tk=tk, scale=scale),
        grid=(bsz, nq),
        in_specs=in_specs,
        out_specs=pl.BlockSpec((tq, hq), lambda b, i: (b * nq + i, 0)),
        out_shape=jax.ShapeDtypeStruct((t, hq), BF16),
        compiler_params=_cparams(("parallel", "arbitrary")),
        name=f"attn_{mode}",
    )(*args)


def _dil_kernel(qkv_ref, o_ref, *, tq, tk, win):
    i = pl.program_id(2)
    lane = _lane_iota((tq, LANES))
    first = lane < HEAD_DIM
    hq = N_HEADS_PER_MIXER * HEAD_DIM
    r0 = pl.multiple_of(i * tq, tq)
    k0 = pl.multiple_of(jnp.maximum(i - 1, 0) * tq, tq)
    row = lax.broadcasted_iota(jnp.int32, (tq, tk), 0)
    colm = lax.broadcasted_iota(jnp.int32, (tq, tk), 1)
    delta = row + (r0 - k0) - colm
    band = jnp.abs(2 * delta - win) <= win
    ones = jnp.ones((tk, LANES), BF16)
    for p in range(N_HEADS_PER_MIXER // 2):
        q2 = qkv_ref[0, 0, pl.ds(r0, tq), p * LANES:(p + 1) * LANES]
        k2 = qkv_ref[0, 0, pl.ds(k0, tk), hq + p * LANES:hq + (p + 1) * LANES]
        v_aug = jnp.concatenate(
            [qkv_ref[0, 0, pl.ds(k0, tk), 2 * hq + p * LANES:2 * hq + (p + 1) * LANES], ones], axis=1)
        zero = jnp.zeros_like(q2)
        outs = []
        lses = []
        for hh in range(2):
            qh = jnp.where(first, q2, zero) if hh == 0 else jnp.where(first, zero, q2)
            s = jnp.where(band, _dot_nt(qh, k2), -jnp.inf)
            m = jnp.max(s, axis=-1, keepdims=True)
            pvx = _dot(jnp.exp(s - m).astype(BF16), v_aug)
            l = pvx[:, LANES:2 * LANES]
            outs.append(pvx[:, 0:LANES] / l)
            lses.append(m + jnp.log(l))
        o_ref[0, :, p * LANES:(p + 1) * LANES] = jnp.where(first, outs[0], outs[1])
        o_ref[0, :, hq + p * LANES:hq + (p + 1) * LANES] = jnp.where(first, lses[0], lses[1])


def _dil_call(qkv, tqs, win):
    bsz, r, n, w = qkv.shape
    hq = N_HEADS_PER_MIXER * HEAD_DIM
    tk = 2 * tqs if n >= 2 * tqs else tqs
    assert tqs >= win and n % tqs == 0 and (tk == 2 * tqs or n == tqs)
    return pl.pallas_call(
        functools.partial(_dil_kernel, tq=tqs, tk=tk, win=win),
        grid=(bsz, r, n // tqs),
        in_specs=[pl.BlockSpec((1, 1, n, w), lambda b, p, c: (b, p, 0, 0))],
        out_specs=pl.BlockSpec((1, tqs, 2 * hq), lambda b, p, c: (b, c, p)),
        out_shape=jax.ShapeDtypeStruct((bsz, n, r * 2 * hq), F32),
        compiler_params=_cparams(("parallel", "parallel", "arbitrary")),
        name=f"dil_r{r}",
    )(qkv)


def _post_kernel(x_ref, osb_ref, ofox_ref, omla_ref, d1_ref, d2_ref, d3_ref, gd_ref, wo_ref,
                 lng_ref, lnb_ref, wr_ref, br_ref, x1_ref, x1t_ref, route_ref, stage_ref, *, alpha, dils):
    hq = N_HEADS_PER_MIXER * HEAD_DIM
    tm = x_ref.shape[0]
    h = _dot(osb_ref[...], wo_ref[0:hq, :])
    h += _dot(ofox_ref[...], wo_ref[hq:2 * hq, :])
    h += _dot(omla_ref[...], wo_ref[2 * hq:3 * hq, :])

    def token_rows(g, d_ref, r, c0):
        if r == 1:
            return d_ref[:, c0:c0 + LANES]
        st = stage_ref.at[g, c0 // LANES]
        for q in range(r):
            st[pl.ds(q, tm // r, stride=r), :] = d_ref[:, q * 2 * hq + c0:q * 2 * hq + c0 + LANES]
        return st[...]

    d_refs = (d1_ref, d2_ref, d3_ref)
    for p in range(N_HEADS_PER_MIXER // 2):
        sl = slice(p * LANES, (p + 1) * LANES)
        lses = [token_rows(g, d_refs[g], dils[g], hq + p * LANES) for g in range(len(dils))]
        vals = [token_rows(g, d_refs[g], dils[g], p * LANES) for g in range(len(dils))]
        l1, l2, l3 = lses
        m = jnp.maximum(jnp.maximum(l1, l2), l3)
        e1, e2, e3 = jnp.exp(l1 - m), jnp.exp(l2 - m), jnp.exp(l3 - m)
        inv = 1.0 / (e1 + e2 + e3)
        od = (e1 * inv) * vals[0] + (e2 * inv) * vals[1] + (e3 * inv) * vals[2]
        od = _head_rms(od, gd_ref[:, sl]).astype(BF16)
        h += _dot(od, wo_ref[3 * hq + p * LANES:3 * hq + (p + 1) * LANES, :])
    x1 = _layernorm(alpha * x_ref[...] + h, lng_ref[...], lnb_ref[...])
    x1_ref[...] = x1
    for j in range(x1.shape[1] // LANES):
        x1t_ref[pl.ds(j, tm, stride=SUBLANES), :] = x1[:, j * LANES:(j + 1) * LANES]

    logits = _dot_f32(x1, wr_ref[...]) + br_ref[...]
    lane = _lane_iota(logits.shape)
    lanef = lane.astype(F32)
    big = float(LANES)
    ninf = -jnp.inf
    gl = jnp.where(lane < N_GROUPS, logits, ninf)
    gmax = jnp.max(gl, axis=-1, keepdims=True)
    gsel = jnp.min(jnp.where(gl == gmax, lanef, big), axis=-1, keepdims=True)
    gw = 1.0 / jnp.sum(jnp.exp(gl - gmax), axis=-1, keepdims=True)
    e_lo = N_GROUPS + EXPERTS_PER_GROUP * gsel
    in_grp = (lanef >= e_lo) & (lanef < e_lo + EXPERTS_PER_GROUP)
    el = jnp.where(in_grp, logits, ninf)
    t1 = jnp.max(el, axis=-1, keepdims=True)
    i1 = jnp.min(jnp.where(el == t1, lanef, big), axis=-1, keepdims=True)
    el2 = jnp.where(lanef == i1, ninf, el)
    t2 = jnp.max(el2, axis=-1, keepdims=True)
    i2 = jnp.min(jnp.where(el2 == t2, lanef, big), axis=-1, keepdims=True)
    ex = jnp.exp(t2 - t1)
    den = 1.0 + ex
    g1 = gw / den
    g2 = gw * ex / den
    out = jnp.where(lane == 0, i1 - N_GROUPS,
                    jnp.where(lane == 1, i2 - N_GROUPS,
                              jnp.where(lane == 2, g1, jnp.where(lane == 3, g2, 0.0))))
    route_ref[...] = out


def _post_call(x, osb, ofox, omla, dil_outs, dils, g_dil, wo, lng, lnb, wr, br, alpha, tm):
    t, d = x.shape
    hq = N_HEADS_PER_MIXER * HEAD_DIM
    row = lambda w: pl.BlockSpec((tm, w), lambda i: (i, 0))
    full = lambda a: pl.BlockSpec(a.shape, lambda i: (0, 0))
    dil_specs = [pl.BlockSpec((tm // r, r * 2 * hq), lambda i: (i, 0)) for r in dils]
    return pl.pallas_call(
        functools.partial(_post_kernel, alpha=alpha, dils=dils),
        grid=(t // tm,),
        in_specs=[row(d), row(hq), row(hq), row(hq)] + dil_specs
                 + [full(g_dil), full(wo), full(lng), full(lnb), full(wr), full(br)],
        out_specs=[row(d), pl.BlockSpec((tm * SUBLANES, LANES), lambda i: (i, 0)), row(LANES)],
        out_shape=[jax.ShapeDtypeStruct((t, d), F32), jax.ShapeDtypeStruct((t * SUBLANES, LANES), F32),
                   jax.ShapeDtypeStruct((t, LANES), F32)],
        scratch_shapes=[pltpu.VMEM((len(dils), 2 * hq // LANES, tm, LANES), F32)],
        compiler_params=_cparams(("parallel",)),
        name="post_mixer",
    )(x, osb, ofox, omla, *dil_outs, g_dil, wo, lng, lnb, wr, br)


DMA_UNROLL = 8


def _moe_kernel(ce_ref, nv_ref, src_ref, srcn_ref, dst_ref, x_hbm, w1_ref, w3_ref, w2_ref, y_hbm,
                xs_ref, ys_ref, gsem, ssem):
    c = pl.program_id(0)
    nc = pl.num_programs(0)
    slot = c % 2
    other = 1 - slot
    nv = nv_ref[c]
    nv_next = jnp.where(c + 1 < nc, nv_ref[jnp.minimum(c + 1, nc - 1)], 0)
    nv_prev = jnp.where(c > 0, nv_ref[jnp.maximum(c - 1, 0)], 0)

    def gather_copy(idx_ref, i, s):
        return pltpu.make_async_copy(
            x_hbm.at[pl.ds(pl.multiple_of(idx_ref[0, 0, i] * SUBLANES, SUBLANES), SUBLANES), :],
            xs_ref.at[s, pl.ds(pl.multiple_of(i * SUBLANES, SUBLANES), SUBLANES), :], gsem.at[s])

    def scatter_copy(i, s):
        return pltpu.make_async_copy(
            ys_ref.at[s, pl.ds(pl.multiple_of(i * SUBLANES, SUBLANES), SUBLANES), :],
            y_hbm.at[pl.ds(pl.multiple_of(dst_ref[0, 0, i] * SUBLANES, SUBLANES), SUBLANES), :], ssem.at[s])

    def start_rows(n, make):
        ngrp = n // DMA_UNROLL

        def grp(g, _):
            base = pl.multiple_of(g * DMA_UNROLL, DMA_UNROLL)
            for k in range(DMA_UNROLL):
                make(base + k).start()
            return 0

        def one(i, _):
            make(i).start()
            return 0

        lax.fori_loop(0, ngrp, grp, 0)
        lax.fori_loop(ngrp * DMA_UNROLL, n, one, 0)

    def wait_rows(n, make_row, make_block):
        @pl.when(n == MOE_BLOCK)
        def _():
            make_block().wait()

        @pl.when(n < MOE_BLOCK)
        def _():
            def one(i, _):
                make_row(i).wait()
                return 0
            lax.fori_loop(0, n, one, 0)

    def gather_block(s):
        return pltpu.make_async_copy(x_hbm.at[pl.ds(0, MOE_BLOCK * SUBLANES), :], xs_ref.at[s], gsem.at[s])

    def scatter_block(s):
        return pltpu.make_async_copy(ys_ref.at[s], y_hbm.at[pl.ds(0, MOE_BLOCK * SUBLANES), :], ssem.at[s])

    @pl.when(c == 0)
    def _():
        xs_ref[...] = jnp.zeros_like(xs_ref)
        start_rows(nv, lambda i: gather_copy(src_ref, i, 0))

    start_rows(nv_next, lambda i: gather_copy(srcn_ref, i, other))
    wait_rows(nv, lambda i: gather_copy(src_ref, i, slot), lambda: gather_block(slot))

    @pl.when(nv > 0)
    def _():
        xt = xs_ref.at[slot]
        a = b = None
        for jj in range(0, xt.shape[0] // MOE_BLOCK, 2):
            xb = jnp.concatenate([xt[pl.ds(jj, MOE_BLOCK, stride=SUBLANES), :],
                                  xt[pl.ds(jj + 1, MOE_BLOCK, stride=SUBLANES), :]], axis=1).astype(BF16)
            ws = slice(jj * LANES, (jj + 2) * LANES)
            da, db = _dot(xb, w1_ref[0, ws, :]), _dot(xb, w3_ref[0, ws, :])
            a, b = (da, db) if a is None else (a + da, b + db)
        hid = (a / (1.0 + jnp.exp(-a)) * b).astype(BF16)
        y = _dot(hid, w2_ref[0])
        yt = ys_ref.at[slot]
        for j in range(y.shape[1] // LANES):
            yt[pl.ds(j, MOE_BLOCK, stride=SUBLANES), :] = y[:, j * LANES:(j + 1) * LANES]

    wait_rows(nv_prev, lambda i: scatter_copy(i, other), lambda: scatter_block(other))
    start_rows(nv, lambda i: scatter_copy(i, slot))

    @pl.when(c == nc - 1)
    def _():
        wait_rows(nv, lambda i: scatter_copy(i, slot), lambda: scatter_block(slot))


def _moe_call(chunk_expert, n_valid, src, dst, x1t, w1, w3, w2, n_rows_out):
    n_chunks = chunk_expert.shape[0]
    d, de = w1.shape[1], w1.shape[2]
    assert d == SUBLANES * LANES and x1t.shape[1] == LANES
    grid_spec = pltpu.PrefetchScalarGridSpec(
        num_scalar_prefetch=2,
        grid=(n_chunks,),
        in_specs=[
            pl.BlockSpec((1, 1, MOE_BLOCK), lambda c, ce, nv: (c, 0, 0), memory_space=pltpu.SMEM),
            pl.BlockSpec((1, 1, MOE_BLOCK), lambda c, ce, nv: (jnp.minimum(c + 1, n_chunks - 1), 0, 0),
                         memory_space=pltpu.SMEM),
            pl.BlockSpec((1, 1, MOE_BLOCK), lambda c, ce, nv: (c, 0, 0), memory_space=pltpu.SMEM),
            pl.BlockSpec(memory_space=pl.ANY),
            pl.BlockSpec((1, d, de), lambda c, ce, nv: (ce[c], 0, 0)),
            pl.BlockSpec((1, d, de), lambda c, ce, nv: (ce[c], 0, 0)),
            pl.BlockSpec((1, de, d), lambda c, ce, nv: (ce[c], 0, 0)),
        ],
        out_specs=pl.BlockSpec(memory_space=pl.ANY),
        scratch_shapes=[
            pltpu.VMEM((2, MOE_BLOCK * SUBLANES, LANES), F32),
            pltpu.VMEM((2, MOE_BLOCK * SUBLANES, LANES), F32),
            pltpu.SemaphoreType.DMA((2,)),
            pltpu.SemaphoreType.DMA((2,)),
        ],
    )
    return pl.pallas_call(
        _moe_kernel,
        grid_spec=grid_spec,
        out_shape=jax.ShapeDtypeStruct((n_rows_out * SUBLANES, LANES), F32),
        compiler_params=_cparams(("arbitrary",)),
        name="moe_experts",
    )(chunk_expert, n_valid, src, src, dst, x1t, w1, w3, w2)


def _combine_kernel(x1_ref, ya_ref, yb_ref, route_ref, lng_ref, lnb_ref, o_ref, *, alpha):
    tm, d = x1_ref.shape
    g1 = route_ref[:, 2:3]
    g2 = route_ref[:, 3:4]

    def rows(y_ref):
        return jnp.concatenate([y_ref[pl.ds(j, tm, stride=SUBLANES), :] for j in range(d // LANES)], axis=1)

    m = g1 * rows(ya_ref) + g2 * rows(yb_ref)
    o_ref[...] = _layernorm(alpha * x1_ref[...] + m, lng_ref[...], lnb_ref[...])


def _combine_call(x1, y, route, lng, lnb, alpha, tm):
    t, d = x1.shape
    nt = t // tm
    full = lambda a: pl.BlockSpec(a.shape, lambda i: (0, 0))
    return pl.pallas_call(
        functools.partial(_combine_kernel, alpha=alpha),
        grid=(nt,),
        in_specs=[pl.BlockSpec((tm, d), lambda i: (i, 0)),
                  pl.BlockSpec((tm * SUBLANES, LANES), lambda i: (i, 0)),
                  pl.BlockSpec((tm * SUBLANES, LANES), lambda i: (i + nt, 0)),
                  pl.BlockSpec((tm, LANES), lambda i: (i, 0)),
                  full(lng), full(lnb)],
        out_specs=pl.BlockSpec((tm, d), lambda i: (i, 0)),
        out_shape=jax.ShapeDtypeStruct((t, d), F32),
        compiler_params=_cparams(("parallel",)),
        name="moe_combine",
    )(x1, y, y, route, lng, lnb)


def _rope_tables(seq, dim, lane_lo):
    half = dim // 2
    inv_freq = ROPE_THETA ** (-jnp.arange(half, dtype=F32) / half)
    ang = jnp.arange(seq, dtype=F32)[:, None] * inv_freq[None, :]
    cos = jnp.concatenate([jnp.cos(ang), jnp.cos(ang)], -1)
    sin = jnp.concatenate([jnp.sin(ang), jnp.sin(ang)], -1)
    if lane_lo == 0:
        reps = LANES // dim
        return jnp.tile(cos, (1, reps)), jnp.tile(sin, (1, reps))
    cos_t = jnp.ones((seq, LANES), F32).at[:, lane_lo:lane_lo + dim].set(cos)
    sin_t = jnp.zeros((seq, LANES), F32).at[:, lane_lo:lane_lo + dim].set(sin)
    return cos_t, sin_t


def _dispatch_tables(route, n_tok):
    expert_id = route[:, 0:TOP_K].astype(jnp.int32).reshape(-1)
    n_assign = n_tok * TOP_K
    n_slots = n_assign + N_EXPERTS * MOE_BLOCK
    n_chunks = n_slots // MOE_BLOCK
    onehot = (expert_id[:, None] == jnp.arange(N_EXPERTS, dtype=jnp.int32)[None, :]).astype(jnp.int32)
    ranks = jnp.cumsum(onehot, axis=0) - onehot
    rank = jnp.sum(ranks * onehot, axis=1)
    counts = jnp.sum(onehot, axis=0)
    padded = (counts + MOE_BLOCK - 1) // MOE_BLOCK * MOE_BLOCK
    pad_end = jnp.cumsum(padded)
    pad_start = pad_end - padded
    dest = pad_start[expert_id] + rank
    assign = jnp.arange(n_assign, dtype=jnp.int32)
    slot_assign = jnp.zeros((n_slots,), jnp.int32).at[dest].set(assign)
    src = slot_assign // TOP_K
    dst = (slot_assign % TOP_K) * n_tok + src
    chunk_start = jnp.arange(n_chunks, dtype=jnp.int32) * MOE_BLOCK
    chunk_expert = jnp.minimum(jnp.searchsorted(pad_end, chunk_start, side="right"),
                               N_EXPERTS - 1).astype(jnp.int32)
    n_valid = jnp.clip(pad_start[chunk_expert] + counts[chunk_expert] - chunk_start,
                       0, MOE_BLOCK).astype(jnp.int32)
    return (chunk_expert, n_valid, src.reshape(n_chunks, 1, MOE_BLOCK),
            dst.reshape(n_chunks, 1, MOE_BLOCK), n_assign)


def _pick_tile(n, pref):
    t = pref
    while n % t:
        t //= 2
    return t


def kernel(x, w_in, b_forget, g_cq, g_ckv, w_uq, w_ukv, g_head, w_out, ln1_g, ln1_b,
           w_group, b_group, w_expert, b_expert, w1, w3, w2, ln2_g, ln2_b):
    bsz, seq, d = x.shape
    depth = w_in.shape[0]
    t = bsz * seq
    alpha = (2.0 * depth) ** 0.25
    hq = N_HEADS_PER_MIXER * HEAD_DIM
    qk_scale = HEAD_DIM ** -0.5
    mla_scale = (MLA_NOPE + MLA_ROPE) ** -0.5
    win = DIL_BRANCHES[0][0]
    assert all(w // r == win for w, r in DIL_BRANCHES)
    assert seq % (DIL_BRANCHES[-1][1] * win) == 0 and d % LANES == 0

    cos64, sin64 = _rope_tables(seq, HEAD_DIM, 0)
    cos_m, sin_m = _rope_tables(seq, MLA_ROPE, MLA_NOPE)
    tq = _pick_tile(seq, 256)
    idx = jnp.arange(tq)
    u_sb = (idx[:, None] > idx[None, :]).astype(BF16)
    tk_sb = _pick_tile(seq, 2 * tq)
    tk_sm = _pick_tile(seq, 4 * tq)

    for l in range(depth):
        wl = w_in[l]
        o_fox, o_mla, o_dil = N_SB, N_SB + N_FOX_QKV + N_HEADS_PER_MIXER, N_SB + N_FOX_QKV + N_HEADS_PER_MIXER + N_MLA
        qs = lambda w: w.at[:, 0:hq].multiply(qk_scale)
        w_sb = qs(wl[:, 0:N_SB])
        w_fx = qs(wl[:, o_fox:o_fox + N_FOX_QKV])
        w_f = wl[:, o_fox + N_FOX_QKV:o_mla]
        w_ml = wl[:, o_mla:o_dil]
        wd = wl[:, o_dil:].reshape(d, 3, len(DIL_BRANCHES), hq)
        w_br = [qs(jnp.concatenate([wd[:, 0, g], wd[:, 1, g], wd[:, 2, g]], axis=1)) for g in range(len(DIL_BRANCHES))]
        w_misc = jnp.concatenate([w_ml, w_f, jnp.zeros((d, MISC_W - N_MLA - N_HEADS_PER_MIXER), F32)], axis=1)
        w_tok = jnp.concatenate([w_sb, w_fx, w_misc] + w_br, axis=1).astype(BF16)

        wq = jnp.pad(w_uq[l].reshape(MLA_Q_LORA, N_HEADS_PER_MIXER, MLA_NOPE + MLA_ROPE),
                     ((0, 0), (0, 0), (0, LANES - MLA_NOPE - MLA_ROPE))).reshape(MLA_Q_LORA, -1).astype(BF16)
        wkv = w_ukv[l].reshape(MLA_KV_LORA, N_HEADS_PER_MIXER, MLA_NOPE + HEAD_DIM)
        wk = jnp.pad(wkv[:, :, :MLA_NOPE], ((0, 0), (0, 0), (0, LANES - MLA_NOPE))).reshape(MLA_KV_LORA, -1).astype(BF16)
        wv = wkv[:, :, MLA_NOPE:].reshape(MLA_KV_LORA, -1).astype(BF16)
        g_flat = g_head[l].reshape(1, -1)
        wr = jnp.concatenate([w_group[l], w_expert[l],
                              jnp.zeros((d, LANES - N_GROUPS - N_EXPERTS), F32)], axis=1)
        br = jnp.concatenate([b_group[l], b_expert[l],
                              jnp.zeros((LANES - N_GROUPS - N_EXPERTS,), F32)]).reshape(1, LANES)

        tm = _pick_tile(seq, 512)
        sb, fx, misc, *qkv_br = _proj_call(
            x, w_tok, cos64, sin64,
            ((N_SB, 0, BF16, 1), (N_FOX_QKV, 0, BF16, 1), (MISC_W, 0, F32, 1))
            + tuple((N_BRANCH, 2 * hq, BF16, r) for _, r in DIL_BRANCHES), tm)
        sb = sb.reshape(t, N_SB)
        fx = fx.reshape(t, N_FOX_QKV)
        misc = misc.reshape(t, MISC_W)

        o_sb = _attn_call("sb", sb, sb, sb, 0, 1, 2, g_flat, 0, bsz, seq, tq, tk_sb, extra=(u_sb,))
        neg_c = _fox_c_call(misc, b_forget[l], bsz, seq)
        o_fx = _attn_call("fox", fx, fx, fx, 0, 1, 2, g_flat, 1, bsz, seq, tq, tk_sm, extra=(neg_c,))
        mq, mk, mv = _mla_prep_call(misc, g_cq[l].reshape(1, -1), g_ckv[l].reshape(1, -1), wq, wk, wv,
                                    cos_m, sin_m, seq, tm)
        o_ml = _attn_call("mla", mq, mk, mv, 0, 0, 0, g_flat, 2, bsz, seq, tq, tk_sm, scale=mla_scale)

        dil = []
        for g, (_, r) in enumerate(DIL_BRANCHES):
            n = seq // r
            tqd = n if n <= tq else max(win, min(tq, n // 2))
            og = _dil_call(qkv_br[g], tqd, win)
            dil.append(og.reshape(t // r, r * 2 * hq))

        x1, x1t, route = _post_call(
            x.reshape(t, d), o_sb, o_fx, o_ml, dil, tuple(r for _, r in DIL_BRANCHES),
            g_flat[:, 3 * hq:], w_out[l].astype(BF16),
            ln1_g[l].reshape(1, d), ln1_b[l].reshape(1, d), wr, br, alpha, _pick_tile(t, 256))

        chunk_expert, n_valid, src, dst, n_rows = _dispatch_tables(route, t)
        y = _moe_call(chunk_expert, n_valid, src, dst, x1t,
                      w1[l].astype(BF16), w3[l].astype(BF16), w2[l].astype(BF16), n_rows)
        x = _combine_call(x1, y, route,
                          ln2_g[l].reshape(1, d), ln2_b[l].reshape(1, d), alpha, _pick_tile(t, 256)).reshape(bsz, seq, d)
    return x
```

```python
import functools

import jax
import jax.numpy as jnp
from jax import lax
from jax.experimental import pallas as pl
from jax.experimental.pallas import tpu as pltpu

F32 = jnp.float32
BF16 = jnp.bfloat16

HEAD_DIM = 64
N_HEADS_PER_MIXER = 4
MLA_Q_LORA = 256
MLA_KV_LORA = 128
MLA_NOPE = 64
MLA_ROPE = 32
DIL_BRANCHES = ((128, 1), (512, 4), (2048, 16))
ROPE_THETA = 10000.0
N_GROUPS = 4
EXPERTS_PER_GROUP = 4
N_EXPERTS = N_GROUPS * EXPERTS_PER_GROUP
TOP_K = 2
MOE_BLOCK = 256
LN_EPS = 1e-5
RMS_EPS = 1e-6

LANES = 128
SUBLANES = 8
VMEM_LIMIT_BYTES = 56 * 1024 * 1024

N_SB = 3 * N_HEADS_PER_MIXER * HEAD_DIM
N_FOX_QKV = 3 * N_HEADS_PER_MIXER * HEAD_DIM
N_MLA = MLA_Q_LORA + MLA_KV_LORA + MLA_ROPE
N_BRANCH = 3 * N_HEADS_PER_MIXER * HEAD_DIM
MISC_W = 512
F_COL = N_MLA


def _cparams(sem):
    return pltpu.CompilerParams(dimension_semantics=sem, vmem_limit_bytes=VMEM_LIMIT_BYTES)


def _split3(a):
    hi = a.astype(BF16)
    r1 = a - hi.astype(F32)
    mid = r1.astype(BF16)
    lo = (r1 - mid.astype(F32)).astype(BF16)
    return hi, mid, lo


def _dot(a, b):
    return jnp.dot(a, b, preferred_element_type=F32)


def _dot_nt(a, b):
    return lax.dot_general(a, b, (((1,), (1,)), ((), ())), preferred_element_type=F32)


def _dot_exact_rhs(a, u):
    hi, mid, lo = _split3(a)
    return _dot(hi, u) + _dot(mid, u) + _dot(lo, u)


def _dot_f32(a, b):
    ah = a.astype(BF16)
    al = (a - ah.astype(F32)).astype(BF16)
    bh = b.astype(BF16)
    bl = (b - bh.astype(F32)).astype(BF16)
    return _dot(ah, bh) + (_dot(ah, bl) + _dot(al, bh))


def _lane_iota(shape):
    return lax.broadcasted_iota(jnp.int32, shape, len(shape) - 1)


def _rotate_half(y, half):
    lane = _lane_iota(y.shape)
    fwd = pltpu.roll(y, half, 1)
    bwd = pltpu.roll(y, LANES - half, 1)
    return jnp.where((lane % (2 * half)) < half, -bwd, fwd)


def _log_sigmoid_pair(z):
    sp = jnp.log(1.0 + jnp.exp(-jnp.abs(z)))
    return jnp.minimum(z, 0.0) - sp, -jnp.maximum(z, 0.0) - sp


def _head_rms(o, g):
    lane = _lane_iota(o.shape)
    first = lane < HEAD_DIM
    sq = o * o
    ss_a = jnp.sum(jnp.where(first, sq, 0.0), axis=-1, keepdims=True)
    ss_b = jnp.sum(jnp.where(first, 0.0, sq), axis=-1, keepdims=True)
    ms = jnp.where(first, ss_a, ss_b) * (1.0 / HEAD_DIM)
    return o * lax.rsqrt(ms + RMS_EPS) * g


def _layernorm(y, g, b):
    mu = jnp.mean(y, axis=-1, keepdims=True)
    d = y - mu
    var = jnp.mean(d * d, axis=-1, keepdims=True)
    return d * lax.rsqrt(var + LN_EPS) * g + b


def _proj_kernel(x_ref, w_ref, cos_ref, sin_ref, *refs, outs):
    out_refs, stage_ref = refs[:len(outs)], refs[len(outs)]
    tm = x_ref.shape[1]
    xb = x_ref[0].astype(BF16)
    col = 0
    slab = 0
    for o_ref, (width, n_rope, _, r) in zip(out_refs, outs):
        for c in range(0, width, 2 * LANES):
            cw = min(2 * LANES, width - c)
            y = _dot(xb, w_ref[:, col + c:col + c + cw])
            for s in range(0, cw, LANES):
                ys = y[:, s:s + LANES]
                if c + s < n_rope:
                    ys = ys * cos_ref[...] + _rotate_half(ys, HEAD_DIM // 2) * sin_ref[...]
                if r == 1:
                    o_ref[0, 0, :, c + s:c + s + LANES] = ys.astype(o_ref.dtype)
                else:
                    st = stage_ref.at[slab % stage_ref.shape[0]]
                    slab += 1
                    st[...] = ys
                    for p in range(r):
                        o_ref[0, p, :, c + s:c + s + LANES] = (
                            st[pl.ds(p, tm // r, stride=r), :].astype(o_ref.dtype))
        col += width


PROJ_STAGE_SLABS = 4


def _proj_call(x, w, cos, sin, outs, tm):
    bsz, seq, d = x.shape
    nt = seq // tm
    in_specs = [
        pl.BlockSpec((1, tm, d), lambda b, i: (b, i, 0)),
        pl.BlockSpec(w.shape, lambda b, i: (0, 0)),
        pl.BlockSpec((tm, LANES), lambda b, i: (i, 0)),
        pl.BlockSpec((tm, LANES), lambda b, i: (i, 0)),
    ]
    out_specs = [pl.BlockSpec((1, r, tm // r, wd), lambda b, i: (b, 0, i, 0)) for wd, _, _, r in outs]
    out_shape = [jax.ShapeDtypeStruct((bsz, r, seq // r, wd), dt) for wd, _, dt, r in outs]
    return pl.pallas_call(
        functools.partial(_proj_kernel, outs=outs),
        grid=(bsz, nt), in_specs=in_specs, out_specs=out_specs, out_shape=out_shape,
        scratch_shapes=[pltpu.VMEM((PROJ_STAGE_SLABS, tm, LANES), F32)],
        compiler_params=_cparams(("parallel", "parallel")),
        name="proj",
    )(x, w, cos, sin)


def _fox_c_kernel(misc_ref, bias_ref, uinc_ref, ones_ref, out_ref, *, seq):
    lane0 = F_COL - 3 * LANES
    nblk = seq // LANES

    def body(j, carry):
        r0 = pl.multiple_of(j * LANES, LANES)
        f = misc_ref[pl.ds(r0, LANES), :] + bias_ref[...]
        lf, _ = _log_sigmoid_pair(f)
        lft = lf.T
        csum = _dot_exact_rhs(lft, uinc_ref[...]) + carry
        tot = _dot_exact_rhs(lft, ones_ref[...])
        out_ref[0, :, pl.ds(r0, LANES)] = -csum[lane0:lane0 + N_HEADS_PER_MIXER, :]
        return carry + tot

    lax.fori_loop(0, nblk, body, jnp.zeros((LANES, LANES), F32))


def _fox_c_call(misc, b_forget, bsz, seq):
    bias = jnp.zeros((1, LANES), F32).at[0, F_COL - 3 * LANES:F_COL - 3 * LANES + N_HEADS_PER_MIXER].set(b_forget)
    idx = jnp.arange(LANES)
    uinc = (idx[:, None] <= idx[None, :]).astype(BF16)
    ones = jnp.ones((LANES, LANES), BF16)
    return pl.pallas_call(
        functools.partial(_fox_c_kernel, seq=seq),
        grid=(bsz,),
        in_specs=[
            pl.BlockSpec((seq, LANES), lambda b: (b, 3)),
            pl.BlockSpec((1, LANES), lambda b: (0, 0)),
            pl.BlockSpec((LANES, LANES), lambda b: (0, 0)),
            pl.BlockSpec((LANES, LANES), lambda b: (0, 0)),
        ],
        out_specs=pl.BlockSpec((1, N_HEADS_PER_MIXER, seq), lambda b: (b, 0, 0)),
        out_shape=jax.ShapeDtypeStruct((bsz, N_HEADS_PER_MIXER, seq), F32),
        compiler_params=_cparams(("parallel",)),
        name="fox_c",
    )(misc, bias, uinc, ones)


def _mla_prep_kernel(misc_ref, gq_ref, gkv_ref, wq_ref, wk_ref, wv_ref, cos_ref, sin_ref,
                     q_ref, k_ref, v_ref):
    def rms(x, g):
        return x * lax.rsqrt(jnp.mean(x * x, axis=-1, keepdims=True) + RMS_EPS) * g

    cq = rms(misc_ref[:, 0:MLA_Q_LORA], gq_ref[...]).astype(BF16)
    ckv = rms(misc_ref[:, MLA_Q_LORA:MLA_Q_LORA + MLA_KV_LORA], gkv_ref[...]).astype(BF16)
    kr_blk = misc_ref[:, 3 * LANES:4 * LANES]
    lane = _lane_iota(kr_blk.shape)
    in_rope = (lane >= MLA_NOPE) & (lane < MLA_NOPE + MLA_ROPE)
    kr = jnp.where(in_rope, pltpu.roll(kr_blk, MLA_NOPE, 1), 0.0)
    cos = cos_ref[...]
    sin = sin_ref[...]

    def rope(y):
        return y * cos + _rotate_half(y, MLA_ROPE // 2) * sin

    q = _dot(cq, wq_ref[...])
    k = _dot(ckv, wk_ref[...])
    for h in range(N_HEADS_PER_MIXER):
        sl = slice(h * LANES, (h + 1) * LANES)
        q_ref[:, sl] = rope(q[:, sl]).astype(BF16)
        k_ref[:, sl] = rope(k[:, sl] + kr).astype(BF16)
    v_ref[...] = _dot(ckv, wv_ref[...]).astype(BF16)


def _mla_prep_call(misc, g_cq, g_ckv, wq, wk, wv, cos, sin, seq, tm):
    t = misc.shape[0]
    nper = seq // tm
    hw = N_HEADS_PER_MIXER * LANES
    vw = N_HEADS_PER_MIXER * HEAD_DIM
    full = lambda a: pl.BlockSpec(a.shape, lambda i: (0, 0))
    return pl.pallas_call(
        _mla_prep_kernel,
        grid=(t // tm,),
        in_specs=[
            pl.BlockSpec((tm, MISC_W), lambda i: (i, 0)),
            full(g_cq), full(g_ckv), full(wq), full(wk), full(wv),
            pl.BlockSpec((tm, LANES), lambda i: (i % nper, 0)),
            pl.BlockSpec((tm, LANES), lambda i: (i % nper, 0)),
        ],
        out_specs=[
            pl.BlockSpec((tm, hw), lambda i: (i, 0)),
            pl.BlockSpec((tm, hw), lambda i: (i, 0)),
            pl.BlockSpec((tm, vw), lambda i: (i, 0)),
        ],
        out_shape=[
            jax.ShapeDtypeStruct((t, hw), BF16),
            jax.ShapeDtypeStruct((t, hw), BF16),
            jax.ShapeDtypeStruct((t, vw), BF16),
        ],
        compiler_params=_cparams(("parallel",)),
        name="mla_prep",
    )(misc, g_cq, g_ckv, wq, wk, wv, cos, sin)


def _attn_kernel(*refs, mode, tq, tk, scale):
    if mode == "sb":
        q_ref, k_ref, v_ref, g_ref, u_ref, o_ref = refs
    elif mode == "fox":
        q_ref, k_ref, v_ref, g_ref, nc_ref, o_ref = refs
    else:
        q_ref, k_ref, v_ref, g_ref, o_ref = refs
    nh = N_HEADS_PER_MIXER
    i = pl.program_id(1)
    lane = _lane_iota((tq, LANES))
    first = lane < HEAD_DIM
    row = lax.broadcasted_iota(jnp.int32, (tq, tk), 0)
    colm = lax.broadcasted_iota(jnp.int32, (tq, tk), 1)

    q_heads = []
    for h in range(nh):
        if mode == "mla":
            q_heads.append(q_ref[:, h * LANES:(h + 1) * LANES])
        else:
            q2 = q_ref[:, (h // 2) * LANES:(h // 2 + 1) * LANES]
            zero = jnp.zeros_like(q2)
            q_heads.append(jnp.where(first, q2, zero) if h % 2 == 0 else jnp.where(first, zero, q2))

    def k_head(j, h):
        r0 = pl.multiple_of(j * tk, tk)
        kb = h if mode == "mla" else h // 2
        return k_ref[pl.ds(r0, tk), kb * LANES:(kb + 1) * LANES]

    def v_pair(j, p):
        r0 = pl.multiple_of(j * tk, tk)
        return v_ref[pl.ds(r0, tk), p * LANES:(p + 1) * LANES]

    jd = (i * tq) // tk
    off = i * tq - jd * tk

    if mode == "sb":
        tu = u_ref.shape[0]

        def step(j, carry, diag):
            accs, rs = carry
            new_accs, new_rs = [], []
            for p in range(nh // 2):
                v2 = v_pair(j, p)
                outs = []
                for h in (2 * p, 2 * p + 1):
                    z = _dot_nt(q_heads[h], k_head(j, h))
                    ls_pos, ls_neg = _log_sigmoid_pair(z)
                    if diag:
                        before = colm < row + off
                        ls_neg = jnp.where(before, ls_neg, 0.0)
                    lb = ls_neg.astype(BF16)
                    pieces = []
                    later = rs[h]
                    for c0 in range(tk - tu, -1, -tu):
                        c = _dot(lb[:, c0:c0 + tu], u_ref[...]) + later
                        pieces.insert(0, c)
                        later = c[:, 0:1] + ls_neg[:, c0:c0 + 1]
                    w = jnp.exp(ls_pos + jnp.concatenate(pieces, axis=1))
                    if diag:
                        w = jnp.where(before, w, 0.0)
                    outs.append(_dot(w.astype(BF16), v2))
                    new_rs.append(later)
                new_accs.append(accs[p] + jnp.where(first, outs[0], outs[1]))
            return tuple(new_accs), tuple(new_rs)

        zacc = jnp.zeros((tq, LANES), F32)
        zr = jnp.zeros((tq, 1), F32)
        carry = step(jd, ((zacc,) * (nh // 2), (zr,) * nh), True)
        carry = lax.fori_loop(0, jd, lambda n, c: step(jd - 1 - n, c, False), carry)
        outs = carry[0]
    else:
        ones = jnp.ones((tk, LANES), BF16)

        def step(j, carry, diag):
            accs, ms, ls = carry
            new_accs, new_ms, new_ls = [], [], []
            r0 = pl.multiple_of(j * tk, tk)
            for p in range(nh // 2):
                v_aug = jnp.concatenate([v_pair(j, p), ones], axis=1)
                pv, alphas = [], []
                for h in (2 * p, 2 * p + 1):
                    s = _dot_nt(q_heads[h], k_head(j, h))
                    if mode == "mla":
                        s = s * scale
                    else:
                        s = s + nc_ref[0, h:h + 1, pl.ds(r0, tk)]
                    if diag:
                        s = jnp.where(colm <= row + off, s, -jnp.inf)
                    m_new = jnp.maximum(ms[h], jnp.max(s, axis=-1, keepdims=True))
                    alpha = jnp.exp(ms[h] - m_new)
                    pvx = _dot(jnp.exp(s - m_new).astype(BF16), v_aug)
                    new_ls.append(alpha * ls[h] + pvx[:, LANES:2 * LANES])
                    new_ms.append(m_new)
                    pv.append(pvx[:, 0:LANES])
                    alphas.append(alpha)
                new_accs.append(accs[p] * jnp.where(first, alphas[0], alphas[1])
                                + jnp.where(first, pv[0], pv[1]))
            return tuple(new_accs), tuple(new_ms), tuple(new_ls)

        neg = jnp.full((tq, 1), -jnp.inf, F32)
        zacc = jnp.zeros((tq, LANES), F32)
        carry = step(jd, ((zacc,) * (nh // 2), (neg,) * nh, (zacc,) * nh), True)
        carry = lax.fori_loop(0, jd, lambda n, c: step(n, c, False), carry)
        accs, _, ls = carry
        outs = [accs[p] / jnp.where(first, ls[2 * p], ls[2 * p + 1]) for p in range(nh // 2)]
    for p in range(nh // 2):
        sl = slice(p * LANES, (p + 1) * LANES)
        o_ref[:, sl] = _head_rms(outs[p], g_ref[:, sl]).astype(o_ref.dtype)


def _attn_call(mode, q, k, v, qcol, kcol, vcol, g_flat, gcol, bsz, seq, tq, tk, extra=(), scale=1.0):
    t = bsz * seq
    nq = seq // tq
    hq = N_HEADS_PER_MIXER * HEAD_DIM
    qw = N_HEADS_PER_MIXER * LANES if mode == "mla" else hq
    assert tk % tq == 0 and seq % tk == 0
    in_specs = [
        pl.BlockSpec((tq, qw), lambda b, i: (b * nq + i, qcol)),
        pl.BlockSpec((seq, qw), lambda b, i: (b, kcol)),
        pl.BlockSpec((seq, hq), lambda b, i: (b, vcol)),
        pl.BlockSpec((1, hq), lambda b, i: (0, gcol)),
    ]
    args = [q, k, v, g_flat]
    if mode == "sb":
        (u,) = extra
        in_specs.append(pl.BlockSpec(u.shape, lambda b, i: (0, 0)))
        args.append(u)
    elif mode == "fox":
        (nc,) = extra
        in_specs.append(pl.BlockSpec((1, N_HEADS_PER_MIXER, seq), lambda b, i: (b, 0, 0)))
        args.append(nc)
    return pl.pallas_call(
        functools.partial(_attn_kernel, mode=mode, tq=tq, tk=tk, scale=scale),
        grid=(bsz, nq),
        in_specs=in_specs,
        out_specs=pl.BlockSpec((tq, hq), lambda b, i: (b * nq + i, 0)),
        out_shape=jax.ShapeDtypeStruct((t, hq), BF16),
        compiler_params=_cparams(("parallel", "arbitrary")),
        name=f"attn_{mode}",
    )(*args)


def _dil_kernel(qkv_ref, o_ref, *, tq, tk, win):
    i = pl.program_id(2)
    lane = _lane_iota((tq, LANES))
    first = lane < HEAD_DIM
    hq = N_HEADS_PER_MIXER * HEAD_DIM
    r0 = pl.multiple_of(i * tq, tq)
    k0 = pl.multiple_of(jnp.maximum(i - 1, 0) * tq, tq)
    row = lax.broadcasted_iota(jnp.int32, (tq, tk), 0)
    colm = lax.broadcasted_iota(jnp.int32, (tq, tk), 1)
    delta = row + (r0 - k0) - colm
    band = jnp.abs(2 * delta - win) <= win
    ones = jnp.ones((tk, LANES), BF16)
    for p in range(N_HEADS_PER_MIXER // 2):
        q2 = qkv_ref[0, 0, pl.ds(r0, tq), p * LANES:(p + 1) * LANES]
        k2 = qkv_ref[0, 0, pl.ds(k0, tk), hq + p * LANES:hq + (p + 1) * LANES]
        v_aug = jnp.concatenate(
            [qkv_ref[0, 0, pl.ds(k0, tk), 2 * hq + p * LANES:2 * hq + (p + 1) * LANES], ones], axis=1)
        zero = jnp.zeros_like(q2)
        outs = []
        lses = []
        for hh in range(2):
            qh = jnp.where(first, q2, zero) if hh == 0 else jnp.where(first, zero, q2)
            s = jnp.where(band, _dot_nt(qh, k2), -jnp.inf)
            m = jnp.max(s, axis=-1, keepdims=True)
            pvx = _dot(jnp.exp(s - m).astype(BF16), v_aug)
            l = pvx[:, LANES:2 * LANES]
            outs.append(pvx[:, 0:LANES] / l)
            lses.append(m + jnp.log(l))
        o_ref[0, :, p * LANES:(p + 1) * LANES] = jnp.where(first, outs[0], outs[1])
        o_ref[0, :, hq + p * LANES:hq + (p + 1) * LANES] = jnp.where(first, lses[0], lses[1])


def _dil_call(qkv, tqs, win):
    bsz, r, n, w = qkv.shape
    hq = N_HEADS_PER_MIXER * HEAD_DIM
    tk = 2 * tqs if n >= 2 * tqs else tqs
    assert tqs >= win and n % tqs == 0 and (tk == 2 * tqs or n == tqs)
    return pl.pallas_call(
        functools.partial(_dil_kernel, tq=tqs, tk=tk, win=win),
        grid=(bsz, r, n // tqs),
        in_specs=[pl.BlockSpec((1, 1, n, w), lambda b, p, c: (b, p, 0, 0))],
        out_specs=pl.BlockSpec((1, tqs, 2 * hq), lambda b, p, c: (b, c, p)),
        out_shape=jax.ShapeDtypeStruct((bsz, n, r * 2 * hq), F32),
        compiler_params=_cparams(("parallel", "parallel", "arbitrary")),
        name=f"dil_r{r}",
    )(qkv)


def _post_kernel(x_ref, osb_ref, ofox_ref, omla_ref, d1_ref, d2_ref, d3_ref, gd_ref, wo_ref,
                 lng_ref, lnb_ref, wr_ref, br_ref, x1_ref, x1t_ref, route_ref, stage_ref, *, alpha, dils):
    hq = N_HEADS_PER_MIXER * HEAD_DIM
    tm = x_ref.shape[0]
    h = _dot(osb_ref[...], wo_ref[0:hq, :])
    h += _dot(ofox_ref[...], wo_ref[hq:2 * hq, :])
    h += _dot(omla_ref[...], wo_ref[2 * hq:3 * hq, :])

    def token_rows(g, d_ref, r, c0):
        if r == 1:
            return d_ref[:, c0:c0 + LANES]
        st = stage_ref.at[g, c0 // LANES]
        for q in range(r):
            st[pl.ds(q, tm // r, stride=r), :] = d_ref[:, q * 2 * hq + c0:q * 2 * hq + c0 + LANES]
        return st[...]

    d_refs = (d1_ref, d2_ref, d3_ref)
    for p in range(N_HEADS_PER_MIXER // 2):
        sl = slice(p * LANES, (p + 1) * LANES)
        lses = [token_rows(g, d_refs[g], dils[g], hq + p * LANES) for g in range(len(dils))]
        vals = [token_rows(g, d_refs[g], dils[g], p * LANES) for g in range(len(dils))]
        l1, l2, l3 = lses
        m = jnp.maximum(jnp.maximum(l1, l2), l3)
        e1, e2, e3 = jnp.exp(l1 - m), jnp.exp(l2 - m), jnp.exp(l3 - m)
        inv = 1.0 / (e1 + e2 + e3)
        od = (e1 * inv) * vals[0] + (e2 * inv) * vals[1] + (e3 * inv) * vals[2]
        od = _head_rms(od, gd_ref[:, sl]).astype(BF16)
        h += _dot(od, wo_ref[3 * hq + p * LANES:3 * hq + (p + 1) * LANES, :])
    x1 = _layernorm(alpha * x_ref[...] + h, lng_ref[...], lnb_ref[...])
    x1_ref[...] = x1
    for j in range(x1.shape[1] // LANES):
        x1t_ref[pl.ds(j, tm, stride=SUBLANES), :] = x1[:, j * LANES:(j + 1) * LANES]

    logits = _dot_f32(x1, wr_ref[...]) + br_ref[...]
    lane = _lane_iota(logits.shape)
    lanef = lane.astype(F32)
    big = float(LANES)
    ninf = -jnp.inf
    gl = jnp.where(lane < N_GROUPS, logits, ninf)
    gmax = jnp.max(gl, axis=-1, keepdims=True)
    gsel = jnp.min(jnp.where(gl == gmax, lanef, big), axis=-1, keepdims=True)
    gw = 1.0 / jnp.sum(jnp.exp(gl - gmax), axis=-1, keepdims=True)
    e_lo = N_GROUPS + EXPERTS_PER_GROUP * gsel
    in_grp = (lanef >= e_lo) & (lanef < e_lo + EXPERTS_PER_GROUP)
    el = jnp.where(in_grp, logits, ninf)
    t1 = jnp.max(el, axis=-1, keepdims=True)
    i1 = jnp.min(jnp.where(el == t1, lanef, big), axis=-1, keepdims=True)
    el2 = jnp.where(lanef == i1, ninf, el)
    t2 = jnp.max(el2, axis=-1, keepdims=True)
    i2 = jnp.min(jnp.where(el2 == t2, lanef, big), axis=-1, keepdims=True)
    ex = jnp.exp(t2 - t1)
    den = 1.0 + ex
    g1 = gw / den
    g2 = gw * ex / den
    out = jnp.where(lane == 0, i1 - N_GROUPS,
                    jnp.where(lane == 1, i2 - N_GROUPS,
                              jnp.where(lane == 2, g1, jnp.where(lane == 3, g2, 0.0))))
    route_ref[...] = out


def _post_call(x, osb, ofox, omla, dil_outs, dils, g_dil, wo, lng, lnb, wr, br, alpha, tm):
    t, d = x.shape
    hq = N_HEADS_PER_MIXER * HEAD_DIM
    row = lambda w: pl.BlockSpec((tm, w), lambda i: (i, 0))
    full = lambda a: pl.BlockSpec(a.shape, lambda i: (0, 0))
    dil_specs = [pl.BlockSpec((tm // r, r * 2 * hq), lambda i: (i, 0)) for r in dils]
    return pl.pallas_call(
        functools.partial(_post_kernel, alpha=alpha, dils=dils),
        grid=(t // tm,),
        in_specs=[row(d), row(hq), row(hq), row(hq)] + dil_specs
                 + [full(g_dil), full(wo), full(lng), full(lnb), full(wr), full(br)],
        out_specs=[row(d), pl.BlockSpec((tm * SUBLANES, LANES), lambda i: (i, 0)), row(LANES)],
        out_shape=[jax.ShapeDtypeStruct((t, d), F32), jax.ShapeDtypeStruct((t * SUBLANES, LANES), F32),
                   jax.ShapeDtypeStruct((t, LANES), F32)],
        scratch_shapes=[pltpu.VMEM((len(dils), 2 * hq // LANES, tm, LANES), F32)],
        compiler_params=_cparams(("parallel",)),
        name="post_mixer",
    )(x, osb, ofox, omla, *dil_outs, g_dil, wo, lng, lnb, wr, br)


DMA_UNROLL = 8


def _moe_kernel(ce_ref, nv_ref, src_ref, srcn_ref, dst_ref, x_hbm, w1_ref, w3_ref, w2_ref, y_hbm,
                xs_ref, ys_ref, gsem, ssem):
    c = pl.program_id(0)
    nc = pl.num_programs(0)
    slot = c % 2
    other = 1 - slot
    nv = nv_ref[c]
    nv_next = jnp.where(c + 1 < nc, nv_ref[jnp.minimum(c + 1, nc - 1)], 0)
    nv_prev = jnp.where(c > 0, nv_ref[jnp.maximum(c - 1, 0)], 0)

    def gather_copy(idx_ref, i, s):
        return pltpu.make_async_copy(
            x_hbm.at[pl.ds(pl.multiple_of(idx_ref[0, 0, i] * SUBLANES, SUBLANES), SUBLANES), :],
            xs_ref.at[s, pl.ds(pl.multiple_of(i * SUBLANES, SUBLANES), SUBLANES), :], gsem.at[s])

    def scatter_copy(i, s):
        return pltpu.make_async_copy(
            ys_ref.at[s, pl.ds(pl.multiple_of(i * SUBLANES, SUBLANES), SUBLANES), :],
            y_hbm.at[pl.ds(pl.multiple_of(dst_ref[0, 0, i] * SUBLANES, SUBLANES), SUBLANES), :], ssem.at[s])

    def start_rows(n, make):
        ngrp = n // DMA_UNROLL

        def grp(g, _):
            base = pl.multiple_of(g * DMA_UNROLL, DMA_UNROLL)
            for k in range(DMA_UNROLL):
                make(base + k).start()
            return 0

        def one(i, _):
            make(i).start()
            return 0

        lax.fori_loop(0, ngrp, grp, 0)
        lax.fori_loop(ngrp * DMA_UNROLL, n, one, 0)

    def wait_rows(n, make_row, make_block):
        @pl.when(n == MOE_BLOCK)
        def _():
            make_block().wait()

        @pl.when(n < MOE_BLOCK)
        def _():
            def one(i, _):
                make_row(i).wait()
                return 0
            lax.fori_loop(0, n, one, 0)

    def gather_block(s):
        return pltpu.make_async_copy(x_hbm.at[pl.ds(0, MOE_BLOCK * SUBLANES), :], xs_ref.at[s], gsem.at[s])

    def scatter_block(s):
        return pltpu.make_async_copy(ys_ref.at[s], y_hbm.at[pl.ds(0, MOE_BLOCK * SUBLANES), :], ssem.at[s])

    @pl.when(c == 0)
    def _():
        xs_ref[...] = jnp.zeros_like(xs_ref)
        start_rows(nv, lambda i: gather_copy(src_ref, i, 0))

    start_rows(nv_next, lambda i: gather_copy(srcn_ref, i, other))
    wait_rows(nv, lambda i: gather_copy(src_ref, i, slot), lambda: gather_block(slot))

    @pl.when(nv > 0)
    def _():
        xt = xs_ref.at[slot]
        a = b = None
        for jj in range(0, xt.shape[0] // MOE_BLOCK, 2):
            xb = jnp.concatenate([xt[pl.ds(jj, MOE_BLOCK, stride=SUBLANES), :],
                                  xt[pl.ds(jj + 1, MOE_BLOCK, stride=SUBLANES), :]], axis=1).astype(BF16)
            ws = slice(jj * LANES, (jj + 2) * LANES)
            da, db = _dot(xb, w1_ref[0, ws, :]), _dot(xb, w3_ref[0, ws, :])
            a, b = (da, db) if a is None else (a + da, b + db)
        hid = (a / (1.0 + jnp.exp(-a)) * b).astype(BF16)
        y = _dot(hid, w2_ref[0])
        yt = ys_ref.at[slot]
        for j in range(y.shape[1] // LANES):
            yt[pl.ds(j, MOE_BLOCK, stride=SUBLANES), :] = y[:, j * LANES:(j + 1) * LANES]

    wait_rows(nv_prev, lambda i: scatter_copy(i, other), lambda: scatter_block(other))
    start_rows(nv, lambda i: scatter_copy(i, slot))

    @pl.when(c == nc - 1)
    def _():
        wait_rows(nv, lambda i: scatter_copy(i, slot), lambda: scatter_block(slot))


def _moe_call(chunk_expert, n_valid, src, dst, x1t, w1, w3, w2, n_rows_out):
    n_chunks = chunk_expert.shape[0]
    d, de = w1.shape[1], w1.shape[2]
    assert d == SUBLANES * LANES and x1t.shape[1] == LANES
    grid_spec = pltpu.PrefetchScalarGridSpec(
        num_scalar_prefetch=2,
        grid=(n_chunks,),
        in_specs=[
            pl.BlockSpec((1, 1, MOE_BLOCK), lambda c, ce, nv: (c, 0, 0), memory_space=pltpu.SMEM),
            pl.BlockSpec((1, 1, MOE_BLOCK), lambda c, ce, nv: (jnp.minimum(c + 1, n_chunks - 1), 0, 0),
                         memory_space=pltpu.SMEM),
            pl.BlockSpec((1, 1, MOE_BLOCK), lambda c, ce, nv: (c, 0, 0), memory_space=pltpu.SMEM),
            pl.BlockSpec(memory_space=pl.ANY),
            pl.BlockSpec((1, d, de), lambda c, ce, nv: (ce[c], 0, 0)),
            pl.BlockSpec((1, d, de), lambda c, ce, nv: (ce[c], 0, 0)),
            pl.BlockSpec((1, de, d), lambda c, ce, nv: (ce[c], 0, 0)),
        ],
        out_specs=pl.BlockSpec(memory_space=pl.ANY),
        scratch_shapes=[
            pltpu.VMEM((2, MOE_BLOCK * SUBLANES, LANES), F32),
            pltpu.VMEM((2, MOE_BLOCK * SUBLANES, LANES), F32),
            pltpu.SemaphoreType.DMA((2,)),
            pltpu.SemaphoreType.DMA((2,)),
        ],
    )
    return pl.pallas_call(
        _moe_kernel,
        grid_spec=grid_spec,
        out_shape=jax.ShapeDtypeStruct((n_rows_out * SUBLANES, LANES), F32),
        compiler_params=_cparams(("arbitrary",)),
        name="moe_experts",
    )(chunk_expert, n_valid, src, src, dst, x1t, w1, w3, w2)


def _combine_kernel(x1_ref, ya_ref, yb_ref, route_ref, lng_ref, lnb_ref, o_ref, *, alpha):
    tm, d = x1_ref.shape
    g1 = route_ref[:, 2:3]
    g2 = route_ref[:, 3:4]

    def rows(y_ref):
        return jnp.concatenate([y_ref[pl.ds(j, tm, stride=SUBLANES), :] for j in range(d // LANES)], axis=1)

    m = g1 * rows(ya_ref) + g2 * rows(yb_ref)
    o_ref[...] = _layernorm(alpha * x1_ref[...] + m, lng_ref[...], lnb_ref[...])


def _combine_call(x1, y, route, lng, lnb, alpha, tm):
    t, d = x1.shape
    nt = t // tm
    full = lambda a: pl.BlockSpec(a.shape, lambda i: (0, 0))
    return pl.pallas_call(
        functools.partial(_combine_kernel, alpha=alpha),
        grid=(nt,),
        in_specs=[pl.BlockSpec((tm, d), lambda i: (i, 0)),
                  pl.BlockSpec((tm * SUBLANES, LANES), lambda i: (i, 0)),
                  pl.BlockSpec((tm * SUBLANES, LANES), lambda i: (i + nt, 0)),
                  pl.BlockSpec((tm, LANES), lambda i: (i, 0)),
                  full(lng), full(lnb)],
        out_specs=pl.BlockSpec((tm, d), lambda i: (i, 0)),
        out_shape=jax.ShapeDtypeStruct((t, d), F32),
        compiler_params=_cparams(("parallel",)),
        name="moe_combine",
    )(x1, y, y, route, lng, lnb)


def _rope_tables(seq, dim, lane_lo):
    half = dim // 2
    inv_freq = ROPE_THETA ** (-jnp.arange(half, dtype=F32) / half)
    ang = jnp.arange(seq, dtype=F32)[:, None] * inv_freq[None, :]
    cos = jnp.concatenate([jnp.cos(ang), jnp.cos(ang)], -1)
    sin = jnp.concatenate([jnp.sin(ang), jnp.sin(ang)], -1)
    if lane_lo == 0:
        reps = LANES // dim
        return jnp.tile(cos, (1, reps)), jnp.tile(sin, (1, reps))
    cos_t = jnp.ones((seq, LANES), F32).at[:, lane_lo:lane_lo + dim].set(cos)
    sin_t = jnp.zeros((seq, LANES), F32).at[:, lane_lo:lane_lo + dim].set(sin)
    return cos_t, sin_t


def _dispatch_tables(route, n_tok):
    expert_id = route[:, 0:TOP_K].astype(jnp.int32).reshape(-1)
    n_assign = n_tok * TOP_K
    n_slots = n_assign + N_EXPERTS * MOE_BLOCK
    n_chunks = n_slots // MOE_BLOCK
    onehot = (expert_id[:, None] == jnp.arange(N_EXPERTS, dtype=jnp.int32)[None, :]).astype(jnp.int32)
    ranks = jnp.cumsum(onehot, axis=0) - onehot
    rank = jnp.sum(ranks * onehot, axis=1)
    counts = jnp.sum(onehot, axis=0)
    padded = (counts + MOE_BLOCK - 1) // MOE_BLOCK * MOE_BLOCK
    pad_end = jnp.cumsum(padded)
    pad_start = pad_end - padded
    dest = pad_start[expert_id] + rank
    assign = jnp.arange(n_assign, dtype=jnp.int32)
    slot_assign = jnp.zeros((n_slots,), jnp.int32).at[dest].set(assign)
    src = slot_assign // TOP_K
    dst = (slot_assign % TOP_K) * n_tok + src
    chunk_start = jnp.arange(n_chunks, dtype=jnp.int32) * MOE_BLOCK
    chunk_expert = jnp.minimum(jnp.searchsorted(pad_end, chunk_start, side="right"),
                               N_EXPERTS - 1).astype(jnp.int32)
    n_valid = jnp.clip(pad_start[chunk_expert] + counts[chunk_expert] - chunk_start,
                       0, MOE_BLOCK).astype(jnp.int32)
    return (chunk_expert, n_valid, src.reshape(n_chunks, 1, MOE_BLOCK),
            dst.reshape(n_chunks, 1, MOE_BLOCK), n_assign)


def _pick_tile(n, pref):
    t = pref
    while n % t:
        t //= 2
    return t


def kernel(x, w_in, b_forget, g_cq, g_ckv, w_uq, w_ukv, g_head, w_out, ln1_g, ln1_b,
           w_group, b_group, w_expert, b_expert, w1, w3, w2, ln2_g, ln2_b):
    bsz, seq, d = x.shape
    depth = w_in.shape[0]
    t = bsz * seq
    alpha = (2.0 * depth) ** 0.25
    hq = N_HEADS_PER_MIXER * HEAD_DIM
    qk_scale = HEAD_DIM ** -0.5
    mla_scale = (MLA_NOPE + MLA_ROPE) ** -0.5
    win = DIL_BRANCHES[0][0]
    assert all(w // r == win for w, r in DIL_BRANCHES)
    assert seq % (DIL_BRANCHES[-1][1] * win) == 0 and d % LANES == 0

    cos64, sin64 = _rope_tables(seq, HEAD_DIM, 0)
    cos_m, sin_m = _rope_tables(seq, MLA_ROPE, MLA_NOPE)
    tq = _pick_tile(seq, 256)
    idx = jnp.arange(tq)
    u_sb = (idx[:, None] > idx[None, :]).astype(BF16)
    tk_sb = _pick_tile(seq, 2 * tq)
    tk_sm = _pick_tile(seq, 4 * tq)

    for l in range(depth):
        wl = w_in[l]
        o_fox, o_mla, o_dil = N_SB, N_SB + N_FOX_QKV + N_HEADS_PER_MIXER, N_SB + N_FOX_QKV + N_HEADS_PER_MIXER + N_MLA
        qs = lambda w: w.at[:, 0:hq].multiply(qk_scale)
        w_sb = qs(wl[:, 0:N_SB])
        w_fx = qs(wl[:, o_fox:o_fox + N_FOX_QKV])
        w_f = wl[:, o_fox + N_FOX_QKV:o_mla]
        w_ml = wl[:, o_mla:o_dil]
        wd = wl[:, o_dil:].reshape(d, 3, len(DIL_BRANCHES), hq)
        w_br = [qs(jnp.concatenate([wd[:, 0, g], wd[:, 1, g], wd[:, 2, g]], axis=1)) for g in range(len(DIL_BRANCHES))]
        w_misc = jnp.concatenate([w_ml, w_f, jnp.zeros((d, MISC_W - N_MLA - N_HEADS_PER_MIXER), F32)], axis=1)
        w_tok = jnp.concatenate([w_sb, w_fx, w_misc] + w_br, axis=1).astype(BF16)

        wq = jnp.pad(w_uq[l].reshape(MLA_Q_LORA, N_HEADS_PER_MIXER, MLA_NOPE + MLA_ROPE),
                     ((0, 0), (0, 0), (0, LANES - MLA_NOPE - MLA_ROPE))).reshape(MLA_Q_LORA, -1).astype(BF16)
        wkv = w_ukv[l].reshape(MLA_KV_LORA, N_HEADS_PER_MIXER, MLA_NOPE + HEAD_DIM)
        wk = jnp.pad(wkv[:, :, :MLA_NOPE], ((0, 0), (0, 0), (0, LANES - MLA_NOPE))).reshape(MLA_KV_LORA, -1).astype(BF16)
        wv = wkv[:, :, MLA_NOPE:].reshape(MLA_KV_LORA, -1).astype(BF16)
        g_flat = g_head[l].reshape(1, -1)
        wr = jnp.concatenate([w_group[l], w_expert[l],
                              jnp.zeros((d, LANES - N_GROUPS - N_EXPERTS), F32)], axis=1)
        br = jnp.concatenate([b_group[l], b_expert[l],
                              jnp.zeros((LANES - N_GROUPS - N_EXPERTS,), F32)]).reshape(1, LANES)

        tm = _pick_tile(seq, 512)
        sb, fx, misc, *qkv_br = _proj_call(
            x, w_tok, cos64, sin64,
            ((N_SB, 0, BF16, 1), (N_FOX_QKV, 0, BF16, 1), (MISC_W, 0, F32, 1))
            + tuple((N_BRANCH, 2 * hq, BF16, r) for _, r in DIL_BRANCHES), tm)
        sb = sb.reshape(t, N_SB)
        fx = fx.reshape(t, N_FOX_QKV)
        misc = misc.reshape(t, MISC_W)

        o_sb = _attn_call("sb", sb, sb, sb, 0, 1, 2, g_flat, 0, bsz, seq, tq, tk_sb, extra=(u_sb,))
        neg_c = _fox_c_call(misc, b_forget[l], bsz, seq)
        o_fx = _attn_call("fox", fx, fx, fx, 0, 1, 2, g_flat, 1, bsz, seq, tq, tk_sm, extra=(neg_c,))
        mq, mk, mv = _mla_prep_call(misc, g_cq[l].reshape(1, -1), g_ckv[l].reshape(1, -1), wq, wk, wv,
                                    cos_m, sin_m, seq, tm)
        o_ml = _attn_call("mla", mq, mk, mv, 0, 0, 0, g_flat, 2, bsz, seq, tq, tk_sm, scale=mla_scale)

        dil = []
        for g, (_, r) in enumerate(DIL_BRANCHES):
            n = seq // r
            tqd = n if n <= tq else max(win, min(tq, n // 2))
            og = _dil_call(qkv_br[g], tqd, win)
            dil.append(og.reshape(t // r, r * 2 * hq))

        x1, x1t, route = _post_call(
            x.reshape(t, d), o_sb, o_fx, o_ml, dil, tuple(r for _, r in DIL_BRANCHES),
            g_flat[:, 3 * hq:], w_out[l].astype(BF16),
            ln1_g[l].reshape(1, d), ln1_b[l].reshape(1, d), wr, br, alpha, _pick_tile(t, 256))

        chunk_expert, n_valid, src, dst, n_rows = _dispatch_tables(route, t)
        y = _moe_call(chunk_expert, n_valid, src, dst, x1t,
                      w1[l].astype(BF16), w3[l].astype(BF16), w2[l].astype(BF16), n_rows)
        x = _combine_call(x1, y, route,
                          ln2_g[l].reshape(1, d), ln2_b[l].reshape(1, d), alpha, _pick_tile(t, 512)).reshape(bsz, seq, d)
    return x
```

```python
import functools

import jax
import jax.numpy as jnp
from jax import lax
from jax.experimental import pallas as pl
from jax.experimental.pallas import tpu as pltpu

F32 = jnp.float32
BF16 = jnp.bfloat16

HEAD_DIM = 64
N_HEADS_PER_MIXER = 4
MLA_Q_LORA = 256
MLA_KV_LORA = 128
MLA_NOPE = 64
MLA_ROPE = 32
DIL_BRANCHES = ((128, 1), (512, 4), (2048, 16))
ROPE_THETA = 10000.0
N_GROUPS = 4
EXPERTS_PER_GROUP = 4
N_EXPERTS = N_GROUPS * EXPERTS_PER_GROUP
TOP_K = 2
MOE_BLOCK = 256
LN_EPS = 1e-5
RMS_EPS = 1e-6

LANES = 128
SUBLANES = 8
VMEM_LIMIT_BYTES = 56 * 1024 * 1024

N_SB = 3 * N_HEADS_PER_MIXER * HEAD_DIM
N_FOX_QKV = 3 * N_HEADS_PER_MIXER * HEAD_DIM
N_MLA = MLA_Q_LORA + MLA_KV_LORA + MLA_ROPE
N_BRANCH = 3 * N_HEADS_PER_MIXER * HEAD_DIM
MISC_W = 512
F_COL = N_MLA


def _cparams(sem):
    return pltpu.CompilerParams(dimension_semantics=sem, vmem_limit_bytes=VMEM_LIMIT_BYTES)


def _split3(a):
    hi = a.astype(BF16)
    r1 = a - hi.astype(F32)
    mid = r1.astype(BF16)
    lo = (r1 - mid.astype(F32)).astype(BF16)
    return hi, mid, lo


def _dot(a, b):
    return jnp.dot(a, b, preferred_element_type=F32)


def _dot_nt(a, b):
    return lax.dot_general(a, b, (((1,), (1,)), ((), ())), preferred_element_type=F32)


def _dot_exact_rhs(a, u):
    hi, mid, lo = _split3(a)
    return _dot(hi, u) + _dot(mid, u) + _dot(lo, u)


def _dot_f32(a, b):
    ah = a.astype(BF16)
    al = (a - ah.astype(F32)).astype(BF16)
    bh = b.astype(BF16)
    bl = (b - bh.astype(F32)).astype(BF16)
    return _dot(ah, bh) + (_dot(ah, bl) + _dot(al, bh))


def _lane_iota(shape):
    return lax.broadcasted_iota(jnp.int32, shape, len(shape) - 1)


def _rotate_half(y, half):
    lane = _lane_iota(y.shape)
    fwd = pltpu.roll(y, half, 1)
    bwd = pltpu.roll(y, LANES - half, 1)
    return jnp.where((lane % (2 * half)) < half, -bwd, fwd)


def _log_sigmoid_pair(z):
    sp = jnp.log(1.0 + jnp.exp(-jnp.abs(z)))
    return jnp.minimum(z, 0.0) - sp, -jnp.maximum(z, 0.0) - sp


def _head_rms(o, g):
    lane = _lane_iota(o.shape)
    first = lane < HEAD_DIM
    sq = o * o
    ss_a = jnp.sum(jnp.where(first, sq, 0.0), axis=-1, keepdims=True)
    ss_b = jnp.sum(jnp.where(first, 0.0, sq), axis=-1, keepdims=True)
    ms = jnp.where(first, ss_a, ss_b) * (1.0 / HEAD_DIM)
    return o * lax.rsqrt(ms + RMS_EPS) * g


def _layernorm(y, g, b):
    mu = jnp.mean(y, axis=-1, keepdims=True)
    d = y - mu
    var = jnp.mean(d * d, axis=-1, keepdims=True)
    return d * lax.rsqrt(var + LN_EPS) * g + b


def _proj_kernel(x_ref, w_ref, cos_ref, sin_ref, *refs, outs):
    out_refs, stage_ref = refs[:len(outs)], refs[len(outs)]
    tm = x_ref.shape[1]
    xb = x_ref[0].astype(BF16)
    col = 0
    slab = 0
    for o_ref, (width, n_rope, _, r) in zip(out_refs, outs):
        for c in range(0, width, 2 * LANES):
            cw = min(2 * LANES, width - c)
            y = _dot(xb, w_ref[:, col + c:col + c + cw])
            for s in range(0, cw, LANES):
                ys = y[:, s:s + LANES]
                if c + s < n_rope:
                    ys = ys * cos_ref[...] + _rotate_half(ys, HEAD_DIM // 2) * sin_ref[...]
                if r == 1:
                    o_ref[0, 0, :, c + s:c + s + LANES] = ys.astype(o_ref.dtype)
                else:
                    st = stage_ref.at[slab % stage_ref.shape[0]]
                    slab += 1
                    st[...] = ys
                    for p in range(r):
                        o_ref[0, p, :, c + s:c + s + LANES] = (
                            st[pl.ds(p, tm // r, stride=r), :].astype(o_ref.dtype))
        col += width


PROJ_STAGE_SLABS = 4


def _proj_call(x, w, cos, sin, outs, tm):
    bsz, seq, d = x.shape
    nt = seq // tm
    in_specs = [
        pl.BlockSpec((1, tm, d), lambda b, i: (b, i, 0)),
        pl.BlockSpec(w.shape, lambda b, i: (0, 0)),
        pl.BlockSpec((tm, LANES), lambda b, i: (i, 0)),
        pl.BlockSpec((tm, LANES), lambda b, i: (i, 0)),
    ]
    out_specs = [pl.BlockSpec((1, r, tm // r, wd), lambda b, i: (b, 0, i, 0)) for wd, _, _, r in outs]
    out_shape = [jax.ShapeDtypeStruct((bsz, r, seq // r, wd), dt) for wd, _, dt, r in outs]
    return pl.pallas_call(
        functools.partial(_proj_kernel, outs=outs),
        grid=(bsz, nt), in_specs=in_specs, out_specs=out_specs, out_shape=out_shape,
        scratch_shapes=[pltpu.VMEM((PROJ_STAGE_SLABS, tm, LANES), F32)],
        compiler_params=_cparams(("parallel", "parallel")),
        name="proj",
    )(x, w, cos, sin)


def _fox_c_kernel(misc_ref, bias_ref, uinc_ref, ones_ref, out_ref, *, seq):
    lane0 = F_COL - 3 * LANES
    nblk = seq // LANES

    def body(j, carry):
        r0 = pl.multiple_of(j * LANES, LANES)
        f = misc_ref[pl.ds(r0, LANES), :] + bias_ref[...]
        lf, _ = _log_sigmoid_pair(f)
        lft = lf.T
        csum = _dot_exact_rhs(lft, uinc_ref[...]) + carry
        tot = _dot_exact_rhs(lft, ones_ref[...])
        out_ref[0, :, pl.ds(r0, LANES)] = -csum[lane0:lane0 + N_HEADS_PER_MIXER, :]
        return carry + tot

    lax.fori_loop(0, nblk, body, jnp.zeros((LANES, LANES), F32))


def _fox_c_call(misc, b_forget, bsz, seq):
    bias = jnp.zeros((1, LANES), F32).at[0, F_COL - 3 * LANES:F_COL - 3 * LANES + N_HEADS_PER_MIXER].set(b_forget)
    idx = jnp.arange(LANES)
    uinc = (idx[:, None] <= idx[None, :]).astype(BF16)
    ones = jnp.ones((LANES, LANES), BF16)
    return pl.pallas_call(
        functools.partial(_fox_c_kernel, seq=seq),
        grid=(bsz,),
        in_specs=[
            pl.BlockSpec((seq, LANES), lambda b: (b, 3)),
            pl.BlockSpec((1, LANES), lambda b: (0, 0)),
            pl.BlockSpec((LANES, LANES), lambda b: (0, 0)),
            pl.BlockSpec((LANES, LANES), lambda b: (0, 0)),
        ],
        out_specs=pl.BlockSpec((1, N_HEADS_PER_MIXER, seq), lambda b: (b, 0, 0)),
        out_shape=jax.ShapeDtypeStruct((bsz, N_HEADS_PER_MIXER, seq), F32),
        compiler_params=_cparams(("parallel",)),
        name="fox_c",
    )(misc, bias, uinc, ones)


def _mla_prep_kernel(misc_ref, gq_ref, gkv_ref, wq_ref, wk_ref, wv_ref, cos_ref, sin_ref,
                     q_ref, k_ref, v_ref):
    def rms(x, g):
        return x * lax.rsqrt(jnp.mean(x * x, axis=-1, keepdims=True) + RMS_EPS) * g

    cq = rms(misc_ref[:, 0:MLA_Q_LORA], gq_ref[...]).astype(BF16)
    ckv = rms(misc_ref[:, MLA_Q_LORA:MLA_Q_LORA + MLA_KV_LORA], gkv_ref[...]).astype(BF16)
    kr_blk = misc_ref[:, 3 * LANES:4 * LANES]
    lane = _lane_iota(kr_blk.shape)
    in_rope = (lane >= MLA_NOPE) & (lane < MLA_NOPE + MLA_ROPE)
    kr = jnp.where(in_rope, pltpu.roll(kr_blk, MLA_NOPE, 1), 0.0)
    cos = cos_ref[...]
    sin = sin_ref[...]

    def rope(y):
        return y * cos + _rotate_half(y, MLA_ROPE // 2) * sin

    q = _dot(cq, wq_ref[...])
    k = _dot(ckv, wk_ref[...])
    for h in range(N_HEADS_PER_MIXER):
        sl = slice(h * LANES, (h + 1) * LANES)
        q_ref[:, sl] = rope(q[:, sl]).astype(BF16)
        k_ref[:, sl] = rope(k[:, sl] + kr).astype(BF16)
    v_ref[...] = _dot(ckv, wv_ref[...]).astype(BF16)


def _mla_prep_call(misc, g_cq, g_ckv, wq, wk, wv, cos, sin, seq, tm):
    t = misc.shape[0]
    nper = seq // tm
    hw = N_HEADS_PER_MIXER * LANES
    vw = N_HEADS_PER_MIXER * HEAD_DIM
    full = lambda a: pl.BlockSpec(a.shape, lambda i: (0, 0))
    return pl.pallas_call(
        _mla_prep_kernel,
        grid=(t // tm,),
        in_specs=[
            pl.BlockSpec((tm, MISC_W), lambda i: (i, 0)),
            full(g_cq), full(g_ckv), full(wq), full(wk), full(wv),
            pl.BlockSpec((tm, LANES), lambda i: (i % nper, 0)),
            pl.BlockSpec((tm, LANES), lambda i: (i % nper, 0)),
        ],
        out_specs=[
            pl.BlockSpec((tm, hw), lambda i: (i, 0)),
            pl.BlockSpec((tm, hw), lambda i: (i, 0)),
            pl.BlockSpec((tm, vw), lambda i: (i, 0)),
        ],
        out_shape=[
            jax.ShapeDtypeStruct((t, hw), BF16),
            jax.ShapeDtypeStruct((t, hw), BF16),
            jax.ShapeDtypeStruct((t, vw), BF16),
        ],
        compiler_params=_cparams(("parallel",)),
        name="mla_prep",
    )(misc, g_cq, g_ckv, wq, wk, wv, cos, sin)


def _attn_kernel(*refs, mode, tq, tk, scale):
    if mode == "sb":
        q_ref, k_ref, v_ref, g_ref, u_ref, o_ref = refs
    elif mode == "fox":
        q_ref, k_ref, v_ref, g_ref, nc_ref, o_ref = refs
    else:
        q_ref, k_ref, v_ref, g_ref, o_ref = refs
    nh = N_HEADS_PER_MIXER
    i = pl.program_id(1)
    lane = _lane_iota((tq, LANES))
    first = lane < HEAD_DIM
    row = lax.broadcasted_iota(jnp.int32, (tq, tk), 0)
    colm = lax.broadcasted_iota(jnp.int32, (tq, tk), 1)

    q_heads = []
    for h in range(nh):
        if mode == "mla":
            q_heads.append(q_ref[:, h * LANES:(h + 1) * LANES])
        else:
            q2 = q_ref[:, (h // 2) * LANES:(h // 2 + 1) * LANES]
            zero = jnp.zeros_like(q2)
            q_heads.append(jnp.where(first, q2, zero) if h % 2 == 0 else jnp.where(first, zero, q2))

    def k_head(j, h):
        r0 = pl.multiple_of(j * tk, tk)
        kb = h if mode == "mla" else h // 2
        return k_ref[pl.ds(r0, tk), kb * LANES:(kb + 1) * LANES]

    def v_pair(j, p):
        r0 = pl.multiple_of(j * tk, tk)
        return v_ref[pl.ds(r0, tk), p * LANES:(p + 1) * LANES]

    jd = (i * tq) // tk
    off = i * tq - jd * tk

    if mode == "sb":
        tu = u_ref.shape[0]

        def step(j, carry, diag):
            accs, rs = carry
            new_accs, new_rs = [], []
            for p in range(nh // 2):
                v2 = v_pair(j, p)
                outs = []
                for h in (2 * p, 2 * p + 1):
                    z = _dot_nt(q_heads[h], k_head(j, h))
                    ls_pos, ls_neg = _log_sigmoid_pair(z)
                    if diag:
                        before = colm < row + off
                        ls_neg = jnp.where(before, ls_neg, 0.0)
                    lb = ls_neg.astype(BF16)
                    pieces = []
                    later = rs[h]
                    for c0 in range(tk - tu, -1, -tu):
                        c = _dot(lb[:, c0:c0 + tu], u_ref[...]) + later
                        pieces.insert(0, c)
                        later = c[:, 0:1] + ls_neg[:, c0:c0 + 1]
                    w = jnp.exp(ls_pos + jnp.concatenate(pieces, axis=1))
                    if diag:
                        w = jnp.where(before, w, 0.0)
                    outs.append(_dot(w.astype(BF16), v2))
                    new_rs.append(later)
                new_accs.append(accs[p] + jnp.where(first, outs[0], outs[1]))
            return tuple(new_accs), tuple(new_rs)

        zacc = jnp.zeros((tq, LANES), F32)
        zr = jnp.zeros((tq, 1), F32)
        carry = step(jd, ((zacc,) * (nh // 2), (zr,) * nh), True)
        carry = lax.fori_loop(0, jd, lambda n, c: step(jd - 1 - n, c, False), carry)
        outs = carry[0]
    else:
        ones = jnp.ones((tk, LANES), BF16)

        def step(j, carry, diag):
            accs, ms, ls = carry
            new_accs, new_ms, new_ls = [], [], []
            r0 = pl.multiple_of(j * tk, tk)
            for p in range(nh // 2):
                v_aug = jnp.concatenate([v_pair(j, p), ones], axis=1)
                pv, alphas = [], []
                for h in (2 * p, 2 * p + 1):
                    s = _dot_nt(q_heads[h], k_head(j, h))
                    if mode == "mla":
                        s = s * scale
                    else:
                        s = s + nc_ref[0, h:h + 1, pl.ds(r0, tk)]
                    if diag:
                        s = jnp.where(colm <= row + off, s, -jnp.inf)
                    m_new = jnp.maximum(ms[h], jnp.max(s, axis=-1, keepdims=True))
                    alpha = jnp.exp(ms[h] - m_new)
                    pvx = _dot(jnp.exp(s - m_new).astype(BF16), v_aug)
                    new_ls.append(alpha * ls[h] + pvx[:, LANES:2 * LANES])
                    new_ms.append(m_new)
                    pv.append(pvx[:, 0:LANES])
                    alphas.append(alpha)
                new_accs.append(accs[p] * jnp.where(first, alphas[0], alphas[1])
                                + jnp.where(first, pv[0], pv[1]))
            return tuple(new_accs), tuple(new_ms), tuple(new_ls)

        neg = jnp.full((tq, 1), -jnp.inf, F32)
        zacc = jnp.zeros((tq, LANES), F32)
        carry = step(jd, ((zacc,) * (nh // 2), (neg,) * nh, (zacc,) * nh), True)
        carry = lax.fori_loop(0, jd, lambda n, c: step(n, c, False), carry)
        accs, _, ls = carry
        outs = [accs[p] / jnp.where(first, ls[2 * p], ls[2 * p + 1]) for p in range(nh // 2)]
    for p in range(nh // 2):
        sl = slice(p * LANES, (p + 1) * LANES)
        o_ref[:, sl] = _head_rms(outs[p], g_ref[:, sl]).astype(o_ref.dtype)


def _attn_call(mode, q, k, v, qcol, kcol, vcol, g_flat, gcol, bsz, seq, tq, tk, extra=(), scale=1.0):
    t = bsz * seq
    nq = seq // tq
    hq = N_HEADS_PER_MIXER * HEAD_DIM
    qw = N_HEADS_PER_MIXER * LANES if mode == "mla" else hq
    assert tk % tq == 0 and seq % tk == 0
    in_specs = [
        pl.BlockSpec((tq, qw), lambda b, i: (b * nq + i, qcol)),
        pl.BlockSpec((seq, qw), lambda b, i: (b, kcol)),
        pl.BlockSpec((seq, hq), lambda b, i: (b, vcol)),
        pl.BlockSpec((1, hq), lambda b, i: (0, gcol)),
    ]
    args = [q, k, v, g_flat]
    if mode == "sb":
        (u,) = extra
        in_specs.append(pl.BlockSpec(u.shape, lambda b, i: (0, 0)))
        args.append(u)
    elif mode == "fox":
        (nc,) = extra
        in_specs.append(pl.BlockSpec((1, N_HEADS_PER_MIXER, seq), lambda b, i: (b, 0, 0)))
        args.append(nc)
    return pl.pallas_call(
        functools.partial(_attn_kernel, mode=mode, tq=tq, tk=tk, scale=scale),
        grid=(bsz, nq),
        in_specs=in_specs,
        out_specs=pl.BlockSpec((tq, hq), lambda b, i: (b * nq + i, 0)),
        out_shape=jax.ShapeDtypeStruct((t, hq), BF16),
        compiler_params=_cparams(("parallel", "arbitrary")),
        name=f"attn_{mode}",
    )(*args)


def _dil_kernel(qkv_ref, o_ref, *, tq, tk, win):
    i = pl.program_id(2)
    lane = _lane_iota((tq, LANES))
    first = lane < HEAD_DIM
    hq = N_HEADS_PER_MIXER * HEAD_DIM
    r0 = pl.multiple_of(i * tq, tq)
    k0 = pl.multiple_of(jnp.maximum(i - 1, 0) * tq, tq)
    row = lax.broadcasted_iota(jnp.int32, (tq, tk), 0)
    colm = lax.broadcasted_iota(jnp.int32, (tq, tk), 1)
    delta = row + (r0 - k0) - colm
    band = jnp.abs(2 * delta - win) <= win
    ones = jnp.ones((tk, LANES), BF16)
    for p in range(N_HEADS_PER_MIXER // 2):
        q2 = qkv_ref[0, 0, pl.ds(r0, tq), p * LANES:(p + 1) * LANES]
        k2 = qkv_ref[0, 0, pl.ds(k0, tk), hq + p * LANES:hq + (p + 1) * LANES]
        v_aug = jnp.concatenate(
            [qkv_ref[0, 0, pl.ds(k0, tk), 2 * hq + p * LANES:2 * hq + (p + 1) * LANES], ones], axis=1)
        zero = jnp.zeros_like(q2)
        outs = []
        lses = []
        for hh in range(2):
            qh = jnp.where(first, q2, zero) if hh == 0 else jnp.where(first, zero, q2)
            s = jnp.where(band, _dot_nt(qh, k2), -jnp.inf)
            m = jnp.max(s, axis=-1, keepdims=True)
            pvx = _dot(jnp.exp(s - m).astype(BF16), v_aug)
            l = pvx[:, LANES:2 * LANES]
            outs.append(pvx[:, 0:LANES] / l)
            lses.append(m + jnp.log(l))
        o_ref[0, :, p * LANES:(p + 1) * LANES] = jnp.where(first, outs[0], outs[1])
        o_ref[0, :, hq + p * LANES:hq + (p + 1) * LANES] = jnp.where(first, lses[0], lses[1])


def _dil_call(qkv, tqs, win):
    bsz, r, n, w = qkv.shape
    hq = N_HEADS_PER_MIXER * HEAD_DIM
    tk = 2 * tqs if n >= 2 * tqs else tqs
    assert tqs >= win and n % tqs == 0 and (tk == 2 * tqs or n == tqs)
    return pl.pallas_call(
        functools.partial(_dil_kernel, tq=tqs, tk=tk, win=win),
        grid=(bsz, r, n // tqs),
        in_specs=[pl.BlockSpec((1, 1, n, w), lambda b, p, c: (b, p, 0, 0))],
        out_specs=pl.BlockSpec((1, tqs, 2 * hq), lambda b, p, c: (b, c, p)),
        out_shape=jax.ShapeDtypeStruct((bsz, n, r * 2 * hq), F32),
        compiler_params=_cparams(("parallel", "parallel", "arbitrary")),
        name=f"dil_r{r}",
    )(qkv)


def _post_kernel(x_ref, osb_ref, ofox_ref, omla_ref, d1_ref, d2_ref, d3_ref, gd_ref, wo_ref,
                 lng_ref, lnb_ref, wr_ref, br_ref, x1_ref, x1t_ref, route_ref, stage_ref, *, alpha, dils):
    hq = N_HEADS_PER_MIXER * HEAD_DIM
    tm = x_ref.shape[0]
    h = _dot(osb_ref[...], wo_ref[0:hq, :])
    h += _dot(ofox_ref[...], wo_ref[hq:2 * hq, :])
    h += _dot(omla_ref[...], wo_ref[2 * hq:3 * hq, :])

    def token_rows(g, d_ref, r, c0):
        if r == 1:
            return d_ref[:, c0:c0 + LANES]
        st = stage_ref.at[g, c0 // LANES]
        for q in range(r):
            st[pl.ds(q, tm // r, stride=r), :] = d_ref[:, q * 2 * hq + c0:q * 2 * hq + c0 + LANES]
        return st[...]

    d_refs = (d1_ref, d2_ref, d3_ref)
    for p in range(N_HEADS_PER_MIXER // 2):
        sl = slice(p * LANES, (p + 1) * LANES)
        lses = [token_rows(g, d_refs[g], dils[g], hq + p * LANES) for g in range(len(dils))]
        vals = [token_rows(g, d_refs[g], dils[g], p * LANES) for g in range(len(dils))]
        l1, l2, l3 = lses
        m = jnp.maximum(jnp.maximum(l1, l2), l3)
        e1, e2, e3 = jnp.exp(l1 - m), jnp.exp(l2 - m), jnp.exp(l3 - m)
        inv = 1.0 / (e1 + e2 + e3)
        od = (e1 * inv) * vals[0] + (e2 * inv) * vals[1] + (e3 * inv) * vals[2]
        od = _head_rms(od, gd_ref[:, sl]).astype(BF16)
        h += _dot(od, wo_ref[3 * hq + p * LANES:3 * hq + (p + 1) * LANES, :])
    x1 = _layernorm(alpha * x_ref[...] + h, lng_ref[...], lnb_ref[...])
    x1_ref[...] = x1
    for j in range(x1.shape[1] // LANES):
        x1t_ref[pl.ds(j, tm, stride=SUBLANES), :] = x1[:, j * LANES:(j + 1) * LANES]

    logits = _dot_f32(x1, wr_ref[...]) + br_ref[...]
    lane = _lane_iota(logits.shape)
    lanef = lane.astype(F32)
    big = float(LANES)
    ninf = -jnp.inf
    gl = jnp.where(lane < N_GROUPS, logits, ninf)
    gmax = jnp.max(gl, axis=-1, keepdims=True)
    gsel = jnp.min(jnp.where(gl == gmax, lanef, big), axis=-1, keepdims=True)
    gw = 1.0 / jnp.sum(jnp.exp(gl - gmax), axis=-1, keepdims=True)
    e_lo = N_GROUPS + EXPERTS_PER_GROUP * gsel
    in_grp = (lanef >= e_lo) & (lanef < e_lo + EXPERTS_PER_GROUP)
    el = jnp.where(in_grp, logits, ninf)
    t1 = jnp.max(el, axis=-1, keepdims=True)
    i1 = jnp.min(jnp.where(el == t1, lanef, big), axis=-1, keepdims=True)
    el2 = jnp.where(lanef == i1, ninf, el)
    t2 = jnp.max(el2, axis=-1, keepdims=True)
    i2 = jnp.min(jnp.where(el2 == t2, lanef, big), axis=-1, keepdims=True)
    ex = jnp.exp(t2 - t1)
    den = 1.0 + ex
    g1 = gw / den
    g2 = gw * ex / den
    out = jnp.where(lane == 0, i1 - N_GROUPS,
                    jnp.where(lane == 1, i2 - N_GROUPS,
                              jnp.where(lane == 2, g1, jnp.where(lane == 3, g2, 0.0))))
    route_ref[...] = out


def _post_call(x, osb, ofox, omla, dil_outs, dils, g_dil, wo, lng, lnb, wr, br, alpha, tm):
    t, d = x.shape
    hq = N_HEADS_PER_MIXER * HEAD_DIM
    row = lambda w: pl.BlockSpec((tm, w), lambda i: (i, 0))
    full = lambda a: pl.BlockSpec(a.shape, lambda i: (0, 0))
    dil_specs = [pl.BlockSpec((tm // r, r * 2 * hq), lambda i: (i, 0)) for r in dils]
    return pl.pallas_call(
        functools.partial(_post_kernel, alpha=alpha, dils=dils),
        grid=(t // tm,),
        in_specs=[row(d), row(hq), row(hq), row(hq)] + dil_specs
                 + [full(g_dil), full(wo), full(lng), full(lnb), full(wr), full(br)],
        out_specs=[row(d), pl.BlockSpec((tm * SUBLANES, LANES), lambda i: (i, 0)), row(LANES)],
        out_shape=[jax.ShapeDtypeStruct((t, d), F32), jax.ShapeDtypeStruct((t * SUBLANES, LANES), F32),
                   jax.ShapeDtypeStruct((t, LANES), F32)],
        scratch_shapes=[pltpu.VMEM((len(dils), 2 * hq // LANES, tm, LANES), F32)],
        compiler_params=_cparams(("parallel",)),
        name="post_mixer",
    )(x, osb, ofox, omla, *dil_outs, g_dil, wo, lng, lnb, wr, br)


DMA_UNROLL = 8


def _moe_kernel(ce_ref, nv_ref, src_ref, srcn_ref, dst_ref, x_hbm, w1_ref, w3_ref, w2_ref, y_hbm,
                xs_ref, ys_ref, gsem, ssem):
    c = pl.program_id(0)
    nc = pl.num_programs(0)
    slot = c % 2
    other = 1 - slot
    nv = nv_ref[c]
    nv_next = jnp.where(c + 1 < nc, nv_ref[jnp.minimum(c + 1, nc - 1)], 0)
    nv_prev = jnp.where(c > 0, nv_ref[jnp.maximum(c - 1, 0)], 0)

    def gather_copy(idx_ref, i, s):
        return pltpu.make_async_copy(
            x_hbm.at[pl.ds(pl.multiple_of(idx_ref[0, 0, i] * SUBLANES, SUBLANES), SUBLANES), :],
            xs_ref.at[s, pl.ds(pl.multiple_of(i * SUBLANES, SUBLANES), SUBLANES), :], gsem.at[s])

    def scatter_copy(i, s):
        return pltpu.make_async_copy(
            ys_ref.at[s, pl.ds(pl.multiple_of(i * SUBLANES, SUBLANES), SUBLANES), :],
            y_hbm.at[pl.ds(pl.multiple_of(dst_ref[0, 0, i] * SUBLANES, SUBLANES), SUBLANES), :], ssem.at[s])

    def start_rows(n, make):
        ngrp = n // DMA_UNROLL

        def grp(g, _):
            base = pl.multiple_of(g * DMA_UNROLL, DMA_UNROLL)
            for k in range(DMA_UNROLL):
                make(base + k).start()
            return 0

        def one(i, _):
            make(i).start()
            return 0

        lax.fori_loop(0, ngrp, grp, 0)
        lax.fori_loop(ngrp * DMA_UNROLL, n, one, 0)

    def wait_rows(n, make_row, make_block):
        @pl.when(n == MOE_BLOCK)
        def _():
            make_block().wait()

        @pl.when(n < MOE_BLOCK)
        def _():
            def one(i, _):
                make_row(i).wait()
                return 0
            lax.fori_loop(0, n, one, 0)

    def gather_block(s):
        return pltpu.make_async_copy(x_hbm.at[pl.ds(0, MOE_BLOCK * SUBLANES), :], xs_ref.at[s], gsem.at[s])

    def scatter_block(s):
        return pltpu.make_async_copy(ys_ref.at[s], y_hbm.at[pl.ds(0, MOE_BLOCK * SUBLANES), :], ssem.at[s])

    @pl.when(c == 0)
    def _():
        xs_ref[...] = jnp.zeros_like(xs_ref)
        start_rows(nv, lambda i: gather_copy(src_ref, i, 0))

    start_rows(nv_next, lambda i: gather_copy(srcn_ref, i, other))
    wait_rows(nv, lambda i: gather_copy(src_ref, i, slot), lambda: gather_block(slot))

    @pl.when(nv > 0)
    def _():
        xt = xs_ref.at[slot]
        a = b = None
        for jj in range(0, xt.shape[0] // MOE_BLOCK, 2):
            xb = jnp.concatenate([xt[pl.ds(jj, MOE_BLOCK, stride=SUBLANES), :],
                                  xt[pl.ds(jj + 1, MOE_BLOCK, stride=SUBLANES), :]], axis=1).astype(BF16)
            ws = slice(jj * LANES, (jj + 2) * LANES)
            da, db = _dot(xb, w1_ref[0, ws, :]), _dot(xb, w3_ref[0, ws, :])
            a, b = (da, db) if a is None else (a + da, b + db)
        hid = (a / (1.0 + jnp.exp(-a)) * b).astype(BF16)
        y = _dot(hid, w2_ref[0])
        yt = ys_ref.at[slot]
        for j in range(y.shape[1] // LANES):
            yt[pl.ds(j, MOE_BLOCK, stride=SUBLANES), :] = y[:, j * LANES:(j + 1) * LANES]

    wait_rows(nv_prev, lambda i: scatter_copy(i, other), lambda: scatter_block(other))
    start_rows(nv, lambda i: scatter_copy(i, slot))

    @pl.when(c == nc - 1)
    def _():
        wait_rows(nv, lambda i: scatter_copy(i, slot), lambda: scatter_block(slot))


def _moe_call(chunk_expert, n_valid, src, dst, x1t, w1, w3, w2, n_rows_out):
    n_chunks = chunk_expert.shape[0]
    d, de = w1.shape[1], w1.shape[2]
    assert d == SUBLANES * LANES and x1t.shape[1] == LANES
    grid_spec = pltpu.PrefetchScalarGridSpec(
        num_scalar_prefetch=2,
        grid=(n_chunks,),
        in_specs=[
            pl.BlockSpec((1, 1, MOE_BLOCK), lambda c, ce, nv: (c, 0, 0), memory_space=pltpu.SMEM),
            pl.BlockSpec((1, 1, MOE_BLOCK), lambda c, ce, nv: (jnp.minimum(c + 1, n_chunks - 1), 0, 0),
                         memory_space=pltpu.SMEM),
            pl.BlockSpec((1, 1, MOE_BLOCK), lambda c, ce, nv: (c, 0, 0), memory_space=pltpu.SMEM),
            pl.BlockSpec(memory_space=pl.ANY),
            pl.BlockSpec((1, d, de), lambda c, ce, nv: (ce[c], 0, 0)),
            pl.BlockSpec((1, d, de), lambda c, ce, nv: (ce[c], 0, 0)),
            pl.BlockSpec((1, de, d), lambda c, ce, nv: (ce[c], 0, 0)),
        ],
        out_specs=pl.BlockSpec(memory_space=pl.ANY),
        scratch_shapes=[
            pltpu.VMEM((2, MOE_BLOCK * SUBLANES, LANES), F32),
            pltpu.VMEM((2, MOE_BLOCK * SUBLANES, LANES), F32),
            pltpu.SemaphoreType.DMA((2,)),
            pltpu.SemaphoreType.DMA((2,)),
        ],
    )
    return pl.pallas_call(
        _moe_kernel,
        grid_spec=grid_spec,
        out_shape=jax.ShapeDtypeStruct((n_rows_out * SUBLANES, LANES), F32),
        compiler_params=_cparams(("arbitrary",)),
        name="moe_experts",
    )(chunk_expert, n_valid, src, src, dst, x1t, w1, w3, w2)


def _combine_kernel(x1_ref, ya_ref, yb_ref, route_ref, lng_ref, lnb_ref, o_ref, *, alpha):
    tm, d = x1_ref.shape
    g1 = route_ref[:, 2:3]
    g2 = route_ref[:, 3:4]

    def rows(y_ref):
        return jnp.concatenate([y_ref[pl.ds(j, tm, stride=SUBLANES), :] for j in range(d // LANES)], axis=1)

    m = g1 * rows(ya_ref) + g2 * rows(yb_ref)
    o_ref[...] = _layernorm(alpha * x1_ref[...] + m, lng_ref[...], lnb_ref[...])


def _combine_call(x1, y, route, lng, lnb, alpha, tm):
    t, d = x1.shape
    nt = t // tm
    full = lambda a: pl.BlockSpec(a.shape, lambda i: (0, 0))
    return pl.pallas_call(
        functools.partial(_combine_kernel, alpha=alpha),
        grid=(nt,),
        in_specs=[pl.BlockSpec((tm, d), lambda i: (i, 0)),
                  pl.BlockSpec((tm * SUBLANES, LANES), lambda i: (i, 0)),
                  pl.BlockSpec((tm * SUBLANES, LANES), lambda i: (i + nt, 0)),
                  pl.BlockSpec((tm, LANES), lambda i: (i, 0)),
                  full(lng), full(lnb)],
        out_specs=pl.BlockSpec((tm, d), lambda i: (i, 0)),
        out_shape=jax.ShapeDtypeStruct((t, d), F32),
        compiler_params=_cparams(("parallel",)),
        name="moe_combine",
    )(x1, y, y, route, lng, lnb)


def _rope_tables(seq, dim, lane_lo):
    half = dim // 2
    inv_freq = ROPE_THETA ** (-jnp.arange(half, dtype=F32) / half)
    ang = jnp.arange(seq, dtype=F32)[:, None] * inv_freq[None, :]
    cos = jnp.concatenate([jnp.cos(ang), jnp.cos(ang)], -1)
    sin = jnp.concatenate([jnp.sin(ang), jnp.sin(ang)], -1)
    if lane_lo == 0:
        reps = LANES // dim
        return jnp.tile(cos, (1, reps)), jnp.tile(sin, (1, reps))
    cos_t = jnp.ones((seq, LANES), F32).at[:, lane_lo:lane_lo + dim].set(cos)
    sin_t = jnp.zeros((seq, LANES), F32).at[:, lane_lo:lane_lo + dim].set(sin)
    return cos_t, sin_t


def _dispatch_tables(route, n_tok):
    expert_id = route[:, 0:TOP_K].astype(jnp.int32).reshape(-1)
    n_assign = n_tok * TOP_K
    n_slots = n_assign + N_EXPERTS * MOE_BLOCK
    n_chunks = n_slots // MOE_BLOCK
    onehot = (expert_id[:, None] == jnp.arange(N_EXPERTS, dtype=jnp.int32)[None, :]).astype(jnp.int32)
    ranks = jnp.cumsum(onehot, axis=0) - onehot
    rank = jnp.sum(ranks * onehot, axis=1)
    counts = jnp.sum(onehot, axis=0)
    padded = (counts + MOE_BLOCK - 1) // MOE_BLOCK * MOE_BLOCK
    pad_end = jnp.cumsum(padded)
    pad_start = pad_end - padded
    dest = pad_start[expert_id] + rank
    assign = jnp.arange(n_assign, dtype=jnp.int32)
    slot_assign = jnp.zeros((n_slots,), jnp.int32).at[dest].set(assign)
    src = slot_assign // TOP_K
    dst = (slot_assign % TOP_K) * n_tok + src
    chunk_start = jnp.arange(n_chunks, dtype=jnp.int32) * MOE_BLOCK
    chunk_expert = jnp.minimum(jnp.searchsorted(pad_end, chunk_start, side="right"),
                               N_EXPERTS - 1).astype(jnp.int32)
    n_valid = jnp.clip(pad_start[chunk_expert] + counts[chunk_expert] - chunk_start,
                       0, MOE_BLOCK).astype(jnp.int32)
    return (chunk_expert, n_valid, src.reshape(n_chunks, 1, MOE_BLOCK),
            dst.reshape(n_chunks, 1, MOE_BLOCK), n_assign)


def _pick_tile(n, pref):
    t = pref
    while n % t:
        t //= 2
    return t


def kernel(x, w_in, b_forget, g_cq, g_ckv, w_uq, w_ukv, g_head, w_out, ln1_g, ln1_b,
           w_group, b_group, w_expert, b_expert, w1, w3, w2, ln2_g, ln2_b):
    bsz, seq, d = x.shape
    depth = w_in.shape[0]
    t = bsz * seq
    alpha = (2.0 * depth) ** 0.25
    hq = N_HEADS_PER_MIXER * HEAD_DIM
    qk_scale = HEAD_DIM ** -0.5
    mla_scale = (MLA_NOPE + MLA_ROPE) ** -0.5
    win = DIL_BRANCHES[0][0]
    assert all(w // r == win for w, r in DIL_BRANCHES)
    assert seq % (DIL_BRANCHES[-1][1] * win) == 0 and d % LANES == 0

    cos64, sin64 = _rope_tables(seq, HEAD_DIM, 0)
    cos_m, sin_m = _rope_tables(seq, MLA_ROPE, MLA_NOPE)
    tq = _pick_tile(seq, 256)
    idx = jnp.arange(tq)
    u_sb = (idx[:, None] > idx[None, :]).astype(BF16)
    tk_sb = _pick_tile(seq, 2 * tq)
    tk_sm = _pick_tile(seq, 4 * tq)

    for l in range(depth):
        wl = w_in[l]
        o_fox, o_mla, o_dil = N_SB, N_SB + N_FOX_QKV + N_HEADS_PER_MIXER, N_SB + N_FOX_QKV + N_HEADS_PER_MIXER + N_MLA
        qs = lambda w: w.at[:, 0:hq].multiply(qk_scale)
        w_sb = qs(wl[:, 0:N_SB])
        w_fx = qs(wl[:, o_fox:o_fox + N_FOX_QKV])
        w_f = wl[:, o_fox + N_FOX_QKV:o_mla]
        w_ml = wl[:, o_mla:o_dil]
        wd = wl[:, o_dil:].reshape(d, 3, len(DIL_BRANCHES), hq)
        w_br = [qs(jnp.concatenate([wd[:, 0, g], wd[:, 1, g], wd[:, 2, g]], axis=1)) for g in range(len(DIL_BRANCHES))]
        w_misc = jnp.concatenate([w_ml, w_f, jnp.zeros((d, MISC_W - N_MLA - N_HEADS_PER_MIXER), F32)], axis=1)
        w_tok = jnp.concatenate([w_sb, w_fx, w_misc] + w_br, axis=1).astype(BF16)

        wq = jnp.pad(w_uq[l].reshape(MLA_Q_LORA, N_HEADS_PER_MIXER, MLA_NOPE + MLA_ROPE),
                     ((0, 0), (0, 0), (0, LANES - MLA_NOPE - MLA_ROPE))).reshape(MLA_Q_LORA, -1).astype(BF16)
        wkv = w_ukv[l].reshape(MLA_KV_LORA, N_HEADS_PER_MIXER, MLA_NOPE + HEAD_DIM)
        wk = jnp.pad(wkv[:, :, :MLA_NOPE], ((0, 0), (0, 0), (0, LANES - MLA_NOPE))).reshape(MLA_KV_LORA, -1).astype(BF16)
        wv = wkv[:, :, MLA_NOPE:].reshape(MLA_KV_LORA, -1).astype(BF16)
        g_flat = g_head[l].reshape(1, -1)
        wr = jnp.concatenate([w_group[l], w_expert[l],
                              jnp.zeros((d, LANES - N_GROUPS - N_EXPERTS), F32)], axis=1)
        br = jnp.concatenate([b_group[l], b_expert[l],
                              jnp.zeros((LANES - N_GROUPS - N_EXPERTS,), F32)]).reshape(1, LANES)

        tm = _pick_tile(seq, 512)
        sb, fx, misc, *qkv_br = _proj_call(
            x, w_tok, cos64, sin64,
            ((N_SB, 0, BF16, 1), (N_FOX_QKV, 0, BF16, 1), (MISC_W, 0, F32, 1))
            + tuple((N_BRANCH, 2 * hq, BF16, r) for _, r in DIL_BRANCHES), tm)
        sb = sb.reshape(t, N_SB)
        fx = fx.reshape(t, N_FOX_QKV)
        misc = misc.reshape(t, MISC_W)

        o_sb = _attn_call("sb", sb, sb, sb, 0, 1, 2, g_flat, 0, bsz, seq, tq, tk_sb, extra=(u_sb,))
        neg_c = _fox_c_call(misc, b_forget[l], bsz, seq)
        o_fx = _attn_call("fox", fx, fx, fx, 0, 1, 2, g_flat, 1, bsz, seq, tq, tk_sm, extra=(neg_c,))
        mq, mk, mv = _mla_prep_call(misc, g_cq[l].reshape(1, -1), g_ckv[l].reshape(1, -1), wq, wk, wv,
                                    cos_m, sin_m, seq, tm)
        o_ml = _attn_call("mla", mq, mk, mv, 0, 0, 0, g_flat, 2, bsz, seq, tq, tk_sm, scale=mla_scale)

        dil = []
        for g, (_, r) in enumerate(DIL_BRANCHES):
            n = seq // r
            tqd = n if n <= tq else max(win, min(tq, n // 2))
            og = _dil_call(qkv_br[g], tqd, win)
            dil.append(og.reshape(t // r, r * 2 * hq))

        x1, x1t, route = _post_call(
            x.reshape(t, d), o_sb, o_fx, o_ml, dil, tuple(r for _, r in DIL_BRANCHES),
            g_flat[:, 3 * hq:], w_out[l].astype(BF16),
            ln1_g[l].reshape(1, d), ln1_b[l].reshape(1, d), wr, br, alpha, _pick_tile(t, 512))

        chunk_expert, n_valid, src, dst, n_rows = _dispatch_tables(route, t)
        y = _moe_call(chunk_expert, n_valid, src, dst, x1t,
                      w1[l].astype(BF16), w3[l].astype(BF16), w2[l].astype(BF16), n_rows)
        x = _combine_call(x1, y, route,
                          ln2_g[l].reshape(1, d), ln2_b[l].reshape(1, d), alpha, _pick_tile(t, 512)).reshape(bsz, seq, d)
    return x
```
